```python
import math
import jax, jax.numpy as jnp
from jax import lax
import numpy as np

D_MODEL = 1024
BATCH = 32
SEQ = 256
DEPTH = 4
DEC_BATCH = 2
DEC_SEQ = 1024
PAST_LEN = 256

GRID_W = 64
N_MIXERS = 4
D_FF = 4 * D_MODEL
EPS = 1e-6
ROPE_BASE = 10000.0
CONV_W = 5
CHUNK = 64
Q_BLOCK = 128
DENSE_KEY_LIMIT = 2048

N_SSD = (DEPTH + 3) // 4
N_MLA = (DEPTH + 2) // 4
N_MLSTM = (DEPTH + 1) // 4
N_DIFF = DEPTH // 4

SSD_DI = 2 * D_MODEL
SSD_HEADDIM = 64
SSD_HEADS = SSD_DI // SSD_HEADDIM
SSD_GROUPS = 8
SSD_DSTATE = 128
SSD_CONV_CH = SSD_DI + 2 * SSD_GROUPS * SSD_DSTATE
SSD_IN = SSD_DI + SSD_CONV_CH + 2 * SSD_HEADS

MLA_HEADS = 16
MLA_Q_RANK = D_MODEL // 2
MLA_KV_RANK = D_MODEL // 4
MLA_D_NOPE = 64
MLA_D_ROPE = 32
MLA_D_V = 64
MLA_IN = MLA_Q_RANK + MLA_KV_RANK + MLA_D_ROPE

MLSTM_DI = 2 * D_MODEL
MLSTM_HEADS = 8
MLSTM_DQK = MLSTM_DI // MLSTM_HEADS // 2
MLSTM_DV = MLSTM_DI // MLSTM_HEADS
MLSTM_IN = 2 * MLSTM_DI + 4 * MLSTM_HEADS

DIFF_HEADS = 8
DIFF_D = D_MODEL // DIFF_HEADS // 2

kernel_name = 'hybrid_diffusion_ssd_mla_mlstm_diffattn_step'


def rmsnorm(x, g):
    xf = x.astype(jnp.float32)
    r = lax.rsqrt(jnp.mean(xf * xf, axis=-1, keepdims=True) + EPS)
    return (xf * r).astype(x.dtype) * g


def modulate(x, shift, scale):
    return x * (1.0 + scale) + shift


def centred_dwconv(x, w, b):
    L = x.shape[1]
    p = CONV_W // 2
    xp = jnp.pad(x, ((0, 0), (p, p), (0, 0)))
    out = b + xp[:, 0:L] * w[0]
    for k in range(1, CONV_W):
        out = out + xp[:, k:k + L] * w[k]
    return out


def axial_rope(x, pos_r, pos_c):
    d = x.shape[-1]
    half = d // 2
    nf = half // 2
    inv = ROPE_BASE ** (-jnp.arange(nf, dtype=jnp.float32) / nf)
    ang = jnp.concatenate([pos_r[:, None] * inv, pos_c[:, None] * inv], axis=-1)
    cos = jnp.cos(ang)[None, :, None, :].astype(x.dtype)
    sin = jnp.sin(ang)[None, :, None, :].astype(x.dtype)
    x1, x2 = x[..., :half], x[..., half:]
    return jnp.concatenate([x1 * cos - x2 * sin, x1 * sin + x2 * cos], axis=-1)


def attend(q, k, v, scale):
    def block(qb):
        s = jnp.einsum('bqhd,bkhd->bhqk', qb, k, preferred_element_type=jnp.float32) * scale
        p = jax.nn.softmax(s, axis=-1).astype(v.dtype)
        return jnp.einsum('bhqk,bkhd->bqhd', p, v)
    if k.shape[1] < DENSE_KEY_LIMIT:
        return block(q)
    Bsz, Tq, H, dq = q.shape
    qb = jnp.moveaxis(q.reshape(Bsz, Tq // Q_BLOCK, Q_BLOCK, H, dq), 1, 0)
    out = lax.map(block, qb)
    return jnp.moveaxis(out, 0, 1).reshape(Bsz, Tq, H, v.shape[-1])


def sqrelu_mlp(h, w1, w2):
    return jnp.square(jax.nn.relu(h @ w1)) @ w2


def ssd_scan(x, dt, A, Bm, Cm, s0):
    Bsz, L, H, P = x.shape
    G, N = Bm.shape[2], Bm.shape[3]
    R = H // G
    nc = L // CHUNK
    f32 = jnp.float32

    def chunks(a):
        return jnp.moveaxis(a.reshape((Bsz, nc, CHUNK) + a.shape[2:]), 1, 0)

    dt32 = dt.astype(f32)
    xdt = chunks((x.astype(f32) * dt32[..., None]).reshape(Bsz, L, G, R, P))
    a = chunks((dt32 * A).reshape(Bsz, L, G, R))
    Bc = chunks(Bm.astype(f32))
    Cc = chunks(Cm.astype(f32))
    mask = jnp.tril(jnp.ones((CHUNK, CHUNK), dtype=bool))[None, :, :, None, None]

    def step(S, inp):
        xdt_c, a_c, B_c, C_c = inp
        cum = jnp.cumsum(a_c, axis=1)
        decay = jnp.exp(jnp.where(mask, cum[:, :, None] - cum[:, None, :], -jnp.inf))
        cb = jnp.einsum('bign,bjgn->bijg', C_c, B_c)
        y = jnp.einsum('bijg,bijgr,bjgrp->bigrp', cb, decay, xdt_c)
        y = y + jnp.einsum('bign,bgrpn->bigrp', C_c, S) * jnp.exp(cum)[..., None]
        tot = cum[:, -1]
        S = jnp.exp(tot)[..., None, None] * S + jnp.einsum(
            'bjgn,bjgr,bjgrp->bgrpn', B_c, jnp.exp(tot[:, None] - cum), xdt_c)
        return S, y

    S, ys = lax.scan(step, s0.astype(f32).reshape(Bsz, G, R, P, N), (xdt, a, Bc, Cc))
    y = jnp.moveaxis(ys, 0, 1).reshape(Bsz, L, H, P).astype(x.dtype)
    return y, S.reshape(Bsz, H, P, N).astype(x.dtype)


def ssd_mixer(h, s0, w_in, conv_w, conv_b, dt_bias, A_log, D_skip, norm_g, w_out):
    Bsz, L, _ = h.shape
    proj = h @ w_in
    z = proj[..., :SSD_DI]
    xbc = jax.nn.silu(centred_dwconv(proj[..., SSD_DI:SSD_DI + SSD_CONV_CH], conv_w, conv_b))
    dt = jax.nn.softplus(proj[..., SSD_DI + SSD_CONV_CH:].reshape(Bsz, L, 2, SSD_HEADS) + dt_bias)
    gn = SSD_GROUPS * SSD_DSTATE
    xs = xbc[..., :SSD_DI].reshape(Bsz, L, SSD_HEADS, SSD_HEADDIM)
    Bm = xbc[..., SSD_DI:SSD_DI + gn].reshape(Bsz, L, SSD_GROUPS, SSD_DSTATE)
    Cm = xbc[..., SSD_DI + gn:].reshape(Bsz, L, SSD_GROUPS, SSD_DSTATE)
    A = -jnp.exp(A_log.astype(jnp.float32))
    rev = lambda t: jnp.flip(t, axis=1)
    yf, sf = ssd_scan(xs, dt[:, :, 0], A[0], Bm, Cm, s0[:, 0])
    yb, sb = ssd_scan(rev(xs), rev(dt[:, :, 1]), A[1], rev(Bm), rev(Cm), s0[:, 1])
    y = yf + rev(yb) + D_skip[:, None] * xs
    y = rmsnorm(y.reshape(Bsz, L, SSD_DI) * jax.nn.silu(z), norm_g)
    return y @ w_out, jnp.stack([sf, sb], axis=1)


def mla_mixer(h, ctx_ckv, ctx_kr, pos, w_in, q_norm_g, kv_norm_g, w_uq, w_ukv, w_o):
    Bsz, L, _ = h.shape
    proj = h @ w_in
    cq = rmsnorm(proj[..., :MLA_Q_RANK], q_norm_g)
    ckv = rmsnorm(proj[..., MLA_Q_RANK:MLA_Q_RANK + MLA_KV_RANK], kv_norm_g)
    kr = proj[..., MLA_Q_RANK + MLA_KV_RANK:]
    q = (cq @ w_uq).reshape(Bsz, L, MLA_HEADS, MLA_D_NOPE + MLA_D_ROPE)
    q_nope, q_rope = q[..., :MLA_D_NOPE], q[..., MLA_D_NOPE:]
    if pos is None:
        keys_ckv, keys_kr = ckv, kr
    else:
        q_rope = axial_rope(q_rope, *pos)
        keys_ckv = jnp.concatenate([ctx_ckv, ckv], axis=1)
        keys_kr = jnp.concatenate([ctx_kr, axial_rope(kr[:, :, None, :], *pos)[:, :, 0]], axis=1)
    Tk = keys_ckv.shape[1]
    kv = (keys_ckv @ w_ukv).reshape(Bsz, Tk, MLA_HEADS, MLA_D_NOPE + MLA_D_V)
    k = jnp.concatenate([kv[..., :MLA_D_NOPE],
                         jnp.broadcast_to(keys_kr[:, :, None, :], (Bsz, Tk, MLA_HEADS, MLA_D_ROPE))], axis=-1)
    o = attend(jnp.concatenate([q_nope, q_rope], axis=-1), k, kv[..., MLA_D_NOPE:],
               (MLA_D_NOPE + MLA_D_ROPE) ** -0.5)
    return o.reshape(Bsz, L, MLA_HEADS * MLA_D_V) @ w_o, (ckv, kr)


def mlstm_scan(q, k, v, logi, logf, C0, n0, m0):
    Bsz, L, H, dk = q.shape
    dv = v.shape[-1]
    nc = L // CHUNK
    f32 = jnp.float32

    def chunks(a):
        return jnp.moveaxis(a.astype(f32).reshape((Bsz, nc, CHUNK) + a.shape[2:]), 1, 0)

    mask = jnp.tril(jnp.ones((CHUNK, CHUNK), dtype=bool))[None, :, :, None]

    def step(carry, inp):
        C, n, m = carry
        q_c, k_c, v_c, i_c, f_c = inp
        b = jnp.cumsum(f_c, axis=1)
        dlog = jnp.where(mask, b[:, :, None, :] - b[:, None, :, :] + i_c[:, None, :, :], -jnp.inf)
        inter = b + m[:, None, :]
        m_comb = jnp.maximum(inter, jnp.max(dlog, axis=2))
        w = jnp.exp(dlog - m_comb[:, :, None, :])
        iw = jnp.exp(inter - m_comb)
        s = jnp.einsum('bihd,bjhd->bijh', q_c, k_c) * w
        num = jnp.einsum('bijh,bjhv->bihv', s, v_c) + iw[..., None] * jnp.einsum('bihd,bhdv->bihv', q_c, C)
        den = jnp.sum(s, axis=2) + iw * jnp.einsum('bihd,bhd->bih', q_c, n)
        h_c = num / jnp.maximum(jnp.abs(den), jnp.exp(-m_comb))[..., None]
        bQ = b[:, -1]
        wlog = bQ[:, None] - b + i_c
        m_new = jnp.maximum(bQ + m, jnp.max(wlog, axis=1))
        sw = jnp.exp(wlog - m_new[:, None])
        cw = jnp.exp(bQ + m - m_new)
        C = cw[..., None, None] * C + jnp.einsum('bjh,bjhd,bjhv->bhdv', sw, k_c, v_c)
        n = cw[..., None] * n + jnp.einsum('bjh,bjhd->bhd', sw, k_c)
        return (C, n, m_new), h_c

    (C, n, m), hs = lax.scan(step, (C0.astype(f32), n0.astype(f32), m0.astype(f32)),
                             (chunks(q), chunks(k), chunks(v), chunks(logi), chunks(logf)))
    hs = jnp.moveaxis(hs, 0, 1).reshape(Bsz, L, H, dv).astype(q.dtype)
    return hs, (C.astype(q.dtype), n.astype(q.dtype), m.astype(q.dtype))


def mlstm_mixer(h, C0, n0, m0, w_up, conv_w, conv_b, gate_b, w_q, w_k, w_v, skip, norm_g, w_down):
    Bsz, L, _ = h.shape
    proj = h @ w_up
    xm = proj[..., :MLSTM_DI]
    z = proj[..., MLSTM_DI:2 * MLSTM_DI]
    gates = proj[..., 2 * MLSTM_DI:].reshape(Bsz, L, 2, 2, MLSTM_HEADS) + gate_b
    xc = jax.nn.silu(centred_dwconv(xm, conv_w, conv_b))
    q = (xc @ w_q).reshape(Bsz, L, MLSTM_HEADS, MLSTM_DQK)
    k = (xc @ w_k).reshape(Bsz, L, MLSTM_HEADS, MLSTM_DQK) * (MLSTM_DQK ** -0.5)
    v = (xm @ w_v).reshape(Bsz, L, MLSTM_HEADS, MLSTM_DV)
    logi = gates[:, :, :, 0]
    logf = jax.nn.log_sigmoid(gates[:, :, :, 1])
    rev = lambda t: jnp.flip(t, axis=1)
    hf, sf = mlstm_scan(q, k, v, logi[:, :, 0], logf[:, :, 0], C0[:, 0], n0[:, 0], m0[:, 0])
    hb, sb = mlstm_scan(rev(q), rev(k), rev(v), rev(logi[:, :, 1]), rev(logf[:, :, 1]),
                        C0[:, 1], n0[:, 1], m0[:, 1])
    hsum = hf + rev(hb)
    hn = rmsnorm(hsum, norm_g.reshape(MLSTM_HEADS, MLSTM_DV)).reshape(Bsz, L, MLSTM_DI)
    y = (hn + skip * xc) * jax.nn.silu(z)
    states = (jnp.stack([sf[0], sb[0]], axis=1), jnp.stack([sf[1], sb[1]], axis=1),
              jnp.stack([sf[2], sb[2]], axis=1))
    return y @ w_down, states


def diff_mixer(h, ctx_k, ctx_v, pos, lam_init, w_qkv, lq1, lk1, lq2, lk2, subln_g, w_o):
    Bsz, L, _ = h.shape
    proj = h @ w_qkv
    q = proj[..., :D_MODEL].reshape(Bsz, L, 2 * DIFF_HEADS, DIFF_D)
    k = proj[..., D_MODEL:2 * D_MODEL].reshape(Bsz, L, 2 * DIFF_HEADS, DIFF_D)
    v = proj[..., 2 * D_MODEL:].reshape(Bsz, L, DIFF_HEADS, 2 * DIFF_D)
    k_own = k.reshape(Bsz, L, DIFF_HEADS, 2 * DIFF_D)
    if pos is None:
        keys, vals = k, v
    else:
        q = axial_rope(q, *pos)
        keys = jnp.concatenate([ctx_k.reshape(Bsz, ctx_k.shape[1], 2 * DIFF_HEADS, DIFF_D),
                                axial_rope(k, *pos)], axis=1)
        vals = jnp.concatenate([ctx_v, v], axis=1)
    Tk = keys.shape[1]
    q = q.reshape(Bsz, L, DIFF_HEADS, 2, DIFF_D)
    keys = keys.reshape(Bsz, Tk, DIFF_HEADS, 2, DIFF_D)
    lam = jnp.exp(jnp.sum(lq1 * lk1)) - jnp.exp(jnp.sum(lq2 * lk2)) + lam_init
    scale = DIFF_D ** -0.5
    o = attend(q[:, :, :, 0], keys[:, :, :, 0], vals, scale) - lam * attend(q[:, :, :, 1], keys[:, :, :, 1], vals, scale)
    o = rmsnorm(o, subln_g) * (1.0 - lam_init)
    return o.reshape(Bsz, L, D_MODEL) @ w_o, (k_own, v)


def setup_inputs(seed: int = 0) -> dict:
    key = jax.random.key(seed)
    keys = iter(jax.random.split(key, 96))
    f32 = jnp.float32
    D = D_MODEL

    def nrm(shape, scale=1.0):
        return jax.random.normal(next(keys), shape, f32) * scale

    def gain(shape):
        return 1.0 + nrm(shape, 0.02)

    dt0 = jnp.exp(jax.random.uniform(next(keys), (N_SSD, 2, SSD_HEADS), f32, math.log(1e-3), math.log(1e-1)))
    dt_bias = dt0 + jnp.log(-jnp.expm1(-dt0))
    a_log = jnp.log(jax.random.uniform(next(keys), (N_SSD, 2, SSD_HEADS), f32, 1.0, 16.0))
    f_bias = jnp.linspace(3.0, 6.0, MLSTM_HEADS, dtype=f32)
    gate_b = jnp.stack([nrm((N_MLSTM, 2, MLSTM_HEADS), 0.1),
                        f_bias + nrm((N_MLSTM, 2, MLSTM_HEADS), 0.1)], axis=2)
    return {
        'x_prompt': nrm((BATCH, SEQ, D)),
        'x_sample': nrm((DEC_BATCH, DEC_SEQ, D)),
        'state_ssd': nrm((DEC_BATCH, N_SSD, 2, SSD_HEADS, SSD_HEADDIM, SSD_DSTATE), 0.1),
        'cache_mla_ckv': nrm((DEC_BATCH, N_MLA, PAST_LEN, MLA_KV_RANK)),
        'cache_mla_krope': nrm((DEC_BATCH, N_MLA, PAST_LEN, MLA_D_ROPE)),
        'state_mlstm_C': nrm((DEC_BATCH, N_MLSTM, 2, MLSTM_HEADS, MLSTM_DQK, MLSTM_DV), 0.1),
        'state_mlstm_n': nrm((DEC_BATCH, N_MLSTM, 2, MLSTM_HEADS, MLSTM_DQK), 0.1),
        'state_mlstm_m': nrm((DEC_BATCH, N_MLSTM, 2, MLSTM_HEADS)),
        'cache_diff_k': nrm((DEC_BATCH, N_DIFF, PAST_LEN, DIFF_HEADS, 2 * DIFF_D)),
        'cache_diff_v': nrm((DEC_BATCH, N_DIFF, PAST_LEN, DIFF_HEADS, 2 * DIFF_D)),
        'c': nrm((DEC_BATCH, D)),
        'c_ctx': nrm((D,)),
        'norm1_g': gain((DEPTH, D)),
        'norm2_g': gain((DEPTH, D)),
        'ada_w': nrm((DEPTH, D, 6 * D), 0.5 * D ** -0.5),
        'ada_b': nrm((DEPTH, 6 * D), 0.02),
        'mlp_w1': nrm((DEPTH, D, D_FF), D ** -0.5),
        'mlp_w2': nrm((DEPTH, D_FF, D), D_FF ** -0.5),
        'final_g': gain((D,)),
        'ssd_w_in': nrm((N_SSD, D, SSD_IN), D ** -0.5),
        'ssd_conv_w': nrm((N_SSD, CONV_W, SSD_CONV_CH), CONV_W ** -0.5),
        'ssd_conv_b': nrm((N_SSD, SSD_CONV_CH), 0.02),
        'ssd_dt_bias': dt_bias,
        'ssd_A_log': a_log,
        'ssd_D': 1.0 + nrm((N_SSD, SSD_HEADS), 0.1),
        'ssd_norm_g': gain((N_SSD, SSD_DI)),
        'ssd_w_out': nrm((N_SSD, SSD_DI, D), SSD_DI ** -0.5),
        'mla_w_in': nrm((N_MLA, D, MLA_IN), D ** -0.5),
        'mla_q_norm_g': gain((N_MLA, MLA_Q_RANK)),
        'mla_kv_norm_g': gain((N_MLA, MLA_KV_RANK)),
        'mla_w_uq': nrm((N_MLA, MLA_Q_RANK, MLA_HEADS * (MLA_D_NOPE + MLA_D_ROPE)), MLA_Q_RANK ** -0.5),
        'mla_w_ukv': nrm((N_MLA, MLA_KV_RANK, MLA_HEADS * (MLA_D_NOPE + MLA_D_V)), MLA_KV_RANK ** -0.5),
        'mla_w_o': nrm((N_MLA, MLA_HEADS * MLA_D_V, D), (MLA_HEADS * MLA_D_V) ** -0.5),
        'mlstm_w_up': nrm((N_MLSTM, D, MLSTM_IN), D ** -0.5),
        'mlstm_conv_w': nrm((N_MLSTM, CONV_W, MLSTM_DI), CONV_W ** -0.5),
        'mlstm_conv_b': nrm((N_MLSTM, MLSTM_DI), 0.02),
        'mlstm_gate_b': gate_b,
        'mlstm_w_q': nrm((N_MLSTM, MLSTM_DI, MLSTM_HEADS * MLSTM_DQK), MLSTM_DI ** -0.5),
        'mlstm_w_k': nrm((N_MLSTM, MLSTM_DI, MLSTM_HEADS * MLSTM_DQK), MLSTM_DI ** -0.5),
        'mlstm_w_v': nrm((N_MLSTM, MLSTM_DI, MLSTM_DI), MLSTM_DI ** -0.5),
        'mlstm_skip': 1.0 + nrm((N_MLSTM, MLSTM_DI), 0.1),
        'mlstm_norm_g': gain((N_MLSTM, MLSTM_DI)),
        'mlstm_w_down': nrm((N_MLSTM, MLSTM_DI, D), MLSTM_DI ** -0.5),
        'diff_w_qkv': nrm((N_DIFF, D, 3 * D), D ** -0.5),
        'diff_lq1': nrm((N_DIFF, DIFF_D), 0.1),
        'diff_lk1': nrm((N_DIFF, DIFF_D), 0.1),
        'diff_lq2': nrm((N_DIFF, DIFF_D), 0.1),
        'diff_lk2': nrm((N_DIFF, DIFF_D), 0.1),
        'diff_subln_g': gain((N_DIFF, 2 * DIFF_D)),
        'diff_w_o': nrm((N_DIFF, D, D), D ** -0.5),
    }


def reference(x_prompt, x_sample, state_ssd, cache_mla_ckv, cache_mla_krope, state_mlstm_C, state_mlstm_n,
              state_mlstm_m, cache_diff_k, cache_diff_v, c, c_ctx, norm1_g, norm2_g, ada_w, ada_b, mlp_w1, mlp_w2,
              final_g, ssd_w_in, ssd_conv_w, ssd_conv_b, ssd_dt_bias, ssd_A_log, ssd_D, ssd_norm_g, ssd_w_out,
              mla_w_in, mla_q_norm_g, mla_kv_norm_g, mla_w_uq, mla_w_ukv, mla_w_o, mlstm_w_up, mlstm_conv_w,
              mlstm_conv_b, mlstm_gate_b, mlstm_w_q, mlstm_w_k, mlstm_w_v, mlstm_skip, mlstm_norm_g, mlstm_w_down,
              diff_w_qkv, diff_lq1, diff_lk1, diff_lq2, diff_lk2, diff_subln_g, diff_w_o):
    rows = x_sample.shape[1] // GRID_W
    pos = (jnp.repeat(jnp.arange(rows, dtype=jnp.float32), GRID_W),
           jnp.tile(jnp.arange(GRID_W, dtype=jnp.float32), rows))
    bp = x_prompt.shape[0]
    ada_p = jax.nn.silu(c_ctx)
    ada_s = jax.nn.silu(c)
    xp, xs = x_prompt, x_sample
    new_ssd, new_ckv, new_kr, new_C, new_n, new_m, new_dk, new_dv = [], [], [], [], [], [], [], []
    for i in range(DEPTH):
        kind, j = i % N_MIXERS, i // N_MIXERS
        mp = jnp.split(ada_p @ ada_w[i] + ada_b[i], 6, axis=-1)
        ms = jnp.split((ada_s @ ada_w[i] + ada_b[i])[:, None, :], 6, axis=-1)
        hp = modulate(rmsnorm(xp, norm1_g[i]), mp[0], mp[1])
        hs = modulate(rmsnorm(xs, norm1_g[i]), ms[0], ms[1])
        if kind == 0:
            prm = (ssd_w_in[j], ssd_conv_w[j], ssd_conv_b[j], ssd_dt_bias[j], ssd_A_log[j], ssd_D[j],
                   ssd_norm_g[j], ssd_w_out[j])
            s0 = jnp.zeros((bp, 2, SSD_HEADS, SSD_HEADDIM, SSD_DSTATE), hp.dtype)
            op, st = ssd_mixer(hp, s0, *prm)
            new_ssd.append(st)
            os_, _ = ssd_mixer(hs, state_ssd[:, j], *prm)
        elif kind == 1:
            prm = (mla_w_in[j], mla_q_norm_g[j], mla_kv_norm_g[j], mla_w_uq[j], mla_w_ukv[j], mla_w_o[j])
            op, (ckv, kr) = mla_mixer(hp, None, None, None, *prm)
            new_ckv.append(ckv)
            new_kr.append(kr)
            os_, _ = mla_mixer(hs, cache_mla_ckv[:, j], cache_mla_krope[:, j], pos, *prm)
        elif kind == 2:
            prm = (mlstm_w_up[j], mlstm_conv_w[j], mlstm_conv_b[j], mlstm_gate_b[j], mlstm_w_q[j], mlstm_w_k[j],
                   mlstm_w_v[j], mlstm_skip[j], mlstm_norm_g[j], mlstm_w_down[j])
            C0 = jnp.zeros((bp, 2, MLSTM_HEADS, MLSTM_DQK, MLSTM_DV), hp.dtype)
            n0 = jnp.zeros((bp, 2, MLSTM_HEADS, MLSTM_DQK), hp.dtype)
            m0 = jnp.zeros((bp, 2, MLSTM_HEADS), hp.dtype)
            op, (Cn, nn_, mn) = mlstm_mixer(hp, C0, n0, m0, *prm)
            new_C.append(Cn)
            new_n.append(nn_)
            new_m.append(mn)
            os_, _ = mlstm_mixer(hs, state_mlstm_C[:, j], state_mlstm_n[:, j], state_mlstm_m[:, j], *prm)
        else:
            lam_init = 0.8 - 0.6 * math.exp(-0.3 * i)
            prm = (lam_init, diff_w_qkv[j], diff_lq1[j], diff_lk1[j], diff_lq2[j], diff_lk2[j], diff_subln_g[j],
                   diff_w_o[j])
            op, (kk, vv) = diff_mixer(hp, None, None, None, *prm)
            new_dk.append(kk)
            new_dv.append(vv)
            os_, _ = diff_mixer(hs, cache_diff_k[:, j], cache_diff_v[:, j], pos, *prm)
        xp = xp + mp[2] * op
        xs = xs + ms[2] * os_
        hp = modulate(rmsnorm(xp, norm2_g[i]), mp[3], mp[4])
        hs = modulate(rmsnorm(xs, norm2_g[i]), ms[3], ms[4])
        xp = xp + mp[5] * sqrelu_mlp(hp, mlp_w1[i], mlp_w2[i])
        xs = xs + ms[5] * sqrelu_mlp(hs, mlp_w1[i], mlp_w2[i])
    y_prompt = rmsnorm(xp, final_g)
    y_sample = rmsnorm(xs, final_g)
    return (y_prompt, y_sample, jnp.stack(new_ssd, axis=1), jnp.stack(new_ckv, axis=1), jnp.stack(new_kr, axis=1),
            jnp.stack(new_C, axis=1), jnp.stack(new_n, axis=1), jnp.stack(new_m, axis=1),
            jnp.stack(new_dk, axis=1), jnp.stack(new_dv, axis=1))
```

```python
import functools
import math

import jax
import jax.numpy as jnp
from jax import lax
from jax.experimental import pallas as pl
from jax.experimental.pallas import tpu as pltpu

F32 = jnp.float32
BF16 = jnp.bfloat16

D = 1024
DEPTH = 4
D_FF = 4 * D
EPS = 1e-6
ROPE_BASE = 10000.0
CONV_W = 5
CHUNK = 64
GRID_W = 64

N_PROMPT_SEQ = 32
L_PROMPT = 256
N_SAMPLE_SEQ = 2
L_SAMPLE = 1024
PAST = 256
ROWS_P = N_PROMPT_SEQ * L_PROMPT
ROWS_S = N_SAMPLE_SEQ * L_SAMPLE
ROWS = ROWS_P + ROWS_S

SSD_DI = 2 * D
SSD_HEADS = 32
SSD_P = 64
SSD_GROUPS = 8
SSD_N = 128
SSD_HPG = SSD_HEADS // SSD_GROUPS

MLA_HEADS = 16
MLA_Q_RANK = 512
MLA_KV_RANK = 256
MLA_NOPE = 64
MLA_ROPE = 32
MLA_V = 64

ML_DI = 2 * D
ML_HEADS = 8
ML_DQK = 128
ML_DV = 256

DF_HEADS = 8
DF_D = 64

LANES = 128
VMEM_LIMIT = 56 * 1024 * 1024


def _cparams(sem):
    return pltpu.CompilerParams(dimension_semantics=sem, vmem_limit_bytes=VMEM_LIMIT)


def _silu(x):
    return x * jax.nn.sigmoid(x)


def _softplus(x):
    return jnp.maximum(x, 0.0) + jnp.log1p(jnp.exp(-jnp.abs(x)))


def _rms(x, g):
    r = lax.rsqrt(jnp.mean(x * x, axis=-1, keepdims=True) + EPS)
    return (x * r) * g


def _dot(a, b):
    return jnp.dot(a, b, preferred_element_type=F32)


def _dot_nt(a, b):
    return lax.dot_general(a, b, (((1,), (1,)), ((), ())), preferred_element_type=F32)


def _dot_tn(a, b):
    return lax.dot_general(a, b, (((0,), (0,)), ((), ())), preferred_element_type=F32)


def _group_of_tile(i, tm):
    npt = ROWS_P // tm
    return jnp.where(i < npt, 0, 1 + (i - npt) // (L_SAMPLE // tm))


def _ada_body(c_ref, w_ref, b_ref, o_ref):
    s = _silu(c_ref[...]).astype(BF16)
    o_ref[...] = _dot(s, w_ref[...].astype(BF16)) + b_ref[...]


def _ada_mod(cvec, ada_w, ada_b):
    tn = 1536
    out = pl.pallas_call(
        _ada_body,
        grid=(DEPTH, 6 * D // tn),
        in_specs=[
            pl.BlockSpec((8, D), lambda l, j: (0, 0)),
            pl.BlockSpec((None, D, tn), lambda l, j: (l, 0, j)),
            pl.BlockSpec((None, 1, tn), lambda l, j: (l, 0, j)),
        ],
        out_specs=pl.BlockSpec((None, 8, tn), lambda l, j: (l, 0, j)),
        out_shape=jax.ShapeDtypeStruct((DEPTH, 8, 6 * D), F32),
        compiler_params=_cparams(("parallel", "parallel")),
        name="ada_mod",
    )(cvec, ada_w, ada_b.reshape(DEPTH, 1, 6 * D))
    return out[:, :3].reshape(DEPTH, 3, 6, D)


class _Pro:
    def __init__(self, rows, consts, mod, i, hs, lhs_out):
        self.rows, self.consts, self.mod, self.i = rows, consts, mod, i
        self._hs, self._lhs_out = hs, lhs_out

    def emit(self, val, cols=slice(None)):
        self._hs[:, cols] = val.astype(BF16)
        if self._lhs_out is not None:
            self._lhs_out[:, cols] = val


def _fused_mm(*, rows, w, k_dim, n_cols, tm, tn, prologue, consts=(), mod=None, epilogue=None, erows=(),
              emod=None, w_col0=0, emit_lhs=False, name):
    npt = ROWS_P // tm
    grid = (ROWS // tm, n_cols // tn)
    has_dual = any(r[0] == "d" for r in rows)

    in_specs, args = [], []
    for r in rows:
        if r[0] == "u":
            _, arr, width, cb = r
            in_specs.append(pl.BlockSpec((tm, width), lambda i, j, cb=cb: (i, cb)))
            args.append(arr)
        else:
            _, arr_p, arr_s, width, cb = r
            in_specs.append(pl.BlockSpec((tm, width), lambda i, j, cb=cb: (jnp.minimum(i, npt - 1), cb)))
            in_specs.append(pl.BlockSpec((tm, width), lambda i, j, cb=cb: (jnp.maximum(i - npt, 0), cb)))
            args += [arr_p, arr_s]
    for c in consts:
        in_specs.append(pl.BlockSpec(c.shape, lambda i, j, nd=c.ndim: (0,) * nd))
        args.append(c)
    if mod is not None:
        in_specs.append(pl.BlockSpec((None, 6, D), lambda i, j: (_group_of_tile(i, tm), 0, 0)))
        args.append(mod)
    in_specs.append(pl.BlockSpec((k_dim, tn), lambda i, j: (0, w_col0 // tn + j)))
    args.append(w)
    for e in erows:
        in_specs.append(pl.BlockSpec((tm, tn), lambda i, j: (i, j)))
        args.append(e)
    if emod is not None:
        in_specs.append(pl.BlockSpec((None, 6, tn), lambda i, j: (_group_of_tile(i, tm), 0, j)))
        args.append(emod)

    out_specs = [pl.BlockSpec((tm, tn), lambda i, j: (i, j))]
    out_shape = [jax.ShapeDtypeStruct((ROWS, n_cols), F32)]
    if emit_lhs:
        out_specs.append(pl.BlockSpec((tm, k_dim), lambda i, j: (i, 0)))
        out_shape.append(jax.ShapeDtypeStruct((ROWS, k_dim), F32))

    def body(*refs):
        it = iter(refs)
        row_refs = [(next(it),) if r[0] == "u" else (next(it), next(it)) for r in rows]
        const_refs = [next(it) for _ in consts]
        mod_ref = next(it) if mod is not None else None
        w_ref = next(it)
        erow_refs = [next(it) for _ in erows]
        emod_ref = next(it) if emod is not None else None
        out_ref = next(it)
        lhs_out = next(it) if emit_lhs else None
        hs = next(it)
        i = pl.program_id(0)
        j = pl.program_id(1)

        def fill(use_prompt):
            chosen = [rr[0] if (len(rr) == 1 or use_prompt) else rr[1] for rr in row_refs]
            prologue(_Pro(chosen, const_refs, mod_ref, i, hs, lhs_out))

        if has_dual:
            pl.when(jnp.logical_and(j == 0, i < npt))(lambda: fill(True))
            pl.when(jnp.logical_and(j == 0, i >= npt))(lambda: fill(False))
        else:
            pl.when(j == 0)(lambda: fill(True))

        acc = _dot(hs[...], w_ref[...].astype(BF16))
        if epilogue is not None:
            acc = epilogue(acc, erow_refs, emod_ref)
        out_ref[...] = acc

    res = pl.pallas_call(
        body,
        grid=grid,
        in_specs=in_specs,
        out_specs=out_specs,
        out_shape=out_shape,
        scratch_shapes=[pltpu.VMEM((tm, k_dim), BF16)],
        compiler_params=_cparams(("parallel", "arbitrary")),
        name=name,
    )(*args)
    return res if emit_lhs else res[0]


def _pro_normmod(p):
    x = p.rows[0][...]
    h = _rms(x, p.consts[0][...]) * (1.0 + p.mod[1:2, :]) + p.mod[0:1, :]
    p.emit(h)


def _pro_cast(p):
    p.emit(p.rows[0][...])


def _pro_rms(p):
    p.emit(_rms(p.rows[0][...], p.consts[0][...]))


def _epi_residual(gate_row):
    def epi(acc, erows, emod):
        return erows[0][...] + emod[gate_row:gate_row + 1, :] * acc
    return epi


def _mlp_body(x_ref, g_ref, mod_ref, w1_ref, w2_ref, o_ref, hs, acc):
    f = pl.program_id(1)

    @pl.when(f == 0)
    def _():
        h = _rms(x_ref[...], g_ref[...]) * (1.0 + mod_ref[4:5, :]) + mod_ref[3:4, :]
        hs[...] = h.astype(BF16)
        acc[...] = jnp.zeros_like(acc)

    u = _dot(hs[...], w1_ref[...].astype(BF16))
    u = jnp.square(jnp.maximum(u, 0.0))
    acc[...] += _dot(u.astype(BF16), w2_ref[...].astype(BF16))

    @pl.when(f == pl.num_programs(1) - 1)
    def _():
        o_ref[...] = x_ref[...] + mod_ref[5:6, :] * acc[...]


def _mlp(x, g, mod, w1, w2):
    tm, tf = 1024, 512
    return pl.pallas_call(
        _mlp_body,
        grid=(ROWS // tm, D_FF // tf),
        in_specs=[
            pl.BlockSpec((tm, D), lambda i, f: (i, 0)),
            pl.BlockSpec((1, D), lambda i, f: (0, 0)),
            pl.BlockSpec((None, 6, D), lambda i, f: (_group_of_tile(i, tm), 0, 0)),
            pl.BlockSpec((D, tf), lambda i, f: (0, f)),
            pl.BlockSpec((tf, D), lambda i, f: (f, 0)),
        ],
        out_specs=pl.BlockSpec((tm, D), lambda i, f: (i, 0)),
        out_shape=jax.ShapeDtypeStruct((ROWS, D), F32),
        scratch_shapes=[pltpu.VMEM((tm, D), BF16), pltpu.VMEM((tm, D), F32)],
        compiler_params=_cparams(("parallel", "arbitrary")),
        name="mlp",
    )(x, g, mod, w1, w2)


def _final_norm_body(x_ref, g_ref, o_ref):
    o_ref[...] = _rms(x_ref[...], g_ref[...])


def _final_norm(x, g, row0, nrows):
    tm = 1024
    b0 = row0 // tm
    return pl.pallas_call(
        _final_norm_body,
        grid=(nrows // tm,),
        in_specs=[pl.BlockSpec((tm, D), lambda i: (b0 + i, 0)), pl.BlockSpec((1, D), lambda i: (0, 0))],
        out_specs=pl.BlockSpec((tm, D), lambda i: (i, 0)),
        out_shape=jax.ShapeDtypeStruct((nrows, D), F32),
        compiler_params=_cparams(("parallel",)),
        name="final_norm",
    )(x, g)


def _chunk_prefix(v, t_in):
    for s in (1, 2, 4, 8, 16, 32):
        v = v + jnp.where(t_in >= s, pltpu.roll(v, s, 0), 0.0)
    return v


def _chunk_suffix(v, t_in):
    n = v.shape[0]
    for s in (1, 2, 4, 8, 16, 32):
        v = v + jnp.where(t_in < CHUNK - s, pltpu.roll(v, n - s, 0), 0.0)
    return v


def _ssd_body(*refs, L, has_s0, emit_state):
    it = iter(refs)
    z_ref, x_ref, b_ref, c_ref, dt_ref = (next(it) for _ in range(5))
    cwx, cbx, cwb, cbb, cwc, cbc = (next(it) for _ in range(6))
    dtb_ref, alog_ref, dl_ref = next(it), next(it), next(it)
    s0_ref = next(it) if has_s0 else None
    y_ref = next(it)
    st_out = next(it) if emit_state else None
    padx, padb, xa, ba, ca, cb4, bt, dtx, cumx, yacc, st = (next(it) for _ in range(11))

    nc = L // CHUNK
    gw = SSD_HPG * SSD_P
    g = pl.program_id(1)

    def conv(in_ref, w_ref, bias_ref, pad):
        width = in_ref.shape[1]
        pad[0:8, :] = jnp.zeros((8, width), F32)
        pad[L + 8:L + 16, :] = jnp.zeros((8, width), F32)
        pad[8:L + 8, :] = in_ref[...]
        acc = bias_ref[...] + pad[6:6 + L, :] * w_ref[0:1, :]
        for k in range(1, CONV_W):
            acc = acc + pad[6 + k:6 + k + L, :] * w_ref[k:k + 1, :]
        return _silu(acc)

    xa[...] = conv(x_ref, cwx, cbx, padx)
    ba[...] = conv(b_ref, cwb, cbb, padb)
    ca[...] = conv(c_ref, cwc, cbc, padb)

    dt_all = _softplus(dt_ref[...] + dtb_ref[...])
    a_all = dt_all * (-jnp.exp(alog_ref[...]))
    shift = jnp.where(g == 0, 0, LANES - SSD_HPG * g)
    dtr = pltpu.roll(dt_all, shift, 1)
    ar = pltpu.roll(a_all, shift, 1)
    lane_blk = lax.broadcasted_iota(jnp.int32, (L, gw), 1) // SSD_P

    def expand(v, off):
        out = jnp.broadcast_to(v[:, off + 3:off + 4], (L, gw))
        for r in (2, 1, 0):
            out = jnp.where(lane_blk == r, v[:, off + r:off + r + 1], out)
        return out

    t_in = lax.broadcasted_iota(jnp.int32, (L, gw), 0) % CHUNK
    dtx[0] = expand(dtr, 0)
    dtx[1] = expand(dtr, SSD_HEADS)
    cumx[0] = _chunk_prefix(expand(ar, 0), t_in)
    cumx[1] = _chunk_suffix(expand(ar, SSD_HEADS), t_in)

    for c in range(nc):
        rows = slice(CHUNK * c, CHUNK * (c + 1))
        bac = ba[rows, :]
        b4 = jnp.concatenate([bac] * SSD_HPG, axis=0).astype(BF16)
        cb4[c] = _dot_nt(ca[rows, :].astype(BF16), b4)
        bt[c] = bac.T.astype(BF16)

    if has_s0:
        for d in range(2):
            for r in range(SSD_HPG):
                st[d, :, SSD_P * r:SSD_P * (r + 1)] = s0_ref[d, r].T
    else:
        st[...] = jnp.zeros_like(st)
    yacc[...] = xa[...] * dl_ref[...]

    ii = lax.broadcasted_iota(jnp.int32, (CHUNK, gw), 0)
    jj = lax.broadcasted_iota(jnp.int32, (CHUNK, gw), 1) % SSD_P
    bdmask = (lax.broadcasted_iota(jnp.int32, (gw, gw), 0) // SSD_P
              == lax.broadcasted_iota(jnp.int32, (gw, gw), 1) // SSD_P)

    def chunk(cidx, d):
        r0 = pl.multiple_of(cidx * CHUNK, CHUNK)
        rows = pl.ds(r0, CHUNK)
        cum = cumx[d, rows, :]
        xdt = xa[rows, :] * dtx[d, rows, :]
        rowv = jnp.sum(jnp.where(ii == jj, cum, 0.0), axis=0, keepdims=True)
        mask = (ii >= jj) if d == 0 else (ii <= jj)
        dec = jnp.where(mask, jnp.exp(cum - rowv), 0.0)
        m = (cb4[cidx] * dec).astype(BF16)
        xb = xdt.astype(BF16)
        x4 = jnp.concatenate([xb] * SSD_HPG, axis=0)
        rhs = jnp.where(bdmask, x4, jnp.zeros_like(x4))
        stv = st[d]
        y = _dot(m, rhs) + _dot(ca[rows, :].astype(BF16), stv.astype(BF16)) * jnp.exp(cum)
        yacc[rows, :] += y
        tot = cumx[d, pl.ds(r0 + (CHUNK - 1 if d == 0 else 0), 1), :]
        wx = (xdt * jnp.exp(tot - cum)).astype(BF16)
        st[d] = jnp.exp(tot) * stv + _dot(bt[cidx], wx)

    def step(c, carry):
        chunk(c, 0)
        chunk(nc - 1 - c, 1)
        return carry

    lax.fori_loop(0, nc, step, 0)

    y_ref[...] = yacc[...] * _silu(z_ref[...])
    if emit_state:
        for d in range(2):
            for r in range(SSD_HPG):
                st_out[d, r] = st[d, :, SSD_P * r:SSD_P * (r + 1)].T


def _ssd_scan(p0, pdt, conv_w, conv_b, dtb, alog, dl, s0, *, nb, L, emit_state):
    rb0 = 0 if s0 is None else ROWS_P // L
    gw = SSD_HPG * SSD_P
    nc = L // CHUNK
    x0 = SSD_DI // gw
    b0 = 2 * SSD_DI // SSD_N
    c0 = b0 + SSD_GROUPS
    wb0 = SSD_DI // SSD_N
    wc0 = wb0 + SSD_GROUPS
    in_specs = [
        pl.BlockSpec((L, gw), lambda b, g: (rb0 + b, g)),
        pl.BlockSpec((L, gw), lambda b, g: (rb0 + b, x0 + g)),
        pl.BlockSpec((L, SSD_N), lambda b, g: (rb0 + b, b0 + g)),
        pl.BlockSpec((L, SSD_N), lambda b, g: (rb0 + b, c0 + g)),
        pl.BlockSpec((L, LANES), lambda b, g: (rb0 + b, 0)),
        pl.BlockSpec((CONV_W, gw), lambda b, g: (0, g)),
        pl.BlockSpec((1, gw), lambda b, g: (0, g)),
        pl.BlockSpec((CONV_W, SSD_N), lambda b, g: (0, wb0 + g)),
        pl.BlockSpec((1, SSD_N), lambda b, g: (0, wb0 + g)),
        pl.BlockSpec((CONV_W, SSD_N), lambda b, g: (0, wc0 + g)),
        pl.BlockSpec((1, SSD_N), lambda b, g: (0, wc0 + g)),
        pl.BlockSpec((1, LANES), lambda b, g: (0, 0)),
        pl.BlockSpec((1, LANES), lambda b, g: (0, 0)),
        pl.BlockSpec((1, gw), lambda b, g: (0, g)),
    ]
    args = [p0, p0, p0, p0, pdt, conv_w, conv_b, conv_w, conv_b, conv_w, conv_b, dtb, alog, dl]
    state_spec = pl.BlockSpec((None, None, 2, SSD_HPG, SSD_P, SSD_N), lambda b, g: (b, 0, 0, g, 0, 0))
    if s0 is not None:
        in_specs.append(state_spec)
        args.append(s0)
    out_specs = [pl.BlockSpec((L, gw), lambda b, g: (b, g))]
    out_shape = [jax.ShapeDtypeStruct((nb * L, SSD_DI), F32)]
    if emit_state:
        out_specs.append(state_spec)
        out_shape.append(jax.ShapeDtypeStruct((nb, 1, 2, SSD_HEADS, SSD_P, SSD_N), F32))
    scratch = [
        pltpu.VMEM((L + 16, gw), F32),
        pltpu.VMEM((L + 16, SSD_N), F32),
        pltpu.VMEM((L, gw), F32),
        pltpu.VMEM((L, SSD_N), F32),
        pltpu.VMEM((L, SSD_N), F32),
        pltpu.VMEM((nc, CHUNK, gw), F32),
        pltpu.VMEM((nc, SSD_N, CHUNK), BF16),
        pltpu.VMEM((2, L, gw), F32),
        pltpu.VMEM((2, L, gw), F32),
        pltpu.VMEM((L, gw), F32),
        pltpu.VMEM((2, SSD_N, gw), F32),
    ]
    res = pl.pallas_call(
        functools.partial(_ssd_body, L=L, has_s0=s0 is not None, emit_state=emit_state),
        grid=(nb, SSD_GROUPS),
        in_specs=in_specs,
        out_specs=out_specs,
        out_shape=out_shape,
        scratch_shapes=scratch,
        compiler_params=_cparams(("parallel", "parallel")),
        name="ssd_scan_p" if s0 is None else "ssd_scan_s",
    )(*args)
    return res


TQ = 256


def _mla_body(*refs, L, sample):
    it = iter(refs)
    cq_ref, ckv_ref, krs_ref = next(it), next(it), next(it)
    gq_ref, gkv_ref = next(it), next(it)
    wqn, wqr, wqrs, wk, wv = (next(it) for _ in range(5))
    if sample:
        cckv_ref, ckr_ref, cq_t, sq_t, ck_t, sk_t = (next(it) for _ in range(6))
    o_ref = next(it)
    if not sample:
        ckv_out, kr_out = next(it), next(it)
    kk, vv, kr = next(it), next(it), next(it)
    qt = pl.program_id(1)
    scale = (MLA_NOPE + MLA_ROPE) ** -0.5

    @pl.when(qt == 0)
    def _():
        ckv = _rms(ckv_ref[...], gkv_ref[...])
        krs = krs_ref[...]
        if sample:
            keys = jnp.concatenate([cckv_ref[...], ckv], axis=0)
            kr_own = krs[:, :MLA_ROPE] * ck_t[...] + krs[:, MLA_ROPE:] * sk_t[...]
            kr_all = jnp.concatenate([ckr_ref[...], kr_own], axis=0)
        else:
            ckv_out[...] = ckv
            kr_out[...] = krs[:, :MLA_ROPE]
            keys = ckv
            kr_all = krs[:, :MLA_ROPE]
        kb = keys.astype(BF16)
        for h in range(MLA_HEADS):
            kk[h] = _dot(kb, wk[h].astype(BF16)).astype(BF16)
            vv[h] = _dot(kb, wv[h].astype(BF16)).astype(BF16)
        kr[...] = kr_all.astype(BF16)

    cq = _rms(cq_ref[...], gq_ref[...]).astype(BF16)
    for h in range(MLA_HEADS):
        qn = _dot(cq, wqn[h].astype(BF16))
        qr = _dot(cq, wqr[h].astype(BF16))
        if sample:
            qr = qr * cq_t[...] + _dot(cq, wqrs[h].astype(BF16)) * sq_t[...]
        s = (_dot_nt(qn.astype(BF16), kk[h]) + _dot_nt(qr.astype(BF16), kr[...])) * scale
        e = jnp.exp(s - jnp.max(s, axis=1, keepdims=True))
        o = _dot(e.astype(BF16), vv[h]) / jnp.sum(e, axis=1, keepdims=True)
        o_ref[:, MLA_V * h:MLA_V * (h + 1)] = o


def _mla_attn(p1, p1s, gq, gkv, wqn, wqr, wqrs, wk, wv, ctx, *, nb, L):
    sample = ctx is not None
    nq = L // TQ
    tk = L + (PAST if sample else 0)
    rbq0 = ROWS_P // TQ if sample else 0
    rbs0 = ROWS_P // L if sample else 0
    ckv_blk = MLA_Q_RANK // MLA_KV_RANK

    def const(a):
        return pl.BlockSpec(a.shape, lambda b, q, nd=a.ndim: (0,) * nd)

    in_specs = [
        pl.BlockSpec((TQ, MLA_Q_RANK), lambda b, q: (rbq0 + b * nq + q, 0)),
        pl.BlockSpec((L, MLA_KV_RANK), lambda b, q: (rbs0 + b, ckv_blk)),
        pl.BlockSpec((L, 2 * MLA_ROPE), lambda b, q: (rbs0 + b, 0)),
        const(gq), const(gkv), const(wqn), const(wqr), const(wqrs), const(wk), const(wv),
    ]
    args = [p1, p1, p1s, gq, gkv, wqn, wqr, wqrs, wk, wv]
    if sample:
        cckv, ckr, c32, s32 = ctx
        in_specs += [
            pl.BlockSpec((None, None, PAST, MLA_KV_RANK), lambda b, q: (b, 0, 0, 0)),
            pl.BlockSpec((None, None, PAST, MLA_ROPE), lambda b, q: (b, 0, 0, 0)),
            pl.BlockSpec((TQ, MLA_ROPE), lambda b, q: (q, 0)),
            pl.BlockSpec((TQ, MLA_ROPE), lambda b, q: (q, 0)),
            const(c32), const(s32),
        ]
        args += [cckv, ckr, c32, s32, c32, s32]
    out_specs = [pl.BlockSpec((TQ, MLA_HEADS * MLA_V), lambda b, q: (b * nq + q, 0))]
    out_shape = [jax.ShapeDtypeStruct((nb * L, MLA_HEADS * MLA_V), F32)]
    if not sample:
        out_specs += [
            pl.BlockSpec((None, None, L, MLA_KV_RANK), lambda b, q: (b, 0, 0, 0)),
            pl.BlockSpec((None, None, L, MLA_ROPE), lambda b, q: (b, 0, 0, 0)),
        ]
        out_shape += [
            jax.ShapeDtypeStruct((nb, 1, L, MLA_KV_RANK), F32),
            jax.ShapeDtypeStruct((nb, 1, L, MLA_ROPE), F32),
        ]
    scratch = [
        pltpu.VMEM((MLA_HEADS, tk, MLA_NOPE), BF16),
        pltpu.VMEM((MLA_HEADS, tk, MLA_V), BF16),
        pltpu.VMEM((tk, MLA_ROPE), BF16),
    ]
    return pl.pallas_call(
        functools.partial(_mla_body, L=L, sample=sample),
        grid=(nb, nq),
        in_specs=in_specs,
        out_specs=out_specs,
        out_shape=out_shape,
        scratch_shapes=scratch,
        compiler_params=_cparams(("parallel", "arbitrary")),
        name="mla_attn_s" if sample else "mla_attn_p",
    )(*args)


def _log_sigmoid(x):
    return -_softplus(-x)


def _mlstm_body(*refs, L, has_s0, emit_state):
    it = iter(refs)
    q_ref, k_ref, v_ref, g_ref, gb_ref, ng_ref = (next(it) for _ in range(6))
    if has_s0:
        c0_ref, n0_ref, m0_ref = next(it), next(it), next(it)
    h_ref = next(it)
    if emit_state:
        c_out, n_out, m_out = next(it), next(it), next(it)
    gsc, hacc, cst, nst, mst = (next(it) for _ in range(5))

    nc = L // CHUNK
    hd = pl.program_id(1)

    gts = g_ref[...] + gb_ref[...]
    gr = pltpu.roll(gts, jnp.where(hd == 0, 0, LANES - hd), 1)
    lf = _log_sigmoid(gr)
    t_in = lax.broadcasted_iota(jnp.int32, (L, LANES), 0) % CHUNK
    gsc[0] = gr
    gsc[1] = _chunk_prefix(lf, t_in)
    gsc[2] = _chunk_suffix(lf, t_in)

    if has_s0:
        for d in range(2):
            cst[d] = c0_ref[d]
            nst[d] = n0_ref[d, pl.ds(hd, 1), :]
            mst[d] = m0_ref[d, pl.ds(hd, 1), :]
    else:
        cst[...] = jnp.zeros_like(cst)
        nst[...] = jnp.zeros_like(nst)
        mst[...] = jnp.zeros_like(mst)
    hacc[...] = jnp.zeros_like(hacc)

    ii = lax.broadcasted_iota(jnp.int32, (CHUNK, CHUNK), 0)
    jj = lax.broadcasted_iota(jnp.int32, (CHUNK, CHUNK), 1)
    eye = ii == jj

    def to_row(col):
        return jnp.sum(jnp.where(eye, col, 0.0), axis=0, keepdims=True)

    def chunk(cidx, d):
        r0 = pl.multiple_of(cidx * CHUNK, CHUNK)
        rows = pl.ds(r0, CHUNK)
        li = 2 * ML_HEADS * d
        lb = li + ML_HEADS
        logi = gsc[0, rows, li:li + 1]
        bc = gsc[1 + d, rows, lb:lb + 1]
        br = to_row(bc)
        ir = to_row(logi)
        mask = (ii >= jj) if d == 0 else (ii <= jj)
        dlog = jnp.where(mask, bc - br + ir, -jnp.inf)
        m_prev = mst[d][:, 0:1]
        inter = bc + m_prev
        mcomb = jnp.maximum(inter, jnp.max(dlog, axis=1, keepdims=True))
        wgt = jnp.exp(dlog - mcomb)
        iw = jnp.exp(inter - mcomb)
        qc = q_ref[rows, :]
        kc = k_ref[rows, :]
        vb = v_ref[rows, :].astype(BF16)
        qb = qc.astype(BF16)
        s = _dot_nt(qb, kc.astype(BF16)) * wgt
        num = _dot(s.astype(BF16), vb) + iw * _dot(qb, cst[d].astype(BF16))
        den = jnp.sum(s, axis=1, keepdims=True) + iw * jnp.sum(qc * nst[d], axis=1, keepdims=True)
        hacc[rows, :] += num / jnp.maximum(jnp.abs(den), jnp.exp(-mcomb))
        bq = gsc[1 + d, pl.ds(r0 + (CHUNK - 1 if d == 0 else 0), 1), lb:lb + 1]
        wlog = bq - bc + logi
        m_new = jnp.maximum(bq + m_prev, jnp.max(wlog, axis=0, keepdims=True))
        sw = jnp.exp(wlog - m_new)
        cw = jnp.exp(bq + m_prev - m_new)
        kw = kc * sw
        cst[d] = cw * cst[d] + _dot_tn(kw.astype(BF16), vb)
        nst[d] = cw * nst[d] + jnp.sum(kw, axis=0, keepdims=True)
        mst[d] = jnp.broadcast_to(m_new, (1, LANES))

    def step(c, carry):
        chunk(c, 0)
        chunk(nc - 1 - c, 1)
        return carry

    lax.fori_loop(0, nc, step, 0)

    h_ref[...] = _rms(hacc[...], ng_ref[...])
    if emit_state:
        for d in range(2):
            c_out[d] = cst[d]
            n_out[d, pl.ds(hd, 1), :] = nst[d]
            m_out[d, pl.ds(hd, 1), :] = mst[d]


def _mlstm_scan(q, k, v, gates, gb, ng, s0, *, nb, L, emit_state):
    rb0 = 0 if s0 is None else ROWS_P // L
    in_specs = [
        pl.BlockSpec((L, ML_DQK), lambda b, h: (rb0 + b, h)),
        pl.BlockSpec((L, ML_DQK), lambda b, h: (rb0 + b, h)),
        pl.BlockSpec((L, ML_DV), lambda b, h: (rb0 + b, h)),
        pl.BlockSpec((L, LANES), lambda b, h: (rb0 + b, 0)),
        pl.BlockSpec((1, LANES), lambda b, h: (0, 0)),
        pl.BlockSpec((1, ML_DV), lambda b, h: (0, h)),
    ]
    args = [q, k, v, gates, gb, ng]
    c_spec = pl.BlockSpec((None, None, 2, None, ML_DQK, ML_DV), lambda b, h: (b, 0, 0, h, 0, 0))
    n_spec = pl.BlockSpec((None, 2, ML_HEADS, LANES), lambda b, h: (b, 0, 0, 0))
    if s0 is not None:
        in_specs += [c_spec, n_spec, n_spec]
        args += list(s0)
    out_specs = [pl.BlockSpec((L, ML_DV), lambda b, h: (b, h))]
    out_shape = [jax.ShapeDtypeStruct((nb * L, ML_DI), F32)]
    if emit_state:
        out_specs += [c_spec, n_spec, n_spec]
        out_shape += [
            jax.ShapeDtypeStruct((nb, 1, 2, ML_HEADS, ML_DQK, ML_DV), F32),
            jax.ShapeDtypeStruct((nb, 2, ML_HEADS, LANES), F32),
            jax.ShapeDtypeStruct((nb, 2, ML_HEADS, LANES), F32),
        ]
    scratch = [
        pltpu.VMEM((3, L, LANES), F32),
        pltpu.VMEM((L, ML_DV), F32),
        pltpu.VMEM((2, ML_DQK, ML_DV), F32),
        pltpu.VMEM((2, 1, ML_DQK), F32),
        pltpu.VMEM((2, 1, LANES), F32),
    ]
    return pl.pallas_call(
        functools.partial(_mlstm_body, L=L, has_s0=s0 is not None, emit_state=emit_state),
        grid=(nb, ML_HEADS),
        in_specs=in_specs,
        out_specs=out_specs,
        out_shape=out_shape,
        scratch_shapes=scratch,
        compiler_params=_cparams(("parallel", "arbitrary")),
        name="mlstm_scan_p" if s0 is None else "mlstm_scan_s",
    )(*args)


def _rope_blocks(x, ct, st):
    lane = lax.broadcasted_iota(jnp.int32, (x.shape[0], LANES), 1)
    first_half = lane % DF_D < DF_D // 2
    outs = []
    for blk in range(x.shape[1] // LANES):
        xb = x[:, LANES * blk:LANES * (blk + 1)]
        swapped = jnp.where(first_half, pltpu.roll(xb, LANES - DF_D // 2, 1), pltpu.roll(xb, DF_D // 2, 1))
        outs.append(xb * ct + swapped * st)
    return jnp.concatenate(outs, axis=1)


def _diff_body(*refs, L, sample, lam_init):
    it = iter(refs)
    q_ref, k_ref, v_ref = next(it), next(it), next(it)
    lq1, lk1, lq2, lk2, sg_ref = (next(it) for _ in range(5))
    if sample:
        ck_ref, cv_ref, cq_t, sq_t, ck_t, sk_t = (next(it) for _ in range(6))
    o_ref = next(it)
    if sample:
        ka, va = next(it), next(it)
    else:
        k_out, v_out = next(it), next(it)
    qt = pl.program_id(1)

    lam = (jnp.exp(jnp.sum(lq1[...] * lk1[...], axis=1, keepdims=True))
           - jnp.exp(jnp.sum(lq2[...] * lk2[...], axis=1, keepdims=True)) + lam_init)

    @pl.when(qt == 0)
    def _():
        if sample:
            ka[0:PAST, :] = ck_ref[...].astype(BF16)
            ka[PAST:PAST + L, :] = _rope_blocks(k_ref[...], ck_t[...], sk_t[...]).astype(BF16)
            va[0:PAST, :] = cv_ref[...].astype(BF16)
            va[PAST:PAST + L, :] = v_ref[...].astype(BF16)
        else:
            k_out[...] = k_ref[...]
            v_out[...] = v_ref[...]

    q = q_ref[...]
    if sample:
        q = _rope_blocks(q, cq_t[...], sq_t[...])
    q = q * (DF_D ** -0.5)
    lo = lax.broadcasted_iota(jnp.int32, (TQ, LANES), 1) < DF_D

    def attend(qm, kh, vh):
        s = _dot_nt(qm, kh)
        e = jnp.exp(s - jnp.max(s, axis=1, keepdims=True))
        return _dot(e.astype(BF16), vh) / jnp.sum(e, axis=1, keepdims=True)

    for h in range(DF_HEADS):
        cols = slice(LANES * h, LANES * (h + 1))
        qh = q[:, cols]
        if sample:
            kh, vh = ka[:, cols], va[:, cols]
        else:
            kh, vh = k_ref[:, cols].astype(BF16), v_ref[:, cols].astype(BF16)
        a0 = attend(jnp.where(lo, qh, 0.0).astype(BF16), kh, vh)
        a1 = attend(jnp.where(lo, 0.0, qh).astype(BF16), kh, vh)
        o = a0 - lam * a1
        o_ref[:, cols] = _rms(o, sg_ref[...]) * (1.0 - lam_init)


def _diff_attn(p3, lq1, lk1, lq2, lk2, sg, ctx, *, nb, L, lam_init):
    sample = ctx is not None
    nq = L // TQ
    rbq0 = ROWS_P // TQ if sample else 0
    rbs0 = ROWS_P // L if sample else 0

    def const(a):
        return pl.BlockSpec(a.shape, lambda b, q, nd=a.ndim: (0,) * nd)

    in_specs = [
        pl.BlockSpec((TQ, D), lambda b, q: (rbq0 + b * nq + q, 0)),
        pl.BlockSpec((L, D), lambda b, q: (rbs0 + b, 1)),
        pl.BlockSpec((L, D), lambda b, q: (rbs0 + b, 2)),
        const(lq1), const(lk1), const(lq2), const(lk2), const(sg),
    ]
    args = [p3, p3, p3, lq1, lk1, lq2, lk2, sg]
    kv_spec = pl.BlockSpec((None, None, PAST if sample else L, D), lambda b, q: (b, 0, 0, 0))
    if sample:
        ck, cv, c128, s128 = ctx
        in_specs += [
            kv_spec, kv_spec,
            pl.BlockSpec((TQ, LANES), lambda b, q: (q, 0)),
            pl.BlockSpec((TQ, LANES), lambda b, q: (q, 0)),
            const(c128), const(s128),
        ]
        args += [ck, cv, c128, s128, c128, s128]
    out_specs = [pl.BlockSpec((TQ, D), lambda b, q: (b * nq + q, 0))]
    out_shape = [jax.ShapeDtypeStruct((nb * L, D), F32)]
    scratch = []
    if sample:
        scratch = [pltpu.VMEM((PAST + L, D), BF16), pltpu.VMEM((PAST + L, D), BF16)]
    else:
        out_specs += [kv_spec, kv_spec]
        out_shape += [jax.ShapeDtypeStruct((nb, 1, L, D), F32)] * 2
    return pl.pallas_call(
        functools.partial(_diff_body, L=L, sample=sample, lam_init=lam_init),
        grid=(nb, nq),
        in_specs=in_specs,
        out_specs=out_specs,
        out_shape=out_shape,
        scratch_shapes=scratch,
        compiler_params=_cparams(("parallel", "arbitrary")),
        name="diff_attn_s" if sample else "diff_attn_p",
    )(*args)


def _pro_mlstm_conv(p):
    x_ref = p.rows[0]
    w_ref, b_ref = p.consts
    tm = x_ref.shape[0]
    seq = jnp.where(p.i < ROWS_P // tm, L_PROMPT, L_SAMPLE)
    cw = 256
    pos = lax.broadcasted_iota(jnp.int32, (tm, cw), 0) & (seq - 1)
    for cb in range(ML_DI // cw):
        cols = slice(cw * cb, cw * (cb + 1))
        x = x_ref[:, cols]
        acc = None
        for k in range(CONV_W):
            off = k - CONV_W // 2
            src = pos + off
            tap = x if off == 0 else pltpu.roll(x, (-off) % tm, 0)
            tap = jnp.where(jnp.logical_and(src >= 0, src < seq), tap, 0.0) * w_ref[k:k + 1, cols]
            acc = b_ref[:, cols] + tap if acc is None else acc + tap
        p.emit(_silu(acc), cols)


def _pro_mlstm_gate(p):
    hn, xc, z = p.rows[0][...], p.rows[1][...], p.rows[2][...]
    p.emit((hn + p.consts[0][...] * xc) * _silu(z))


def _rope_tables(d):
    rows = L_SAMPLE // GRID_W
    pos_r = jnp.repeat(jnp.arange(rows, dtype=F32), GRID_W)
    pos_c = jnp.tile(jnp.arange(GRID_W, dtype=F32), rows)
    nf = d // 4
    inv = ROPE_BASE ** (-jnp.arange(nf, dtype=F32) / nf)
    ang = jnp.concatenate([pos_r[:, None] * inv, pos_c[:, None] * inv], axis=-1)
    cos, sin = jnp.cos(ang), jnp.sin(ang)
    return jnp.concatenate([cos, cos], axis=-1), jnp.concatenate([-sin, sin], axis=-1)


def _pad_cols(a, n):
    return jnp.pad(a, ((0, 0), (0, n - a.shape[1])))


def kernel(x_prompt, x_sample, state_ssd, cache_mla_ckv, cache_mla_krope, state_mlstm_C, state_mlstm_n, state_mlstm_m, cache_diff_k, cache_diff_v, c, c_ctx, norm1_g, norm2_g, ada_w, ada_b, mlp_w1, mlp_w2, final_g, ssd_w_in, ssd_conv_w, ssd_conv_b, ssd_dt_bias, ssd_A_log, ssd_D, ssd_norm_g, ssd_w_out, mla_w_in, mla_q_norm_g, mla_kv_norm_g, mla_w_uq, mla_w_ukv, mla_w_o, mlstm_w_up, mlstm_conv_w, mlstm_conv_b, mlstm_gate_b, mlstm_w_q, mlstm_w_k, mlstm_w_v, mlstm_skip, mlstm_norm_g, mlstm_w_down, diff_w_qkv, diff_lq1, diff_lk1, diff_lq2, diff_lk2, diff_subln_g, diff_w_o):
    x = jnp.concatenate([x_prompt.reshape(ROWS_P, D), x_sample.reshape(ROWS_S, D)], axis=0)
    cvec = jnp.concatenate([c_ctx[None, :], c, jnp.zeros((5, D), F32)], axis=0)
    mod_all = _ada_mod(cvec, ada_w, ada_b)

    def in_proj(xin, layer, w, n_cols, tn, w_col0=0, name="in_proj"):
        return _fused_mm(rows=[("u", xin, D, 0)], consts=[norm1_g[layer][None, :]], mod=mod_all[layer], w=w, k_dim=D,
                         n_cols=n_cols, tm=1024, tn=tn, prologue=_pro_normmod, w_col0=w_col0, name=name)

    def out_proj(xin, layer, rows, consts, prologue, w, k_dim, name):
        return _fused_mm(rows=rows, consts=consts, w=w, k_dim=k_dim, n_cols=D, tm=512, tn=512, prologue=prologue,
                         epilogue=_epi_residual(2), erows=[xin], emod=mod_all[layer], name=name)

    p0 = in_proj(x, 0, ssd_w_in[0], 3 * SSD_DI, 512, name="ssd_in")
    pdt = in_proj(x, 0, _pad_cols(ssd_w_in[0][:, 3 * SSD_DI:], LANES), LANES, LANES, name="ssd_in_dt")
    dtb = _pad_cols(ssd_dt_bias[0].reshape(1, 2 * SSD_HEADS), LANES)
    alog = _pad_cols(ssd_A_log[0].reshape(1, 2 * SSD_HEADS), LANES)
    dl = jnp.repeat(ssd_D[0], SSD_P)[None, :]
    scan_args = (p0, pdt, ssd_conv_w[0], ssd_conv_b[0][None, :], dtb, alog, dl)
    yg_p, new_ssd = _ssd_scan(*scan_args, None, nb=N_PROMPT_SEQ, L=L_PROMPT, emit_state=True)
    (yg_s,) = _ssd_scan(*scan_args, state_ssd, nb=N_SAMPLE_SEQ, L=L_SAMPLE, emit_state=False)
    x = out_proj(x, 0, [("d", yg_p, yg_s, SSD_DI, 0)], [ssd_norm_g[0][None, :]], _pro_rms, ssd_w_out[0], SSD_DI,
                 "ssd_out")
    x = _mlp(x, norm2_g[0][None, :], mod_all[0], mlp_w1[0], mlp_w2[0])

    w_in = mla_w_in[0]
    kr0 = MLA_Q_RANK + MLA_KV_RANK
    half = MLA_ROPE // 2
    w_kr = jnp.concatenate([w_in[:, kr0:kr0 + MLA_ROPE], w_in[:, kr0 + half:kr0 + MLA_ROPE], w_in[:, kr0:kr0 + half]],
                           axis=1)
    p1 = in_proj(x, 1, w_in, kr0, 256, name="mla_in")
    p1s = in_proj(x, 1, w_kr, 2 * MLA_ROPE, 2 * MLA_ROPE, name="mla_in_kr")
    wuq = mla_w_uq[0].reshape(MLA_Q_RANK, MLA_HEADS, MLA_NOPE + MLA_ROPE)
    wqn = jnp.transpose(wuq[:, :, :MLA_NOPE], (1, 0, 2))
    wqr = jnp.transpose(wuq[:, :, MLA_NOPE:], (1, 0, 2))
    wqrs = jnp.concatenate([wqr[..., half:], wqr[..., :half]], axis=-1)
    wukv = mla_w_ukv[0].reshape(MLA_KV_RANK, MLA_HEADS, MLA_NOPE + MLA_V)
    wk = jnp.transpose(wukv[:, :, :MLA_NOPE], (1, 0, 2))
    wv = jnp.transpose(wukv[:, :, MLA_NOPE:], (1, 0, 2))
    c32, s32 = _rope_tables(MLA_ROPE)
    mla_w = (mla_q_norm_g[0][None, :], mla_kv_norm_g[0][None, :], wqn, wqr, wqrs, wk, wv)
    o_p, new_ckv, new_kr = _mla_attn(p1, p1s, *mla_w, None, nb=N_PROMPT_SEQ, L=L_PROMPT)
    (o_s,) = _mla_attn(p1, p1s, *mla_w, (cache_mla_ckv, cache_mla_krope, c32, s32), nb=N_SAMPLE_SEQ, L=L_SAMPLE)
    x = out_proj(x, 1, [("d", o_p, o_s, D, 0)], [], _pro_cast, mla_w_o[0], D, "mla_out")
    x = _mlp(x, norm2_g[1][None, :], mod_all[1], mlp_w1[1], mlp_w2[1])

    p2 = in_proj(x, 2, mlstm_w_up[0], 2 * ML_DI, 512, name="mlstm_up")
    gates = in_proj(x, 2, _pad_cols(mlstm_w_up[0][:, 2 * ML_DI:], LANES), LANES, LANES, name="mlstm_up_gates")
    q, xc = _fused_mm(rows=[("u", p2, ML_DI, 0)], consts=[mlstm_conv_w[0], mlstm_conv_b[0][None, :]], w=mlstm_w_q[0],
                      k_dim=ML_DI, n_cols=ML_HEADS * ML_DQK, tm=1024, tn=256, prologue=_pro_mlstm_conv, emit_lhs=True,
                      name="mlstm_q")
    k = _fused_mm(rows=[("u", xc, ML_DI, 0)], w=mlstm_w_k[0], k_dim=ML_DI, n_cols=ML_HEADS * ML_DQK, tm=1024, tn=512,
                  prologue=_pro_cast, epilogue=lambda acc, e, m: acc * (ML_DQK ** -0.5), name="mlstm_k")
    v = _fused_mm(rows=[("u", p2, ML_DI, 0)], w=mlstm_w_v[0], k_dim=ML_DI, n_cols=ML_DI, tm=1024, tn=512,
                  prologue=_pro_cast, name="mlstm_v")
    gb = _pad_cols(mlstm_gate_b[0].reshape(1, 4 * ML_HEADS), LANES)
    ng = mlstm_norm_g[0][None, :]
    n0 = _pad_cols(state_mlstm_n[:, 0].reshape(N_SAMPLE_SEQ * 2 * ML_HEADS, ML_DQK), LANES).reshape(
        N_SAMPLE_SEQ, 2, ML_HEADS, LANES)
    m0 = jnp.broadcast_to(state_mlstm_m[:, 0][..., None], (N_SAMPLE_SEQ, 2, ML_HEADS, LANES))
    hn_p, new_c, new_n, new_m = _mlstm_scan(q, k, v, gates, gb, ng, None, nb=N_PROMPT_SEQ, L=L_PROMPT, emit_state=True)
    (hn_s,) = _mlstm_scan(q, k, v, gates, gb, ng, (state_mlstm_C, n0, m0), nb=N_SAMPLE_SEQ, L=L_SAMPLE,
                          emit_state=False)
    x = out_proj(x, 2, [("d", hn_p, hn_s, ML_DI, 0), ("u", xc, ML_DI, 0), ("u", p2, ML_DI, 1)],
                 [mlstm_skip[0][None, :]], _pro_mlstm_gate, mlstm_w_down[0], ML_DI, "mlstm_down")
    x = _mlp(x, norm2_g[2][None, :], mod_all[2], mlp_w1[2], mlp_w2[2])

    lam_init = 0.8 - 0.6 * math.exp(-0.3 * 3)
    p3 = in_proj(x, 3, diff_w_qkv[0], 3 * D, 512, name="diff_qkv")
    c64, s64 = _rope_tables(DF_D)
    c128 = jnp.concatenate([c64, c64], axis=-1)
    s128 = jnp.concatenate([s64, s64], axis=-1)
    dparams = (diff_lq1, diff_lk1, diff_lq2, diff_lk2, diff_subln_g)
    od_p, new_dk, new_dv = _diff_attn(p3, *dparams, None, nb=N_PROMPT_SEQ, L=L_PROMPT, lam_init=lam_init)
    ctx = (cache_diff_k.reshape(N_SAMPLE_SEQ, 1, PAST, D), cache_diff_v.reshape(N_SAMPLE_SEQ, 1, PAST, D), c128, s128)
    (od_s,) = _diff_attn(p3, *dparams, ctx, nb=N_SAMPLE_SEQ, L=L_SAMPLE, lam_init=lam_init)
    x = out_proj(x, 3, [("d", od_p, od_s, D, 0)], [], _pro_cast, diff_w_o[0], D, "diff_out")
    x = _mlp(x, norm2_g[3][None, :], mod_all[3], mlp_w1[3], mlp_w2[3])

    fg = final_g[None, :]
    y_prompt = _final_norm(x, fg, 0, ROWS_P).reshape(N_PROMPT_SEQ, L_PROMPT, D)
    y_sample = _final_norm(x, fg, ROWS_P, ROWS_S).reshape(N_SAMPLE_SEQ, L_SAMPLE, D)
    return (y_prompt, y_sample, new_ssd, new_ckv, new_kr, new_c,
            new_n[None].reshape(N_PROMPT_SEQ, 1, 2, ML_HEADS, ML_DQK),
            new_m[..., 0].reshape(N_PROMPT_SEQ, 1, 2, ML_HEADS),
            new_dk.reshape(N_PROMPT_SEQ, 1, L_PROMPT, DF_HEADS, 2 * DF_D),
            new_dv.reshape(N_PROMPT_SEQ, 1, L_PROMPT, DF_HEADS, 2 * DF_D))
```

```python
import functools
import math

import jax
import jax.numpy as jnp
from jax import lax
from jax.experimental import pallas as pl
from jax.experimental.pallas import tpu as pltpu

F32 = jnp.float32
BF16 = jnp.bfloat16

D = 1024
DEPTH = 4
D_FF = 4 * D
EPS = 1e-6
ROPE_BASE = 10000.0
CONV_W = 5
GRID_W = 64

N_PROMPT_SEQ = 32
L_PROMPT = 256
N_SAMPLE_SEQ = 2
L_SAMPLE = 1024
PAST = 256
ROWS_P = N_PROMPT_SEQ * L_PROMPT
ROWS_S = N_SAMPLE_SEQ * L_SAMPLE
ROWS = ROWS_P + ROWS_S

SSD_DI = 2 * D
SSD_HEADS = 32
SSD_P = 64
SSD_GROUPS = 8
SSD_N = 128
SSD_HPG = SSD_HEADS // SSD_GROUPS

MLA_HEADS = 16
MLA_Q_RANK = 512
MLA_KV_RANK = 256
MLA_NOPE = 64
MLA_ROPE = 32
MLA_V = 64

ML_DI = 2 * D
ML_HEADS = 8
ML_DQK = 128
ML_DV = 256

DF_HEADS = 8
DF_D = 64

LANES = 128
VMEM_LIMIT = 56 * 1024 * 1024


def _cparams(sem):
    return pltpu.CompilerParams(dimension_semantics=sem, vmem_limit_bytes=VMEM_LIMIT)


def _silu(x):
    return x * jax.nn.sigmoid(x)


def _softplus(x):
    return jnp.maximum(x, 0.0) + jnp.log1p(jnp.exp(-jnp.abs(x)))


def _rms(x, g):
    r = lax.rsqrt(jnp.mean(x * x, axis=-1, keepdims=True) + EPS)
    return (x * r) * g


def _dot(a, b):
    return jnp.dot(a, b, preferred_element_type=F32)


def _dot_nt(a, b):
    return lax.dot_general(a, b, (((1,), (1,)), ((), ())), preferred_element_type=F32)


def _dot_tn(a, b):
    return lax.dot_general(a, b, (((0,), (0,)), ((), ())), preferred_element_type=F32)


def _group_of_tile(i, tm):
    npt = ROWS_P // tm
    return jnp.where(i < npt, 0, 1 + (i - npt) // (L_SAMPLE // tm))


def _ada_body(c_ref, w_ref, b_ref, o_ref):
    s = _silu(c_ref[...]).astype(BF16)
    o_ref[...] = _dot(s, w_ref[...].astype(BF16)) + b_ref[...]


def _ada_mod(cvec, ada_w, ada_b):
    tn = 1536
    out = pl.pallas_call(
        _ada_body,
        grid=(DEPTH, 6 * D // tn),
        in_specs=[
            pl.BlockSpec((8, D), lambda l, j: (0, 0)),
            pl.BlockSpec((None, D, tn), lambda l, j: (l, 0, j)),
            pl.BlockSpec((None, 1, tn), lambda l, j: (l, 0, j)),
        ],
        out_specs=pl.BlockSpec((None, 8, tn), lambda l, j: (l, 0, j)),
        out_shape=jax.ShapeDtypeStruct((DEPTH, 8, 6 * D), F32),
        compiler_params=_cparams(("parallel", "parallel")),
        name="ada_mod",
    )(cvec, ada_w, ada_b.reshape(DEPTH, 1, 6 * D))
    return out[:, :3].reshape(DEPTH, 3, 6, D)


class _Pro:
    def __init__(self, rows, consts, mod, i, hs, lhs_out):
        self.rows, self.consts, self.mod, self.i = rows, consts, mod, i
        self._hs, self._lhs_out = hs, lhs_out

    def emit(self, val, cols=slice(None)):
        self._hs[:, cols] = val.astype(BF16)
        if self._lhs_out is not None:
            self._lhs_out[:, cols] = val


def _fused_mm(*, rows, w, k_dim, n_cols, tm, tn, prologue, consts=(), mod=None, epilogue=None, erows=(),
              emod=None, w_col0=0, emit_lhs=False, name):
    npt = ROWS_P // tm
    grid = (ROWS // tm, n_cols // tn)
    has_dual = any(r[0] == "d" for r in rows)

    in_specs, args = [], []
    for r in rows:
        if r[0] == "u":
            _, arr, width, cb = r
            in_specs.append(pl.BlockSpec((tm, width), lambda i, j, cb=cb: (i, cb)))
            args.append(arr)
        else:
            _, arr_p, arr_s, width, cb = r
            in_specs.append(pl.BlockSpec((tm, width), lambda i, j, cb=cb: (jnp.minimum(i, npt - 1), cb)))
            in_specs.append(pl.BlockSpec((tm, width), lambda i, j, cb=cb: (jnp.maximum(i - npt, 0), cb)))
            args += [arr_p, arr_s]
    for c in consts:
        in_specs.append(pl.BlockSpec(c.shape, lambda i, j, nd=c.ndim: (0,) * nd))
        args.append(c)
    if mod is not None:
        mod_arr, mod_layer = mod
        in_specs.append(pl.BlockSpec((None, None, 6, D), lambda i, j: (mod_layer, _group_of_tile(i, tm), 0, 0)))
        args.append(mod_arr)
    in_specs.append(pl.BlockSpec((k_dim, tn), lambda i, j: (0, w_col0 // tn + j)))
    args.append(w)
    for e in erows:
        in_specs.append(pl.BlockSpec((tm, tn), lambda i, j: (i, j)))
        args.append(e)
    if emod is not None:
        emod_arr, emod_layer = emod
        in_specs.append(pl.BlockSpec((None, None, 6, tn), lambda i, j: (emod_layer, _group_of_tile(i, tm), 0, j)))
        args.append(emod_arr)

    out_specs = [pl.BlockSpec((tm, tn), lambda i, j: (i, j))]
    out_shape = [jax.ShapeDtypeStruct((ROWS, n_cols), F32)]
    if emit_lhs:
        out_specs.append(pl.BlockSpec((tm, k_dim), lambda i, j: (i, 0)))
        out_shape.append(jax.ShapeDtypeStruct((ROWS, k_dim), F32))

    def body(*refs):
        it = iter(refs)
        row_refs = [(next(it),) if r[0] == "u" else (next(it), next(it)) for r in rows]
        const_refs = [next(it) for _ in consts]
        mod_ref = next(it) if mod is not None else None
        w_ref = next(it)
        erow_refs = [next(it) for _ in erows]
        emod_ref = next(it) if emod is not None else None
        out_ref = next(it)
        lhs_out = next(it) if emit_lhs else None
        hs = next(it)
        i = pl.program_id(0)
        j = pl.program_id(1)

        def fill(use_prompt):
            chosen = [rr[0] if (len(rr) == 1 or use_prompt) else rr[1] for rr in row_refs]
            prologue(_Pro(chosen, const_refs, mod_ref, i, hs, lhs_out))

        if has_dual:
            pl.when(jnp.logical_and(j == 0, i < npt))(lambda: fill(True))
            pl.when(jnp.logical_and(j == 0, i >= npt))(lambda: fill(False))
        else:
            pl.when(j == 0)(lambda: fill(True))

        acc = _dot(hs[...], w_ref[...].astype(BF16))
        if epilogue is not None:
            acc = epilogue(acc, erow_refs, emod_ref)
        out_ref[...] = acc

    res = pl.pallas_call(
        body,
        grid=grid,
        in_specs=in_specs,
        out_specs=out_specs,
        out_shape=out_shape,
        scratch_shapes=[pltpu.VMEM((tm, k_dim), BF16)],
        compiler_params=_cparams(("parallel", "arbitrary")),
        name=name,
    )(*args)
    return res if emit_lhs else res[0]


def _pro_normmod(p):
    x = p.rows[0][...]
    h = _rms(x, p.consts[0][...]) * (1.0 + p.mod[1:2, :]) + p.mod[0:1, :]
    p.emit(h)


def _pro_cast(p):
    p.emit(p.rows[0][...])


def _pro_rms(p):
    p.emit(_rms(p.rows[0][...], p.consts[0][...]))


def _epi_residual(gate_row):
    def epi(acc, erows, emod):
        return erows[0][...] + emod[gate_row:gate_row + 1, :] * acc
    return epi


def _mlp_body(x_ref, g_ref, mod_ref, w1_ref, w2_ref, o_ref, hs, acc):
    f = pl.program_id(1)

    @pl.when(f == 0)
    def _():
        h = _rms(x_ref[...], g_ref[...]) * (1.0 + mod_ref[4:5, :]) + mod_ref[3:4, :]
        hs[...] = h.astype(BF16)
        acc[...] = jnp.zeros_like(acc)

    u = _dot(hs[...], w1_ref[...].astype(BF16))
    u = jnp.square(jnp.maximum(u, 0.0))
    acc[...] += _dot(u.astype(BF16), w2_ref[...].astype(BF16))

    @pl.when(f == pl.num_programs(1) - 1)
    def _():
        o_ref[...] = x_ref[...] + mod_ref[5:6, :] * acc[...]


def _mlp(x, layer, g, mod, w1, w2):
    tm, tf = 1024, 512
    return pl.pallas_call(
        _mlp_body,
        grid=(ROWS // tm, D_FF // tf),
        in_specs=[
            pl.BlockSpec((tm, D), lambda i, f: (i, 0)),
            pl.BlockSpec((None, 1, D), lambda i, f: (layer, 0, 0)),
            pl.BlockSpec((None, None, 6, D), lambda i, f: (layer, _group_of_tile(i, tm), 0, 0)),
            pl.BlockSpec((None, D, tf), lambda i, f: (layer, 0, f)),
            pl.BlockSpec((None, tf, D), lambda i, f: (layer, f, 0)),
        ],
        out_specs=pl.BlockSpec((tm, D), lambda i, f: (i, 0)),
        out_shape=jax.ShapeDtypeStruct((ROWS, D), F32),
        scratch_shapes=[pltpu.VMEM((tm, D), BF16), pltpu.VMEM((tm, D), F32)],
        compiler_params=_cparams(("parallel", "arbitrary")),
        name="mlp",
    )(x, g, mod, w1, w2)


def _final_norm_body(x_ref, g_ref, o_ref):
    o_ref[...] = _rms(x_ref[...], g_ref[...])


def _final_norm(x, g, row0, nrows):
    tm = 1024
    b0 = row0 // tm
    return pl.pallas_call(
        _final_norm_body,
        grid=(nrows // tm,),
        in_specs=[pl.BlockSpec((tm, D), lambda i: (b0 + i, 0)), pl.BlockSpec((1, D), lambda i: (0, 0))],
        out_specs=pl.BlockSpec((tm, D), lambda i: (i, 0)),
        out_shape=jax.ShapeDtypeStruct((nrows, D), F32),
        compiler_params=_cparams(("parallel",)),
        name="final_norm",
    )(x, g)


SCAN_T = 256


def _seg_prefix(v, t_in):
    s = 1
    while s < SCAN_T:
        v = v + jnp.where(t_in >= s, pltpu.roll(v, s, 0), 0.0)
        s *= 2
    return v


def _seg_suffix(v, t_in):
    n = v.shape[0]
    s = 1
    while s < SCAN_T:
        v = v + jnp.where(t_in < SCAN_T - s, pltpu.roll(v, n - s, 0), 0.0)
        s *= 2
    return v


def _ssd_body(*refs, L, has_s0, emit_state):
    it = iter(refs)
    z_ref, x_ref, b_ref, c_ref, dt_ref = (next(it) for _ in range(5))
    cwx, cbx, cwb, cbb, cwc, cbc = (next(it) for _ in range(6))
    dtb_ref, alog_ref, dl_ref = next(it), next(it), next(it)
    s0_ref = next(it) if has_s0 else None
    y_ref = next(it)
    st_out = next(it) if emit_state else None
    padx, padb, xa, ba, ca, cbs, cumc, cumt, xdt, yacc, st = (next(it) for _ in range(11))

    T = SCAN_T
    nc = L // T
    gw = SSD_HPG * SSD_P
    g = pl.program_id(1)

    def conv(in_ref, w_ref, bias_ref, pad):
        width = in_ref.shape[1]
        pad[0:8, :] = jnp.zeros((8, width), F32)
        pad[L + 8:L + 16, :] = jnp.zeros((8, width), F32)
        pad[8:L + 8, :] = in_ref[...]
        acc = bias_ref[...] + pad[6:6 + L, :] * w_ref[0:1, :]
        for k in range(1, CONV_W):
            acc = acc + pad[6 + k:6 + k + L, :] * w_ref[k:k + 1, :]
        return _silu(acc)

    xa[...] = conv(x_ref, cwx, cbx, padx)
    ba[...] = conv(b_ref, cwb, cbb, padb)
    ca[...] = conv(c_ref, cwc, cbc, padb)

    dt_all = _softplus(dt_ref[...] + dtb_ref[...])
    a_all = dt_all * (-jnp.exp(alog_ref[...]))
    shift = jnp.where(g == 0, 0, LANES - SSD_HPG * g)
    dtr = pltpu.roll(dt_all, shift, 1)
    ar = pltpu.roll(a_all, shift, 1)
    t_in = lax.broadcasted_iota(jnp.int32, (L, LANES), 0) % T
    cum_f = _seg_prefix(ar, t_in)
    cum_b = _seg_suffix(ar, t_in)
    cumc[0] = cum_f
    cumc[1] = cum_b
    cumt[0] = cum_f.T
    cumt[1] = cum_b.T

    def expand(v, off):
        n = v.shape[0]
        lane_blk = lax.broadcasted_iota(jnp.int32, (n, gw), 1) // SSD_P
        out = jnp.broadcast_to(v[:, off + 3:off + 4], (n, gw))
        for r in (2, 1, 0):
            out = jnp.where(lane_blk == r, v[:, off + r:off + r + 1], out)
        return out

    xav = xa[...]
    xdt[0] = xav * expand(dtr, 0)
    xdt[1] = xav * expand(dtr, SSD_HEADS)
    yacc[...] = xav * dl_ref[...]

    for c in range(nc):
        rows = slice(T * c, T * (c + 1))
        cbs[c] = _dot_nt(ca[rows, :].astype(BF16), ba[rows, :].astype(BF16))

    if has_s0:
        for d in range(2):
            for r in range(SSD_HPG):
                st[d, :, SSD_P * r:SSD_P * (r + 1)] = s0_ref[d, r].T
    else:
        st[...] = jnp.zeros_like(st)

    ii = lax.broadcasted_iota(jnp.int32, (T, T), 0)
    jj = lax.broadcasted_iota(jnp.int32, (T, T), 1)

    def chunk(c, d):
        rows = slice(T * c, T * (c + 1))
        off = SSD_HEADS * d
        first = c == (0 if d == 0 else nc - 1)
        last = c == (nc - 1 if d == 0 else 0)
        zero_state = first and not has_s0
        mask = (ii >= jj) if d == 0 else (ii <= jj)
        cab = ca[rows, :].astype(BF16)
        for r in range(SSD_HPG):
            lane = off + r
            hs = slice(SSD_P * r, SSD_P * (r + 1))
            cc = cumc[d, rows, lane:lane + 1]
            cr = cumt[d, lane:lane + 1, rows]
            dec = jnp.where(mask, jnp.exp(cc - cr), 0.0)
            y = _dot((cbs[c] * dec).astype(BF16), xdt[d, rows, hs].astype(BF16))
            if not zero_state:
                y = y + _dot(cab, st[d, :, hs].astype(BF16)) * jnp.exp(cc)
            yacc[rows, hs] += y
        if emit_state or not last:
            end = T * c + (T - 1 if d == 0 else 0)
            tot = cumc[d, end:end + 1, :]
            wx = xdt[d, rows, :] * jnp.exp(expand(tot - cumc[d, rows, :], off))
            upd = _dot_tn(ba[rows, :].astype(BF16), wx.astype(BF16))
            st[d] = upd if zero_state else jnp.exp(expand(tot, off)) * st[d] + upd

    for c in range(nc):
        chunk(c, 0)
        chunk(nc - 1 - c, 1)

    y_ref[...] = yacc[...] * _silu(z_ref[...])
    if emit_state:
        for d in range(2):
            for r in range(SSD_HPG):
                st_out[d, r] = st[d, :, SSD_P * r:SSD_P * (r + 1)].T


def _ssd_scan(p0, pdt, conv_w, conv_b, dtb, alog, dl, s0, *, nb, L, emit_state):
    rb0 = 0 if s0 is None else ROWS_P // L
    gw = SSD_HPG * SSD_P
    nc = L // SCAN_T
    x0 = SSD_DI // gw
    b0 = 2 * SSD_DI // SSD_N
    c0 = b0 + SSD_GROUPS
    wb0 = SSD_DI // SSD_N
    wc0 = wb0 + SSD_GROUPS
    in_specs = [
        pl.BlockSpec((L, gw), lambda b, g: (rb0 + b, g)),
        pl.BlockSpec((L, gw), lambda b, g: (rb0 + b, x0 + g)),
        pl.BlockSpec((L, SSD_N), lambda b, g: (rb0 + b, b0 + g)),
        pl.BlockSpec((L, SSD_N), lambda b, g: (rb0 + b, c0 + g)),
        pl.BlockSpec((L, LANES), lambda b, g: (rb0 + b, 0)),
        pl.BlockSpec((CONV_W, gw), lambda b, g: (0, g)),
        pl.BlockSpec((1, gw), lambda b, g: (0, g)),
        pl.BlockSpec((CONV_W, SSD_N), lambda b, g: (0, wb0 + g)),
        pl.BlockSpec((1, SSD_N), lambda b, g: (0, wb0 + g)),
        pl.BlockSpec((CONV_W, SSD_N), lambda b, g: (0, wc0 + g)),
        pl.BlockSpec((1, SSD_N), lambda b, g: (0, wc0 + g)),
        pl.BlockSpec((1, LANES), lambda b, g: (0, 0)),
        pl.BlockSpec((1, LANES), lambda b, g: (0, 0)),
        pl.BlockSpec((1, gw), lambda b, g: (0, g)),
    ]
    args = [p0, p0, p0, p0, pdt, conv_w, conv_b, conv_w, conv_b, conv_w, conv_b, dtb, alog, dl]
    state_spec = pl.BlockSpec((None, None, 2, SSD_HPG, SSD_P, SSD_N), lambda b, g: (b, 0, 0, g, 0, 0))
    if s0 is not None:
        in_specs.append(state_spec)
        args.append(s0)
    out_specs = [pl.BlockSpec((L, gw), lambda b, g: (b, g))]
    out_shape = [jax.ShapeDtypeStruct((nb * L, SSD_DI), F32)]
    if emit_state:
        out_specs.append(state_spec)
        out_shape.append(jax.ShapeDtypeStruct((nb, 1, 2, SSD_HEADS, SSD_P, SSD_N), F32))
    scratch = [
        pltpu.VMEM((L + 16, gw), F32),
        pltpu.VMEM((L + 16, SSD_N), F32),
        pltpu.VMEM((L, gw), F32),
        pltpu.VMEM((L, SSD_N), F32),
        pltpu.VMEM((L, SSD_N), F32),
        pltpu.VMEM((nc, SCAN_T, SCAN_T), F32),
        pltpu.VMEM((2, L, LANES), F32),
        pltpu.VMEM((2, LANES, L), F32),
        pltpu.VMEM((2, L, gw), F32),
        pltpu.VMEM((L, gw), F32),
        pltpu.VMEM((2, SSD_N, gw), F32),
    ]
    res = pl.pallas_call(
        functools.partial(_ssd_body, L=L, has_s0=s0 is not None, emit_state=emit_state),
        grid=(nb, SSD_GROUPS),
        in_specs=in_specs,
        out_specs=out_specs,
        out_shape=out_shape,
        scratch_shapes=scratch,
        compiler_params=_cparams(("parallel", "parallel")),
        name="ssd_scan_p" if s0 is None else "ssd_scan_s",
    )(*args)
    return res


TQ = 256


def _mla_body(*refs, L, sample):
    it = iter(refs)
    cq_ref, ckv_ref, krs_ref = next(it), next(it), next(it)
    gq_ref, gkv_ref = next(it), next(it)
    wq, wqs, wk, wv = (next(it) for _ in range(4))
    if sample:
        cckv_ref, ckr_ref, cq_t, sq_t, ck_t, sk_t = (next(it) for _ in range(6))
    o_ref = next(it)
    if not sample:
        ckv_out, kr_out = next(it), next(it)
    kk, vv = next(it), next(it)
    qt = pl.program_id(1)
    scale = (MLA_NOPE + MLA_ROPE) ** -0.5
    hb = 4 * LANES

    @pl.when(qt == 0)
    def _():
        ckv = _rms(ckv_ref[...], gkv_ref[...])
        kr_own = krs_ref[:, 0:LANES]
        if sample:
            keys = jnp.concatenate([cckv_ref[...], ckv], axis=0)
            kr_own = kr_own * ck_t[...] + krs_ref[:, LANES:2 * LANES] * sk_t[...]
            kr_all = jnp.concatenate([ckr_ref[...], kr_own], axis=0)
        else:
            ckv_out[...] = ckv
            kr_out[...] = kr_own[:, MLA_NOPE:MLA_NOPE + MLA_ROPE]
            keys = ckv
            kr_all = kr_own
        kb = keys.astype(BF16)
        for blk in range(MLA_HEADS * LANES // hb):
            cols = slice(hb * blk, hb * (blk + 1))
            kn = _dot(kb, wk[:, cols])
            kk[:, cols] = (kn + jnp.concatenate([kr_all] * 4, axis=1)).astype(BF16)
            vv[:, cols] = _dot(kb, wv[:, cols]).astype(BF16)

    cq = _rms(cq_ref[...], gq_ref[...]).astype(BF16)
    for blk in range(MLA_HEADS * LANES // hb):
        qa = _dot(cq, wq[:, hb * blk:hb * (blk + 1)])
        if sample:
            qs = _dot(cq, wqs[:, hb * blk:hb * (blk + 1)])
        pair = None
        for hh in range(4):
            h = 4 * blk + hh
            cols = slice(LANES * h, LANES * (h + 1))
            qh = qa[:, LANES * hh:LANES * (hh + 1)]
            if sample:
                qh = qh * cq_t[...] + qs[:, LANES * hh:LANES * (hh + 1)] * sq_t[...]
            s = _dot_nt(qh.astype(BF16), kk[:, cols]) * scale
            e = jnp.exp(s - jnp.max(s, axis=1, keepdims=True))
            o = _dot(e.astype(BF16), vv[:, cols]) / jnp.sum(e, axis=1, keepdims=True)
            if h % 2 == 0:
                pair = o
            else:
                o_ref[:, LANES * (h // 2):LANES * (h // 2 + 1)] = pair + o


def _mla_attn(p1, p1s, gq, gkv, wq, wqs, wk, wv, ctx, *, nb, L):
    sample = ctx is not None
    nq = L // TQ
    tk = L + (PAST if sample else 0)
    rbq0 = ROWS_P // TQ if sample else 0
    rbs0 = ROWS_P // L if sample else 0
    ckv_blk = MLA_Q_RANK // MLA_KV_RANK

    def const(a):
        return pl.BlockSpec(a.shape, lambda b, q, nd=a.ndim: (0,) * nd)

    in_specs = [
        pl.BlockSpec((TQ, MLA_Q_RANK), lambda b, q: (rbq0 + b * nq + q, 0)),
        pl.BlockSpec((L, MLA_KV_RANK), lambda b, q: (rbs0 + b, ckv_blk)),
        pl.BlockSpec((L, 2 * LANES), lambda b, q: (rbs0 + b, 0)),
        const(gq), const(gkv), const(wq), const(wqs), const(wk), const(wv),
    ]
    args = [p1, p1, p1s, gq, gkv, wq, wqs, wk, wv]
    if sample:
        cckv, ckr, cpad, spad = ctx
        in_specs += [
            pl.BlockSpec((None, None, PAST, MLA_KV_RANK), lambda b, q: (b, 0, 0, 0)),
            pl.BlockSpec((None, None, PAST, LANES), lambda b, q: (b, 0, 0, 0)),
            pl.BlockSpec((TQ, LANES), lambda b, q: (q, 0)),
            pl.BlockSpec((TQ, LANES), lambda b, q: (q, 0)),
            const(cpad), const(spad),
        ]
        args += [cckv, ckr, cpad, spad, cpad, spad]
    out_specs = [pl.BlockSpec((TQ, MLA_HEADS * MLA_V), lambda b, q: (b * nq + q, 0))]
    out_shape = [jax.ShapeDtypeStruct((nb * L, MLA_HEADS * MLA_V), F32)]
    if not sample:
        out_specs += [
            pl.BlockSpec((None, None, L, MLA_KV_RANK), lambda b, q: (b, 0, 0, 0)),
            pl.BlockSpec((None, None, L, MLA_ROPE), lambda b, q: (b, 0, 0, 0)),
        ]
        out_shape += [
            jax.ShapeDtypeStruct((nb, 1, L, MLA_KV_RANK), F32),
            jax.ShapeDtypeStruct((nb, 1, L, MLA_ROPE), F32),
        ]
    scratch = [
        pltpu.VMEM((tk, MLA_HEADS * LANES), BF16),
        pltpu.VMEM((tk, MLA_HEADS * LANES), BF16),
    ]
    return pl.pallas_call(
        functools.partial(_mla_body, L=L, sample=sample),
        grid=(nb, nq),
        in_specs=in_specs,
        out_specs=out_specs,
        out_shape=out_shape,
        scratch_shapes=scratch,
        compiler_params=_cparams(("parallel", "arbitrary")),
        name="mla_attn_s" if sample else "mla_attn_p",
    )(*args)


def _log_sigmoid(x):
    return -_softplus(-x)


def _mlstm_body(*refs, L, has_s0, emit_state):
    it = iter(refs)
    q_ref, k_ref, v_ref, g_ref, gb_ref, ng_ref = (next(it) for _ in range(6))
    if has_s0:
        c0_ref, n0_ref, m0_ref = next(it), next(it), next(it)
    h_ref = next(it)
    if emit_state:
        c_out, n_out, m_out = next(it), next(it), next(it)
    gsc, gtr, qk, hacc, cst, nst, mst = (next(it) for _ in range(7))

    T = SCAN_T
    nc = L // T
    hd = pl.program_id(1)

    gts = g_ref[...] + gb_ref[...]
    gr = pltpu.roll(gts, jnp.where(hd == 0, 0, LANES - hd), 1)
    lf = _log_sigmoid(gr)
    t_in = lax.broadcasted_iota(jnp.int32, (L, LANES), 0) % T
    b_f = _seg_prefix(lf, t_in)
    b_b = _seg_suffix(lf, t_in)
    gsc[0] = gr
    gsc[1] = b_f
    gsc[2] = b_b
    gtr[0] = gr.T
    gtr[1] = b_f.T
    gtr[2] = b_b.T

    if has_s0:
        for d in range(2):
            cst[d] = c0_ref[d]
            nst[d] = n0_ref[d, pl.ds(hd, 1), :]
            mst[d] = m0_ref[d, pl.ds(hd, 1), :]
    else:
        cst[...] = jnp.zeros_like(cst)
        nst[...] = jnp.zeros_like(nst)
        mst[...] = jnp.zeros_like(mst)
    hacc[...] = jnp.zeros_like(hacc)

    for c in range(nc):
        rows = slice(T * c, T * (c + 1))
        qk[c] = _dot_nt(q_ref[rows, :].astype(BF16), k_ref[rows, :].astype(BF16))

    ii = lax.broadcasted_iota(jnp.int32, (T, T), 0)
    jj = lax.broadcasted_iota(jnp.int32, (T, T), 1)

    def chunk(c, d):
        rows = slice(T * c, T * (c + 1))
        first = c == (0 if d == 0 else nc - 1)
        last = c == (nc - 1 if d == 0 else 0)
        zero_state = first and not has_s0
        li = 2 * ML_HEADS * d
        lb = li + ML_HEADS
        logi = gsc[0, rows, li:li + 1]
        bc = gsc[1 + d, rows, lb:lb + 1]
        ir = gtr[0, li:li + 1, rows]
        br = gtr[1 + d, lb:lb + 1, rows]
        mask = (ii >= jj) if d == 0 else (ii <= jj)
        dlog = jnp.where(mask, bc - br + ir, -jnp.inf)
        m_prev = mst[d][:, 0:1]
        inter = bc + m_prev
        mcomb = jnp.maximum(inter, jnp.max(dlog, axis=1, keepdims=True))
        s = qk[c] * jnp.exp(dlog - mcomb)
        vb = v_ref[rows, :].astype(BF16)
        num = _dot(s.astype(BF16), vb)
        den = jnp.sum(s, axis=1, keepdims=True)
        if not zero_state:
            iw = jnp.exp(inter - mcomb)
            qc = q_ref[rows, :]
            num = num + iw * _dot(qc.astype(BF16), cst[d].astype(BF16))
            den = den + iw * jnp.sum(qc * nst[d], axis=1, keepdims=True)
        hacc[rows, :] += num / jnp.maximum(jnp.abs(den), jnp.exp(-mcomb))
        if emit_state or not last:
            end = T * c + (T - 1 if d == 0 else 0)
            bq = gsc[1 + d, end:end + 1, lb:lb + 1]
            wlog = bq - bc + logi
            m_new = jnp.maximum(bq + m_prev, jnp.max(wlog, axis=0, keepdims=True))
            kw = k_ref[rows, :] * jnp.exp(wlog - m_new)
            upd = _dot_tn(kw.astype(BF16), vb)
            nsum = jnp.sum(kw, axis=0, keepdims=True)
            if zero_state:
                cst[d] = upd
                nst[d] = nsum
            else:
                cw = jnp.exp(bq + m_prev - m_new)
                cst[d] = cw * cst[d] + upd
                nst[d] = cw * nst[d] + nsum
            mst[d] = jnp.broadcast_to(m_new, (1, LANES))

    for c in range(nc):
        chunk(c, 0)
        chunk(nc - 1 - c, 1)

    h_ref[...] = _rms(hacc[...], ng_ref[...])
    if emit_state:
        for d in range(2):
            c_out[d] = cst[d]
            n_out[d, pl.ds(hd, 1), :] = nst[d]
            m_out[d, pl.ds(hd, 1), :] = mst[d]


def _mlstm_scan(q, k, v, gates, gb, ng, s0, *, nb, L, emit_state):
    rb0 = 0 if s0 is None else ROWS_P // L
    in_specs = [
        pl.BlockSpec((L, ML_DQK), lambda b, h: (rb0 + b, h)),
        pl.BlockSpec((L, ML_DQK), lambda b, h: (rb0 + b, h)),
        pl.BlockSpec((L, ML_DV), lambda b, h: (rb0 + b, h)),
        pl.BlockSpec((L, LANES), lambda b, h: (rb0 + b, 0)),
        pl.BlockSpec((1, LANES), lambda b, h: (0, 0)),
        pl.BlockSpec((1, ML_DV), lambda b, h: (0, h)),
    ]
    args = [q, k, v, gates, gb, ng]
    c_spec = pl.BlockSpec((None, None, 2, None, ML_DQK, ML_DV), lambda b, h: (b, 0, 0, h, 0, 0))
    n_spec = pl.BlockSpec((None, 2, ML_HEADS, LANES), lambda b, h: (b, 0, 0, 0))
    if s0 is not None:
        in_specs += [c_spec, n_spec, n_spec]
        args += list(s0)
    out_specs = [pl.BlockSpec((L, ML_DV), lambda b, h: (b, h))]
    out_shape = [jax.ShapeDtypeStruct((nb * L, ML_DI), F32)]
    if emit_state:
        out_specs += [c_spec, n_spec, n_spec]
        out_shape += [
            jax.ShapeDtypeStruct((nb, 1, 2, ML_HEADS, ML_DQK, ML_DV), F32),
            jax.ShapeDtypeStruct((nb, 2, ML_HEADS, LANES), F32),
            jax.ShapeDtypeStruct((nb, 2, ML_HEADS, LANES), F32),
        ]
    scratch = [
        pltpu.VMEM((3, L, LANES), F32),
        pltpu.VMEM((3, LANES, L), F32),
        pltpu.VMEM((L // SCAN_T, SCAN_T, SCAN_T), F32),
        pltpu.VMEM((L, ML_DV), F32),
        pltpu.VMEM((2, ML_DQK, ML_DV), F32),
        pltpu.VMEM((2, 1, ML_DQK), F32),
        pltpu.VMEM((2, 1, LANES), F32),
    ]
    return pl.pallas_call(
        functools.partial(_mlstm_body, L=L, has_s0=s0 is not None, emit_state=emit_state),
        grid=(nb, ML_HEADS),
        in_specs=in_specs,
        out_specs=out_specs,
        out_shape=out_shape,
        scratch_shapes=scratch,
        compiler_params=_cparams(("parallel", "arbitrary")),
        name="mlstm_scan_p" if s0 is None else "mlstm_scan_s",
    )(*args)


def _rope_blocks(x, ct, st):
    lane = lax.broadcasted_iota(jnp.int32, (x.shape[0], LANES), 1)
    first_half = lane % DF_D < DF_D // 2
    outs = []
    for blk in range(x.shape[1] // LANES):
        xb = x[:, LANES * blk:LANES * (blk + 1)]
        swapped = jnp.where(first_half, pltpu.roll(xb, LANES - DF_D // 2, 1), pltpu.roll(xb, DF_D // 2, 1))
        outs.append(xb * ct + swapped * st)
    return jnp.concatenate(outs, axis=1)


def _diff_body(*refs, L, sample, lam_init):
    it = iter(refs)
    q_ref, k_ref, v_ref = next(it), next(it), next(it)
    lq1, lk1, lq2, lk2, sg_ref = (next(it) for _ in range(5))
    if sample:
        ck_ref, cv_ref, cq_t, sq_t, ck_t, sk_t = (next(it) for _ in range(6))
    o_ref = next(it)
    if sample:
        ka, va = next(it), next(it)
    else:
        k_out, v_out = next(it), next(it)
    qt = pl.program_id(1)

    lam = (jnp.exp(jnp.sum(lq1[...] * lk1[...], axis=1, keepdims=True))
           - jnp.exp(jnp.sum(lq2[...] * lk2[...], axis=1, keepdims=True)) + lam_init)

    @pl.when(qt == 0)
    def _():
        if sample:
            ka[0:PAST, :] = ck_ref[...].astype(BF16)
            ka[PAST:PAST + L, :] = _rope_blocks(k_ref[...], ck_t[...], sk_t[...]).astype(BF16)
            va[0:PAST, :] = cv_ref[...].astype(BF16)
            va[PAST:PAST + L, :] = v_ref[...].astype(BF16)
        else:
            k_out[...] = k_ref[...]
            v_out[...] = v_ref[...]

    q = q_ref[...]
    if sample:
        q = _rope_blocks(q, cq_t[...], sq_t[...])
    q = q * (DF_D ** -0.5)
    lo = lax.broadcasted_iota(jnp.int32, (TQ, LANES), 1) < DF_D

    def attend(qm, kh, vh):
        s = _dot_nt(qm, kh)
        e = jnp.exp(s - jnp.max(s, axis=1, keepdims=True))
        return _dot(e.astype(BF16), vh) / jnp.sum(e, axis=1, keepdims=True)

    for h in range(DF_HEADS):
        cols = slice(LANES * h, LANES * (h + 1))
        qh = q[:, cols]
        if sample:
            kh, vh = ka[:, cols], va[:, cols]
        else:
            kh, vh = k_ref[:, cols].astype(BF16), v_ref[:, cols].astype(BF16)
        a0 = attend(jnp.where(lo, qh, 0.0).astype(BF16), kh, vh)
        a1 = attend(jnp.where(lo, 0.0, qh).astype(BF16), kh, vh)
        o = a0 - lam * a1
        o_ref[:, cols] = _rms(o, sg_ref[...]) * (1.0 - lam_init)


def _diff_attn(p3, lq1, lk1, lq2, lk2, sg, ctx, *, nb, L, lam_init):
    sample = ctx is not None
    nq = L // TQ
    rbq0 = ROWS_P // TQ if sample else 0
    rbs0 = ROWS_P // L if sample else 0

    def const(a):
        return pl.BlockSpec(a.shape, lambda b, q, nd=a.ndim: (0,) * nd)

    in_specs = [
        pl.BlockSpec((TQ, D), lambda b, q: (rbq0 + b * nq + q, 0)),
        pl.BlockSpec((L, D), lambda b, q: (rbs0 + b, 1)),
        pl.BlockSpec((L, D), lambda b, q: (rbs0 + b, 2)),
        const(lq1), const(lk1), const(lq2), const(lk2), const(sg),
    ]
    args = [p3, p3, p3, lq1, lk1, lq2, lk2, sg]
    kv_spec = pl.BlockSpec((None, None, PAST if sample else L, D), lambda b, q: (b, 0, 0, 0))
    if sample:
        ck, cv, c128, s128 = ctx
        in_specs += [
            kv_spec, kv_spec,
            pl.BlockSpec((TQ, LANES), lambda b, q: (q, 0)),
            pl.BlockSpec((TQ, LANES), lambda b, q: (q, 0)),
            const(c128), const(s128),
        ]
        args += [ck, cv, c128, s128, c128, s128]
    out_specs = [pl.BlockSpec((TQ, D), lambda b, q: (b * nq + q, 0))]
    out_shape = [jax.ShapeDtypeStruct((nb * L, D), F32)]
    scratch = []
    if sample:
        scratch = [pltpu.VMEM((PAST + L, D), BF16), pltpu.VMEM((PAST + L, D), BF16)]
    else:
        out_specs += [kv_spec, kv_spec]
        out_shape += [jax.ShapeDtypeStruct((nb, 1, L, D), F32)] * 2
    return pl.pallas_call(
        functools.partial(_diff_body, L=L, sample=sample, lam_init=lam_init),
        grid=(nb, nq),
        in_specs=in_specs,
        out_specs=out_specs,
        out_shape=out_shape,
        scratch_shapes=scratch,
        compiler_params=_cparams(("parallel", "arbitrary")),
        name="diff_attn_s" if sample else "diff_attn_p",
    )(*args)


def _pro_mlstm_conv(p):
    x_ref = p.rows[0]
    w_ref, b_ref = p.consts
    tm = x_ref.shape[0]
    seq = jnp.where(p.i < ROWS_P // tm, L_PROMPT, L_SAMPLE)
    cw = 256
    pos = lax.broadcasted_iota(jnp.int32, (tm, cw), 0) & (seq - 1)
    for cb in range(ML_DI // cw):
        cols = slice(cw * cb, cw * (cb + 1))
        x = x_ref[:, cols]
        acc = None
        for k in range(CONV_W):
            off = k - CONV_W // 2
            src = pos + off
            tap = x if off == 0 else pltpu.roll(x, (-off) % tm, 0)
            tap = jnp.where(jnp.logical_and(src >= 0, src < seq), tap, 0.0) * w_ref[k:k + 1, cols]
            acc = b_ref[:, cols] + tap if acc is None else acc + tap
        p.emit(_silu(acc), cols)


def _pro_mlstm_gate(p):
    hn, xc, z = p.rows[0][...], p.rows[1][...], p.rows[2][...]
    p.emit((hn + p.consts[0][...] * xc) * _silu(z))


def _rope_tables(d):
    rows = L_SAMPLE // GRID_W
    pos_r = jnp.repeat(jnp.arange(rows, dtype=F32), GRID_W)
    pos_c = jnp.tile(jnp.arange(GRID_W, dtype=F32), rows)
    nf = d // 4
    inv = ROPE_BASE ** (-jnp.arange(nf, dtype=F32) / nf)
    ang = jnp.concatenate([pos_r[:, None] * inv, pos_c[:, None] * inv], axis=-1)
    cos, sin = jnp.cos(ang), jnp.sin(ang)
    return jnp.concatenate([cos, cos], axis=-1), jnp.concatenate([-sin, sin], axis=-1)


def _pad_cols(a, n):
    return jnp.pad(a, ((0, 0), (0, n - a.shape[1])))


def kernel(x_prompt, x_sample, state_ssd, cache_mla_ckv, cache_mla_krope, state_mlstm_C, state_mlstm_n, state_mlstm_m, cache_diff_k, cache_diff_v, c, c_ctx, norm1_g, norm2_g, ada_w, ada_b, mlp_w1, mlp_w2, final_g, ssd_w_in, ssd_conv_w, ssd_conv_b, ssd_dt_bias, ssd_A_log, ssd_D, ssd_norm_g, ssd_w_out, mla_w_in, mla_q_norm_g, mla_kv_norm_g, mla_w_uq, mla_w_ukv, mla_w_o, mlstm_w_up, mlstm_conv_w, mlstm_conv_b, mlstm_gate_b, mlstm_w_q, mlstm_w_k, mlstm_w_v, mlstm_skip, mlstm_norm_g, mlstm_w_down, diff_w_qkv, diff_lq1, diff_lk1, diff_lq2, diff_lk2, diff_subln_g, diff_w_o):
    x = jnp.concatenate([x_prompt.reshape(ROWS_P, D), x_sample.reshape(ROWS_S, D)], axis=0)
    cvec = jnp.concatenate([c_ctx[None, :], c, jnp.zeros((5, D), F32)], axis=0)
    mod_all = _ada_mod(cvec, ada_w, ada_b)
    g2 = norm2_g.reshape(DEPTH, 1, D)

    def in_proj(xin, layer, w, n_cols, tn, w_col0=0, name="in_proj"):
        return _fused_mm(rows=[("u", xin, D, 0)], consts=[norm1_g[layer][None, :]], mod=(mod_all, layer), w=w, k_dim=D,
                         n_cols=n_cols, tm=1024, tn=tn, prologue=_pro_normmod, w_col0=w_col0, name=name)

    def out_proj(xin, layer, rows, consts, prologue, w, k_dim, name):
        return _fused_mm(rows=rows, consts=consts, w=w, k_dim=k_dim, n_cols=D, tm=512, tn=512, prologue=prologue,
                         epilogue=_epi_residual(2), erows=[xin], emod=(mod_all, layer), name=name)

    p0 = in_proj(x, 0, ssd_w_in[0], 3 * SSD_DI, 512, name="ssd_in")
    pdt = in_proj(x, 0, _pad_cols(ssd_w_in[0][:, 3 * SSD_DI:], LANES), LANES, LANES, name="ssd_in_dt")
    dtb = _pad_cols(ssd_dt_bias[0].reshape(1, 2 * SSD_HEADS), LANES)
    alog = _pad_cols(ssd_A_log[0].reshape(1, 2 * SSD_HEADS), LANES)
    dl = jnp.repeat(ssd_D[0], SSD_P)[None, :]
    scan_args = (p0, pdt, ssd_conv_w[0], ssd_conv_b[0][None, :], dtb, alog, dl)
    yg_p, new_ssd = _ssd_scan(*scan_args, None, nb=N_PROMPT_SEQ, L=L_PROMPT, emit_state=True)
    (yg_s,) = _ssd_scan(*scan_args, state_ssd, nb=N_SAMPLE_SEQ, L=L_SAMPLE, emit_state=False)
    x = out_proj(x, 0, [("d", yg_p, yg_s, SSD_DI, 0)], [ssd_norm_g[0][None, :]], _pro_rms, ssd_w_out[0], SSD_DI,
                 "ssd_out")
    x = _mlp(x, 0, g2, mod_all, mlp_w1, mlp_w2)

    w_in = mla_w_in[0]
    kr0 = MLA_Q_RANK + MLA_KV_RANK
    half = MLA_ROPE // 2
    zk = jnp.zeros((D, MLA_NOPE), F32)
    zr = jnp.zeros((D, LANES - MLA_NOPE - MLA_ROPE), F32)
    w_kr = jnp.concatenate([zk, w_in[:, kr0:kr0 + MLA_ROPE], zr,
                            zk, w_in[:, kr0 + half:kr0 + MLA_ROPE], w_in[:, kr0:kr0 + half], zr], axis=1)
    p1 = in_proj(x, 1, w_in, kr0, 256, name="mla_in")
    p1s = in_proj(x, 1, w_kr, 2 * LANES, 2 * LANES, name="mla_in_kr")
    wuq = mla_w_uq[0].reshape(MLA_Q_RANK, MLA_HEADS, MLA_NOPE + MLA_ROPE)
    zq = jnp.zeros((MLA_Q_RANK, MLA_HEADS, LANES - MLA_NOPE - MLA_ROPE), F32)
    zqn = jnp.zeros((MLA_Q_RANK, MLA_HEADS, MLA_NOPE), F32)
    wq = jnp.concatenate([wuq, zq], axis=-1).reshape(MLA_Q_RANK, MLA_HEADS * LANES).astype(BF16)
    wqs = jnp.concatenate([zqn, wuq[..., MLA_NOPE + half:], wuq[..., MLA_NOPE:MLA_NOPE + half], zq],
                          axis=-1).reshape(MLA_Q_RANK, MLA_HEADS * LANES).astype(BF16)
    wukv = mla_w_ukv[0].reshape(MLA_KV_RANK, MLA_HEADS, MLA_NOPE + MLA_V)
    zkv = jnp.zeros((MLA_KV_RANK, MLA_HEADS, MLA_NOPE), F32)
    wk = jnp.concatenate([wukv[..., :MLA_NOPE], zkv], axis=-1).reshape(MLA_KV_RANK, MLA_HEADS * LANES).astype(BF16)
    wv_own = wukv[..., MLA_NOPE:]
    odd = (jnp.arange(MLA_HEADS) % 2 == 1)[None, :, None]
    wv = jnp.where(odd, jnp.concatenate([zkv, wv_own], axis=-1), jnp.concatenate([wv_own, zkv], axis=-1))
    wv = wv.reshape(MLA_KV_RANK, MLA_HEADS * LANES).astype(BF16)
    c32, s32 = _rope_tables(MLA_ROPE)
    tz = jnp.zeros((L_SAMPLE, LANES - MLA_NOPE - MLA_ROPE), F32)
    cpad = jnp.concatenate([jnp.ones((L_SAMPLE, MLA_NOPE), F32), c32, tz], axis=1)
    spad = jnp.concatenate([jnp.zeros((L_SAMPLE, MLA_NOPE), F32), s32, tz], axis=1)
    ckr_pad = jnp.pad(cache_mla_krope, ((0, 0), (0, 0), (0, 0), (MLA_NOPE, LANES - MLA_NOPE - MLA_ROPE)))
    mla_w = (mla_q_norm_g[0][None, :], mla_kv_norm_g[0][None, :], wq, wqs, wk, wv)
    o_p, new_ckv, new_kr = _mla_attn(p1, p1s, *mla_w, None, nb=N_PROMPT_SEQ, L=L_PROMPT)
    (o_s,) = _mla_attn(p1, p1s, *mla_w, (cache_mla_ckv, ckr_pad, cpad, spad), nb=N_SAMPLE_SEQ, L=L_SAMPLE)
    x = out_proj(x, 1, [("d", o_p, o_s, D, 0)], [], _pro_cast, mla_w_o[0], D, "mla_out")
    x = _mlp(x, 1, g2, mod_all, mlp_w1, mlp_w2)

    p2 = in_proj(x, 2, mlstm_w_up[0], 2 * ML_DI, 512, name="mlstm_up")
    gates = in_proj(x, 2, _pad_cols(mlstm_w_up[0][:, 2 * ML_DI:], LANES), LANES, LANES, name="mlstm_up_gates")
    q, xc = _fused_mm(rows=[("u", p2, ML_DI, 0)], consts=[mlstm_conv_w[0], mlstm_conv_b[0][None, :]], w=mlstm_w_q[0],
                      k_dim=ML_DI, n_cols=ML_HEADS * ML_DQK, tm=1024, tn=256, prologue=_pro_mlstm_conv, emit_lhs=True,
                      name="mlstm_q")
    k = _fused_mm(rows=[("u", xc, ML_DI, 0)], w=mlstm_w_k[0], k_dim=ML_DI, n_cols=ML_HEADS * ML_DQK, tm=1024, tn=512,
                  prologue=_pro_cast, epilogue=lambda acc, e, m: acc * (ML_DQK ** -0.5), name="mlstm_k")
    v = _fused_mm(rows=[("u", p2, ML_DI, 0)], w=mlstm_w_v[0], k_dim=ML_DI, n_cols=ML_DI, tm=1024, tn=512,
                  prologue=_pro_cast, name="mlstm_v")
    gb = _pad_cols(mlstm_gate_b[0].reshape(1, 4 * ML_HEADS), LANES)
    ng = mlstm_norm_g[0][None, :]
    n0 = _pad_cols(state_mlstm_n[:, 0].reshape(N_SAMPLE_SEQ * 2 * ML_HEADS, ML_DQK), LANES).reshape(
        N_SAMPLE_SEQ, 2, ML_HEADS, LANES)
    m0 = jnp.broadcast_to(state_mlstm_m[:, 0][..., None], (N_SAMPLE_SEQ, 2, ML_HEADS, LANES))
    hn_p, new_c, new_n, new_m = _mlstm_scan(q, k, v, gates, gb, ng, None, nb=N_PROMPT_SEQ, L=L_PROMPT, emit_state=True)
    (hn_s,) = _mlstm_scan(q, k, v, gates, gb, ng, (state_mlstm_C, n0, m0), nb=N_SAMPLE_SEQ, L=L_SAMPLE,
                          emit_state=False)
    x = out_proj(x, 2, [("d", hn_p, hn_s, ML_DI, 0), ("u", xc, ML_DI, 0), ("u", p2, ML_DI, 1)],
                 [mlstm_skip[0][None, :]], _pro_mlstm_gate, mlstm_w_down[0], ML_DI, "mlstm_down")
    x = _mlp(x, 2, g2, mod_all, mlp_w1, mlp_w2)

    lam_init = 0.8 - 0.6 * math.exp(-0.3 * 3)
    p3 = in_proj(x, 3, diff_w_qkv[0], 3 * D, 512, name="diff_qkv")
    c64, s64 = _rope_tables(DF_D)
    c128 = jnp.concatenate([c64, c64], axis=-1)
    s128 = jnp.concatenate([s64, s64], axis=-1)
    dparams = (diff_lq1, diff_lk1, diff_lq2, diff_lk2, diff_subln_g)
    od_p, new_dk, new_dv = _diff_attn(p3, *dparams, None, nb=N_PROMPT_SEQ, L=L_PROMPT, lam_init=lam_init)
    ctx = (cache_diff_k.reshape(N_SAMPLE_SEQ, 1, PAST, D), cache_diff_v.reshape(N_SAMPLE_SEQ, 1, PAST, D), c128, s128)
    (od_s,) = _diff_attn(p3, *dparams, ctx, nb=N_SAMPLE_SEQ, L=L_SAMPLE, lam_init=lam_init)
    x = out_proj(x, 3, [("d", od_p, od_s, D, 0)], [], _pro_cast, diff_w_o[0], D, "diff_out")
    x = _mlp(x, 3, g2, mod_all, mlp_w1, mlp_w2)

    fg = final_g[None, :]
    y_prompt = _final_norm(x, fg, 0, ROWS_P).reshape(N_PROMPT_SEQ, L_PROMPT, D)
    y_sample = _final_norm(x, fg, ROWS_P, ROWS_S).reshape(N_SAMPLE_SEQ, L_SAMPLE, D)
    return (y_prompt, y_sample, new_ssd, new_ckv, new_kr, new_c,
            new_n[None].reshape(N_PROMPT_SEQ, 1, 2, ML_HEADS, ML_DQK),
            new_m[..., 0].reshape(N_PROMPT_SEQ, 1, 2, ML_HEADS),
            new_dk.reshape(N_PROMPT_SEQ, 1, L_PROMPT, DF_HEADS, 2 * DF_D),
            new_dv.reshape(N_PROMPT_SEQ, 1, L_PROMPT, DF_HEADS, 2 * DF_D))
```

```python
import functools
import math

import jax
import jax.numpy as jnp
from jax import lax
from jax.experimental import pallas as pl
from jax.experimental.pallas import tpu as pltpu

F32 = jnp.float32
BF16 = jnp.bfloat16

D = 1024
DEPTH = 4
D_FF = 4 * D
EPS = 1e-6
ROPE_BASE = 10000.0
CONV_W = 5
GRID_W = 64

N_PROMPT_SEQ = 32
L_PROMPT = 256
N_SAMPLE_SEQ = 2
L_SAMPLE = 1024
PAST = 256
ROWS_P = N_PROMPT_SEQ * L_PROMPT
ROWS_S = N_SAMPLE_SEQ * L_SAMPLE
ROWS = ROWS_P + ROWS_S

SSD_DI = 2 * D
SSD_HEADS = 32
SSD_P = 64
SSD_GROUPS = 8
SSD_N = 128
SSD_HPG = SSD_HEADS // SSD_GROUPS

MLA_HEADS = 16
MLA_Q_RANK = 512
MLA_KV_RANK = 256
MLA_NOPE = 64
MLA_ROPE = 32
MLA_V = 64

ML_DI = 2 * D
ML_HEADS = 8
ML_DQK = 128
ML_DV = 256

DF_HEADS = 8
DF_D = 64

LANES = 128
VMEM_LIMIT = 56 * 1024 * 1024


def _cparams(sem):
    return pltpu.CompilerParams(dimension_semantics=sem, vmem_limit_bytes=VMEM_LIMIT)


def _silu(x):
    return x * jax.nn.sigmoid(x)


def _softplus(x):
    return jnp.maximum(x, 0.0) + jnp.log1p(jnp.exp(-jnp.abs(x)))


def _rms(x, g):
    r = lax.rsqrt(jnp.mean(x * x, axis=-1, keepdims=True) + EPS)
    return (x * r) * g


def _dot(a, b):
    return jnp.dot(a, b, preferred_element_type=F32)


def _dot_nt(a, b):
    return lax.dot_general(a, b, (((1,), (1,)), ((), ())), preferred_element_type=F32)


def _dot_tn(a, b):
    return lax.dot_general(a, b, (((0,), (0,)), ((), ())), preferred_element_type=F32)


MOD_ROWS = 1024


def _group_of_tile(i, tm, sub=0):
    row0 = i * tm + sub * MOD_ROWS
    return jnp.where(row0 < ROWS_P, 0, 1 + (row0 - ROWS_P) // L_SAMPLE)


def _ada_body(c_ref, w_ref, b_ref, o_ref):
    s = _silu(c_ref[...]).astype(BF16)
    o_ref[...] = _dot(s, w_ref[...].astype(BF16)) + b_ref[...]


def _ada_mod(cvec, ada_w, ada_b):
    tn = 1536
    out = pl.pallas_call(
        _ada_body,
        grid=(DEPTH, 6 * D // tn),
        in_specs=[
            pl.BlockSpec((8, D), lambda l, j: (0, 0)),
            pl.BlockSpec((None, D, tn), lambda l, j: (l, 0, j)),
            pl.BlockSpec((None, 1, tn), lambda l, j: (l, 0, j)),
        ],
        out_specs=pl.BlockSpec((None, 8, tn), lambda l, j: (l, 0, j)),
        out_shape=jax.ShapeDtypeStruct((DEPTH, 8, 6 * D), F32),
        compiler_params=_cparams(("parallel", "parallel")),
        name="ada_mod",
    )(cvec, ada_w, ada_b.reshape(DEPTH, 1, 6 * D))
    return out[:, :3].reshape(DEPTH, 3, 6, D)


class _Pro:
    def __init__(self, rows, consts, mod, i, hs, lhs_out):
        self.rows, self.consts, self.mod, self.i = rows, consts, mod, i
        self._hs, self._lhs_out = hs, lhs_out

    def emit(self, val, cols=slice(None), rows=slice(None)):
        vb = val.astype(BF16)
        self._hs[rows, cols] = vb
        if self._lhs_out is not None:
            self._lhs_out[rows, cols] = vb


def _fused_mm(*, rows, w, k_dim, n_cols, tm, tn, prologue, consts=(), mod=None, epilogue=None, erows=(),
              emod=None, w_col0=0, emit_lhs=False, out_dtype=BF16, name):
    npt = ROWS_P // tm
    grid = (ROWS // tm, n_cols // tn)
    has_dual = any(r[0] == "d" for r in rows)

    in_specs, args = [], []
    for r in rows:
        if r[0] == "u":
            _, arr, width, cb = r
            in_specs.append(pl.BlockSpec((tm, width), lambda i, j, cb=cb: (i, cb)))
            args.append(arr)
        else:
            _, arr_p, arr_s, width, cb = r
            in_specs.append(pl.BlockSpec((tm, width), lambda i, j, cb=cb: (jnp.minimum(i, npt - 1), cb)))
            in_specs.append(pl.BlockSpec((tm, width), lambda i, j, cb=cb: (jnp.maximum(i - npt, 0), cb)))
            args += [arr_p, arr_s]
    for c in consts:
        in_specs.append(pl.BlockSpec(c.shape, lambda i, j, nd=c.ndim: (0,) * nd))
        args.append(c)
    n_sub = max(tm // MOD_ROWS, 1)
    if mod is not None:
        mod_arr, mod_layer = mod
        for s in range(n_sub):
            in_specs.append(pl.BlockSpec((None, None, 6, D),
                                         lambda i, j, s=s: (mod_layer, _group_of_tile(i, tm, s), 0, 0)))
            args.append(mod_arr)
    in_specs.append(pl.BlockSpec((k_dim, tn), lambda i, j: (0, w_col0 // tn + j)))
    args.append(w)
    for e in erows:
        in_specs.append(pl.BlockSpec((tm, tn), lambda i, j: (i, j)))
        args.append(e)
    if emod is not None:
        emod_arr, emod_layer = emod
        in_specs.append(pl.BlockSpec((None, None, 6, tn), lambda i, j: (emod_layer, _group_of_tile(i, tm), 0, j)))
        args.append(emod_arr)

    out_specs = [pl.BlockSpec((tm, tn), lambda i, j: (i, j))]
    out_shape = [jax.ShapeDtypeStruct((ROWS, n_cols), out_dtype)]
    if emit_lhs:
        out_specs.append(pl.BlockSpec((tm, k_dim), lambda i, j: (i, 0)))
        out_shape.append(jax.ShapeDtypeStruct((ROWS, k_dim), BF16))

    def body(*refs):
        it = iter(refs)
        row_refs = [(next(it),) if r[0] == "u" else (next(it), next(it)) for r in rows]
        const_refs = [next(it) for _ in consts]
        mod_ref = [next(it) for _ in range(n_sub)] if mod is not None else None
        w_ref = next(it)
        erow_refs = [next(it) for _ in erows]
        emod_ref = next(it) if emod is not None else None
        out_ref = next(it)
        lhs_out = next(it) if emit_lhs else None
        hs = next(it)
        i = pl.program_id(0)
        j = pl.program_id(1)

        def fill(use_prompt):
            chosen = [rr[0] if (len(rr) == 1 or use_prompt) else rr[1] for rr in row_refs]
            prologue(_Pro(chosen, const_refs, mod_ref, i, hs, lhs_out))

        if has_dual:
            pl.when(jnp.logical_and(j == 0, i < npt))(lambda: fill(True))
            pl.when(jnp.logical_and(j == 0, i >= npt))(lambda: fill(False))
        else:
            pl.when(j == 0)(lambda: fill(True))

        acc = _dot(hs[...], w_ref[...].astype(BF16))
        if epilogue is not None:
            acc = epilogue(acc, erow_refs, emod_ref)
        out_ref[...] = acc.astype(out_dtype)

    res = pl.pallas_call(
        body,
        grid=grid,
        in_specs=in_specs,
        out_specs=out_specs,
        out_shape=out_shape,
        scratch_shapes=[pltpu.VMEM((tm, k_dim), BF16)],
        compiler_params=_cparams(("parallel", "arbitrary")),
        name=name,
    )(*args)
    return res if emit_lhs else res[0]


def _pro_normmod(p):
    sub = p.rows[0].shape[0] // len(p.mod)
    for s, m in enumerate(p.mod):
        rows = slice(sub * s, sub * (s + 1))
        h = _rms(p.rows[0][rows, :], p.consts[0][...]) * (1.0 + m[1:2, :]) + m[0:1, :]
        p.emit(h, rows=rows)


def _pro_cast(p):
    p.emit(p.rows[0][...])


def _pro_rms(p):
    p.emit(_rms(p.rows[0][...].astype(F32), p.consts[0][...]))


def _epi_residual(gate_row):
    def epi(acc, erows, emod):
        return erows[0][...] + emod[gate_row:gate_row + 1, :] * acc
    return epi


def _mlp_body(x_ref, g_ref, mod_ref, w1_ref, w2_ref, o_ref, hs, acc):
    f = pl.program_id(1)

    @pl.when(f == 0)
    def _():
        h = _rms(x_ref[...], g_ref[...]) * (1.0 + mod_ref[4:5, :]) + mod_ref[3:4, :]
        hs[...] = h.astype(BF16)
        acc[...] = jnp.zeros_like(acc)

    u = _dot(hs[...], w1_ref[...].astype(BF16))
    u = jnp.square(jnp.maximum(u, 0.0))
    acc[...] += _dot(u.astype(BF16), w2_ref[...].astype(BF16))

    @pl.when(f == pl.num_programs(1) - 1)
    def _():
        o_ref[...] = x_ref[...] + mod_ref[5:6, :] * acc[...]


def _mlp(x, layer, g, mod, w1, w2):
    tm, tf = 1024, 1024
    return pl.pallas_call(
        _mlp_body,
        grid=(ROWS // tm, D_FF // tf),
        in_specs=[
            pl.BlockSpec((tm, D), lambda i, f: (i, 0)),
            pl.BlockSpec((None, 1, D), lambda i, f: (layer, 0, 0)),
            pl.BlockSpec((None, None, 6, D), lambda i, f: (layer, _group_of_tile(i, tm), 0, 0)),
            pl.BlockSpec((None, D, tf), lambda i, f: (layer, 0, f)),
            pl.BlockSpec((None, tf, D), lambda i, f: (layer, f, 0)),
        ],
        out_specs=pl.BlockSpec((tm, D), lambda i, f: (i, 0)),
        out_shape=jax.ShapeDtypeStruct((ROWS, D), F32),
        scratch_shapes=[pltpu.VMEM((tm, D), BF16), pltpu.VMEM((tm, D), F32)],
        compiler_params=_cparams(("parallel", "arbitrary")),
        name="mlp",
    )(x, g, mod, w1, w2)


def _final_norm_body(x_ref, g_ref, o_ref):
    o_ref[...] = _rms(x_ref[...], g_ref[...])


def _final_norm(x, g, row0, nrows):
    tm = 1024
    b0 = row0 // tm
    return pl.pallas_call(
        _final_norm_body,
        grid=(nrows // tm,),
        in_specs=[pl.BlockSpec((tm, D), lambda i: (b0 + i, 0)), pl.BlockSpec((1, D), lambda i: (0, 0))],
        out_specs=pl.BlockSpec((tm, D), lambda i: (i, 0)),
        out_shape=jax.ShapeDtypeStruct((nrows, D), F32),
        compiler_params=_cparams(("parallel",)),
        name="final_norm",
    )(x, g)


SCAN_T = 256


def _seg_prefix(v, t_in):
    s = 1
    while s < SCAN_T:
        v = v + jnp.where(t_in >= s, pltpu.roll(v, s, 0), 0.0)
        s *= 2
    return v


def _seg_suffix(v, t_in):
    n = v.shape[0]
    s = 1
    while s < SCAN_T:
        v = v + jnp.where(t_in < SCAN_T - s, pltpu.roll(v, n - s, 0), 0.0)
        s *= 2
    return v


def _ssd_body(*refs, L, has_s0, emit_state):
    it = iter(refs)
    z_ref, x_ref, b_ref, c_ref, dt_ref = (next(it) for _ in range(5))
    cwx, cbx, cwb, cbb, cwc, cbc = (next(it) for _ in range(6))
    dtb_ref, alog_ref, dl_ref = next(it), next(it), next(it)
    s0_ref = next(it) if has_s0 else None
    y_ref = next(it)
    st_out = next(it) if emit_state else None
    padx, padb, xa, ba, ca, cbs, cumc, crp, yacc, st = (next(it) for _ in range(10))

    T = SCAN_T
    nc = L // T
    g = pl.program_id(1)

    def conv(in_ref, w_ref, bias_ref, pad):
        width = in_ref.shape[1]
        pad[0:8, :] = jnp.zeros((8, width), F32)
        pad[L + 8:L + 16, :] = jnp.zeros((8, width), F32)
        pad[8:L + 8, :] = in_ref[...].astype(F32)
        acc = bias_ref[...] + pad[6:6 + L, :] * w_ref[0:1, :]
        for k in range(1, CONV_W):
            acc = acc + pad[6 + k:6 + k + L, :] * w_ref[k:k + 1, :]
        return _silu(acc)

    xa[...] = conv(x_ref, cwx, cbx, padx)
    ba[...] = conv(b_ref, cwb, cbb, padb)
    ca[...] = conv(c_ref, cwc, cbc, padb)

    dt_all = _softplus(dt_ref[...] + dtb_ref[...])
    a_all = dt_all * (-jnp.exp(alog_ref[...]))
    shift = jnp.where(g == 0, 0, LANES - SSD_HPG * g)
    ldt = jnp.log(pltpu.roll(dt_all, shift, 1))
    ar = pltpu.roll(a_all, shift, 1)
    t_in = lax.broadcasted_iota(jnp.int32, (L, LANES), 0) % T
    cum_f = _seg_prefix(ar, t_in)
    cum_b = _seg_suffix(ar, t_in)
    cumc[0] = cum_f
    cumc[1] = cum_b
    crp[0] = (cum_f - ldt).T
    crp[1] = (cum_b - ldt).T

    yacc[...] = xa[...] * dl_ref[...]
    for c in range(nc):
        rows = slice(T * c, T * (c + 1))
        cbs[c] = _dot_nt(ca[rows, :].astype(BF16), ba[rows, :].astype(BF16))
    if has_s0:
        st[...] = s0_ref[...]

    ii = lax.broadcasted_iota(jnp.int32, (T, T), 0)
    jj = lax.broadcasted_iota(jnp.int32, (T, T), 1)

    def chunk(c, d, xab, xat, bat, cab):
        rows = slice(T * c, T * (c + 1))
        off = SSD_HEADS * d
        first = c == (0 if d == 0 else nc - 1)
        last = c == (nc - 1 if d == 0 else 0)
        zero_state = first and not has_s0
        need_state = emit_state or not last
        mask = (ii >= jj) if d == 0 else (ii <= jj)
        end = T * c + (T - 1 if d == 0 else 0)
        for r in range(SSD_HPG):
            lane = off + r
            hs = slice(SSD_P * r, SSD_P * (r + 1))
            cc = cumc[d, rows, lane:lane + 1]
            cr = crp[d, lane:lane + 1, rows]
            dec = jnp.where(mask, jnp.exp(cc - cr), 0.0)
            y = _dot((cbs[c] * dec).astype(BF16), xab[:, hs])
            if not zero_state:
                y = y + _dot_nt(cab, st[d, r].astype(BF16)) * jnp.exp(cc)
            yacc[rows, hs] += y
            if need_state:
                tot = cumc[d, end:end + 1, lane:lane + 1]
                upd = _dot_nt(xat[hs, :], (bat * jnp.exp(tot - cr)).astype(BF16))
                st[d, r] = upd if zero_state else jnp.exp(tot) * st[d, r] + upd

    for c in range(nc):
        for d, cd in ((0, c), (1, nc - 1 - c)):
            rows = slice(T * cd, T * (cd + 1))
            xav = xa[rows, :]
            chunk(cd, d, xav.astype(BF16), xav.T.astype(BF16), ba[rows, :].T, ca[rows, :].astype(BF16))

    y_ref[...] = (yacc[...] * _silu(z_ref[...].astype(F32))).astype(BF16)
    if emit_state:
        st_out[...] = st[...]


def _ssd_scan(p0, pdt, conv_w, conv_b, dtb, alog, dl, s0, *, nb, L, emit_state):
    rb0 = 0 if s0 is None else ROWS_P // L
    gw = SSD_HPG * SSD_P
    nc = L // SCAN_T
    x0 = SSD_DI // gw
    b0 = 2 * SSD_DI // SSD_N
    c0 = b0 + SSD_GROUPS
    wb0 = SSD_DI // SSD_N
    wc0 = wb0 + SSD_GROUPS
    in_specs = [
        pl.BlockSpec((L, gw), lambda b, g: (rb0 + b, g)),
        pl.BlockSpec((L, gw), lambda b, g: (rb0 + b, x0 + g)),
        pl.BlockSpec((L, SSD_N), lambda b, g: (rb0 + b, b0 + g)),
        pl.BlockSpec((L, SSD_N), lambda b, g: (rb0 + b, c0 + g)),
        pl.BlockSpec((L, LANES), lambda b, g: (rb0 + b, 0)),
        pl.BlockSpec((CONV_W, gw), lambda b, g: (0, g)),
        pl.BlockSpec((1, gw), lambda b, g: (0, g)),
        pl.BlockSpec((CONV_W, SSD_N), lambda b, g: (0, wb0 + g)),
        pl.BlockSpec((1, SSD_N), lambda b, g: (0, wb0 + g)),
        pl.BlockSpec((CONV_W, SSD_N), lambda b, g: (0, wc0 + g)),
        pl.BlockSpec((1, SSD_N), lambda b, g: (0, wc0 + g)),
        pl.BlockSpec((1, LANES), lambda b, g: (0, 0)),
        pl.BlockSpec((1, LANES), lambda b, g: (0, 0)),
        pl.BlockSpec((1, gw), lambda b, g: (0, g)),
    ]
    args = [p0, p0, p0, p0, pdt, conv_w, conv_b, conv_w, conv_b, conv_w, conv_b, dtb, alog, dl]
    state_spec = pl.BlockSpec((None, None, 2, SSD_HPG, SSD_P, SSD_N), lambda b, g: (b, 0, 0, g, 0, 0))
    if s0 is not None:
        in_specs.append(state_spec)
        args.append(s0)
    out_specs = [pl.BlockSpec((L, gw), lambda b, g: (b, g))]
    out_shape = [jax.ShapeDtypeStruct((nb * L, SSD_DI), BF16)]
    if emit_state:
        out_specs.append(state_spec)
        out_shape.append(jax.ShapeDtypeStruct((nb, 1, 2, SSD_HEADS, SSD_P, SSD_N), F32))
    scratch = [
        pltpu.VMEM((L + 16, gw), F32),
        pltpu.VMEM((L + 16, SSD_N), F32),
        pltpu.VMEM((L, gw), F32),
        pltpu.VMEM((L, SSD_N), F32),
        pltpu.VMEM((L, SSD_N), F32),
        pltpu.VMEM((nc, SCAN_T, SCAN_T), F32),
        pltpu.VMEM((2, L, LANES), F32),
        pltpu.VMEM((2, LANES, L), F32),
        pltpu.VMEM((L, gw), F32),
        pltpu.VMEM((2, SSD_HPG, SSD_P, SSD_N), F32),
    ]
    res = pl.pallas_call(
        functools.partial(_ssd_body, L=L, has_s0=s0 is not None, emit_state=emit_state),
        grid=(nb, SSD_GROUPS),
        in_specs=in_specs,
        out_specs=out_specs,
        out_shape=out_shape,
        scratch_shapes=scratch,
        compiler_params=_cparams(("parallel", "parallel")),
        name="ssd_scan_p" if s0 is None else "ssd_scan_s",
    )(*args)
    return res


TQ = 256


def _mla_body(*refs, L, sample):
    it = iter(refs)
    cq_ref, ckv_ref, krs_ref = next(it), next(it), next(it)
    gq_ref, gkv_ref = next(it), next(it)
    wq, wqs, wk, wv = (next(it) for _ in range(4))
    if sample:
        cckv_ref, ckr_ref, cq_t, sq_t, ck_t, sk_t = (next(it) for _ in range(6))
    o_ref = next(it)
    if not sample:
        ckv_out, kr_out = next(it), next(it)
    kk, vv = next(it), next(it)
    qt = pl.program_id(1)
    scale = (MLA_NOPE + MLA_ROPE) ** -0.5
    hb = 4 * LANES

    @pl.when(qt == 0)
    def _():
        ckv = _rms(ckv_ref[...].astype(F32), gkv_ref[...])
        kr_own = krs_ref[:, 0:LANES]
        if sample:
            keys = jnp.concatenate([cckv_ref[...], ckv], axis=0)
            kr_own = kr_own * ck_t[...] + krs_ref[:, LANES:2 * LANES] * sk_t[...]
            kr_all = jnp.concatenate([ckr_ref[...], kr_own], axis=0)
        else:
            ckv_out[...] = ckv
            kr_out[...] = kr_own[:, MLA_NOPE:MLA_NOPE + MLA_ROPE]
            keys = ckv
            kr_all = kr_own
        kb = keys.astype(BF16)
        for blk in range(MLA_HEADS * LANES // hb):
            cols = slice(hb * blk, hb * (blk + 1))
            kn = _dot(kb, wk[:, cols])
            kk[:, cols] = (kn + jnp.concatenate([kr_all] * 4, axis=1)).astype(BF16)
            vv[:, cols] = _dot(kb, wv[:, cols]).astype(BF16)

    cq = _rms(cq_ref[...].astype(F32), gq_ref[...]).astype(BF16)
    for blk in range(MLA_HEADS * LANES // hb):
        qa = _dot(cq, wq[:, hb * blk:hb * (blk + 1)])
        if sample:
            qs = _dot(cq, wqs[:, hb * blk:hb * (blk + 1)])
        pair = None
        for hh in range(4):
            h = 4 * blk + hh
            cols = slice(LANES * h, LANES * (h + 1))
            qh = qa[:, LANES * hh:LANES * (hh + 1)]
            if sample:
                qh = qh * cq_t[...] + qs[:, LANES * hh:LANES * (hh + 1)] * sq_t[...]
            s = _dot_nt(qh.astype(BF16), kk[:, cols]) * scale
            e = jnp.exp(s - jnp.max(s, axis=1, keepdims=True))
            o = _dot(e.astype(BF16), vv[:, cols]) / jnp.sum(e, axis=1, keepdims=True)
            if h % 2 == 0:
                pair = o
            else:
                o_ref[:, LANES * (h // 2):LANES * (h // 2 + 1)] = (pair + o).astype(BF16)


def _mla_attn(p1, p1s, gq, gkv, wq, wqs, wk, wv, ctx, *, nb, L):
    sample = ctx is not None
    nq = L // TQ
    tk = L + (PAST if sample else 0)
    rbq0 = ROWS_P // TQ if sample else 0
    rbs0 = ROWS_P // L if sample else 0
    ckv_blk = MLA_Q_RANK // MLA_KV_RANK

    def const(a):
        return pl.BlockSpec(a.shape, lambda b, q, nd=a.ndim: (0,) * nd)

    in_specs = [
        pl.BlockSpec((TQ, MLA_Q_RANK), lambda b, q: (rbq0 + b * nq + q, 0)),
        pl.BlockSpec((L, MLA_KV_RANK), lambda b, q: (rbs0 + b, ckv_blk)),
        pl.BlockSpec((L, 2 * LANES), lambda b, q: (rbs0 + b, 0)),
        const(gq), const(gkv), const(wq), const(wqs), const(wk), const(wv),
    ]
    args = [p1, p1, p1s, gq, gkv, wq, wqs, wk, wv]
    if sample:
        cckv, ckr, cpad, spad = ctx
        in_specs += [
            pl.BlockSpec((None, None, PAST, MLA_KV_RANK), lambda b, q: (b, 0, 0, 0)),
            pl.BlockSpec((None, None, PAST, LANES), lambda b, q: (b, 0, 0, 0)),
            pl.BlockSpec((TQ, LANES), lambda b, q: (q, 0)),
            pl.BlockSpec((TQ, LANES), lambda b, q: (q, 0)),
            const(cpad), const(spad),
        ]
        args += [cckv, ckr, cpad, spad, cpad, spad]
    out_specs = [pl.BlockSpec((TQ, MLA_HEADS * MLA_V), lambda b, q: (b * nq + q, 0))]
    out_shape = [jax.ShapeDtypeStruct((nb * L, MLA_HEADS * MLA_V), BF16)]
    if not sample:
        out_specs += [
            pl.BlockSpec((None, None, L, MLA_KV_RANK), lambda b, q: (b, 0, 0, 0)),
            pl.BlockSpec((None, None, L, MLA_ROPE), lambda b, q: (b, 0, 0, 0)),
        ]
        out_shape += [
            jax.ShapeDtypeStruct((nb, 1, L, MLA_KV_RANK), F32),
            jax.ShapeDtypeStruct((nb, 1, L, MLA_ROPE), F32),
        ]
    scratch = [
        pltpu.VMEM((tk, MLA_HEADS * LANES), BF16),
        pltpu.VMEM((tk, MLA_HEADS * LANES), BF16),
    ]
    return pl.pallas_call(
        functools.partial(_mla_body, L=L, sample=sample),
        grid=(nb, nq),
        in_specs=in_specs,
        out_specs=out_specs,
        out_shape=out_shape,
        scratch_shapes=scratch,
        compiler_params=_cparams(("parallel", "arbitrary")),
        name="mla_attn_s" if sample else "mla_attn_p",
    )(*args)


def _log_sigmoid(x):
    return -_softplus(-x)


def _mlstm_body(*refs, L, has_s0, emit_state):
    it = iter(refs)
    q_ref, k_ref, v_ref, g_ref, gb_ref, ng_ref = (next(it) for _ in range(6))
    if has_s0:
        c0_ref, n0_ref, m0_ref = next(it), next(it), next(it)
    h_ref = next(it)
    if emit_state:
        c_out, n_out, m_out = next(it), next(it), next(it)
    gsc, gtr, qk, hacc, cst, nst, mst = (next(it) for _ in range(7))

    T = SCAN_T
    nc = L // T
    hd = pl.program_id(1)

    gts = g_ref[...] + gb_ref[...]
    gr = pltpu.roll(gts, jnp.where(hd == 0, 0, LANES - hd), 1)
    lf = _log_sigmoid(gr)
    t_in = lax.broadcasted_iota(jnp.int32, (L, LANES), 0) % T
    b_f = _seg_prefix(lf, t_in)
    b_b = _seg_suffix(lf, t_in)
    gsc[0] = gr
    gsc[1] = b_f
    gsc[2] = b_b
    gtr[0] = gr.T
    gtr[1] = b_f.T
    gtr[2] = b_b.T

    if has_s0:
        for d in range(2):
            cst[d] = c0_ref[d]
            nst[d] = n0_ref[d, pl.ds(hd, 1), :]
            mst[d] = m0_ref[d, pl.ds(hd, 1), :]
    else:
        cst[...] = jnp.zeros_like(cst)
        nst[...] = jnp.zeros_like(nst)
        mst[...] = jnp.zeros_like(mst)
    hacc[...] = jnp.zeros_like(hacc)

    for c in range(nc):
        rows = slice(T * c, T * (c + 1))
        qk[c] = _dot_nt(q_ref[rows, :], k_ref[rows, :])

    ii = lax.broadcasted_iota(jnp.int32, (T, T), 0)
    jj = lax.broadcasted_iota(jnp.int32, (T, T), 1)

    def chunk(c, d):
        rows = slice(T * c, T * (c + 1))
        first = c == (0 if d == 0 else nc - 1)
        last = c == (nc - 1 if d == 0 else 0)
        zero_state = first and not has_s0
        li = 2 * ML_HEADS * d
        lb = li + ML_HEADS
        logi = gsc[0, rows, li:li + 1]
        bc = gsc[1 + d, rows, lb:lb + 1]
        ir = gtr[0, li:li + 1, rows]
        br = gtr[1 + d, lb:lb + 1, rows]
        mask = (ii >= jj) if d == 0 else (ii <= jj)
        dlog = jnp.where(mask, bc - br + ir, -jnp.inf)
        m_prev = mst[d][:, 0:1]
        inter = bc + m_prev
        mcomb = jnp.maximum(inter, jnp.max(dlog, axis=1, keepdims=True))
        s = qk[c] * jnp.exp(dlog - mcomb)
        vb = v_ref[rows, :]
        num = _dot(s.astype(BF16), vb)
        den = jnp.sum(s, axis=1, keepdims=True)
        if not zero_state:
            iw = jnp.exp(inter - mcomb)
            qc = q_ref[rows, :]
            num = num + iw * _dot(qc, cst[d].astype(BF16))
            den = den + iw * jnp.sum(qc.astype(F32) * nst[d], axis=1, keepdims=True)
        hacc[rows, :] += num / jnp.maximum(jnp.abs(den), jnp.exp(-mcomb))
        if emit_state or not last:
            end = T * c + (T - 1 if d == 0 else 0)
            bq = gsc[1 + d, end:end + 1, lb:lb + 1]
            wlog = bq - bc + logi
            m_new = jnp.maximum(bq + m_prev, jnp.max(wlog, axis=0, keepdims=True))
            kw = k_ref[rows, :].astype(F32) * jnp.exp(wlog - m_new)
            upd = _dot_tn(kw.astype(BF16), vb)
            nsum = jnp.sum(kw, axis=0, keepdims=True)
            if zero_state:
                cst[d] = upd
                nst[d] = nsum
            else:
                cw = jnp.exp(bq + m_prev - m_new)
                cst[d] = cw * cst[d] + upd
                nst[d] = cw * nst[d] + nsum
            mst[d] = jnp.broadcast_to(m_new, (1, LANES))

    for c in range(nc):
        chunk(c, 0)
        chunk(nc - 1 - c, 1)

    h_ref[...] = _rms(hacc[...], ng_ref[...]).astype(BF16)
    if emit_state:
        for d in range(2):
            c_out[d] = cst[d]
            n_out[d, pl.ds(hd, 1), :] = nst[d]
            m_out[d, pl.ds(hd, 1), :] = mst[d]


def _mlstm_scan(q, k, v, gates, gb, ng, s0, *, nb, L, emit_state):
    rb0 = 0 if s0 is None else ROWS_P // L
    in_specs = [
        pl.BlockSpec((L, ML_DQK), lambda b, h: (rb0 + b, h)),
        pl.BlockSpec((L, ML_DQK), lambda b, h: (rb0 + b, h)),
        pl.BlockSpec((L, ML_DV), lambda b, h: (rb0 + b, h)),
        pl.BlockSpec((L, LANES), lambda b, h: (rb0 + b, 0)),
        pl.BlockSpec((1, LANES), lambda b, h: (0, 0)),
        pl.BlockSpec((1, ML_DV), lambda b, h: (0, h)),
    ]
    args = [q, k, v, gates, gb, ng]
    c_spec = pl.BlockSpec((None, None, 2, None, ML_DQK, ML_DV), lambda b, h: (b, 0, 0, h, 0, 0))
    n_spec = pl.BlockSpec((None, 2, ML_HEADS, LANES), lambda b, h: (b, 0, 0, 0))
    if s0 is not None:
        in_specs += [c_spec, n_spec, n_spec]
        args += list(s0)
    out_specs = [pl.BlockSpec((L, ML_DV), lambda b, h: (b, h))]
    out_shape = [jax.ShapeDtypeStruct((nb * L, ML_DI), BF16)]
    if emit_state:
        out_specs += [c_spec, n_spec, n_spec]
        out_shape += [
            jax.ShapeDtypeStruct((nb, 1, 2, ML_HEADS, ML_DQK, ML_DV), F32),
            jax.ShapeDtypeStruct((nb, 2, ML_HEADS, LANES), F32),
            jax.ShapeDtypeStruct((nb, 2, ML_HEADS, LANES), F32),
        ]
    scratch = [
        pltpu.VMEM((3, L, LANES), F32),
        pltpu.VMEM((3, LANES, L), F32),
        pltpu.VMEM((L // SCAN_T, SCAN_T, SCAN_T), F32),
        pltpu.VMEM((L, ML_DV), F32),
        pltpu.VMEM((2, ML_DQK, ML_DV), F32),
        pltpu.VMEM((2, 1, ML_DQK), F32),
        pltpu.VMEM((2, 1, LANES), F32),
    ]
    return pl.pallas_call(
        functools.partial(_mlstm_body, L=L, has_s0=s0 is not None, emit_state=emit_state),
        grid=(nb, ML_HEADS),
        in_specs=in_specs,
        out_specs=out_specs,
        out_shape=out_shape,
        scratch_shapes=scratch,
        compiler_params=_cparams(("parallel", "arbitrary")),
        name="mlstm_scan_p" if s0 is None else "mlstm_scan_s",
    )(*args)


def _rope_blocks(x, ct, st):
    lane = lax.broadcasted_iota(jnp.int32, (x.shape[0], LANES), 1)
    first_half = lane % DF_D < DF_D // 2
    outs = []
    for blk in range(x.shape[1] // LANES):
        xb = x[:, LANES * blk:LANES * (blk + 1)]
        swapped = jnp.where(first_half, pltpu.roll(xb, LANES - DF_D // 2, 1), pltpu.roll(xb, DF_D // 2, 1))
        outs.append(xb * ct + swapped * st)
    return jnp.concatenate(outs, axis=1)


def _diff_body(*refs, L, sample, lam_init):
    it = iter(refs)
    q_ref, k_ref, v_ref = next(it), next(it), next(it)
    lq1, lk1, lq2, lk2, sg_ref = (next(it) for _ in range(5))
    if sample:
        ck_ref, cv_ref, cq_t, sq_t, ck_t, sk_t = (next(it) for _ in range(6))
    o_ref = next(it)
    if sample:
        ka, va = next(it), next(it)
    else:
        k_out, v_out = next(it), next(it)
    qt = pl.program_id(1)

    lam = (jnp.exp(jnp.sum(lq1[...] * lk1[...], axis=1, keepdims=True))
           - jnp.exp(jnp.sum(lq2[...] * lk2[...], axis=1, keepdims=True)) + lam_init)

    @pl.when(qt == 0)
    def _():
        if sample:
            ka[0:PAST, :] = ck_ref[...].astype(BF16)
            ka[PAST:PAST + L, :] = _rope_blocks(k_ref[...].astype(F32), ck_t[...], sk_t[...]).astype(BF16)
            va[0:PAST, :] = cv_ref[...].astype(BF16)
            va[PAST:PAST + L, :] = v_ref[...]
        else:
            k_out[...] = k_ref[...].astype(F32)
            v_out[...] = v_ref[...].astype(F32)

    q = q_ref[...].astype(F32)
    if sample:
        q = _rope_blocks(q, cq_t[...], sq_t[...])
    q = q * (DF_D ** -0.5)
    lo = lax.broadcasted_iota(jnp.int32, (TQ, LANES), 1) < DF_D

    def attend(qm, kh, vh):
        s = _dot_nt(qm, kh)
        e = jnp.exp(s - jnp.max(s, axis=1, keepdims=True))
        return _dot(e.astype(BF16), vh) / jnp.sum(e, axis=1, keepdims=True)

    for h in range(DF_HEADS):
        cols = slice(LANES * h, LANES * (h + 1))
        qh = q[:, cols]
        if sample:
            kh, vh = ka[:, cols], va[:, cols]
        else:
            kh, vh = k_ref[:, cols], v_ref[:, cols]
        a0 = attend(jnp.where(lo, qh, 0.0).astype(BF16), kh, vh)
        a1 = attend(jnp.where(lo, 0.0, qh).astype(BF16), kh, vh)
        o = a0 - lam * a1
        o_ref[:, cols] = (_rms(o, sg_ref[...]) * (1.0 - lam_init)).astype(BF16)


def _diff_attn(p3, lq1, lk1, lq2, lk2, sg, ctx, *, nb, L, lam_init):
    sample = ctx is not None
    nq = L // TQ
    rbq0 = ROWS_P // TQ if sample else 0
    rbs0 = ROWS_P // L if sample else 0

    def const(a):
        return pl.BlockSpec(a.shape, lambda b, q, nd=a.ndim: (0,) * nd)

    in_specs = [
        pl.BlockSpec((TQ, D), lambda b, q: (rbq0 + b * nq + q, 0)),
        pl.BlockSpec((L, D), lambda b, q: (rbs0 + b, 1)),
        pl.BlockSpec((L, D), lambda b, q: (rbs0 + b, 2)),
        const(lq1), const(lk1), const(lq2), const(lk2), const(sg),
    ]
    args = [p3, p3, p3, lq1, lk1, lq2, lk2, sg]
    kv_spec = pl.BlockSpec((None, None, PAST if sample else L, D), lambda b, q: (b, 0, 0, 0))
    if sample:
        ck, cv, c128, s128 = ctx
        in_specs += [
            kv_spec, kv_spec,
            pl.BlockSpec((TQ, LANES), lambda b, q: (q, 0)),
            pl.BlockSpec((TQ, LANES), lambda b, q: (q, 0)),
            const(c128), const(s128),
        ]
        args += [ck, cv, c128, s128, c128, s128]
    out_specs = [pl.BlockSpec((TQ, D), lambda b, q: (b * nq + q, 0))]
    out_shape = [jax.ShapeDtypeStruct((nb * L, D), BF16)]
    scratch = []
    if sample:
        scratch = [pltpu.VMEM((PAST + L, D), BF16), pltpu.VMEM((PAST + L, D), BF16)]
    else:
        out_specs += [kv_spec, kv_spec]
        out_shape += [jax.ShapeDtypeStruct((nb, 1, L, D), F32)] * 2
    return pl.pallas_call(
        functools.partial(_diff_body, L=L, sample=sample, lam_init=lam_init),
        grid=(nb, nq),
        in_specs=in_specs,
        out_specs=out_specs,
        out_shape=out_shape,
        scratch_shapes=scratch,
        compiler_params=_cparams(("parallel", "arbitrary")),
        name="diff_attn_s" if sample else "diff_attn_p",
    )(*args)


def _pro_mlstm_conv(p):
    x_ref = p.rows[0]
    w_ref, b_ref = p.consts
    tm = x_ref.shape[0]
    seq = jnp.where(p.i < ROWS_P // tm, L_PROMPT, L_SAMPLE)
    cw = 256
    pos = lax.broadcasted_iota(jnp.int32, (tm, cw), 0) & (seq - 1)
    for cb in range(ML_DI // cw):
        cols = slice(cw * cb, cw * (cb + 1))
        x = x_ref[:, cols].astype(F32)
        acc = None
        for k in range(CONV_W):
            off = k - CONV_W // 2
            src = pos + off
            tap = x if off == 0 else pltpu.roll(x, (-off) % tm, 0)
            tap = jnp.where(jnp.logical_and(src >= 0, src < seq), tap, 0.0) * w_ref[k:k + 1, cols]
            acc = b_ref[:, cols] + tap if acc is None else acc + tap
        p.emit(_silu(acc), cols)


def _pro_mlstm_gate(p):
    hn, xc, z = (r[...].astype(F32) for r in p.rows)
    p.emit((hn + p.consts[0][...] * xc) * _silu(z))


def _rope_tables(d):
    rows = L_SAMPLE // GRID_W
    pos_r = jnp.repeat(jnp.arange(rows, dtype=F32), GRID_W)
    pos_c = jnp.tile(jnp.arange(GRID_W, dtype=F32), rows)
    nf = d // 4
    inv = ROPE_BASE ** (-jnp.arange(nf, dtype=F32) / nf)
    ang = jnp.concatenate([pos_r[:, None] * inv, pos_c[:, None] * inv], axis=-1)
    cos, sin = jnp.cos(ang), jnp.sin(ang)
    return jnp.concatenate([cos, cos], axis=-1), jnp.concatenate([-sin, sin], axis=-1)


def _pad_cols(a, n):
    return jnp.pad(a, ((0, 0), (0, n - a.shape[1])))


def kernel(x_prompt, x_sample, state_ssd, cache_mla_ckv, cache_mla_krope, state_mlstm_C, state_mlstm_n, state_mlstm_m, cache_diff_k, cache_diff_v, c, c_ctx, norm1_g, norm2_g, ada_w, ada_b, mlp_w1, mlp_w2, final_g, ssd_w_in, ssd_conv_w, ssd_conv_b, ssd_dt_bias, ssd_A_log, ssd_D, ssd_norm_g, ssd_w_out, mla_w_in, mla_q_norm_g, mla_kv_norm_g, mla_w_uq, mla_w_ukv, mla_w_o, mlstm_w_up, mlstm_conv_w, mlstm_conv_b, mlstm_gate_b, mlstm_w_q, mlstm_w_k, mlstm_w_v, mlstm_skip, mlstm_norm_g, mlstm_w_down, diff_w_qkv, diff_lq1, diff_lk1, diff_lq2, diff_lk2, diff_subln_g, diff_w_o):
    x = jnp.concatenate([x_prompt.reshape(ROWS_P, D), x_sample.reshape(ROWS_S, D)], axis=0)
    cvec = jnp.concatenate([c_ctx[None, :], c, jnp.zeros((5, D), F32)], axis=0)
    mod_all = _ada_mod(cvec, ada_w, ada_b)
    g2 = norm2_g.reshape(DEPTH, 1, D)

    def in_proj(xin, layer, w, n_cols, tn, w_col0=0, out_dtype=BF16, name="in_proj"):
        return _fused_mm(rows=[("u", xin, D, 0)], consts=[norm1_g[layer][None, :]], mod=(mod_all, layer), w=w, k_dim=D,
                         n_cols=n_cols, tm=2048, tn=tn, prologue=_pro_normmod, w_col0=w_col0, out_dtype=out_dtype,
                         name=name)

    def out_proj(xin, layer, rows, consts, prologue, w, k_dim, name):
        return _fused_mm(rows=rows, consts=consts, w=w, k_dim=k_dim, n_cols=D, tm=1024, tn=512, prologue=prologue,
                         epilogue=_epi_residual(2), erows=[xin], emod=(mod_all, layer), out_dtype=F32, name=name)

    p0 = in_proj(x, 0, ssd_w_in[0], 3 * SSD_DI, 512, name="ssd_in")
    pdt = in_proj(x, 0, _pad_cols(ssd_w_in[0][:, 3 * SSD_DI:], LANES), LANES, LANES, out_dtype=F32, name="ssd_in_dt")
    dtb = _pad_cols(ssd_dt_bias[0].reshape(1, 2 * SSD_HEADS), LANES)
    alog = _pad_cols(ssd_A_log[0].reshape(1, 2 * SSD_HEADS), LANES)
    dl = jnp.repeat(ssd_D[0], SSD_P)[None, :]
    scan_args = (p0, pdt, ssd_conv_w[0], ssd_conv_b[0][None, :], dtb, alog, dl)
    yg_p, new_ssd = _ssd_scan(*scan_args, None, nb=N_PROMPT_SEQ, L=L_PROMPT, emit_state=True)
    (yg_s,) = _ssd_scan(*scan_args, state_ssd, nb=N_SAMPLE_SEQ, L=L_SAMPLE, emit_state=False)
    x = out_proj(x, 0, [("d", yg_p, yg_s, SSD_DI, 0)], [ssd_norm_g[0][None, :]], _pro_rms, ssd_w_out[0], SSD_DI,
                 "ssd_out")
    x = _mlp(x, 0, g2, mod_all, mlp_w1, mlp_w2)

    w_in = mla_w_in[0]
    kr0 = MLA_Q_RANK + MLA_KV_RANK
    half = MLA_ROPE // 2
    zk = jnp.zeros((D, MLA_NOPE), F32)
    zr = jnp.zeros((D, LANES - MLA_NOPE - MLA_ROPE), F32)
    w_kr = jnp.concatenate([zk, w_in[:, kr0:kr0 + MLA_ROPE], zr,
                            zk, w_in[:, kr0 + half:kr0 + MLA_ROPE], w_in[:, kr0:kr0 + half], zr], axis=1)
    p1 = in_proj(x, 1, w_in, kr0, 256, name="mla_in")
    p1s = in_proj(x, 1, w_kr, 2 * LANES, 2 * LANES, out_dtype=F32, name="mla_in_kr")
    wuq = mla_w_uq[0].reshape(MLA_Q_RANK, MLA_HEADS, MLA_NOPE + MLA_ROPE)
    zq = jnp.zeros((MLA_Q_RANK, MLA_HEADS, LANES - MLA_NOPE - MLA_ROPE), F32)
    zqn = jnp.zeros((MLA_Q_RANK, MLA_HEADS, MLA_NOPE), F32)
    wq = jnp.concatenate([wuq, zq], axis=-1).reshape(MLA_Q_RANK, MLA_HEADS * LANES).astype(BF16)
    wqs = jnp.concatenate([zqn, wuq[..., MLA_NOPE + half:], wuq[..., MLA_NOPE:MLA_NOPE + half], zq],
                          axis=-1).reshape(MLA_Q_RANK, MLA_HEADS * LANES).astype(BF16)
    wukv = mla_w_ukv[0].reshape(MLA_KV_RANK, MLA_HEADS, MLA_NOPE + MLA_V)
    zkv = jnp.zeros((MLA_KV_RANK, MLA_HEADS, MLA_NOPE), F32)
    wk = jnp.concatenate([wukv[..., :MLA_NOPE], zkv], axis=-1).reshape(MLA_KV_RANK, MLA_HEADS * LANES).astype(BF16)
    wv_own = wukv[..., MLA_NOPE:]
    odd = (jnp.arange(MLA_HEADS) % 2 == 1)[None, :, None]
    wv = jnp.where(odd, jnp.concatenate([zkv, wv_own], axis=-1), jnp.concatenate([wv_own, zkv], axis=-1))
    wv = wv.reshape(MLA_KV_RANK, MLA_HEADS * LANES).astype(BF16)
    c32, s32 = _rope_tables(MLA_ROPE)
    tz = jnp.zeros((L_SAMPLE, LANES - MLA_NOPE - MLA_ROPE), F32)
    cpad = jnp.concatenate([jnp.ones((L_SAMPLE, MLA_NOPE), F32), c32, tz], axis=1)
    spad = jnp.concatenate([jnp.zeros((L_SAMPLE, MLA_NOPE), F32), s32, tz], axis=1)
    ckr_pad = jnp.pad(cache_mla_krope, ((0, 0), (0, 0), (0, 0), (MLA_NOPE, LANES - MLA_NOPE - MLA_ROPE)))
    mla_w = (mla_q_norm_g[0][None, :], mla_kv_norm_g[0][None, :], wq, wqs, wk, wv)
    o_p, new_ckv, new_kr = _mla_attn(p1, p1s, *mla_w, None, nb=N_PROMPT_SEQ, L=L_PROMPT)
    (o_s,) = _mla_attn(p1, p1s, *mla_w, (cache_mla_ckv, ckr_pad, cpad, spad), nb=N_SAMPLE_SEQ, L=L_SAMPLE)
    x = out_proj(x, 1, [("d", o_p, o_s, D, 0)], [], _pro_cast, mla_w_o[0], D, "mla_out")
    x = _mlp(x, 1, g2, mod_all, mlp_w1, mlp_w2)

    p2 = in_proj(x, 2, mlstm_w_up[0], 2 * ML_DI, 512, name="mlstm_up")
    gates = in_proj(x, 2, _pad_cols(mlstm_w_up[0][:, 2 * ML_DI:], LANES), LANES, LANES, out_dtype=F32, name="mlstm_up_gates")
    q, xc = _fused_mm(rows=[("u", p2, ML_DI, 0)], consts=[mlstm_conv_w[0], mlstm_conv_b[0][None, :]], w=mlstm_w_q[0],
                      k_dim=ML_DI, n_cols=ML_HEADS * ML_DQK, tm=1024, tn=256, prologue=_pro_mlstm_conv, emit_lhs=True,
                      name="mlstm_q")
    k = _fused_mm(rows=[("u", xc, ML_DI, 0)], w=mlstm_w_k[0], k_dim=ML_DI, n_cols=ML_HEADS * ML_DQK, tm=1024, tn=512,
                  prologue=_pro_cast, epilogue=lambda acc, e, m: acc * (ML_DQK ** -0.5), name="mlstm_k")
    v = _fused_mm(rows=[("u", p2, ML_DI, 0)], w=mlstm_w_v[0], k_dim=ML_DI, n_cols=ML_DI, tm=1024, tn=512,
                  prologue=_pro_cast, name="mlstm_v")
    gb = _pad_cols(mlstm_gate_b[0].reshape(1, 4 * ML_HEADS), LANES)
    ng = mlstm_norm_g[0][None, :]
    n0 = _pad_cols(state_mlstm_n[:, 0].reshape(N_SAMPLE_SEQ * 2 * ML_HEADS, ML_DQK), LANES).reshape(
        N_SAMPLE_SEQ, 2, ML_HEADS, LANES)
    m0 = jnp.broadcast_to(state_mlstm_m[:, 0][..., None], (N_SAMPLE_SEQ, 2, ML_HEADS, LANES))
    hn_p, new_c, new_n, new_m = _mlstm_scan(q, k, v, gates, gb, ng, None, nb=N_PROMPT_SEQ, L=L_PROMPT, emit_state=True)
    (hn_s,) = _mlstm_scan(q, k, v, gates, gb, ng, (state_mlstm_C, n0, m0), nb=N_SAMPLE_SEQ, L=L_SAMPLE,
                          emit_state=False)
    x = out_proj(x, 2, [("d", hn_p, hn_s, ML_DI, 0), ("u", xc, ML_DI, 0), ("u", p2, ML_DI, 1)],
                 [mlstm_skip[0][None, :]], _pro_mlstm_gate, mlstm_w_down[0], ML_DI, "mlstm_down")
    x = _mlp(x, 2, g2, mod_all, mlp_w1, mlp_w2)

    lam_init = 0.8 - 0.6 * math.exp(-0.3 * 3)
    p3 = in_proj(x, 3, diff_w_qkv[0], 3 * D, 512, name="diff_qkv")
    c64, s64 = _rope_tables(DF_D)
    c128 = jnp.concatenate([c64, c64], axis=-1)
    s128 = jnp.concatenate([s64, s64], axis=-1)
    dparams = (diff_lq1, diff_lk1, diff_lq2, diff_lk2, diff_subln_g)
    od_p, new_dk, new_dv = _diff_attn(p3, *dparams, None, nb=N_PROMPT_SEQ, L=L_PROMPT, lam_init=lam_init)
    ctx = (cache_diff_k.reshape(N_SAMPLE_SEQ, 1, PAST, D), cache_diff_v.reshape(N_SAMPLE_SEQ, 1, PAST, D), c128, s128)
    (od_s,) = _diff_attn(p3, *dparams, ctx, nb=N_SAMPLE_SEQ, L=L_SAMPLE, lam_init=lam_init)
    x = out_proj(x, 3, [("d", od_p, od_s, D, 0)], [], _pro_cast, diff_w_o[0], D, "diff_out")
    x = _mlp(x, 3, g2, mod_all, mlp_w1, mlp_w2)

    fg = final_g[None, :]
    y_prompt = _final_norm(x, fg, 0, ROWS_P).reshape(N_PROMPT_SEQ, L_PROMPT, D)
    y_sample = _final_norm(x, fg, ROWS_P, ROWS_S).reshape(N_SAMPLE_SEQ, L_SAMPLE, D)
    return (y_prompt, y_sample, new_ssd, new_ckv, new_kr, new_c,
            new_n[None].reshape(N_PROMPT_SEQ, 1, 2, ML_HEADS, ML_DQK),
            new_m[..., 0].reshape(N_PROMPT_SEQ, 1, 2, ML_HEADS),
            new_dk.reshape(N_PROMPT_SEQ, 1, L_PROMPT, DF_HEADS, 2 * DF_D),
            new_dv.reshape(N_PROMPT_SEQ, 1, L_PROMPT, DF_HEADS, 2 * DF_D))
```

```python
import functools
import math

import jax
import jax.numpy as jnp
from jax import lax
from jax.experimental import pallas as pl
from jax.experimental.pallas import tpu as pltpu

F32 = jnp.float32
BF16 = jnp.bfloat16

D = 1024
DEPTH = 4
D_FF = 4 * D
EPS = 1e-6
ROPE_BASE = 10000.0
CONV_W = 5
GRID_W = 64

N_PROMPT_SEQ = 32
L_PROMPT = 256
N_SAMPLE_SEQ = 2
L_SAMPLE = 1024
PAST = 256
ROWS_P = N_PROMPT_SEQ * L_PROMPT
ROWS_S = N_SAMPLE_SEQ * L_SAMPLE
ROWS = ROWS_P + ROWS_S

SSD_DI = 2 * D
SSD_HEADS = 32
SSD_P = 64
SSD_GROUPS = 8
SSD_N = 128
SSD_HPG = SSD_HEADS // SSD_GROUPS

MLA_HEADS = 16
MLA_Q_RANK = 512
MLA_KV_RANK = 256
MLA_NOPE = 64
MLA_ROPE = 32
MLA_V = 64

ML_DI = 2 * D
ML_HEADS = 8
ML_DQK = 128
ML_DV = 256

DF_HEADS = 8
DF_D = 64

LANES = 128
VMEM_LIMIT = 56 * 1024 * 1024


def _cparams(sem):
    return pltpu.CompilerParams(dimension_semantics=sem, vmem_limit_bytes=VMEM_LIMIT)


def _silu(x):
    return x * jax.nn.sigmoid(x)


def _softplus(x):
    return jnp.maximum(x, 0.0) + jnp.log1p(jnp.exp(-jnp.abs(x)))


def _rms(x, g):
    r = lax.rsqrt(jnp.mean(x * x, axis=-1, keepdims=True) + EPS)
    return (x * r) * g


def _dot(a, b):
    return jnp.dot(a, b, preferred_element_type=F32)


def _dot_nt(a, b):
    return lax.dot_general(a, b, (((1,), (1,)), ((), ())), preferred_element_type=F32)


def _dot_tn(a, b):
    return lax.dot_general(a, b, (((0,), (0,)), ((), ())), preferred_element_type=F32)


MOD_ROWS = 1024


def _group_of_tile(i, tm, sub=0):
    row0 = i * tm + sub * MOD_ROWS
    return jnp.where(row0 < ROWS_P, 0, 1 + (row0 - ROWS_P) // L_SAMPLE)


def _ada_body(c_ref, w_ref, b_ref, o_ref):
    s = _silu(c_ref[...]).astype(BF16)
    o_ref[...] = _dot(s, w_ref[...].astype(BF16)) + b_ref[...]


def _ada_mod(cvec, ada_w, ada_b):
    tn = 1536
    out = pl.pallas_call(
        _ada_body,
        grid=(DEPTH, 6 * D // tn),
        in_specs=[
            pl.BlockSpec((8, D), lambda l, j: (0, 0)),
            pl.BlockSpec((None, D, tn), lambda l, j: (l, 0, j)),
            pl.BlockSpec((None, 1, tn), lambda l, j: (l, 0, j)),
        ],
        out_specs=pl.BlockSpec((None, 8, tn), lambda l, j: (l, 0, j)),
        out_shape=jax.ShapeDtypeStruct((DEPTH, 8, 6 * D), F32),
        compiler_params=_cparams(("parallel", "parallel")),
        name="ada_mod",
    )(cvec, ada_w, ada_b.reshape(DEPTH, 1, 6 * D))
    return out[:, :3].reshape(DEPTH, 3, 6, D)


class _Pro:
    def __init__(self, rows, consts, mod, i, hs, lhs_out):
        self.rows, self.consts, self.mod, self.i = rows, consts, mod, i
        self._hs, self._lhs_out = hs, lhs_out

    def emit(self, val, cols=slice(None), rows=slice(None)):
        vb = val.astype(BF16)
        self._hs[rows, cols] = vb
        if self._lhs_out is not None:
            self._lhs_out[rows, cols] = vb


def _fused_mm(*, rows, w, k_dim, n_cols, tm, tn, prologue, consts=(), mod=None, epilogue=None, erows=(),
              emod=None, w_col0=0, emit_lhs=False, out_dtype=BF16, name):
    npt = ROWS_P // tm
    grid = (ROWS // tm, n_cols // tn)
    has_dual = any(r[0] == "d" for r in rows)

    in_specs, args = [], []
    for r in rows:
        if r[0] == "u":
            _, arr, width, cb = r
            in_specs.append(pl.BlockSpec((tm, width), lambda i, j, cb=cb: (i, cb)))
            args.append(arr)
        else:
            _, arr_p, arr_s, width, cb = r
            in_specs.append(pl.BlockSpec((tm, width), lambda i, j, cb=cb: (jnp.minimum(i, npt - 1), cb)))
            in_specs.append(pl.BlockSpec((tm, width), lambda i, j, cb=cb: (jnp.maximum(i - npt, 0), cb)))
            args += [arr_p, arr_s]
    for c in consts:
        in_specs.append(pl.BlockSpec(c.shape, lambda i, j, nd=c.ndim: (0,) * nd))
        args.append(c)
    n_sub = max(tm // MOD_ROWS, 1)
    if mod is not None:
        mod_arr, mod_layer = mod
        for s in range(n_sub):
            in_specs.append(pl.BlockSpec((None, None, 6, D),
                                         lambda i, j, s=s: (mod_layer, _group_of_tile(i, tm, s), 0, 0)))
            args.append(mod_arr)
    in_specs.append(pl.BlockSpec((k_dim, tn), lambda i, j: (0, w_col0 // tn + j)))
    args.append(w)
    for e in erows:
        in_specs.append(pl.BlockSpec((tm, tn), lambda i, j: (i, j)))
        args.append(e)
    if emod is not None:
        emod_arr, emod_layer = emod
        in_specs.append(pl.BlockSpec((None, None, 6, tn), lambda i, j: (emod_layer, _group_of_tile(i, tm), 0, j)))
        args.append(emod_arr)

    out_specs = [pl.BlockSpec((tm, tn), lambda i, j: (i, j))]
    out_shape = [jax.ShapeDtypeStruct((ROWS, n_cols), out_dtype)]
    if emit_lhs:
        out_specs.append(pl.BlockSpec((tm, k_dim), lambda i, j: (i, 0)))
        out_shape.append(jax.ShapeDtypeStruct((ROWS, k_dim), BF16))

    def body(*refs):
        it = iter(refs)
        row_refs = [(next(it),) if r[0] == "u" else (next(it), next(it)) for r in rows]
        const_refs = [next(it) for _ in consts]
        mod_ref = [next(it) for _ in range(n_sub)] if mod is not None else None
        w_ref = next(it)
        erow_refs = [next(it) for _ in erows]
        emod_ref = next(it) if emod is not None else None
        out_ref = next(it)
        lhs_out = next(it) if emit_lhs else None
        hs = next(it)
        i = pl.program_id(0)
        j = pl.program_id(1)

        def fill(use_prompt):
            chosen = [rr[0] if (len(rr) == 1 or use_prompt) else rr[1] for rr in row_refs]
            prologue(_Pro(chosen, const_refs, mod_ref, i, hs, lhs_out))

        if has_dual:
            pl.when(jnp.logical_and(j == 0, i < npt))(lambda: fill(True))
            pl.when(jnp.logical_and(j == 0, i >= npt))(lambda: fill(False))
        else:
            pl.when(j == 0)(lambda: fill(True))

        acc = _dot(hs[...], w_ref[...].astype(BF16))
        if epilogue is not None:
            acc = epilogue(acc, erow_refs, emod_ref)
        out_ref[...] = acc.astype(out_dtype)

    res = pl.pallas_call(
        body,
        grid=grid,
        in_specs=in_specs,
        out_specs=out_specs,
        out_shape=out_shape,
        scratch_shapes=[pltpu.VMEM((tm, k_dim), BF16)],
        compiler_params=_cparams(("parallel", "arbitrary")),
        name=name,
    )(*args)
    return res if emit_lhs else res[0]


def _pro_normmod(p):
    sub = p.rows[0].shape[0] // len(p.mod)
    for s, m in enumerate(p.mod):
        rows = slice(sub * s, sub * (s + 1))
        h = _rms(p.rows[0][rows, :], p.consts[0][...]) * (1.0 + m[1:2, :]) + m[0:1, :]
        p.emit(h, rows=rows)


def _pro_cast(p):
    p.emit(p.rows[0][...])


def _pro_rms(p):
    p.emit(_rms(p.rows[0][...].astype(F32), p.consts[0][...]))


def _epi_residual(gate_row):
    def epi(acc, erows, emod):
        return erows[0][...] + emod[gate_row:gate_row + 1, :] * acc
    return epi


def _mlp_body(x_ref, g_ref, mod_ref, w1_ref, w2_ref, o_ref, hs, acc):
    f = pl.program_id(1)

    @pl.when(f == 0)
    def _():
        h = _rms(x_ref[...], g_ref[...]) * (1.0 + mod_ref[4:5, :]) + mod_ref[3:4, :]
        hs[...] = h.astype(BF16)
        acc[...] = jnp.zeros_like(acc)

    u = _dot(hs[...], w1_ref[...].astype(BF16))
    u = jnp.square(jnp.maximum(u, 0.0))
    acc[...] += _dot(u.astype(BF16), w2_ref[...].astype(BF16))

    @pl.when(f == pl.num_programs(1) - 1)
    def _():
        o_ref[...] = x_ref[...] + mod_ref[5:6, :] * acc[...]


def _mlp(x, layer, g, mod, w1, w2):
    tm, tf = 1024, 1024
    return pl.pallas_call(
        _mlp_body,
        grid=(ROWS // tm, D_FF // tf),
        in_specs=[
            pl.BlockSpec((tm, D), lambda i, f: (i, 0)),
            pl.BlockSpec((None, 1, D), lambda i, f: (layer, 0, 0)),
            pl.BlockSpec((None, None, 6, D), lambda i, f: (layer, _group_of_tile(i, tm), 0, 0)),
            pl.BlockSpec((None, D, tf), lambda i, f: (layer, 0, f)),
            pl.BlockSpec((None, tf, D), lambda i, f: (layer, f, 0)),
        ],
        out_specs=pl.BlockSpec((tm, D), lambda i, f: (i, 0)),
        out_shape=jax.ShapeDtypeStruct((ROWS, D), F32),
        scratch_shapes=[pltpu.VMEM((tm, D), BF16), pltpu.VMEM((tm, D), F32)],
        compiler_params=_cparams(("parallel", "arbitrary")),
        name="mlp",
    )(x, g, mod, w1, w2)


def _final_norm_body(x_ref, g_ref, o_ref):
    o_ref[...] = _rms(x_ref[...], g_ref[...])


def _final_norm(x, g, row0, nrows):
    tm = 1024
    b0 = row0 // tm
    return pl.pallas_call(
        _final_norm_body,
        grid=(nrows // tm,),
        in_specs=[pl.BlockSpec((tm, D), lambda i: (b0 + i, 0)), pl.BlockSpec((1, D), lambda i: (0, 0))],
        out_specs=pl.BlockSpec((tm, D), lambda i: (i, 0)),
        out_shape=jax.ShapeDtypeStruct((nrows, D), F32),
        compiler_params=_cparams(("parallel",)),
        name="final_norm",
    )(x, g)


SCAN_T = 256


def _seg_cumsums(a):
    n = a.shape[0]
    ii = lax.broadcasted_iota(jnp.int32, (SCAN_T, SCAN_T), 0)
    jj = lax.broadcasted_iota(jnp.int32, (SCAN_T, SCAN_T), 1)
    lower = jnp.where(ii >= jj, 1.0, 0.0).astype(BF16)
    upper = jnp.where(ii <= jj, 1.0, 0.0).astype(BF16)
    hi = a.astype(BF16)
    rest = a - hi.astype(F32)
    mid = rest.astype(BF16)
    lo = (rest - mid.astype(F32)).astype(BF16)
    parts = jnp.concatenate([hi, mid, lo], axis=1)
    pre, suf = [], []
    for c in range(n // SCAN_T):
        pc = parts[SCAN_T * c:SCAN_T * (c + 1), :]
        for tri, out in ((lower, pre), (upper, suf)):
            s3 = _dot(tri, pc)
            out.append(s3[:, :LANES] + s3[:, LANES:2 * LANES] + s3[:, 2 * LANES:])
    return jnp.concatenate(pre, axis=0), jnp.concatenate(suf, axis=0)


LOG2E = 1.4426950408889634


def _ssd_body(*refs, L, has_s0, emit_state):
    it = iter(refs)
    z_ref, x_ref, b_ref, c_ref, dt_ref = (next(it) for _ in range(5))
    cwx, cbx, cwb, cbb, cwc, cbc = (next(it) for _ in range(6))
    dtb_ref, alog_ref, dl_ref = next(it), next(it), next(it)
    s0_ref = next(it) if has_s0 else None
    y_ref = next(it)
    st_out = next(it) if emit_state else None
    padx, padb, xa, ba, ca, cbs, cumc, crp, yacc, st = (next(it) for _ in range(10))

    T = SCAN_T
    nc = L // T
    g0 = pl.program_id(1)

    def conv(in_ref, w_ref, bias_ref, pad):
        width = in_ref.shape[1]
        pad[0:8, :] = jnp.zeros((8, width), F32)
        pad[L + 8:L + 16, :] = jnp.zeros((8, width), F32)
        pad[8:L + 8, :] = in_ref[...].astype(F32)
        acc = bias_ref[...] + pad[6:6 + L, :] * w_ref[0:1, :]
        for k in range(1, CONV_W):
            acc = acc + pad[6 + k:6 + k + L, :] * w_ref[k:k + 1, :]
        return _silu(acc)

    xa[...] = conv(x_ref, cwx, cbx, padx)
    ba[...] = conv(b_ref, cwb, cbb, padb)
    ca[...] = conv(c_ref, cwc, cbc, padb)

    dt_all = _softplus(dt_ref[...] + dtb_ref[...])
    a_all = dt_all * (-jnp.exp(alog_ref[...]))
    shift = jnp.where(g0 == 0, 0, LANES - SSD_HPG * g0)
    l2dt = jnp.log2(pltpu.roll(dt_all, shift, 1))
    ar = pltpu.roll(a_all, shift, 1) * LOG2E
    cum_f, cum_b = _seg_cumsums(ar)
    cumc[0] = cum_f
    cumc[1] = cum_b
    crp[0] = (cum_f - l2dt).T
    crp[1] = (cum_b - l2dt).T

    yacc[...] = xa[...] * dl_ref[...]
    if has_s0:
        st[...] = s0_ref[...]

    ii = lax.broadcasted_iota(jnp.int32, (T, T), 0)
    jj = lax.broadcasted_iota(jnp.int32, (T, T), 1)

    def chunk(c, d):
        rows = slice(T * c, T * (c + 1))
        first = c == (0 if d == 0 else nc - 1)
        last = c == (nc - 1 if d == 0 else 0)
        zero_state = first and not has_s0
        need_state = emit_state or not last
        mask = (ii >= jj) if d == 0 else (ii <= jj)
        end = T * c + (T - 1 if d == 0 else 0)
        xav = xa[rows, :]
        xab = xav.astype(BF16)
        cab = ca[rows, :].astype(BF16)
        if need_state:
            xat = xav.T.astype(BF16)
            bat = ba[rows, :].T
        for r in range(SSD_HPG):
            lane = SSD_HEADS * d + r
            hs = slice(SSD_P * r, SSD_P * (r + 1))
            cc = cumc[d, rows, lane:lane + 1]
            cr = crp[d, lane:lane + 1, rows]
            dec = jnp.where(mask, jnp.exp2(cc - cr), 0.0)
            y = _dot((cbs[c] * dec).astype(BF16), xab[:, hs])
            if not zero_state:
                y = y + _dot_nt(cab, st[d, r].astype(BF16)) * jnp.exp2(cc)
            yacc[rows, hs] += y
            if need_state:
                tot = cumc[d, end:end + 1, lane:lane + 1]
                upd = _dot_nt(xat[hs, :], (bat * jnp.exp2(tot - cr)).astype(BF16))
                st[d, r] = upd if zero_state else jnp.exp2(tot) * st[d, r] + upd

    for c in range(nc):
        rows = slice(T * c, T * (c + 1))
        cbs[c] = _dot_nt(ca[rows, :].astype(BF16), ba[rows, :].astype(BF16))
    for c in range(nc):
        chunk(c, 0)
        chunk(nc - 1 - c, 1)

    y_ref[...] = (yacc[...] * _silu(z_ref[...].astype(F32))).astype(BF16)
    if emit_state:
        st_out[...] = st[...]


def _ssd_scan(p0, pdt, conv_w, conv_b, dtb, alog, dl, s0, *, nb, L, emit_state):
    rb0 = 0 if s0 is None else ROWS_P // L
    gw = SSD_HPG * SSD_P
    nc = L // SCAN_T
    x0 = SSD_DI // gw
    b0 = 2 * SSD_DI // SSD_N
    c0 = b0 + SSD_GROUPS
    wb0 = SSD_DI // SSD_N
    wc0 = wb0 + SSD_GROUPS
    in_specs = [
        pl.BlockSpec((L, gw), lambda b, g: (rb0 + b, g)),
        pl.BlockSpec((L, gw), lambda b, g: (rb0 + b, x0 + g)),
        pl.BlockSpec((L, SSD_N), lambda b, g: (rb0 + b, b0 + g)),
        pl.BlockSpec((L, SSD_N), lambda b, g: (rb0 + b, c0 + g)),
        pl.BlockSpec((L, LANES), lambda b, g: (rb0 + b, 0)),
        pl.BlockSpec((CONV_W, gw), lambda b, g: (0, g)),
        pl.BlockSpec((1, gw), lambda b, g: (0, g)),
        pl.BlockSpec((CONV_W, SSD_N), lambda b, g: (0, wb0 + g)),
        pl.BlockSpec((1, SSD_N), lambda b, g: (0, wb0 + g)),
        pl.BlockSpec((CONV_W, SSD_N), lambda b, g: (0, wc0 + g)),
        pl.BlockSpec((1, SSD_N), lambda b, g: (0, wc0 + g)),
        pl.BlockSpec((1, LANES), lambda b, g: (0, 0)),
        pl.BlockSpec((1, LANES), lambda b, g: (0, 0)),
        pl.BlockSpec((1, gw), lambda b, g: (0, g)),
    ]
    args = [p0, p0, p0, p0, pdt, conv_w, conv_b, conv_w, conv_b, conv_w, conv_b, dtb, alog, dl]
    state_spec = pl.BlockSpec((None, None, 2, SSD_HPG, SSD_P, SSD_N), lambda b, g: (b, 0, 0, g, 0, 0))
    if s0 is not None:
        in_specs.append(state_spec)
        args.append(s0)
    out_specs = [pl.BlockSpec((L, gw), lambda b, g: (b, g))]
    out_shape = [jax.ShapeDtypeStruct((nb * L, SSD_DI), BF16)]
    if emit_state:
        out_specs.append(state_spec)
        out_shape.append(jax.ShapeDtypeStruct((nb, 1, 2, SSD_HEADS, SSD_P, SSD_N), F32))
    scratch = [
        pltpu.VMEM((L + 16, gw), F32),
        pltpu.VMEM((L + 16, SSD_N), F32),
        pltpu.VMEM((L, gw), F32),
        pltpu.VMEM((L, SSD_N), F32),
        pltpu.VMEM((L, SSD_N), F32),
        pltpu.VMEM((nc, SCAN_T, SCAN_T), F32),
        pltpu.VMEM((2, L, LANES), F32),
        pltpu.VMEM((2, LANES, L), F32),
        pltpu.VMEM((L, gw), F32),
        pltpu.VMEM((2, SSD_HPG, SSD_P, SSD_N), F32),
    ]
    res = pl.pallas_call(
        functools.partial(_ssd_body, L=L, has_s0=s0 is not None, emit_state=emit_state),
        grid=(nb, SSD_GROUPS),
        in_specs=in_specs,
        out_specs=out_specs,
        out_shape=out_shape,
        scratch_shapes=scratch,
        compiler_params=_cparams(("parallel", "parallel")),
        name="ssd_scan_p" if s0 is None else "ssd_scan_s",
    )(*args)
    return res


TQ = 256
HP_P = 4
HP_S = 2


def _mla_body(*refs, L, sample):
    it = iter(refs)
    cq_ref, ckv_ref, krs_ref = next(it), next(it), next(it)
    gq_ref, gkv_ref = next(it), next(it)
    wq, wqs, wk, wv = (next(it) for _ in range(4))
    if sample:
        cckv_ref, ckr_ref, cq_t, sq_t, ck_t, sk_t = (next(it) for _ in range(6))
    o_ref = next(it)
    if not sample:
        ckv_out, kr_out = next(it), next(it)
    kk, vv = next(it), next(it)
    qt = pl.program_id(1)
    scale = (MLA_NOPE + MLA_ROPE) ** -0.5
    hb = 4 * LANES

    @pl.when(qt == 0)
    def _():
        ckv = _rms(ckv_ref[...].astype(F32), gkv_ref[...])
        kr_own = krs_ref[:, 0:LANES]
        if sample:
            keys = jnp.concatenate([cckv_ref[...], ckv], axis=0)
            kr_own = kr_own * ck_t[...] + krs_ref[:, LANES:2 * LANES] * sk_t[...]
            kr_all = jnp.concatenate([ckr_ref[...], kr_own], axis=0)
        else:
            ckv_out[...] = ckv
            kr_out[...] = kr_own[:, MLA_NOPE:MLA_NOPE + MLA_ROPE]
            keys = ckv
            kr_all = kr_own
        kb = keys.astype(BF16)
        for blk in range(MLA_HEADS * LANES // hb):
            cols = slice(hb * blk, hb * (blk + 1))
            kn = _dot(kb, wk[:, cols])
            kk[:, cols] = (kn + jnp.concatenate([kr_all] * 4, axis=1)).astype(BF16)
            vv[:, cols] = _dot(kb, wv[:, cols]).astype(BF16)

    cq = _rms(cq_ref[...].astype(F32), gq_ref[...]).astype(BF16)
    for blk in range(MLA_HEADS * LANES // hb):
        qa = _dot(cq, wq[:, hb * blk:hb * (blk + 1)])
        if sample:
            qs = _dot(cq, wqs[:, hb * blk:hb * (blk + 1)])
        pair = None
        for hh in range(4):
            h = 4 * blk + hh
            cols = slice(LANES * h, LANES * (h + 1))
            qh = qa[:, LANES * hh:LANES * (hh + 1)]
            if sample:
                qh = qh * cq_t[...] + qs[:, LANES * hh:LANES * (hh + 1)] * sq_t[...]
            s = _dot_nt(qh.astype(BF16), kk[:, cols]) * scale
            e = jnp.exp(s - jnp.max(s, axis=1, keepdims=True))
            o = _dot(e.astype(BF16), vv[:, cols]) / jnp.sum(e, axis=1, keepdims=True)
            if h % 2 == 0:
                pair = o
            else:
                o_ref[:, LANES * (h // 2):LANES * (h // 2 + 1)] = (pair + o).astype(BF16)


def _mla_attn(p1, p1s, gq, gkv, wq, wqs, wk, wv, ctx, *, nb, L):
    sample = ctx is not None
    nq = L // TQ
    tk = L + (PAST if sample else 0)
    rbq0 = ROWS_P // TQ if sample else 0
    rbs0 = ROWS_P // L if sample else 0
    ckv_blk = MLA_Q_RANK // MLA_KV_RANK

    def const(a):
        return pl.BlockSpec(a.shape, lambda b, q, nd=a.ndim: (0,) * nd)

    in_specs = [
        pl.BlockSpec((TQ, MLA_Q_RANK), lambda b, q: (rbq0 + b * nq + q, 0)),
        pl.BlockSpec((L, MLA_KV_RANK), lambda b, q: (rbs0 + b, ckv_blk)),
        pl.BlockSpec((L, 2 * LANES), lambda b, q: (rbs0 + b, 0)),
        const(gq), const(gkv), const(wq), const(wqs), const(wk), const(wv),
    ]
    args = [p1, p1, p1s, gq, gkv, wq, wqs, wk, wv]
    if sample:
        cckv, ckr, cpad, spad = ctx
        in_specs += [
            pl.BlockSpec((None, None, PAST, MLA_KV_RANK), lambda b, q: (b, 0, 0, 0)),
            pl.BlockSpec((None, None, PAST, LANES), lambda b, q: (b, 0, 0, 0)),
            pl.BlockSpec((TQ, LANES), lambda b, q: (q, 0)),
            pl.BlockSpec((TQ, LANES), lambda b, q: (q, 0)),
            const(cpad), const(spad),
        ]
        args += [cckv, ckr, cpad, spad, cpad, spad]
    out_specs = [pl.BlockSpec((TQ, MLA_HEADS * MLA_V), lambda b, q: (b * nq + q, 0))]
    out_shape = [jax.ShapeDtypeStruct((nb * L, MLA_HEADS * MLA_V), BF16)]
    if not sample:
        out_specs += [
            pl.BlockSpec((None, None, L, MLA_KV_RANK), lambda b, q: (b, 0, 0, 0)),
            pl.BlockSpec((None, None, L, MLA_ROPE), lambda b, q: (b, 0, 0, 0)),
        ]
        out_shape += [
            jax.ShapeDtypeStruct((nb, 1, L, MLA_KV_RANK), F32),
            jax.ShapeDtypeStruct((nb, 1, L, MLA_ROPE), F32),
        ]
    scratch = [
        pltpu.VMEM((tk, MLA_HEADS * LANES), BF16),
        pltpu.VMEM((tk, MLA_HEADS * LANES), BF16),
    ]
    return pl.pallas_call(
        functools.partial(_mla_body, L=L, sample=sample),
        grid=(nb, nq),
        in_specs=in_specs,
        out_specs=out_specs,
        out_shape=out_shape,
        scratch_shapes=scratch,
        compiler_params=_cparams(("parallel", "arbitrary")),
        name="mla_attn_s" if sample else "mla_attn_p",
    )(*args)


def _log_sigmoid(x):
    return -_softplus(-x)


def _mlstm_body(*refs, L, hp, has_s0, emit_state):
    it = iter(refs)
    q_ref, k_ref, v_ref, g_ref, gb_ref, ng_ref = (next(it) for _ in range(6))
    if has_s0:
        c0_ref, n0_ref, m0_ref = next(it), next(it), next(it)
    h_ref = next(it)
    if emit_state:
        c_out, n_out, m_out = next(it), next(it), next(it)
    gsc, gtr, kq, vts, kts, hacc, cst, nst, mst = (next(it) for _ in range(9))

    T = SCAN_T
    nc = L // T
    h0 = hp * pl.program_id(1)

    gts = g_ref[...] + gb_ref[...]
    gr = pltpu.roll(gts, jnp.where(h0 == 0, 0, LANES - h0), 1)
    b_f, b_b = _seg_cumsums(_log_sigmoid(gr))
    gsc[0] = gr
    gsc[1] = b_f
    gsc[2] = b_b
    gtr[0] = gr.T
    gtr[1] = b_f.T
    gtr[2] = b_b.T

    jj = lax.broadcasted_iota(jnp.int32, (T, T), 0)
    ii = lax.broadcasted_iota(jnp.int32, (T, T), 1)

    def chunk(hh, c, d):
        rows = slice(T * c, T * (c + 1))
        qcols = slice(ML_DQK * hh, ML_DQK * (hh + 1))
        first = c == (0 if d == 0 else nc - 1)
        last = c == (nc - 1 if d == 0 else 0)
        zero_state = first and not has_s0
        li = 2 * ML_HEADS * d + hh
        lb = li + ML_HEADS
        b_row = gtr[1 + d, lb:lb + 1, rows]
        logi_row = gtr[0, li:li + 1, rows]
        cj = gsc[1 + d, rows, lb:lb + 1] - gsc[0, rows, li:li + 1]
        mask = (jj <= ii) if d == 0 else (jj >= ii)
        dlog = jnp.where(mask, b_row - cj, -jnp.inf)
        m_prev = jnp.zeros((1, 1), F32) if zero_state else mst[hh, d][:, 0:1]
        inter = b_row + m_prev
        mcomb = jnp.maximum(inter, jnp.max(dlog, axis=0, keepdims=True))
        s = kq[hh, c] * jnp.exp(dlog - mcomb)
        vt = vts[hh, c]
        num = _dot(vt, s.astype(BF16))
        den = jnp.sum(s, axis=0, keepdims=True)
        if not zero_state:
            iw = jnp.exp(inter - mcomb)
            qt = q_ref[rows, qcols].T
            num = num + iw * _dot_tn(cst[hh, d].astype(BF16), qt)
            n8 = jnp.broadcast_to(nst[hh, d], (8, ML_DQK)).astype(BF16)
            den = den + iw * _dot(n8, qt)[0:1]
        hc = num / jnp.maximum(jnp.abs(den), jnp.exp(-mcomb))
        if d == 0:
            hacc[hh, :, rows] = hc
        else:
            hacc[hh, :, rows] += hc
        if emit_state or not last:
            end = T * c + (T - 1 if d == 0 else 0)
            bq = gtr[1 + d, lb:lb + 1, end:end + 1]
            wlog = bq - b_row + logi_row
            m_new = jnp.maximum(bq + m_prev, jnp.max(wlog, axis=1, keepdims=True))
            sw = jnp.exp(wlog - m_new)
            upd = _dot_nt((kts[hh, c].astype(F32) * sw).astype(BF16), vt)
            nsum = _dot(jnp.broadcast_to(sw, (8, T)).astype(BF16), k_ref[rows, qcols])[0:1]
            if zero_state:
                cst[hh, d] = upd
                nst[hh, d] = nsum
            else:
                cw = jnp.exp(bq + m_prev - m_new)
                cst[hh, d] = cw * cst[hh, d] + upd
                nst[hh, d] = cw * nst[hh, d] + nsum
            mst[hh, d] = jnp.broadcast_to(m_new, (1, LANES))

    for hh in range(hp):
        qcols = slice(ML_DQK * hh, ML_DQK * (hh + 1))
        vcols = slice(ML_DV * hh, ML_DV * (hh + 1))
        if has_s0:
            for d in range(2):
                cst[hh, d] = c0_ref[d, hh]
                nst[hh, d] = n0_ref[d, pl.ds(h0 + hh, 1), :]
                mst[hh, d] = m0_ref[d, pl.ds(h0 + hh, 1), :]
        for c in range(nc):
            rows = slice(T * c, T * (c + 1))
            kc = k_ref[rows, qcols]
            kq[hh, c] = _dot_nt(kc, q_ref[rows, qcols])
            kts[hh, c] = kc.T
            vts[hh, c] = v_ref[rows, vcols].T
        for c in range(nc):
            chunk(hh, c, 0)
        for c in range(nc):
            chunk(hh, nc - 1 - c, 1)
        ht = hacc[hh]
        r = lax.rsqrt(jnp.mean(ht * ht, axis=0, keepdims=True) + EPS)
        h_ref[:, vcols] = ((ht * r).T * ng_ref[:, vcols]).astype(BF16)
        if emit_state:
            for d in range(2):
                c_out[d, hh] = cst[hh, d]
                n_out[d, pl.ds(h0 + hh, 1), :] = nst[hh, d]
                m_out[d, pl.ds(h0 + hh, 1), :] = mst[hh, d]


def _mlstm_scan(q, k, v, gates, gb, ng, s0, *, nb, L, hp, emit_state):
    rb0 = 0 if s0 is None else ROWS_P // L
    in_specs = [
        pl.BlockSpec((L, hp * ML_DQK), lambda b, h: (rb0 + b, h)),
        pl.BlockSpec((L, hp * ML_DQK), lambda b, h: (rb0 + b, h)),
        pl.BlockSpec((L, hp * ML_DV), lambda b, h: (rb0 + b, h)),
        pl.BlockSpec((L, LANES), lambda b, h: (rb0 + b, 0)),
        pl.BlockSpec((1, LANES), lambda b, h: (0, 0)),
        pl.BlockSpec((1, hp * ML_DV), lambda b, h: (0, h)),
    ]
    args = [q, k, v, gates, gb, ng]
    c_spec = pl.BlockSpec((None, None, 2, hp, ML_DQK, ML_DV), lambda b, h: (b, 0, 0, h, 0, 0))
    n_spec = pl.BlockSpec((None, 2, ML_HEADS, LANES), lambda b, h: (b, 0, 0, 0))
    if s0 is not None:
        in_specs += [c_spec, n_spec, n_spec]
        args += list(s0)
    out_specs = [pl.BlockSpec((L, hp * ML_DV), lambda b, h: (b, h))]
    out_shape = [jax.ShapeDtypeStruct((nb * L, ML_DI), BF16)]
    if emit_state:
        out_specs += [c_spec, n_spec, n_spec]
        out_shape += [
            jax.ShapeDtypeStruct((nb, 1, 2, ML_HEADS, ML_DQK, ML_DV), F32),
            jax.ShapeDtypeStruct((nb, 2, ML_HEADS, LANES), F32),
            jax.ShapeDtypeStruct((nb, 2, ML_HEADS, LANES), F32),
        ]
    scratch = [
        pltpu.VMEM((3, L, LANES), F32),
        pltpu.VMEM((3, LANES, L), F32),
        pltpu.VMEM((hp, L // SCAN_T, SCAN_T, SCAN_T), F32),
        pltpu.VMEM((hp, L // SCAN_T, ML_DV, SCAN_T), BF16),
        pltpu.VMEM((hp, L // SCAN_T, ML_DQK, SCAN_T), BF16),
        pltpu.VMEM((hp, ML_DV, L), F32),
        pltpu.VMEM((hp, 2, ML_DQK, ML_DV), F32),
        pltpu.VMEM((hp, 2, 1, ML_DQK), F32),
        pltpu.VMEM((hp, 2, 1, LANES), F32),
    ]
    return pl.pallas_call(
        functools.partial(_mlstm_body, L=L, hp=hp, has_s0=s0 is not None, emit_state=emit_state),
        grid=(nb, ML_HEADS // hp),
        in_specs=in_specs,
        out_specs=out_specs,
        out_shape=out_shape,
        scratch_shapes=scratch,
        compiler_params=_cparams(("parallel", "arbitrary")),
        name="mlstm_scan_p" if s0 is None else "mlstm_scan_s",
    )(*args)


def _rope_blocks(x, ct, st):
    lane = lax.broadcasted_iota(jnp.int32, (x.shape[0], LANES), 1)
    first_half = lane % DF_D < DF_D // 2
    outs = []
    for blk in range(x.shape[1] // LANES):
        xb = x[:, LANES * blk:LANES * (blk + 1)]
        swapped = jnp.where(first_half, pltpu.roll(xb, LANES - DF_D // 2, 1), pltpu.roll(xb, DF_D // 2, 1))
        outs.append(xb * ct + swapped * st)
    return jnp.concatenate(outs, axis=1)


def _diff_body(*refs, L, sample, lam_init):
    it = iter(refs)
    q_ref, k_ref, v_ref = next(it), next(it), next(it)
    lq1, lk1, lq2, lk2, sg_ref = (next(it) for _ in range(5))
    if sample:
        ck_ref, cv_ref, cq_t, sq_t, ck_t, sk_t = (next(it) for _ in range(6))
    o_ref = next(it)
    if sample:
        ka, va = next(it), next(it)
    else:
        k_out, v_out = next(it), next(it)
    qt = pl.program_id(1)

    lam = (jnp.exp(jnp.sum(lq1[...] * lk1[...], axis=1, keepdims=True))
           - jnp.exp(jnp.sum(lq2[...] * lk2[...], axis=1, keepdims=True)) + lam_init)

    @pl.when(qt == 0)
    def _():
        if sample:
            ka[0:PAST, :] = ck_ref[...].astype(BF16)
            ka[PAST:PAST + L, :] = _rope_blocks(k_ref[...].astype(F32), ck_t[...], sk_t[...]).astype(BF16)
            va[0:PAST, :] = cv_ref[...].astype(BF16)
            va[PAST:PAST + L, :] = v_ref[...]
        else:
            k_out[...] = k_ref[...].astype(F32)
            v_out[...] = v_ref[...].astype(F32)

    q = q_ref[...].astype(F32)
    if sample:
        q = _rope_blocks(q, cq_t[...], sq_t[...])
    q = q * (DF_D ** -0.5)
    lo = lax.broadcasted_iota(jnp.int32, (TQ, LANES), 1) < DF_D

    def attend(qm, kh, vh):
        s = _dot_nt(qm, kh)
        e = jnp.exp(s - jnp.max(s, axis=1, keepdims=True))
        return _dot(e.astype(BF16), vh) / jnp.sum(e, axis=1, keepdims=True)

    for h in range(DF_HEADS):
        cols = slice(LANES * h, LANES * (h + 1))
        qh = q[:, cols]
        if sample:
            kh, vh = ka[:, cols], va[:, cols]
        else:
            kh, vh = k_ref[:, cols], v_ref[:, cols]
        a0 = attend(jnp.where(lo, qh, 0.0).astype(BF16), kh, vh)
        a1 = attend(jnp.where(lo, 0.0, qh).astype(BF16), kh, vh)
        o = a0 - lam * a1
        o_ref[:, cols] = (_rms(o, sg_ref[...]) * (1.0 - lam_init)).astype(BF16)


def _diff_attn(p3, lq1, lk1, lq2, lk2, sg, ctx, *, nb, L, lam_init):
    sample = ctx is not None
    nq = L // TQ
    rbq0 = ROWS_P // TQ if sample else 0
    rbs0 = ROWS_P // L if sample else 0

    def const(a):
        return pl.BlockSpec(a.shape, lambda b, q, nd=a.ndim: (0,) * nd)

    in_specs = [
        pl.BlockSpec((TQ, D), lambda b, q: (rbq0 + b * nq + q, 0)),
        pl.BlockSpec((L, D), lambda b, q: (rbs0 + b, 1)),
        pl.BlockSpec((L, D), lambda b, q: (rbs0 + b, 2)),
        const(lq1), const(lk1), const(lq2), const(lk2), const(sg),
    ]
    args = [p3, p3, p3, lq1, lk1, lq2, lk2, sg]
    kv_spec = pl.BlockSpec((None, None, PAST if sample else L, D), lambda b, q: (b, 0, 0, 0))
    if sample:
        ck, cv, c128, s128 = ctx
        in_specs += [
            kv_spec, kv_spec,
            pl.BlockSpec((TQ, LANES), lambda b, q: (q, 0)),
            pl.BlockSpec((TQ, LANES), lambda b, q: (q, 0)),
            const(c128), const(s128),
        ]
        args += [ck, cv, c128, s128, c128, s128]
    out_specs = [pl.BlockSpec((TQ, D), lambda b, q: (b * nq + q, 0))]
    out_shape = [jax.ShapeDtypeStruct((nb * L, D), BF16)]
    scratch = []
    if sample:
        scratch = [pltpu.VMEM((PAST + L, D), BF16), pltpu.VMEM((PAST + L, D), BF16)]
    else:
        out_specs += [kv_spec, kv_spec]
        out_shape += [jax.ShapeDtypeStruct((nb, 1, L, D), F32)] * 2
    return pl.pallas_call(
        functools.partial(_diff_body, L=L, sample=sample, lam_init=lam_init),
        grid=(nb, nq),
        in_specs=in_specs,
        out_specs=out_specs,
        out_shape=out_shape,
        scratch_shapes=scratch,
        compiler_params=_cparams(("parallel", "arbitrary")),
        name="diff_attn_s" if sample else "diff_attn_p",
    )(*args)


def _pro_mlstm_conv(p):
    x_ref = p.rows[0]
    w_ref, b_ref = p.consts
    tm = x_ref.shape[0]
    seq = jnp.where(p.i < ROWS_P // tm, L_PROMPT, L_SAMPLE)
    cw = 256
    pos = lax.broadcasted_iota(jnp.int32, (tm, cw), 0) & (seq - 1)
    for cb in range(ML_DI // cw):
        cols = slice(cw * cb, cw * (cb + 1))
        x = x_ref[:, cols].astype(F32)
        acc = None
        for k in range(CONV_W):
            off = k - CONV_W // 2
            src = pos + off
            tap = x if off == 0 else pltpu.roll(x, (-off) % tm, 0)
            tap = jnp.where(jnp.logical_and(src >= 0, src < seq), tap, 0.0) * w_ref[k:k + 1, cols]
            acc = b_ref[:, cols] + tap if acc is None else acc + tap
        p.emit(_silu(acc), cols)


def _pro_mlstm_gate(p):
    hn, xc, z = (r[...].astype(F32) for r in p.rows)
    p.emit((hn + p.consts[0][...] * xc) * _silu(z))


def _rope_tables(d):
    rows = L_SAMPLE // GRID_W
    pos_r = jnp.repeat(jnp.arange(rows, dtype=F32), GRID_W)
    pos_c = jnp.tile(jnp.arange(GRID_W, dtype=F32), rows)
    nf = d // 4
    inv = ROPE_BASE ** (-jnp.arange(nf, dtype=F32) / nf)
    ang = jnp.concatenate([pos_r[:, None] * inv, pos_c[:, None] * inv], axis=-1)
    cos, sin = jnp.cos(ang), jnp.sin(ang)
    return jnp.concatenate([cos, cos], axis=-1), jnp.concatenate([-sin, sin], axis=-1)


def _pad_cols(a, n):
    return jnp.pad(a, ((0, 0), (0, n - a.shape[1])))


def kernel(x_prompt, x_sample, state_ssd, cache_mla_ckv, cache_mla_krope, state_mlstm_C, state_mlstm_n, state_mlstm_m, cache_diff_k, cache_diff_v, c, c_ctx, norm1_g, norm2_g, ada_w, ada_b, mlp_w1, mlp_w2, final_g, ssd_w_in, ssd_conv_w, ssd_conv_b, ssd_dt_bias, ssd_A_log, ssd_D, ssd_norm_g, ssd_w_out, mla_w_in, mla_q_norm_g, mla_kv_norm_g, mla_w_uq, mla_w_ukv, mla_w_o, mlstm_w_up, mlstm_conv_w, mlstm_conv_b, mlstm_gate_b, mlstm_w_q, mlstm_w_k, mlstm_w_v, mlstm_skip, mlstm_norm_g, mlstm_w_down, diff_w_qkv, diff_lq1, diff_lk1, diff_lq2, diff_lk2, diff_subln_g, diff_w_o):
    x = jnp.concatenate([x_prompt.reshape(ROWS_P, D), x_sample.reshape(ROWS_S, D)], axis=0)
    cvec = jnp.concatenate([c_ctx[None, :], c, jnp.zeros((5, D), F32)], axis=0)
    mod_all = _ada_mod(cvec, ada_w, ada_b)
    g2 = norm2_g.reshape(DEPTH, 1, D)

    def in_proj(xin, layer, w, n_cols, tn, w_col0=0, out_dtype=BF16, name="in_proj"):
        return _fused_mm(rows=[("u", xin, D, 0)], consts=[norm1_g[layer][None, :]], mod=(mod_all, layer), w=w, k_dim=D,
                         n_cols=n_cols, tm=2048, tn=tn, prologue=_pro_normmod, w_col0=w_col0, out_dtype=out_dtype,
                         name=name)

    def out_proj(xin, layer, rows, consts, prologue, w, k_dim, name):
        return _fused_mm(rows=rows, consts=consts, w=w, k_dim=k_dim, n_cols=D, tm=1024, tn=512, prologue=prologue,
                         epilogue=_epi_residual(2), erows=[xin], emod=(mod_all, layer), out_dtype=F32, name=name)

    p0 = in_proj(x, 0, ssd_w_in[0], 3 * SSD_DI, 512, name="ssd_in")
    pdt = in_proj(x, 0, _pad_cols(ssd_w_in[0][:, 3 * SSD_DI:], LANES), LANES, LANES, out_dtype=F32, name="ssd_in_dt")
    dtb = _pad_cols(ssd_dt_bias[0].reshape(1, 2 * SSD_HEADS), LANES)
    alog = _pad_cols(ssd_A_log[0].reshape(1, 2 * SSD_HEADS), LANES)
    dl = jnp.repeat(ssd_D[0], SSD_P)[None, :]
    scan_args = (p0, pdt, ssd_conv_w[0], ssd_conv_b[0][None, :], dtb, alog, dl)
    yg_p, new_ssd = _ssd_scan(*scan_args, None, nb=N_PROMPT_SEQ, L=L_PROMPT, emit_state=True)
    (yg_s,) = _ssd_scan(*scan_args, state_ssd, nb=N_SAMPLE_SEQ, L=L_SAMPLE, emit_state=False)
    x = out_proj(x, 0, [("d", yg_p, yg_s, SSD_DI, 0)], [ssd_norm_g[0][None, :]], _pro_rms, ssd_w_out[0], SSD_DI,
                 "ssd_out")
    x = _mlp(x, 0, g2, mod_all, mlp_w1, mlp_w2)

    w_in = mla_w_in[0]
    kr0 = MLA_Q_RANK + MLA_KV_RANK
    half = MLA_ROPE // 2
    zk = jnp.zeros((D, MLA_NOPE), F32)
    zr = jnp.zeros((D, LANES - MLA_NOPE - MLA_ROPE), F32)
    w_kr = jnp.concatenate([zk, w_in[:, kr0:kr0 + MLA_ROPE], zr,
                            zk, w_in[:, kr0 + half:kr0 + MLA_ROPE], w_in[:, kr0:kr0 + half], zr], axis=1)
    p1 = in_proj(x, 1, w_in, kr0, 256, name="mla_in")
    p1s = in_proj(x, 1, w_kr, 2 * LANES, 2 * LANES, out_dtype=F32, name="mla_in_kr")
    wuq = mla_w_uq[0].reshape(MLA_Q_RANK, MLA_HEADS, MLA_NOPE + MLA_ROPE)
    zq = jnp.zeros((MLA_Q_RANK, MLA_HEADS, LANES - MLA_NOPE - MLA_ROPE), F32)
    zqn = jnp.zeros((MLA_Q_RANK, MLA_HEADS, MLA_NOPE), F32)
    wq = jnp.concatenate([wuq, zq], axis=-1).reshape(MLA_Q_RANK, MLA_HEADS * LANES).astype(BF16)
    wqs = jnp.concatenate([zqn, wuq[..., MLA_NOPE + half:], wuq[..., MLA_NOPE:MLA_NOPE + half], zq],
                          axis=-1).reshape(MLA_Q_RANK, MLA_HEADS * LANES).astype(BF16)
    wukv = mla_w_ukv[0].reshape(MLA_KV_RANK, MLA_HEADS, MLA_NOPE + MLA_V)
    zkv = jnp.zeros((MLA_KV_RANK, MLA_HEADS, MLA_NOPE), F32)
    wk = jnp.concatenate([wukv[..., :MLA_NOPE], zkv], axis=-1).reshape(MLA_KV_RANK, MLA_HEADS * LANES).astype(BF16)
    wv_own = wukv[..., MLA_NOPE:]
    odd = (jnp.arange(MLA_HEADS) % 2 == 1)[None, :, None]
    wv = jnp.where(odd, jnp.concatenate([zkv, wv_own], axis=-1), jnp.concatenate([wv_own, zkv], axis=-1))
    wv = wv.reshape(MLA_KV_RANK, MLA_HEADS * LANES).astype(BF16)
    c32, s32 = _rope_tables(MLA_ROPE)
    tz = jnp.zeros((L_SAMPLE, LANES - MLA_NOPE - MLA_ROPE), F32)
    cpad = jnp.concatenate([jnp.ones((L_SAMPLE, MLA_NOPE), F32), c32, tz], axis=1)
    spad = jnp.concatenate([jnp.zeros((L_SAMPLE, MLA_NOPE), F32), s32, tz], axis=1)
    ckr_pad = jnp.pad(cache_mla_krope, ((0, 0), (0, 0), (0, 0), (MLA_NOPE, LANES - MLA_NOPE - MLA_ROPE)))
    mla_w = (mla_q_norm_g[0][None, :], mla_kv_norm_g[0][None, :], wq, wqs, wk, wv)
    o_p, new_ckv, new_kr = _mla_attn(p1, p1s, *mla_w, None, nb=N_PROMPT_SEQ, L=L_PROMPT)
    (o_s,) = _mla_attn(p1, p1s, *mla_w, (cache_mla_ckv, ckr_pad, cpad, spad), nb=N_SAMPLE_SEQ, L=L_SAMPLE)
    x = out_proj(x, 1, [("d", o_p, o_s, D, 0)], [], _pro_cast, mla_w_o[0], D, "mla_out")
    x = _mlp(x, 1, g2, mod_all, mlp_w1, mlp_w2)

    p2 = in_proj(x, 2, mlstm_w_up[0], 2 * ML_DI, 512, name="mlstm_up")
    gates = in_proj(x, 2, _pad_cols(mlstm_w_up[0][:, 2 * ML_DI:], LANES), LANES, LANES, out_dtype=F32, name="mlstm_up_gates")
    q, xc = _fused_mm(rows=[("u", p2, ML_DI, 0)], consts=[mlstm_conv_w[0], mlstm_conv_b[0][None, :]], w=mlstm_w_q[0],
                      k_dim=ML_DI, n_cols=ML_HEADS * ML_DQK, tm=1024, tn=256, prologue=_pro_mlstm_conv, emit_lhs=True,
                      name="mlstm_q")
    k = _fused_mm(rows=[("u", xc, ML_DI, 0)], w=mlstm_w_k[0], k_dim=ML_DI, n_cols=ML_HEADS * ML_DQK, tm=1024, tn=512,
                  prologue=_pro_cast, epilogue=lambda acc, e, m: acc * (ML_DQK ** -0.5), name="mlstm_k")
    v = _fused_mm(rows=[("u", p2, ML_DI, 0)], w=mlstm_w_v[0], k_dim=ML_DI, n_cols=ML_DI, tm=1024, tn=512,
                  prologue=_pro_cast, name="mlstm_v")
    gb = _pad_cols(mlstm_gate_b[0].reshape(1, 4 * ML_HEADS), LANES)
    ng = mlstm_norm_g[0][None, :]
    n0 = _pad_cols(state_mlstm_n[:, 0].reshape(N_SAMPLE_SEQ * 2 * ML_HEADS, ML_DQK), LANES).reshape(
        N_SAMPLE_SEQ, 2, ML_HEADS, LANES)
    m0 = jnp.broadcast_to(state_mlstm_m[:, 0][..., None], (N_SAMPLE_SEQ, 2, ML_HEADS, LANES))
    hn_p, new_c, new_n, new_m = _mlstm_scan(q, k, v, gates, gb, ng, None, nb=N_PROMPT_SEQ, L=L_PROMPT, hp=HP_P,
                                            emit_state=True)
    (hn_s,) = _mlstm_scan(q, k, v, gates, gb, ng, (state_mlstm_C, n0, m0), nb=N_SAMPLE_SEQ, L=L_SAMPLE, hp=HP_S,
                          emit_state=False)
    x = out_proj(x, 2, [("d", hn_p, hn_s, ML_DI, 0), ("u", xc, ML_DI, 0), ("u", p2, ML_DI, 1)],
                 [mlstm_skip[0][None, :]], _pro_mlstm_gate, mlstm_w_down[0], ML_DI, "mlstm_down")
    x = _mlp(x, 2, g2, mod_all, mlp_w1, mlp_w2)

    lam_init = 0.8 - 0.6 * math.exp(-0.3 * 3)
    p3 = in_proj(x, 3, diff_w_qkv[0], 3 * D, 512, name="diff_qkv")
    c64, s64 = _rope_tables(DF_D)
    c128 = jnp.concatenate([c64, c64], axis=-1)
    s128 = jnp.concatenate([s64, s64], axis=-1)
    dparams = (diff_lq1, diff_lk1, diff_lq2, diff_lk2, diff_subln_g)
    od_p, new_dk, new_dv = _diff_attn(p3, *dparams, None, nb=N_PROMPT_SEQ, L=L_PROMPT, lam_init=lam_init)
    ctx = (cache_diff_k.reshape(N_SAMPLE_SEQ, 1, PAST, D), cache_diff_v.reshape(N_SAMPLE_SEQ, 1, PAST, D), c128, s128)
    (od_s,) = _diff_attn(p3, *dparams, ctx, nb=N_SAMPLE_SEQ, L=L_SAMPLE, lam_init=lam_init)
    x = out_proj(x, 3, [("d", od_p, od_s, D, 0)], [], _pro_cast, diff_w_o[0], D, "diff_out")
    x = _mlp(x, 3, g2, mod_all, mlp_w1, mlp_w2)

    fg = final_g[None, :]
    y_prompt = _final_norm(x, fg, 0, ROWS_P).reshape(N_PROMPT_SEQ, L_PROMPT, D)
    y_sample = _final_norm(x, fg, ROWS_P, ROWS_S).reshape(N_SAMPLE_SEQ, L_SAMPLE, D)
    return (y_prompt, y_sample, new_ssd, new_ckv, new_kr, new_c,
            new_n[None].reshape(N_PROMPT_SEQ, 1, 2, ML_HEADS, ML_DQK),
            new_m[..., 0].reshape(N_PROMPT_SEQ, 1, 2, ML_HEADS),
            new_dk.reshape(N_PROMPT_SEQ, 1, L_PROMPT, DF_HEADS, 2 * DF_D),
            new_dv.reshape(N_PROMPT_SEQ, 1, L_PROMPT, DF_HEADS, 2 * DF_D))
```

```python
import functools
import math

import jax
import jax.numpy as jnp
from jax import lax
from jax.experimental import pallas as pl
from jax.experimental.pallas import tpu as pltpu

F32 = jnp.float32
BF16 = jnp.bfloat16

D = 1024
DEPTH = 4
D_FF = 4 * D
EPS = 1e-6
ROPE_BASE = 10000.0
CONV_W = 5
GRID_W = 64

N_PROMPT_SEQ = 32
L_PROMPT = 256
N_SAMPLE_SEQ = 2
L_SAMPLE = 1024
PAST = 256
ROWS_P = N_PROMPT_SEQ * L_PROMPT
ROWS_S = N_SAMPLE_SEQ * L_SAMPLE
ROWS = ROWS_P + ROWS_S

SSD_DI = 2 * D
SSD_HEADS = 32
SSD_P = 64
SSD_GROUPS = 8
SSD_N = 128
SSD_HPG = SSD_HEADS // SSD_GROUPS

MLA_HEADS = 16
MLA_Q_RANK = 512
MLA_KV_RANK = 256
MLA_NOPE = 64
MLA_ROPE = 32
MLA_V = 64

ML_DI = 2 * D
ML_HEADS = 8
ML_DQK = 128
ML_DV = 256

DF_HEADS = 8
DF_D = 64

LANES = 128
VMEM_LIMIT = 56 * 1024 * 1024


def _cparams(sem):
    return pltpu.CompilerParams(dimension_semantics=sem, vmem_limit_bytes=VMEM_LIMIT)


def _silu(x):
    return x * jax.nn.sigmoid(x)


def _softplus(x):
    return jnp.maximum(x, 0.0) + jnp.log1p(jnp.exp(-jnp.abs(x)))


def _rms(x, g):
    r = lax.rsqrt(jnp.mean(x * x, axis=-1, keepdims=True) + EPS)
    return (x * r) * g


def _dot(a, b):
    return jnp.dot(a, b, preferred_element_type=F32)


def _dot_nt(a, b):
    return lax.dot_general(a, b, (((1,), (1,)), ((), ())), preferred_element_type=F32)


def _dot_tn(a, b):
    return lax.dot_general(a, b, (((0,), (0,)), ((), ())), preferred_element_type=F32)


MOD_ROWS = 1024


def _group_of_tile(i, tm, sub=0):
    row0 = i * tm + sub * MOD_ROWS
    return jnp.where(row0 < ROWS_P, 0, 1 + (row0 - ROWS_P) // L_SAMPLE)


def _ada_body(c_ref, w_ref, b_ref, o_ref):
    s = _silu(c_ref[...]).astype(BF16)
    o_ref[...] = _dot(s, w_ref[...].astype(BF16)) + b_ref[...]


def _ada_mod(cvec, ada_w, ada_b):
    tn = 1536
    out = pl.pallas_call(
        _ada_body,
        grid=(DEPTH, 6 * D // tn),
        in_specs=[
            pl.BlockSpec((8, D), lambda l, j: (0, 0)),
            pl.BlockSpec((None, D, tn), lambda l, j: (l, 0, j)),
            pl.BlockSpec((None, 1, tn), lambda l, j: (l, 0, j)),
        ],
        out_specs=pl.BlockSpec((None, 8, tn), lambda l, j: (l, 0, j)),
        out_shape=jax.ShapeDtypeStruct((DEPTH, 8, 6 * D), F32),
        compiler_params=_cparams(("parallel", "parallel")),
        name="ada_mod",
    )(cvec, ada_w, ada_b.reshape(DEPTH, 1, 6 * D))
    return out[:, :3].reshape(DEPTH, 3, 6, D)


class _Pro:
    def __init__(self, rows, consts, mod, i, hs, lhs_out):
        self.rows, self.consts, self.mod, self.i = rows, consts, mod, i
        self._hs, self._lhs_out = hs, lhs_out

    def emit(self, val, rows=slice(None), cols=slice(None)):
        vb = val.astype(BF16)
        self._hs[rows, cols] = vb
        if self._lhs_out is not None:
            self._lhs_out[rows, cols] = vb


ROW_CHUNK = 512


def _fused_mm(*, rows, w, k_dim, n_cols, tm, tn, prologue=None, whole_tile_prologue=False, consts=(), mod=None,
              epilogue=None, erows=(), emod=None, w_col0=0, emit_lhs=False, extra_w=None, out_dtype=BF16, name):
    npt = ROWS_P // tm
    grid = (ROWS // tm, n_cols // tn)
    has_dual = any(r[0] == "d" for r in rows)
    if prologue is None:
        assert len(rows) == 1 and rows[0][0] == "u" and not emit_lhs

    in_specs, args = [], []
    for r in rows:
        if r[0] == "u":
            _, arr, width, cb = r
            in_specs.append(pl.BlockSpec((tm, width), lambda i, j, cb=cb: (i, cb)))
            args.append(arr)
        else:
            _, arr_p, arr_s, width, cb = r
            in_specs.append(pl.BlockSpec((tm, width), lambda i, j, cb=cb: (jnp.minimum(i, npt - 1), cb)))
            in_specs.append(pl.BlockSpec((tm, width), lambda i, j, cb=cb: (jnp.maximum(i - npt, 0), cb)))
            args += [arr_p, arr_s]
    for c in consts:
        in_specs.append(pl.BlockSpec(c.shape, lambda i, j, nd=c.ndim: (0,) * nd))
        args.append(c)
    n_sub = max(tm // MOD_ROWS, 1)
    if mod is not None:
        mod_arr, mod_layer = mod
        for s in range(n_sub):
            in_specs.append(pl.BlockSpec((None, None, 6, D),
                                         lambda i, j, s=s: (mod_layer, _group_of_tile(i, tm, s), 0, 0)))
            args.append(mod_arr)
    in_specs.append(pl.BlockSpec((k_dim, tn), lambda i, j: (0, w_col0 // tn + j)))
    args.append(w)
    if extra_w is not None:
        in_specs.append(pl.BlockSpec(extra_w.shape, lambda i, j: (0, 0)))
        args.append(extra_w)
    for e in erows:
        in_specs.append(pl.BlockSpec((tm, tn), lambda i, j: (i, j)))
        args.append(e)
    if emod is not None:
        emod_arr, emod_layer = emod
        in_specs.append(pl.BlockSpec((None, None, 6, tn), lambda i, j: (emod_layer, _group_of_tile(i, tm), 0, j)))
        args.append(emod_arr)

    out_specs = [pl.BlockSpec((tm, tn), lambda i, j: (i, j))]
    out_shape = [jax.ShapeDtypeStruct((ROWS, n_cols), out_dtype)]
    if emit_lhs:
        out_specs.append(pl.BlockSpec((tm, k_dim), lambda i, j: (i, 0)))
        out_shape.append(jax.ShapeDtypeStruct((ROWS, k_dim), BF16))
    if extra_w is not None:
        out_specs.append(pl.BlockSpec((tm, extra_w.shape[1]), lambda i, j: (i, 0)))
        out_shape.append(jax.ShapeDtypeStruct((ROWS, extra_w.shape[1]), F32))
    scratch = [] if prologue is None else [pltpu.VMEM((tm, k_dim), BF16)]
    chunks = [slice(r, r + ROW_CHUNK) for r in range(0, tm, ROW_CHUNK)]

    def body(*refs):
        it = iter(refs)
        row_refs = [(next(it),) if r[0] == "u" else (next(it), next(it)) for r in rows]
        const_refs = [next(it) for _ in consts]
        mod_ref = [next(it) for _ in range(n_sub)] if mod is not None else None
        w_ref = next(it)
        extra_ref = next(it) if extra_w is not None else None
        erow_refs = [next(it) for _ in erows]
        emod_ref = next(it) if emod is not None else None
        out_ref = next(it)
        lhs_out = next(it) if emit_lhs else None
        extra_out = next(it) if extra_w is not None else None
        hs = next(it) if prologue is not None else None
        i = pl.program_id(0)
        j = pl.program_id(1)

        def compute(first, use_prompt=True):
            chosen = [rr[0] if (len(rr) == 1 or use_prompt) else rr[1] for rr in row_refs]
            p = _Pro(chosen, const_refs, mod_ref, i, hs, lhs_out)
            wb = w_ref[...].astype(BF16)
            if first and whole_tile_prologue:
                prologue(p, slice(0, tm))
            for rs in chunks:
                if first and not whole_tile_prologue:
                    prologue(p, rs)
                lhs = chosen[0][rs, :] if prologue is None else hs[rs, :]
                acc = _dot(lhs, wb)
                if epilogue is not None:
                    acc = epilogue(acc, erow_refs, emod_ref, rs)
                out_ref[rs, :] = acc.astype(out_dtype)
                if first and extra_w is not None:
                    extra_out[rs, :] = _dot(lhs, extra_ref[...].astype(BF16))

        if prologue is None:
            if extra_w is None:
                compute(False)
            else:
                pl.when(j == 0)(lambda: compute(True))
                pl.when(j > 0)(lambda: compute(False))
        else:
            if has_dual:
                pl.when(jnp.logical_and(j == 0, i < npt))(lambda: compute(True, True))
                pl.when(jnp.logical_and(j == 0, i >= npt))(lambda: compute(True, False))
            else:
                pl.when(j == 0)(lambda: compute(True))
            pl.when(j > 0)(lambda: compute(False))

    res = pl.pallas_call(
        body,
        grid=grid,
        in_specs=in_specs,
        out_specs=out_specs,
        out_shape=out_shape,
        scratch_shapes=scratch,
        compiler_params=_cparams(("parallel", "arbitrary")),
        name=name,
    )(*args)
    return res if len(res) > 1 else res[0]


def _pro_normmod(p, rs):
    m = p.mod[rs.start // MOD_ROWS]
    h = _rms(p.rows[0][rs, :], p.consts[0][...]) * (1.0 + m[1:2, :]) + m[0:1, :]
    p.emit(h, rs)


def _pro_cast(p, rs):
    p.emit(p.rows[0][rs, :], rs)


def _pro_rms(p, rs):
    p.emit(_rms(p.rows[0][rs, :].astype(F32), p.consts[0][...]), rs)


def _epi_residual(gate_row):
    def epi(acc, erows, emod, rs):
        return erows[0][rs, :] + emod[gate_row:gate_row + 1, :] * acc
    return epi


def _mlp_body(*refs, final):
    it = iter(refs)
    x_ref, g_ref, mod_ref, w1_ref, w2_ref = (next(it) for _ in range(5))
    fg_ref = next(it) if final else None
    outs = [next(it), next(it)] if final else [next(it)]
    hs, acc = next(it), next(it)
    i = pl.program_id(0)
    f = pl.program_id(1)
    nf = pl.num_programs(1)
    tm = x_ref.shape[0]
    chunks = [slice(r, r + ROW_CHUNK) for r in range(0, tm, ROW_CHUNK)]

    def step(first, last):
        w1b = w1_ref[...].astype(BF16)
        w2b = w2_ref[...].astype(BF16)
        for rs in chunks:
            if first:
                h = _rms(x_ref[rs, :], g_ref[...]) * (1.0 + mod_ref[4:5, :]) + mod_ref[3:4, :]
                hs[rs, :] = h.astype(BF16)
            u = jnp.square(jnp.maximum(_dot(hs[rs, :], w1b), 0.0))
            tot = _dot(u.astype(BF16), w2b)
            if not first:
                tot = acc[rs, :] + tot
            if not last:
                acc[rs, :] = tot
                continue
            y = x_ref[rs, :] + mod_ref[5:6, :] * tot
            if not final:
                outs[0][rs, :] = y
            else:
                y = _rms(y, fg_ref[...])
                npt = ROWS_P // tm

                @pl.when(i < npt)
                def _():
                    outs[0][rs, :] = y

                @pl.when(i >= npt)
                def _():
                    outs[1][rs, :] = y

    pl.when(f == 0)(lambda: step(True, False))
    pl.when(jnp.logical_and(f > 0, f < nf - 1))(lambda: step(False, False))
    pl.when(f == nf - 1)(lambda: step(False, True))


def _mlp(x, layer, g, mod, w1, w2, final_g=None):
    tm, tf = 1024, 1024
    npt = ROWS_P // tm
    final = final_g is not None
    in_specs = [
        pl.BlockSpec((tm, D), lambda i, f: (i, 0)),
        pl.BlockSpec((None, 1, D), lambda i, f: (layer, 0, 0)),
        pl.BlockSpec((None, None, 6, D), lambda i, f: (layer, _group_of_tile(i, tm), 0, 0)),
        pl.BlockSpec((None, D, tf), lambda i, f: (layer, 0, f)),
        pl.BlockSpec((None, tf, D), lambda i, f: (layer, f, 0)),
    ]
    args = [x, g, mod, w1, w2]
    if final:
        in_specs.append(pl.BlockSpec((1, D), lambda i, f: (0, 0)))
        args.append(final_g)
        out_specs = [pl.BlockSpec((tm, D), lambda i, f: (jnp.minimum(i, npt - 1), 0)),
                     pl.BlockSpec((tm, D), lambda i, f: (jnp.maximum(i - npt, 0), 0))]
        out_shape = [jax.ShapeDtypeStruct((ROWS_P, D), F32), jax.ShapeDtypeStruct((ROWS_S, D), F32)]
    else:
        out_specs = pl.BlockSpec((tm, D), lambda i, f: (i, 0))
        out_shape = jax.ShapeDtypeStruct((ROWS, D), F32)
    return pl.pallas_call(
        functools.partial(_mlp_body, final=final),
        grid=(ROWS // tm, D_FF // tf),
        in_specs=in_specs,
        out_specs=out_specs,
        out_shape=out_shape,
        scratch_shapes=[pltpu.VMEM((tm, D), BF16), pltpu.VMEM((tm, D), F32)],
        compiler_params=_cparams(("arbitrary", "arbitrary")),
        name="mlp_final" if final else "mlp",
    )(*args)


SCAN_T = 256


def _seg_cumsums(a):
    n = a.shape[0]
    ii = lax.broadcasted_iota(jnp.int32, (SCAN_T, SCAN_T), 0)
    jj = lax.broadcasted_iota(jnp.int32, (SCAN_T, SCAN_T), 1)
    lower = jnp.where(ii >= jj, 1.0, 0.0).astype(BF16)
    upper = jnp.where(ii <= jj, 1.0, 0.0).astype(BF16)
    hi = a.astype(BF16)
    rest = a - hi.astype(F32)
    mid = rest.astype(BF16)
    lo = (rest - mid.astype(F32)).astype(BF16)
    parts = jnp.concatenate([hi, mid, lo], axis=1)
    pre, suf = [], []
    for c in range(n // SCAN_T):
        pc = parts[SCAN_T * c:SCAN_T * (c + 1), :]
        for tri, out in ((lower, pre), (upper, suf)):
            s3 = _dot(tri, pc)
            out.append(s3[:, :LANES] + s3[:, LANES:2 * LANES] + s3[:, 2 * LANES:])
    return jnp.concatenate(pre, axis=0), jnp.concatenate(suf, axis=0)


LOG2E = 1.4426950408889634


def _ssd_body(*refs, L, has_s0, emit_state):
    it = iter(refs)
    z_ref, x_ref, b_ref, c_ref, dt_ref = (next(it) for _ in range(5))
    cwx, cbx, cwb, cbb, cwc, cbc = (next(it) for _ in range(6))
    dtb_ref, alog_ref, dl_ref = next(it), next(it), next(it)
    s0_ref = next(it) if has_s0 else None
    y_ref = next(it)
    st_out = next(it) if emit_state else None
    padx, padb, xa, ba, ca, cbs, cumc, crp, yacc, st = (next(it) for _ in range(10))

    T = SCAN_T
    nc = L // T
    g0 = pl.program_id(1)

    def conv(in_ref, w_ref, bias_ref, pad):
        width = in_ref.shape[1]
        pad[0:8, :] = jnp.zeros((8, width), F32)
        pad[L + 8:L + 16, :] = jnp.zeros((8, width), F32)
        pad[8:L + 8, :] = in_ref[...].astype(F32)
        acc = bias_ref[...] + pad[6:6 + L, :] * w_ref[0:1, :]
        for k in range(1, CONV_W):
            acc = acc + pad[6 + k:6 + k + L, :] * w_ref[k:k + 1, :]
        return _silu(acc)

    xa[...] = conv(x_ref, cwx, cbx, padx)
    ba[...] = conv(b_ref, cwb, cbb, padb)
    ca[...] = conv(c_ref, cwc, cbc, padb)

    dt_all = _softplus(dt_ref[...] + dtb_ref[...])
    a_all = dt_all * (-jnp.exp(alog_ref[...]))
    shift = jnp.where(g0 == 0, 0, LANES - SSD_HPG * g0)
    l2dt = jnp.log2(pltpu.roll(dt_all, shift, 1))
    ar = pltpu.roll(a_all, shift, 1) * LOG2E
    cum_f, cum_b = _seg_cumsums(ar)
    cumc[0] = cum_f
    cumc[1] = cum_b
    crp[0] = (cum_f - l2dt).T
    crp[1] = (cum_b - l2dt).T

    yacc[...] = xa[...] * dl_ref[...]
    if has_s0:
        st[...] = s0_ref[...]

    ii = lax.broadcasted_iota(jnp.int32, (T, T), 0)
    jj = lax.broadcasted_iota(jnp.int32, (T, T), 1)

    def chunk(c, d):
        rows = slice(T * c, T * (c + 1))
        first = c == (0 if d == 0 else nc - 1)
        last = c == (nc - 1 if d == 0 else 0)
        zero_state = first and not has_s0
        need_state = emit_state or not last
        mask = (ii >= jj) if d == 0 else (ii <= jj)
        end = T * c + (T - 1 if d == 0 else 0)
        xav = xa[rows, :]
        xab = xav.astype(BF16)
        cab = ca[rows, :].astype(BF16)
        if need_state:
            xat = xav.T.astype(BF16)
            bat = ba[rows, :].T
        for r in range(SSD_HPG):
            lane = SSD_HEADS * d + r
            hs = slice(SSD_P * r, SSD_P * (r + 1))
            cc = cumc[d, rows, lane:lane + 1]
            cr = crp[d, lane:lane + 1, rows]
            dec = jnp.where(mask, jnp.exp2(cc - cr), 0.0)
            y = _dot((cbs[c] * dec).astype(BF16), xab[:, hs])
            if not zero_state:
                y = y + _dot_nt(cab, st[d, r].astype(BF16)) * jnp.exp2(cc)
            yacc[rows, hs] += y
            if need_state:
                tot = cumc[d, end:end + 1, lane:lane + 1]
                upd = _dot_nt(xat[hs, :], (bat * jnp.exp2(tot - cr)).astype(BF16))
                st[d, r] = upd if zero_state else jnp.exp2(tot) * st[d, r] + upd

    for c in range(nc):
        rows = slice(T * c, T * (c + 1))
        cbs[c] = _dot_nt(ca[rows, :].astype(BF16), ba[rows, :].astype(BF16))
    for c in range(nc):
        chunk(c, 0)
        chunk(nc - 1 - c, 1)

    y_ref[...] = (yacc[...] * _silu(z_ref[...].astype(F32))).astype(BF16)
    if emit_state:
        st_out[...] = st[...]


def _ssd_scan(p0, pdt, conv_w, conv_b, dtb, alog, dl, s0, *, nb, L, emit_state):
    rb0 = 0 if s0 is None else ROWS_P // L
    gw = SSD_HPG * SSD_P
    nc = L // SCAN_T
    x0 = SSD_DI // gw
    b0 = 2 * SSD_DI // SSD_N
    c0 = b0 + SSD_GROUPS
    wb0 = SSD_DI // SSD_N
    wc0 = wb0 + SSD_GROUPS
    in_specs = [
        pl.BlockSpec((L, gw), lambda b, g: (rb0 + b, g)),
        pl.BlockSpec((L, gw), lambda b, g: (rb0 + b, x0 + g)),
        pl.BlockSpec((L, SSD_N), lambda b, g: (rb0 + b, b0 + g)),
        pl.BlockSpec((L, SSD_N), lambda b, g: (rb0 + b, c0 + g)),
        pl.BlockSpec((L, LANES), lambda b, g: (rb0 + b, 0)),
        pl.BlockSpec((CONV_W, gw), lambda b, g: (0, g)),
        pl.BlockSpec((1, gw), lambda b, g: (0, g)),
        pl.BlockSpec((CONV_W, SSD_N), lambda b, g: (0, wb0 + g)),
        pl.BlockSpec((1, SSD_N), lambda b, g: (0, wb0 + g)),
        pl.BlockSpec((CONV_W, SSD_N), lambda b, g: (0, wc0 + g)),
        pl.BlockSpec((1, SSD_N), lambda b, g: (0, wc0 + g)),
        pl.BlockSpec((1, LANES), lambda b, g: (0, 0)),
        pl.BlockSpec((1, LANES), lambda b, g: (0, 0)),
        pl.BlockSpec((1, gw), lambda b, g: (0, g)),
    ]
    args = [p0, p0, p0, p0, pdt, conv_w, conv_b, conv_w, conv_b, conv_w, conv_b, dtb, alog, dl]
    state_spec = pl.BlockSpec((None, None, 2, SSD_HPG, SSD_P, SSD_N), lambda b, g: (b, 0, 0, g, 0, 0))
    if s0 is not None:
        in_specs.append(state_spec)
        args.append(s0)
    out_specs = [pl.BlockSpec((L, gw), lambda b, g: (b, g))]
    out_shape = [jax.ShapeDtypeStruct((nb * L, SSD_DI), BF16)]
    if emit_state:
        out_specs.append(state_spec)
        out_shape.append(jax.ShapeDtypeStruct((nb, 1, 2, SSD_HEADS, SSD_P, SSD_N), F32))
    scratch = [
        pltpu.VMEM((L + 16, gw), F32),
        pltpu.VMEM((L + 16, SSD_N), F32),
        pltpu.VMEM((L, gw), F32),
        pltpu.VMEM((L, SSD_N), F32),
        pltpu.VMEM((L, SSD_N), F32),
        pltpu.VMEM((nc, SCAN_T, SCAN_T), F32),
        pltpu.VMEM((2, L, LANES), F32),
        pltpu.VMEM((2, LANES, L), F32),
        pltpu.VMEM((L, gw), F32),
        pltpu.VMEM((2, SSD_HPG, SSD_P, SSD_N), F32),
    ]
    res = pl.pallas_call(
        functools.partial(_ssd_body, L=L, has_s0=s0 is not None, emit_state=emit_state),
        grid=(nb, SSD_GROUPS),
        in_specs=in_specs,
        out_specs=out_specs,
        out_shape=out_shape,
        scratch_shapes=scratch,
        compiler_params=_cparams(("parallel", "parallel")),
        name="ssd_scan_p" if s0 is None else "ssd_scan_s",
    )(*args)
    return res


TQ = 256
HP_P = 4
HP_S = 2


def _mla_body(*refs, L, sample):
    it = iter(refs)
    cq_ref, ckv_ref, krs_ref = next(it), next(it), next(it)
    gq_ref, gkv_ref = next(it), next(it)
    wq, wqs, wk, wv = (next(it) for _ in range(4))
    if sample:
        cckv_ref, ckr_ref, cq_t, sq_t, ck_t, sk_t = (next(it) for _ in range(6))
    o_ref = next(it)
    if not sample:
        ckv_out, kr_out = next(it), next(it)
    kk, vv = next(it), next(it)
    qt = pl.program_id(1)
    scale = (MLA_NOPE + MLA_ROPE) ** -0.5
    hb = 4 * LANES

    @pl.when(qt == 0)
    def _():
        ckv = _rms(ckv_ref[...].astype(F32), gkv_ref[...])
        kr_own = krs_ref[:, 0:LANES]
        if sample:
            keys = jnp.concatenate([cckv_ref[...], ckv], axis=0)
            kr_own = kr_own * ck_t[...] + krs_ref[:, LANES:2 * LANES] * sk_t[...]
            kr_all = jnp.concatenate([ckr_ref[...], kr_own], axis=0)
        else:
            ckv_out[...] = ckv
            kr_out[...] = kr_own[:, MLA_NOPE:MLA_NOPE + MLA_ROPE]
            keys = ckv
            kr_all = kr_own
        kb = keys.astype(BF16)
        for blk in range(MLA_HEADS * LANES // hb):
            cols = slice(hb * blk, hb * (blk + 1))
            kn = _dot(kb, wk[:, cols])
            kk[:, cols] = (kn + jnp.concatenate([kr_all] * 4, axis=1)).astype(BF16)
            vv[:, cols] = _dot(kb, wv[:, cols]).astype(BF16)

    cq = _rms(cq_ref[...].astype(F32), gq_ref[...]).astype(BF16)
    for blk in range(MLA_HEADS * LANES // hb):
        qa = _dot(cq, wq[:, hb * blk:hb * (blk + 1)])
        if sample:
            qs = _dot(cq, wqs[:, hb * blk:hb * (blk + 1)])
        pair = None
        for hh in range(4):
            h = 4 * blk + hh
            cols = slice(LANES * h, LANES * (h + 1))
            qh = qa[:, LANES * hh:LANES * (hh + 1)]
            if sample:
                qh = qh * cq_t[...] + qs[:, LANES * hh:LANES * (hh + 1)] * sq_t[...]
            s = _dot_nt(qh.astype(BF16), kk[:, cols]) * scale
            e = jnp.exp(s - jnp.max(s, axis=1, keepdims=True))
            o = _dot(e.astype(BF16), vv[:, cols]) / jnp.sum(e, axis=1, keepdims=True)
            if h % 2 == 0:
                pair = o
            else:
                o_ref[:, LANES * (h // 2):LANES * (h // 2 + 1)] = (pair + o).astype(BF16)


def _mla_attn(p1, p1s, gq, gkv, wq, wqs, wk, wv, ctx, *, nb, L):
    sample = ctx is not None
    nq = L // TQ
    tk = L + (PAST if sample else 0)
    rbq0 = ROWS_P // TQ if sample else 0
    rbs0 = ROWS_P // L if sample else 0
    ckv_blk = MLA_Q_RANK // MLA_KV_RANK

    def const(a):
        return pl.BlockSpec(a.shape, lambda b, q, nd=a.ndim: (0,) * nd)

    in_specs = [
        pl.BlockSpec((TQ, MLA_Q_RANK), lambda b, q: (rbq0 + b * nq + q, 0)),
        pl.BlockSpec((L, MLA_KV_RANK), lambda b, q: (rbs0 + b, ckv_blk)),
        pl.BlockSpec((L, 2 * LANES), lambda b, q: (rbs0 + b, 0)),
        const(gq), const(gkv), const(wq), const(wqs), const(wk), const(wv),
    ]
    args = [p1, p1, p1s, gq, gkv, wq, wqs, wk, wv]
    if sample:
        cckv, ckr, cpad, spad = ctx
        in_specs += [
            pl.BlockSpec((None, None, PAST, MLA_KV_RANK), lambda b, q: (b, 0, 0, 0)),
            pl.BlockSpec((None, None, PAST, LANES), lambda b, q: (b, 0, 0, 0)),
            pl.BlockSpec((TQ, LANES), lambda b, q: (q, 0)),
            pl.BlockSpec((TQ, LANES), lambda b, q: (q, 0)),
            const(cpad), const(spad),
        ]
        args += [cckv, ckr, cpad, spad, cpad, spad]
    out_specs = [pl.BlockSpec((TQ, MLA_HEADS * MLA_V), lambda b, q: (b * nq + q, 0))]
    out_shape = [jax.ShapeDtypeStruct((nb * L, MLA_HEADS * MLA_V), BF16)]
    if not sample:
        out_specs += [
            pl.BlockSpec((None, None, L, MLA_KV_RANK), lambda b, q: (b, 0, 0, 0)),
            pl.BlockSpec((None, None, L, MLA_ROPE), lambda b, q: (b, 0, 0, 0)),
        ]
        out_shape += [
            jax.ShapeDtypeStruct((nb, 1, L, MLA_KV_RANK), F32),
            jax.ShapeDtypeStruct((nb, 1, L, MLA_ROPE), F32),
        ]
    scratch = [
        pltpu.VMEM((tk, MLA_HEADS * LANES), BF16),
        pltpu.VMEM((tk, MLA_HEADS * LANES), BF16),
    ]
    return pl.pallas_call(
        functools.partial(_mla_body, L=L, sample=sample),
        grid=(nb, nq),
        in_specs=in_specs,
        out_specs=out_specs,
        out_shape=out_shape,
        scratch_shapes=scratch,
        compiler_params=_cparams(("parallel", "arbitrary")),
        name="mla_attn_s" if sample else "mla_attn_p",
    )(*args)


def _log_sigmoid(x):
    return -_softplus(-x)


def _mlstm_body(*refs, L, hp, has_s0, emit_state):
    it = iter(refs)
    q_ref, k_ref, v_ref, g_ref, gb_ref, ng_ref = (next(it) for _ in range(6))
    if has_s0:
        c0_ref, n0_ref, m0_ref = next(it), next(it), next(it)
    h_ref = next(it)
    if emit_state:
        c_out, n_out, m_out = next(it), next(it), next(it)
    gsc, gtr, kq, vts, kts, hacc, cst, nst, mst = (next(it) for _ in range(9))

    T = SCAN_T
    nc = L // T
    h0 = hp * pl.program_id(1)

    gts = g_ref[...] + gb_ref[...]
    gr = pltpu.roll(gts, jnp.where(h0 == 0, 0, LANES - h0), 1)
    b_f, b_b = _seg_cumsums(_log_sigmoid(gr))
    gsc[0] = gr
    gsc[1] = b_f
    gsc[2] = b_b
    gtr[0] = gr.T
    gtr[1] = b_f.T
    gtr[2] = b_b.T

    jj = lax.broadcasted_iota(jnp.int32, (T, T), 0)
    ii = lax.broadcasted_iota(jnp.int32, (T, T), 1)

    def chunk(hh, c, d):
        rows = slice(T * c, T * (c + 1))
        qcols = slice(ML_DQK * hh, ML_DQK * (hh + 1))
        first = c == (0 if d == 0 else nc - 1)
        last = c == (nc - 1 if d == 0 else 0)
        zero_state = first and not has_s0
        li = 2 * ML_HEADS * d + hh
        lb = li + ML_HEADS
        b_row = gtr[1 + d, lb:lb + 1, rows]
        logi_row = gtr[0, li:li + 1, rows]
        cj = gsc[1 + d, rows, lb:lb + 1] - gsc[0, rows, li:li + 1]
        mask = (jj <= ii) if d == 0 else (jj >= ii)
        dlog = jnp.where(mask, b_row - cj, -jnp.inf)
        m_prev = jnp.zeros((1, 1), F32) if zero_state else mst[hh, d][:, 0:1]
        inter = b_row + m_prev
        mcomb = jnp.maximum(inter, jnp.max(dlog, axis=0, keepdims=True))
        s = kq[hh, c] * jnp.exp(dlog - mcomb)
        vt = vts[hh, c]
        num = _dot(vt, s.astype(BF16))
        den = jnp.sum(s, axis=0, keepdims=True)
        if not zero_state:
            iw = jnp.exp(inter - mcomb)
            qt = q_ref[rows, qcols].T
            num = num + iw * _dot_tn(cst[hh, d].astype(BF16), qt)
            n8 = jnp.broadcast_to(nst[hh, d], (8, ML_DQK)).astype(BF16)
            den = den + iw * _dot(n8, qt)[0:1]
        hc = num / jnp.maximum(jnp.abs(den), jnp.exp(-mcomb))
        if d == 0:
            hacc[hh, :, rows] = hc
        else:
            hacc[hh, :, rows] += hc
        if emit_state or not last:
            end = T * c + (T - 1 if d == 0 else 0)
            bq = gtr[1 + d, lb:lb + 1, end:end + 1]
            wlog = bq - b_row + logi_row
            m_new = jnp.maximum(bq + m_prev, jnp.max(wlog, axis=1, keepdims=True))
            sw = jnp.exp(wlog - m_new)
            upd = _dot_nt((kts[hh, c].astype(F32) * sw).astype(BF16), vt)
            nsum = _dot(jnp.broadcast_to(sw, (8, T)).astype(BF16), k_ref[rows, qcols])[0:1]
            if zero_state:
                cst[hh, d] = upd
                nst[hh, d] = nsum
            else:
                cw = jnp.exp(bq + m_prev - m_new)
                cst[hh, d] = cw * cst[hh, d] + upd
                nst[hh, d] = cw * nst[hh, d] + nsum
            mst[hh, d] = jnp.broadcast_to(m_new, (1, LANES))

    for hh in range(hp):
        qcols = slice(ML_DQK * hh, ML_DQK * (hh + 1))
        vcols = slice(ML_DV * hh, ML_DV * (hh + 1))
        if has_s0:
            for d in range(2):
                cst[hh, d] = c0_ref[d, hh]
                nst[hh, d] = n0_ref[d, pl.ds(h0 + hh, 1), :]
                mst[hh, d] = m0_ref[d, pl.ds(h0 + hh, 1), :]
        for c in range(nc):
            rows = slice(T * c, T * (c + 1))
            kc = k_ref[rows, qcols]
            kq[hh, c] = _dot_nt(kc, q_ref[rows, qcols])
            kts[hh, c] = kc.T
            vts[hh, c] = v_ref[rows, vcols].T
        for c in range(nc):
            chunk(hh, c, 0)
        for c in range(nc):
            chunk(hh, nc - 1 - c, 1)
        ht = hacc[hh]
        r = lax.rsqrt(jnp.mean(ht * ht, axis=0, keepdims=True) + EPS)
        h_ref[:, vcols] = ((ht * r).T * ng_ref[:, vcols]).astype(BF16)
        if emit_state:
            for d in range(2):
                c_out[d, hh] = cst[hh, d]
                n_out[d, pl.ds(h0 + hh, 1), :] = nst[hh, d]
                m_out[d, pl.ds(h0 + hh, 1), :] = mst[hh, d]


def _mlstm_scan(q, k, v, gates, gb, ng, s0, *, nb, L, hp, emit_state):
    rb0 = 0 if s0 is None else ROWS_P // L
    in_specs = [
        pl.BlockSpec((L, hp * ML_DQK), lambda b, h: (rb0 + b, h)),
        pl.BlockSpec((L, hp * ML_DQK), lambda b, h: (rb0 + b, h)),
        pl.BlockSpec((L, hp * ML_DV), lambda b, h: (rb0 + b, h)),
        pl.BlockSpec((L, LANES), lambda b, h: (rb0 + b, 0)),
        pl.BlockSpec((1, LANES), lambda b, h: (0, 0)),
        pl.BlockSpec((1, hp * ML_DV), lambda b, h: (0, h)),
    ]
    args = [q, k, v, gates, gb, ng]
    c_spec = pl.BlockSpec((None, None, 2, hp, ML_DQK, ML_DV), lambda b, h: (b, 0, 0, h, 0, 0))
    n_spec = pl.BlockSpec((None, 2, ML_HEADS, LANES), lambda b, h: (b, 0, 0, 0))
    if s0 is not None:
        in_specs += [c_spec, n_spec, n_spec]
        args += list(s0)
    out_specs = [pl.BlockSpec((L, hp * ML_DV), lambda b, h: (b, h))]
    out_shape = [jax.ShapeDtypeStruct((nb * L, ML_DI), BF16)]
    if emit_state:
        out_specs += [c_spec, n_spec, n_spec]
        out_shape += [
            jax.ShapeDtypeStruct((nb, 1, 2, ML_HEADS, ML_DQK, ML_DV), F32),
            jax.ShapeDtypeStruct((nb, 2, ML_HEADS, LANES), F32),
            jax.ShapeDtypeStruct((nb, 2, ML_HEADS, LANES), F32),
        ]
    scratch = [
        pltpu.VMEM((3, L, LANES), F32),
        pltpu.VMEM((3, LANES, L), F32),
        pltpu.VMEM((hp, L // SCAN_T, SCAN_T, SCAN_T), F32),
        pltpu.VMEM((hp, L // SCAN_T, ML_DV, SCAN_T), BF16),
        pltpu.VMEM((hp, L // SCAN_T, ML_DQK, SCAN_T), BF16),
        pltpu.VMEM((hp, ML_DV, L), F32),
        pltpu.VMEM((hp, 2, ML_DQK, ML_DV), F32),
        pltpu.VMEM((hp, 2, 1, ML_DQK), F32),
        pltpu.VMEM((hp, 2, 1, LANES), F32),
    ]
    return pl.pallas_call(
        functools.partial(_mlstm_body, L=L, hp=hp, has_s0=s0 is not None, emit_state=emit_state),
        grid=(nb, ML_HEADS // hp),
        in_specs=in_specs,
        out_specs=out_specs,
        out_shape=out_shape,
        scratch_shapes=scratch,
        compiler_params=_cparams(("parallel", "arbitrary")),
        name="mlstm_scan_p" if s0 is None else "mlstm_scan_s",
    )(*args)


def _rope_blocks(x, ct, st):
    lane = lax.broadcasted_iota(jnp.int32, (x.shape[0], LANES), 1)
    first_half = lane % DF_D < DF_D // 2
    outs = []
    for blk in range(x.shape[1] // LANES):
        xb = x[:, LANES * blk:LANES * (blk + 1)]
        swapped = jnp.where(first_half, pltpu.roll(xb, LANES - DF_D // 2, 1), pltpu.roll(xb, DF_D // 2, 1))
        outs.append(xb * ct + swapped * st)
    return jnp.concatenate(outs, axis=1)


def _diff_body(*refs, L, sample, lam_init):
    it = iter(refs)
    q_ref, k_ref, v_ref = next(it), next(it), next(it)
    lq1, lk1, lq2, lk2, sg_ref = (next(it) for _ in range(5))
    if sample:
        ck_ref, cv_ref, cq_t, sq_t, ck_t, sk_t = (next(it) for _ in range(6))
    o_ref = next(it)
    if sample:
        ka, va = next(it), next(it)
    else:
        k_out, v_out = next(it), next(it)
    qt = pl.program_id(1)

    lam = (jnp.exp(jnp.sum(lq1[...] * lk1[...], axis=1, keepdims=True))
           - jnp.exp(jnp.sum(lq2[...] * lk2[...], axis=1, keepdims=True)) + lam_init)

    @pl.when(qt == 0)
    def _():
        if sample:
            ka[0:PAST, :] = ck_ref[...].astype(BF16)
            ka[PAST:PAST + L, :] = _rope_blocks(k_ref[...].astype(F32), ck_t[...], sk_t[...]).astype(BF16)
            va[0:PAST, :] = cv_ref[...].astype(BF16)
            va[PAST:PAST + L, :] = v_ref[...]
        else:
            k_out[...] = k_ref[...].astype(F32)
            v_out[...] = v_ref[...].astype(F32)

    q = q_ref[...].astype(F32)
    if sample:
        q = _rope_blocks(q, cq_t[...], sq_t[...])
    q = q * (DF_D ** -0.5)
    lo = lax.broadcasted_iota(jnp.int32, (TQ, LANES), 1) < DF_D

    def attend(qm, kh, vh):
        s = _dot_nt(qm, kh)
        e = jnp.exp(s - jnp.max(s, axis=1, keepdims=True))
        return _dot(e.astype(BF16), vh) / jnp.sum(e, axis=1, keepdims=True)

    for h in range(DF_HEADS):
        cols = slice(LANES * h, LANES * (h + 1))
        qh = q[:, cols]
        if sample:
            kh, vh = ka[:, cols], va[:, cols]
        else:
            kh, vh = k_ref[:, cols], v_ref[:, cols]
        a0 = attend(jnp.where(lo, qh, 0.0).astype(BF16), kh, vh)
        a1 = attend(jnp.where(lo, 0.0, qh).astype(BF16), kh, vh)
        o = a0 - lam * a1
        o_ref[:, cols] = (_rms(o, sg_ref[...]) * (1.0 - lam_init)).astype(BF16)


def _diff_attn(p3, lq1, lk1, lq2, lk2, sg, ctx, *, nb, L, lam_init):
    sample = ctx is not None
    nq = L // TQ
    rbq0 = ROWS_P // TQ if sample else 0
    rbs0 = ROWS_P // L if sample else 0

    def const(a):
        return pl.BlockSpec(a.shape, lambda b, q, nd=a.ndim: (0,) * nd)

    in_specs = [
        pl.BlockSpec((TQ, D), lambda b, q: (rbq0 + b * nq + q, 0)),
        pl.BlockSpec((L, D), lambda b, q: (rbs0 + b, 1)),
        pl.BlockSpec((L, D), lambda b, q: (rbs0 + b, 2)),
        const(lq1), const(lk1), const(lq2), const(lk2), const(sg),
    ]
    args = [p3, p3, p3, lq1, lk1, lq2, lk2, sg]
    kv_spec = pl.BlockSpec((None, None, PAST if sample else L, D), lambda b, q: (b, 0, 0, 0))
    if sample:
        ck, cv, c128, s128 = ctx
        in_specs += [
            kv_spec, kv_spec,
            pl.BlockSpec((TQ, LANES), lambda b, q: (q, 0)),
            pl.BlockSpec((TQ, LANES), lambda b, q: (q, 0)),
            const(c128), const(s128),
        ]
        args += [ck, cv, c128, s128, c128, s128]
    out_specs = [pl.BlockSpec((TQ, D), lambda b, q: (b * nq + q, 0))]
    out_shape = [jax.ShapeDtypeStruct((nb * L, D), BF16)]
    scratch = []
    if sample:
        scratch = [pltpu.VMEM((PAST + L, D), BF16), pltpu.VMEM((PAST + L, D), BF16)]
    else:
        out_specs += [kv_spec, kv_spec]
        out_shape += [jax.ShapeDtypeStruct((nb, 1, L, D), F32)] * 2
    return pl.pallas_call(
        functools.partial(_diff_body, L=L, sample=sample, lam_init=lam_init),
        grid=(nb, nq),
        in_specs=in_specs,
        out_specs=out_specs,
        out_shape=out_shape,
        scratch_shapes=scratch,
        compiler_params=_cparams(("parallel", "arbitrary")),
        name="diff_attn_s" if sample else "diff_attn_p",
    )(*args)


def _pro_mlstm_conv(p, rs):
    x_ref = p.rows[0]
    w_ref, b_ref = p.consts
    tm = x_ref.shape[0]
    seq = jnp.where(p.i < ROWS_P // tm, L_PROMPT, L_SAMPLE)
    cw = 256
    pos = lax.broadcasted_iota(jnp.int32, (tm, cw), 0) & (seq - 1)
    for cb in range(ML_DI // cw):
        cols = slice(cw * cb, cw * (cb + 1))
        x = x_ref[:, cols].astype(F32)
        acc = None
        for k in range(CONV_W):
            off = k - CONV_W // 2
            src = pos + off
            tap = x if off == 0 else pltpu.roll(x, (-off) % tm, 0)
            tap = jnp.where(jnp.logical_and(src >= 0, src < seq), tap, 0.0) * w_ref[k:k + 1, cols]
            acc = b_ref[:, cols] + tap if acc is None else acc + tap
        p.emit(_silu(acc), rs, cols)


def _pro_mlstm_gate(p, rs):
    hn, xc, z = (r[rs, :].astype(F32) for r in p.rows)
    p.emit((hn + p.consts[0][...] * xc) * _silu(z), rs)


def _rope_tables(d):
    rows = L_SAMPLE // GRID_W
    pos_r = jnp.repeat(jnp.arange(rows, dtype=F32), GRID_W)
    pos_c = jnp.tile(jnp.arange(GRID_W, dtype=F32), rows)
    nf = d // 4
    inv = ROPE_BASE ** (-jnp.arange(nf, dtype=F32) / nf)
    ang = jnp.concatenate([pos_r[:, None] * inv, pos_c[:, None] * inv], axis=-1)
    cos, sin = jnp.cos(ang), jnp.sin(ang)
    return jnp.concatenate([cos, cos], axis=-1), jnp.concatenate([-sin, sin], axis=-1)


def _pad_cols(a, n):
    return jnp.pad(a, ((0, 0), (0, n - a.shape[1])))


def kernel(x_prompt, x_sample, state_ssd, cache_mla_ckv, cache_mla_krope, state_mlstm_C, state_mlstm_n, state_mlstm_m, cache_diff_k, cache_diff_v, c, c_ctx, norm1_g, norm2_g, ada_w, ada_b, mlp_w1, mlp_w2, final_g, ssd_w_in, ssd_conv_w, ssd_conv_b, ssd_dt_bias, ssd_A_log, ssd_D, ssd_norm_g, ssd_w_out, mla_w_in, mla_q_norm_g, mla_kv_norm_g, mla_w_uq, mla_w_ukv, mla_w_o, mlstm_w_up, mlstm_conv_w, mlstm_conv_b, mlstm_gate_b, mlstm_w_q, mlstm_w_k, mlstm_w_v, mlstm_skip, mlstm_norm_g, mlstm_w_down, diff_w_qkv, diff_lq1, diff_lk1, diff_lq2, diff_lk2, diff_subln_g, diff_w_o):
    x = jnp.concatenate([x_prompt.reshape(ROWS_P, D), x_sample.reshape(ROWS_S, D)], axis=0)
    cvec = jnp.concatenate([c_ctx[None, :], c, jnp.zeros((5, D), F32)], axis=0)
    mod_all = _ada_mod(cvec, ada_w, ada_b)
    g2 = norm2_g.reshape(DEPTH, 1, D)

    def in_proj(xin, layer, w, n_cols, tn, extra_w=None, name="in_proj"):
        return _fused_mm(rows=[("u", xin, D, 0)], consts=[norm1_g[layer][None, :]], mod=(mod_all, layer), w=w, k_dim=D,
                         n_cols=n_cols, tm=2048, tn=tn, prologue=_pro_normmod, extra_w=extra_w, name=name)

    def out_proj(xin, layer, rows, consts, prologue, w, k_dim, name):
        return _fused_mm(rows=rows, consts=consts, w=w, k_dim=k_dim, n_cols=D, tm=1024, tn=512, prologue=prologue,
                         epilogue=_epi_residual(2), erows=[xin], emod=(mod_all, layer), out_dtype=F32, name=name)

    p0, pdt = in_proj(x, 0, ssd_w_in[0], 3 * SSD_DI, 512, extra_w=_pad_cols(ssd_w_in[0][:, 3 * SSD_DI:], LANES),
                      name="ssd_in")
    dtb = _pad_cols(ssd_dt_bias[0].reshape(1, 2 * SSD_HEADS), LANES)
    alog = _pad_cols(ssd_A_log[0].reshape(1, 2 * SSD_HEADS), LANES)
    dl = jnp.repeat(ssd_D[0], SSD_P)[None, :]
    scan_args = (p0, pdt, ssd_conv_w[0], ssd_conv_b[0][None, :], dtb, alog, dl)
    yg_p, new_ssd = _ssd_scan(*scan_args, None, nb=N_PROMPT_SEQ, L=L_PROMPT, emit_state=True)
    (yg_s,) = _ssd_scan(*scan_args, state_ssd, nb=N_SAMPLE_SEQ, L=L_SAMPLE, emit_state=False)
    x = out_proj(x, 0, [("d", yg_p, yg_s, SSD_DI, 0)], [ssd_norm_g[0][None, :]], _pro_rms, ssd_w_out[0], SSD_DI,
                 "ssd_out")
    x = _mlp(x, 0, g2, mod_all, mlp_w1, mlp_w2)

    w_in = mla_w_in[0]
    kr0 = MLA_Q_RANK + MLA_KV_RANK
    half = MLA_ROPE // 2
    zk = jnp.zeros((D, MLA_NOPE), F32)
    zr = jnp.zeros((D, LANES - MLA_NOPE - MLA_ROPE), F32)
    w_kr = jnp.concatenate([zk, w_in[:, kr0:kr0 + MLA_ROPE], zr,
                            zk, w_in[:, kr0 + half:kr0 + MLA_ROPE], w_in[:, kr0:kr0 + half], zr], axis=1)
    p1, p1s = in_proj(x, 1, w_in, kr0, 256, extra_w=w_kr, name="mla_in")
    wuq = mla_w_uq[0].reshape(MLA_Q_RANK, MLA_HEADS, MLA_NOPE + MLA_ROPE)
    zq = jnp.zeros((MLA_Q_RANK, MLA_HEADS, LANES - MLA_NOPE - MLA_ROPE), F32)
    zqn = jnp.zeros((MLA_Q_RANK, MLA_HEADS, MLA_NOPE), F32)
    wq = jnp.concatenate([wuq, zq], axis=-1).reshape(MLA_Q_RANK, MLA_HEADS * LANES).astype(BF16)
    wqs = jnp.concatenate([zqn, wuq[..., MLA_NOPE + half:], wuq[..., MLA_NOPE:MLA_NOPE + half], zq],
                          axis=-1).reshape(MLA_Q_RANK, MLA_HEADS * LANES).astype(BF16)
    wukv = mla_w_ukv[0].reshape(MLA_KV_RANK, MLA_HEADS, MLA_NOPE + MLA_V)
    zkv = jnp.zeros((MLA_KV_RANK, MLA_HEADS, MLA_NOPE), F32)
    wk = jnp.concatenate([wukv[..., :MLA_NOPE], zkv], axis=-1).reshape(MLA_KV_RANK, MLA_HEADS * LANES).astype(BF16)
    wv_own = wukv[..., MLA_NOPE:]
    odd = (jnp.arange(MLA_HEADS) % 2 == 1)[None, :, None]
    wv = jnp.where(odd, jnp.concatenate([zkv, wv_own], axis=-1), jnp.concatenate([wv_own, zkv], axis=-1))
    wv = wv.reshape(MLA_KV_RANK, MLA_HEADS * LANES).astype(BF16)
    c32, s32 = _rope_tables(MLA_ROPE)
    tz = jnp.zeros((L_SAMPLE, LANES - MLA_NOPE - MLA_ROPE), F32)
    cpad = jnp.concatenate([jnp.ones((L_SAMPLE, MLA_NOPE), F32), c32, tz], axis=1)
    spad = jnp.concatenate([jnp.zeros((L_SAMPLE, MLA_NOPE), F32), s32, tz], axis=1)
    ckr_pad = jnp.pad(cache_mla_krope, ((0, 0), (0, 0), (0, 0), (MLA_NOPE, LANES - MLA_NOPE - MLA_ROPE)))
    mla_w = (mla_q_norm_g[0][None, :], mla_kv_norm_g[0][None, :], wq, wqs, wk, wv)
    o_p, new_ckv, new_kr = _mla_attn(p1, p1s, *mla_w, None, nb=N_PROMPT_SEQ, L=L_PROMPT)
    (o_s,) = _mla_attn(p1, p1s, *mla_w, (cache_mla_ckv, ckr_pad, cpad, spad), nb=N_SAMPLE_SEQ, L=L_SAMPLE)
    x = out_proj(x, 1, [("d", o_p, o_s, D, 0)], [], _pro_cast, mla_w_o[0], D, "mla_out")
    x = _mlp(x, 1, g2, mod_all, mlp_w1, mlp_w2)

    p2, gates = in_proj(x, 2, mlstm_w_up[0], 2 * ML_DI, 512, extra_w=_pad_cols(mlstm_w_up[0][:, 2 * ML_DI:], LANES),
                        name="mlstm_up")
    q, xc = _fused_mm(rows=[("u", p2, ML_DI, 0)], consts=[mlstm_conv_w[0], mlstm_conv_b[0][None, :]], w=mlstm_w_q[0],
                      k_dim=ML_DI, n_cols=ML_HEADS * ML_DQK, tm=1024, tn=256, prologue=_pro_mlstm_conv,
                      whole_tile_prologue=True, emit_lhs=True, name="mlstm_q")
    k = _fused_mm(rows=[("u", xc, ML_DI, 0)], w=mlstm_w_k[0], k_dim=ML_DI, n_cols=ML_HEADS * ML_DQK, tm=2048, tn=512,
                  epilogue=lambda acc, e, m, rs: acc * (ML_DQK ** -0.5), name="mlstm_k")
    v = _fused_mm(rows=[("u", p2, ML_DI, 0)], w=mlstm_w_v[0], k_dim=ML_DI, n_cols=ML_DI, tm=2048, tn=512,
                  name="mlstm_v")
    gb = _pad_cols(mlstm_gate_b[0].reshape(1, 4 * ML_HEADS), LANES)
    ng = mlstm_norm_g[0][None, :]
    n0 = _pad_cols(state_mlstm_n[:, 0].reshape(N_SAMPLE_SEQ * 2 * ML_HEADS, ML_DQK), LANES).reshape(
        N_SAMPLE_SEQ, 2, ML_HEADS, LANES)
    m0 = jnp.broadcast_to(state_mlstm_m[:, 0][..., None], (N_SAMPLE_SEQ, 2, ML_HEADS, LANES))
    hn_p, new_c, new_n, new_m = _mlstm_scan(q, k, v, gates, gb, ng, None, nb=N_PROMPT_SEQ, L=L_PROMPT, hp=HP_P,
                                            emit_state=True)
    (hn_s,) = _mlstm_scan(q, k, v, gates, gb, ng, (state_mlstm_C, n0, m0), nb=N_SAMPLE_SEQ, L=L_SAMPLE, hp=HP_S,
                          emit_state=False)
    x = out_proj(x, 2, [("d", hn_p, hn_s, ML_DI, 0), ("u", xc, ML_DI, 0), ("u", p2, ML_DI, 1)],
                 [mlstm_skip[0][None, :]], _pro_mlstm_gate, mlstm_w_down[0], ML_DI, "mlstm_down")
    x = _mlp(x, 2, g2, mod_all, mlp_w1, mlp_w2)

    lam_init = 0.8 - 0.6 * math.exp(-0.3 * 3)
    p3 = in_proj(x, 3, diff_w_qkv[0], 3 * D, 512, name="diff_qkv")
    c64, s64 = _rope_tables(DF_D)
    c128 = jnp.concatenate([c64, c64], axis=-1)
    s128 = jnp.concatenate([s64, s64], axis=-1)
    dparams = (diff_lq1, diff_lk1, diff_lq2, diff_lk2, diff_subln_g)
    od_p, new_dk, new_dv = _diff_attn(p3, *dparams, None, nb=N_PROMPT_SEQ, L=L_PROMPT, lam_init=lam_init)
    ctx = (cache_diff_k.reshape(N_SAMPLE_SEQ, 1, PAST, D), cache_diff_v.reshape(N_SAMPLE_SEQ, 1, PAST, D), c128, s128)
    (od_s,) = _diff_attn(p3, *dparams, ctx, nb=N_SAMPLE_SEQ, L=L_SAMPLE, lam_init=lam_init)
    x = out_proj(x, 3, [("d", od_p, od_s, D, 0)], [], _pro_cast, diff_w_o[0], D, "diff_out")
    y_prompt, y_sample = _mlp(x, 3, g2, mod_all, mlp_w1, mlp_w2, final_g=final_g[None, :])
    y_prompt = y_prompt.reshape(N_PROMPT_SEQ, L_PROMPT, D)
    y_sample = y_sample.reshape(N_SAMPLE_SEQ, L_SAMPLE, D)
    return (y_prompt, y_sample, new_ssd, new_ckv, new_kr, new_c,
            new_n[None].reshape(N_PROMPT_SEQ, 1, 2, ML_HEADS, ML_DQK),
            new_m[..., 0].reshape(N_PROMPT_SEQ, 1, 2, ML_HEADS),
            new_dk.reshape(N_PROMPT_SEQ, 1, L_PROMPT, DF_HEADS, 2 * DF_D),
            new_dv.reshape(N_PROMPT_SEQ, 1, L_PROMPT, DF_HEADS, 2 * DF_D))
```

```python
import functools
import math

import jax
import jax.numpy as jnp
from jax import lax
from jax.experimental import pallas as pl
from jax.experimental.pallas import tpu as pltpu

F32 = jnp.float32
BF16 = jnp.bfloat16

D = 1024
DEPTH = 4
D_FF = 4 * D
EPS = 1e-6
ROPE_BASE = 10000.0
CONV_W = 5
GRID_W = 64

N_PROMPT_SEQ = 32
L_PROMPT = 256
N_SAMPLE_SEQ = 2
L_SAMPLE = 1024
PAST = 256
ROWS_P = N_PROMPT_SEQ * L_PROMPT
ROWS_S = N_SAMPLE_SEQ * L_SAMPLE
ROWS = ROWS_P + ROWS_S

SSD_DI = 2 * D
SSD_HEADS = 32
SSD_P = 64
SSD_GROUPS = 8
SSD_N = 128
SSD_HPG = SSD_HEADS // SSD_GROUPS

MLA_HEADS = 16
MLA_Q_RANK = 512
MLA_KV_RANK = 256
MLA_NOPE = 64
MLA_ROPE = 32
MLA_V = 64

ML_DI = 2 * D
ML_HEADS = 8
ML_DQK = 128
ML_DV = 256

DF_HEADS = 8
DF_D = 64

LANES = 128
VMEM_LIMIT = 56 * 1024 * 1024


def _cparams(sem):
    return pltpu.CompilerParams(dimension_semantics=sem, vmem_limit_bytes=VMEM_LIMIT)


def _silu(x):
    return x * jax.nn.sigmoid(x)


def _softplus(x):
    return jnp.maximum(x, 0.0) + jnp.log1p(jnp.exp(-jnp.abs(x)))


def _rms(x, g):
    r = lax.rsqrt(jnp.mean(x * x, axis=-1, keepdims=True) + EPS)
    return (x * r) * g


def _dot(a, b):
    return jnp.dot(a, b, preferred_element_type=F32)


def _dot_nt(a, b):
    return lax.dot_general(a, b, (((1,), (1,)), ((), ())), preferred_element_type=F32)


def _dot_tn(a, b):
    return lax.dot_general(a, b, (((0,), (0,)), ((), ())), preferred_element_type=F32)


MOD_ROWS = 1024


def _group_of_tile(i, tm, sub=0):
    row0 = i * tm + sub * MOD_ROWS
    return jnp.where(row0 < ROWS_P, 0, 1 + (row0 - ROWS_P) // L_SAMPLE)


def _ada_body(c_ref, w_ref, b_ref, o_ref):
    s = _silu(c_ref[...]).astype(BF16)
    o_ref[...] = _dot(s, w_ref[...].astype(BF16)) + b_ref[...]


def _ada_mod(cvec, ada_w, ada_b):
    tn = 1536
    out = pl.pallas_call(
        _ada_body,
        grid=(DEPTH, 6 * D // tn),
        in_specs=[
            pl.BlockSpec((8, D), lambda l, j: (0, 0)),
            pl.BlockSpec((None, D, tn), lambda l, j: (l, 0, j)),
            pl.BlockSpec((None, 1, tn), lambda l, j: (l, 0, j)),
        ],
        out_specs=pl.BlockSpec((None, 8, tn), lambda l, j: (l, 0, j)),
        out_shape=jax.ShapeDtypeStruct((DEPTH, 8, 6 * D), F32),
        compiler_params=_cparams(("parallel", "parallel")),
        name="ada_mod",
    )(cvec, ada_w, ada_b.reshape(DEPTH, 1, 6 * D))
    return out[:, :3].reshape(DEPTH, 3, 6, D)


class _Pro:
    def __init__(self, rows, consts, mod, i, hs, lhs_out):
        self.rows, self.consts, self.mod, self.i = rows, consts, mod, i
        self._hs, self._lhs_out = hs, lhs_out

    def emit(self, val, rows=slice(None), cols=slice(None)):
        vb = val.astype(BF16)
        self._hs[rows, cols] = vb
        if self._lhs_out is not None:
            self._lhs_out[rows, cols] = vb


ROW_CHUNK = 512


def _fused_mm(*, rows, w, k_dim, n_cols, tm, tn, prologue=None, whole_tile_prologue=False, consts=(), mod=None,
              epilogue=None, erows=(), emod=None, w_col0=0, emit_lhs=False, extra_w=None, out_dtype=BF16, name):
    npt = ROWS_P // tm
    grid = (ROWS // tm, n_cols // tn)
    has_dual = any(r[0] == "d" for r in rows)
    dual_epi = any(isinstance(e, tuple) for e in erows)
    if prologue is None:
        assert len(rows) == 1 and rows[0][0] == "u" and not emit_lhs

    in_specs, args = [], []
    for r in rows:
        if r[0] == "u":
            _, arr, width, cb = r
            in_specs.append(pl.BlockSpec((tm, width), lambda i, j, cb=cb: (i, cb)))
            args.append(arr)
        else:
            _, arr_p, arr_s, width, cb = r
            in_specs.append(pl.BlockSpec((tm, width), lambda i, j, cb=cb: (jnp.minimum(i, npt - 1), cb)))
            in_specs.append(pl.BlockSpec((tm, width), lambda i, j, cb=cb: (jnp.maximum(i - npt, 0), cb)))
            args += [arr_p, arr_s]
    for c in consts:
        in_specs.append(pl.BlockSpec(c.shape, lambda i, j, nd=c.ndim: (0,) * nd))
        args.append(c)
    n_sub = max(tm // MOD_ROWS, 1)
    if mod is not None:
        mod_arr, mod_layer = mod
        for s in range(n_sub):
            in_specs.append(pl.BlockSpec((None, None, 6, D),
                                         lambda i, j, s=s: (mod_layer, _group_of_tile(i, tm, s), 0, 0)))
            args.append(mod_arr)
    in_specs.append(pl.BlockSpec((k_dim, tn), lambda i, j: (0, w_col0 // tn + j)))
    args.append(w)
    if extra_w is not None:
        in_specs.append(pl.BlockSpec(extra_w.shape, lambda i, j: (0, 0)))
        args.append(extra_w)
    for e in erows:
        if isinstance(e, tuple):
            in_specs.append(pl.BlockSpec((tm, tn), lambda i, j: (jnp.minimum(i, npt - 1), j)))
            in_specs.append(pl.BlockSpec((tm, tn), lambda i, j: (jnp.maximum(i - npt, 0), j)))
            args += list(e)
        else:
            in_specs.append(pl.BlockSpec((tm, tn), lambda i, j: (i, j)))
            args.append(e)
    if emod is not None:
        emod_arr, emod_layer = emod
        in_specs.append(pl.BlockSpec((None, None, 6, tn), lambda i, j: (emod_layer, _group_of_tile(i, tm), 0, j)))
        args.append(emod_arr)

    out_specs = [pl.BlockSpec((tm, tn), lambda i, j: (i, j))]
    out_shape = [jax.ShapeDtypeStruct((ROWS, n_cols), out_dtype)]
    if emit_lhs:
        out_specs.append(pl.BlockSpec((tm, k_dim), lambda i, j: (i, 0)))
        out_shape.append(jax.ShapeDtypeStruct((ROWS, k_dim), BF16))
    if extra_w is not None:
        out_specs.append(pl.BlockSpec((tm, extra_w.shape[1]), lambda i, j: (i, 0)))
        out_shape.append(jax.ShapeDtypeStruct((ROWS, extra_w.shape[1]), F32))
    scratch = [] if prologue is None else [pltpu.VMEM((tm, k_dim), BF16)]
    chunks = [slice(r, r + ROW_CHUNK) for r in range(0, tm, ROW_CHUNK)]

    def body(*refs):
        it = iter(refs)
        row_refs = [(next(it),) if r[0] == "u" else (next(it), next(it)) for r in rows]
        const_refs = [next(it) for _ in consts]
        mod_ref = [next(it) for _ in range(n_sub)] if mod is not None else None
        w_ref = next(it)
        extra_ref = next(it) if extra_w is not None else None
        erow_refs = [(next(it), next(it)) if isinstance(e, tuple) else (next(it),) for e in erows]
        emod_ref = next(it) if emod is not None else None
        out_ref = next(it)
        lhs_out = next(it) if emit_lhs else None
        extra_out = next(it) if extra_w is not None else None
        hs = next(it) if prologue is not None else None
        i = pl.program_id(0)
        j = pl.program_id(1)

        def compute(first, use_prompt=True):
            chosen = [rr[0] if (len(rr) == 1 or use_prompt) else rr[1] for rr in row_refs]
            echosen = [er[0] if (len(er) == 1 or use_prompt) else er[1] for er in erow_refs]
            p = _Pro(chosen, const_refs, mod_ref, i, hs, lhs_out)
            wb = w_ref[...].astype(BF16)
            if first and whole_tile_prologue:
                prologue(p, slice(0, tm))
            for rs in chunks:
                if first and not whole_tile_prologue:
                    prologue(p, rs)
                lhs = chosen[0][rs, :] if prologue is None else hs[rs, :]
                acc = _dot(lhs, wb)
                if epilogue is not None:
                    acc = epilogue(acc, echosen, emod_ref, rs)
                out_ref[rs, :] = acc.astype(out_dtype)
                if first and extra_w is not None:
                    extra_out[rs, :] = _dot(lhs, extra_ref[...].astype(BF16))

        if prologue is None:
            if extra_w is None:
                compute(False)
            else:
                pl.when(j == 0)(lambda: compute(True))
                pl.when(j > 0)(lambda: compute(False))
        else:
            if has_dual or dual_epi:
                pl.when(jnp.logical_and(j == 0, i < npt))(lambda: compute(True, True))
                pl.when(jnp.logical_and(j == 0, i >= npt))(lambda: compute(True, False))
            else:
                pl.when(j == 0)(lambda: compute(True))
            if dual_epi:
                pl.when(jnp.logical_and(j > 0, i < npt))(lambda: compute(False, True))
                pl.when(jnp.logical_and(j > 0, i >= npt))(lambda: compute(False, False))
            else:
                pl.when(j > 0)(lambda: compute(False))

    res = pl.pallas_call(
        body,
        grid=grid,
        in_specs=in_specs,
        out_specs=out_specs,
        out_shape=out_shape,
        scratch_shapes=scratch,
        compiler_params=_cparams(("parallel", "arbitrary")),
        name=name,
    )(*args)
    return res if len(res) > 1 else res[0]


def _pro_normmod(p, rs):
    m = p.mod[rs.start // MOD_ROWS]
    h = _rms(p.rows[0][rs, :], p.consts[0][...]) * (1.0 + m[1:2, :]) + m[0:1, :]
    p.emit(h, rs)


def _pro_cast(p, rs):
    p.emit(p.rows[0][rs, :], rs)


def _pro_rms(p, rs):
    p.emit(_rms(p.rows[0][rs, :].astype(F32), p.consts[0][...]), rs)


def _epi_residual(gate_row):
    def epi(acc, erows, emod, rs):
        return erows[0][rs, :] + emod[gate_row:gate_row + 1, :] * acc
    return epi


def _mlp_body(*refs, final):
    it = iter(refs)
    x_ref, g_ref, mod_ref, w1_ref, w2_ref = (next(it) for _ in range(5))
    fg_ref = next(it) if final else None
    outs = [next(it), next(it)] if final else [next(it)]
    hs, acc = next(it), next(it)
    i = pl.program_id(0)
    f = pl.program_id(1)
    nf = pl.num_programs(1)
    tm = x_ref.shape[0]
    chunks = [slice(r, r + ROW_CHUNK) for r in range(0, tm, ROW_CHUNK)]

    def step(first, last):
        w1b = w1_ref[...].astype(BF16)
        w2b = w2_ref[...].astype(BF16)
        for rs in chunks:
            if first:
                h = _rms(x_ref[rs, :], g_ref[...]) * (1.0 + mod_ref[4:5, :]) + mod_ref[3:4, :]
                hs[rs, :] = h.astype(BF16)
            u = jnp.square(jnp.maximum(_dot(hs[rs, :], w1b), 0.0))
            tot = _dot(u.astype(BF16), w2b)
            if not first:
                tot = acc[rs, :] + tot
            if not last:
                acc[rs, :] = tot
                continue
            y = x_ref[rs, :] + mod_ref[5:6, :] * tot
            if not final:
                outs[0][rs, :] = y
            else:
                y = _rms(y, fg_ref[...])
                npt = ROWS_P // tm

                @pl.when(i < npt)
                def _():
                    outs[0][rs, :] = y

                @pl.when(i >= npt)
                def _():
                    outs[1][rs, :] = y

    pl.when(f == 0)(lambda: step(True, False))
    pl.when(jnp.logical_and(f > 0, f < nf - 1))(lambda: step(False, False))
    pl.when(f == nf - 1)(lambda: step(False, True))


def _mlp(x, layer, g, mod, w1, w2, final_g=None):
    tm, tf = 1024, 1024
    npt = ROWS_P // tm
    final = final_g is not None
    in_specs = [
        pl.BlockSpec((tm, D), lambda i, f: (i, 0)),
        pl.BlockSpec((None, 1, D), lambda i, f: (layer, 0, 0)),
        pl.BlockSpec((None, None, 6, D), lambda i, f: (layer, _group_of_tile(i, tm), 0, 0)),
        pl.BlockSpec((None, D, tf), lambda i, f: (layer, 0, f)),
        pl.BlockSpec((None, tf, D), lambda i, f: (layer, f, 0)),
    ]
    args = [x, g, mod, w1, w2]
    if final:
        in_specs.append(pl.BlockSpec((1, D), lambda i, f: (0, 0)))
        args.append(final_g)
        out_specs = [pl.BlockSpec((tm, D), lambda i, f: (jnp.minimum(i, npt - 1), 0)),
                     pl.BlockSpec((tm, D), lambda i, f: (jnp.maximum(i - npt, 0), 0))]
        out_shape = [jax.ShapeDtypeStruct((ROWS_P, D), F32), jax.ShapeDtypeStruct((ROWS_S, D), F32)]
    else:
        out_specs = pl.BlockSpec((tm, D), lambda i, f: (i, 0))
        out_shape = jax.ShapeDtypeStruct((ROWS, D), F32)
    return pl.pallas_call(
        functools.partial(_mlp_body, final=final),
        grid=(ROWS // tm, D_FF // tf),
        in_specs=in_specs,
        out_specs=out_specs,
        out_shape=out_shape,
        scratch_shapes=[pltpu.VMEM((tm, D), BF16), pltpu.VMEM((tm, D), F32)],
        compiler_params=_cparams(("arbitrary", "arbitrary")),
        name="mlp_final" if final else "mlp",
    )(*args)


SCAN_T = 256


def _seg_cumsums(a):
    n = a.shape[0]
    ii = lax.broadcasted_iota(jnp.int32, (SCAN_T, SCAN_T), 0)
    jj = lax.broadcasted_iota(jnp.int32, (SCAN_T, SCAN_T), 1)
    lower = jnp.where(ii >= jj, 1.0, 0.0).astype(BF16)
    upper = jnp.where(ii <= jj, 1.0, 0.0).astype(BF16)
    hi = a.astype(BF16)
    rest = a - hi.astype(F32)
    mid = rest.astype(BF16)
    lo = (rest - mid.astype(F32)).astype(BF16)
    parts = jnp.concatenate([hi, mid, lo], axis=1)
    pre, suf = [], []
    for c in range(n // SCAN_T):
        pc = parts[SCAN_T * c:SCAN_T * (c + 1), :]
        for tri, out in ((lower, pre), (upper, suf)):
            s3 = _dot(tri, pc)
            out.append(s3[:, :LANES] + s3[:, LANES:2 * LANES] + s3[:, 2 * LANES:])
    return jnp.concatenate(pre, axis=0), jnp.concatenate(suf, axis=0)


LOG2E = 1.4426950408889634


def _ssd_body(*refs, L, has_s0, emit_state):
    it = iter(refs)
    z_ref, x_ref, b_ref, c_ref, dt_ref = (next(it) for _ in range(5))
    cwx, cbx, cwb, cbb, cwc, cbc = (next(it) for _ in range(6))
    dtb_ref, alog_ref, dl_ref = next(it), next(it), next(it)
    s0_ref = next(it) if has_s0 else None
    y_ref = next(it)
    st_out = next(it) if emit_state else None
    padx, padb, xa, ba, ca, cbs, cumc, crp, yacc, st = (next(it) for _ in range(10))

    T = SCAN_T
    nc = L // T
    g0 = pl.program_id(0)

    def conv(in_ref, w_ref, bias_ref, pad):
        width = in_ref.shape[1]
        pad[0:8, :] = jnp.zeros((8, width), F32)
        pad[L + 8:L + 16, :] = jnp.zeros((8, width), F32)
        pad[8:L + 8, :] = in_ref[...].astype(F32)
        acc = bias_ref[...] + pad[6:6 + L, :] * w_ref[0:1, :]
        for k in range(1, CONV_W):
            acc = acc + pad[6 + k:6 + k + L, :] * w_ref[k:k + 1, :]
        return _silu(acc)

    xa[...] = conv(x_ref, cwx, cbx, padx)
    ba[...] = conv(b_ref, cwb, cbb, padb)
    ca[...] = conv(c_ref, cwc, cbc, padb)

    dt_all = _softplus(dt_ref[...] + dtb_ref[...])
    a_all = dt_all * (-jnp.exp(alog_ref[...]))
    shift = jnp.where(g0 == 0, 0, LANES - SSD_HPG * g0)
    l2dt = jnp.log2(pltpu.roll(dt_all, shift, 1))
    ar = pltpu.roll(a_all, shift, 1) * LOG2E
    cum_f, cum_b = _seg_cumsums(ar)
    cumc[0] = cum_f
    cumc[1] = cum_b
    crp[0] = (cum_f - l2dt).T
    crp[1] = (cum_b - l2dt).T

    yacc[...] = xa[...] * dl_ref[...]
    if has_s0:
        st[...] = s0_ref[...]

    ii = lax.broadcasted_iota(jnp.int32, (T, T), 0)
    jj = lax.broadcasted_iota(jnp.int32, (T, T), 1)

    def chunk(c, d):
        rows = slice(T * c, T * (c + 1))
        first = c == (0 if d == 0 else nc - 1)
        last = c == (nc - 1 if d == 0 else 0)
        zero_state = first and not has_s0
        need_state = emit_state or not last
        mask = (ii >= jj) if d == 0 else (ii <= jj)
        end = T * c + (T - 1 if d == 0 else 0)
        xav = xa[rows, :]
        xab = xav.astype(BF16)
        cab = ca[rows, :].astype(BF16)
        if need_state:
            xat = xav.T.astype(BF16)
            bat = ba[rows, :].T
        for r in range(SSD_HPG):
            lane = SSD_HEADS * d + r
            hs = slice(SSD_P * r, SSD_P * (r + 1))
            cc = cumc[d, rows, lane:lane + 1]
            cr = crp[d, lane:lane + 1, rows]
            dec = jnp.where(mask, jnp.exp2(cc - cr), 0.0)
            y = _dot((cbs[c] * dec).astype(BF16), xab[:, hs])
            if not zero_state:
                y = y + _dot_nt(cab, st[d, r].astype(BF16)) * jnp.exp2(cc)
            yacc[rows, hs] += y
            if need_state:
                tot = cumc[d, end:end + 1, lane:lane + 1]
                upd = _dot_nt(xat[hs, :], (bat * jnp.exp2(tot - cr)).astype(BF16))
                st[d, r] = upd if zero_state else jnp.exp2(tot) * st[d, r] + upd

    for c in range(nc):
        rows = slice(T * c, T * (c + 1))
        cbs[c] = _dot_nt(ca[rows, :].astype(BF16), ba[rows, :].astype(BF16))
    for c in range(nc):
        chunk(c, 0)
        chunk(nc - 1 - c, 1)

    y_ref[...] = (yacc[...] * _silu(z_ref[...].astype(F32))).astype(BF16)
    if emit_state:
        st_out[...] = st[...]


def _ssd_scan(p0, pdt, conv_w, conv_b, dtb, alog, dl, s0, *, nb, L, emit_state):
    rb0 = 0 if s0 is None else ROWS_P // L
    gw = SSD_HPG * SSD_P
    nc = L // SCAN_T
    x0 = SSD_DI // gw
    b0 = 2 * SSD_DI // SSD_N
    c0 = b0 + SSD_GROUPS
    wb0 = SSD_DI // SSD_N
    wc0 = wb0 + SSD_GROUPS
    in_specs = [
        pl.BlockSpec((L, gw), lambda g, b: (rb0 + b, g)),
        pl.BlockSpec((L, gw), lambda g, b: (rb0 + b, x0 + g)),
        pl.BlockSpec((L, SSD_N), lambda g, b: (rb0 + b, b0 + g)),
        pl.BlockSpec((L, SSD_N), lambda g, b: (rb0 + b, c0 + g)),
        pl.BlockSpec((L, LANES), lambda g, b: (rb0 + b, 0)),
        pl.BlockSpec((CONV_W, gw), lambda g, b: (0, g)),
        pl.BlockSpec((1, gw), lambda g, b: (0, g)),
        pl.BlockSpec((CONV_W, SSD_N), lambda g, b: (0, wb0 + g)),
        pl.BlockSpec((1, SSD_N), lambda g, b: (0, wb0 + g)),
        pl.BlockSpec((CONV_W, SSD_N), lambda g, b: (0, wc0 + g)),
        pl.BlockSpec((1, SSD_N), lambda g, b: (0, wc0 + g)),
        pl.BlockSpec((1, LANES), lambda g, b: (0, 0)),
        pl.BlockSpec((1, LANES), lambda g, b: (0, 0)),
        pl.BlockSpec((1, gw), lambda g, b: (0, g)),
    ]
    args = [p0, p0, p0, p0, pdt, conv_w, conv_b, conv_w, conv_b, conv_w, conv_b, dtb, alog, dl]
    state_spec = pl.BlockSpec((None, None, 2, SSD_HPG, SSD_P, SSD_N), lambda g, b: (b, 0, 0, g, 0, 0))
    if s0 is not None:
        in_specs.append(state_spec)
        args.append(s0)
    out_specs = [pl.BlockSpec((L, gw), lambda g, b: (b, g))]
    out_shape = [jax.ShapeDtypeStruct((nb * L, SSD_DI), BF16)]
    if emit_state:
        out_specs.append(state_spec)
        out_shape.append(jax.ShapeDtypeStruct((nb, 1, 2, SSD_HEADS, SSD_P, SSD_N), F32))
    scratch = [
        pltpu.VMEM((L + 16, gw), F32),
        pltpu.VMEM((L + 16, SSD_N), F32),
        pltpu.VMEM((L, gw), F32),
        pltpu.VMEM((L, SSD_N), F32),
        pltpu.VMEM((L, SSD_N), F32),
        pltpu.VMEM((nc, SCAN_T, SCAN_T), F32),
        pltpu.VMEM((2, L, LANES), F32),
        pltpu.VMEM((2, LANES, L), F32),
        pltpu.VMEM((L, gw), F32),
        pltpu.VMEM((2, SSD_HPG, SSD_P, SSD_N), F32),
    ]
    res = pl.pallas_call(
        functools.partial(_ssd_body, L=L, has_s0=s0 is not None, emit_state=emit_state),
        grid=(SSD_GROUPS, nb),
        in_specs=in_specs,
        out_specs=out_specs,
        out_shape=out_shape,
        scratch_shapes=scratch,
        compiler_params=_cparams(("parallel", "parallel")),
        name="ssd_scan_p" if s0 is None else "ssd_scan_s",
    )(*args)
    return res


TQ = 256
HP_P = 4
HP_S = 2


def _mla_body(*refs, L, sample):
    it = iter(refs)
    cq_ref, ckv_ref, krs_ref = next(it), next(it), next(it)
    gq_ref, gkv_ref = next(it), next(it)
    wq, wqs, wk, wv = (next(it) for _ in range(4))
    if sample:
        cckv_ref, ckr_ref, cq_t, sq_t, ck_t, sk_t = (next(it) for _ in range(6))
    o_ref = next(it)
    if not sample:
        ckv_out, kr_out = next(it), next(it)
    kk, vv = next(it), next(it)
    qt = pl.program_id(1)
    scale = (MLA_NOPE + MLA_ROPE) ** -0.5
    hb = 4 * LANES

    @pl.when(qt == 0)
    def _():
        ckv = _rms(ckv_ref[...].astype(F32), gkv_ref[...])
        kr_own = krs_ref[:, 0:LANES]
        if sample:
            keys = jnp.concatenate([cckv_ref[...], ckv], axis=0)
            kr_own = kr_own * ck_t[...] + krs_ref[:, LANES:2 * LANES] * sk_t[...]
            kr_all = jnp.concatenate([ckr_ref[...], kr_own], axis=0)
        else:
            ckv_out[...] = ckv
            kr_out[...] = kr_own[:, MLA_NOPE:MLA_NOPE + MLA_ROPE]
            keys = ckv
            kr_all = kr_own
        kb = keys.astype(BF16)
        for blk in range(MLA_HEADS * LANES // hb):
            cols = slice(hb * blk, hb * (blk + 1))
            kn = _dot(kb, wk[:, cols])
            kk[:, cols] = (kn + jnp.concatenate([kr_all] * 4, axis=1)).astype(BF16)
            vv[:, cols] = _dot(kb, wv[:, cols]).astype(BF16)

    cq = _rms(cq_ref[...].astype(F32), gq_ref[...]).astype(BF16)
    for blk in range(MLA_HEADS * LANES // hb):
        qa = _dot(cq, wq[:, hb * blk:hb * (blk + 1)])
        if sample:
            qs = _dot(cq, wqs[:, hb * blk:hb * (blk + 1)])
        pair = None
        for hh in range(4):
            h = 4 * blk + hh
            cols = slice(LANES * h, LANES * (h + 1))
            qh = qa[:, LANES * hh:LANES * (hh + 1)]
            if sample:
                qh = qh * cq_t[...] + qs[:, LANES * hh:LANES * (hh + 1)] * sq_t[...]
            s = _dot_nt((qh * scale).astype(BF16), kk[:, cols])
            e = jnp.exp(s - jnp.max(s, axis=1, keepdims=True))
            o = _dot(e.astype(BF16), vv[:, cols]) / jnp.sum(e, axis=1, keepdims=True)
            if h % 2 == 0:
                pair = o
            else:
                o_ref[:, LANES * (h // 2):LANES * (h // 2 + 1)] = (pair + o).astype(BF16)


def _mla_attn(p1, p1s, gq, gkv, wq, wqs, wk, wv, ctx, *, nb, L):
    sample = ctx is not None
    nq = L // TQ
    tk = L + (PAST if sample else 0)
    rbq0 = ROWS_P // TQ if sample else 0
    rbs0 = ROWS_P // L if sample else 0
    ckv_blk = MLA_Q_RANK // MLA_KV_RANK

    def const(a):
        return pl.BlockSpec(a.shape, lambda b, q, nd=a.ndim: (0,) * nd)

    in_specs = [
        pl.BlockSpec((TQ, MLA_Q_RANK), lambda b, q: (rbq0 + b * nq + q, 0)),
        pl.BlockSpec((L, MLA_KV_RANK), lambda b, q: (rbs0 + b, ckv_blk)),
        pl.BlockSpec((L, 2 * LANES), lambda b, q: (rbs0 + b, 0)),
        const(gq), const(gkv), const(wq), const(wqs), const(wk), const(wv),
    ]
    args = [p1, p1, p1s, gq, gkv, wq, wqs, wk, wv]
    if sample:
        cckv, ckr, cpad, spad = ctx
        in_specs += [
            pl.BlockSpec((None, None, PAST, MLA_KV_RANK), lambda b, q: (b, 0, 0, 0)),
            pl.BlockSpec((None, None, PAST, LANES), lambda b, q: (b, 0, 0, 0)),
            pl.BlockSpec((TQ, LANES), lambda b, q: (q, 0)),
            pl.BlockSpec((TQ, LANES), lambda b, q: (q, 0)),
            const(cpad), const(spad),
        ]
        args += [cckv, ckr, cpad, spad, cpad, spad]
    out_specs = [pl.BlockSpec((TQ, MLA_HEADS * MLA_V), lambda b, q: (b * nq + q, 0))]
    out_shape = [jax.ShapeDtypeStruct((nb * L, MLA_HEADS * MLA_V), BF16)]
    if not sample:
        out_specs += [
            pl.BlockSpec((None, None, L, MLA_KV_RANK), lambda b, q: (b, 0, 0, 0)),
            pl.BlockSpec((None, None, L, MLA_ROPE), lambda b, q: (b, 0, 0, 0)),
        ]
        out_shape += [
            jax.ShapeDtypeStruct((nb, 1, L, MLA_KV_RANK), F32),
            jax.ShapeDtypeStruct((nb, 1, L, MLA_ROPE), F32),
        ]
    scratch = [
        pltpu.VMEM((tk, MLA_HEADS * LANES), BF16),
        pltpu.VMEM((tk, MLA_HEADS * LANES), BF16),
    ]
    return pl.pallas_call(
        functools.partial(_mla_body, L=L, sample=sample),
        grid=(nb, nq),
        in_specs=in_specs,
        out_specs=out_specs,
        out_shape=out_shape,
        scratch_shapes=scratch,
        compiler_params=_cparams(("parallel", "arbitrary")),
        name="mla_attn_s" if sample else "mla_attn_p",
    )(*args)


def _log_sigmoid(x):
    return -_softplus(-x)


def _mlstm_body(*refs, L, hp, has_s0, emit_state):
    it = iter(refs)
    q_ref, k_ref, v_ref, g_ref, gb_ref, ng_ref = (next(it) for _ in range(6))
    if has_s0:
        c0_ref, n0_ref, m0_ref = next(it), next(it), next(it)
    h_ref = next(it)
    if emit_state:
        c_out, n_out, m_out = next(it), next(it), next(it)
    gsc, gtr, kq, vts, kts, hacc, cst, nst, mst = (next(it) for _ in range(9))

    T = SCAN_T
    nc = L // T
    h0 = hp * pl.program_id(1)

    gts = g_ref[...] + gb_ref[...]
    gr = pltpu.roll(gts, jnp.where(h0 == 0, 0, LANES - h0), 1)
    b_f, b_b = _seg_cumsums(_log_sigmoid(gr))
    gsc[0] = gr
    gsc[1] = b_f
    gsc[2] = b_b
    gtr[0] = gr.T
    gtr[1] = b_f.T
    gtr[2] = b_b.T

    jj = lax.broadcasted_iota(jnp.int32, (T, T), 0)
    ii = lax.broadcasted_iota(jnp.int32, (T, T), 1)

    def chunk(hh, c, d):
        rows = slice(T * c, T * (c + 1))
        qcols = slice(ML_DQK * hh, ML_DQK * (hh + 1))
        first = c == (0 if d == 0 else nc - 1)
        last = c == (nc - 1 if d == 0 else 0)
        zero_state = first and not has_s0
        li = 2 * ML_HEADS * d + hh
        lb = li + ML_HEADS
        b_row = gtr[1 + d, lb:lb + 1, rows]
        logi_row = gtr[0, li:li + 1, rows]
        cj = gsc[1 + d, rows, lb:lb + 1] - gsc[0, rows, li:li + 1]
        mask = (jj <= ii) if d == 0 else (jj >= ii)
        dlog = jnp.where(mask, b_row - cj, -jnp.inf)
        m_prev = jnp.zeros((1, 1), F32) if zero_state else mst[hh, d][:, 0:1]
        inter = b_row + m_prev
        mcomb = jnp.maximum(inter, jnp.max(dlog, axis=0, keepdims=True))
        s = kq[hh, c] * jnp.exp(dlog - mcomb)
        vt = vts[hh, c]
        num = _dot(vt, s.astype(BF16))
        den = jnp.sum(s, axis=0, keepdims=True)
        if not zero_state:
            iw = jnp.exp(inter - mcomb)
            qt = q_ref[rows, qcols].T
            num = num + iw * _dot_tn(cst[hh, d].astype(BF16), qt)
            n8 = jnp.broadcast_to(nst[hh, d], (8, ML_DQK)).astype(BF16)
            den = den + iw * _dot(n8, qt)[0:1]
        hc = num / jnp.maximum(jnp.abs(den), jnp.exp(-mcomb))
        if d == 0:
            hacc[hh, :, rows] = hc
        else:
            hacc[hh, :, rows] += hc
        if emit_state or not last:
            end = T * c + (T - 1 if d == 0 else 0)
            bq = gtr[1 + d, lb:lb + 1, end:end + 1]
            wlog = bq - b_row + logi_row
            m_new = jnp.maximum(bq + m_prev, jnp.max(wlog, axis=1, keepdims=True))
            sw = jnp.exp(wlog - m_new)
            upd = _dot_nt((kts[hh, c].astype(F32) * sw).astype(BF16), vt)
            nsum = _dot(jnp.broadcast_to(sw, (8, T)).astype(BF16), k_ref[rows, qcols])[0:1]
            if zero_state:
                cst[hh, d] = upd
                nst[hh, d] = nsum
            else:
                cw = jnp.exp(bq + m_prev - m_new)
                cst[hh, d] = cw * cst[hh, d] + upd
                nst[hh, d] = cw * nst[hh, d] + nsum
            mst[hh, d] = jnp.broadcast_to(m_new, (1, LANES))

    for hh in range(hp):
        qcols = slice(ML_DQK * hh, ML_DQK * (hh + 1))
        vcols = slice(ML_DV * hh, ML_DV * (hh + 1))
        if has_s0:
            for d in range(2):
                cst[hh, d] = c0_ref[d, hh]
                nst[hh, d] = n0_ref[d, pl.ds(h0 + hh, 1), :]
                mst[hh, d] = m0_ref[d, pl.ds(h0 + hh, 1), :]
        for c in range(nc):
            rows = slice(T * c, T * (c + 1))
            kc = k_ref[rows, qcols]
            kq[hh, c] = _dot_nt(kc, q_ref[rows, qcols])
            kts[hh, c] = kc.T
            vts[hh, c] = v_ref[rows, vcols].T
        for c in range(nc):
            chunk(hh, c, 0)
        for c in range(nc):
            chunk(hh, nc - 1 - c, 1)
        ht = hacc[hh]
        r = lax.rsqrt(jnp.mean(ht * ht, axis=0, keepdims=True) + EPS)
        h_ref[:, vcols] = ((ht * r).T * ng_ref[:, vcols]).astype(BF16)
        if emit_state:
            for d in range(2):
                c_out[d, hh] = cst[hh, d]
                n_out[d, pl.ds(h0 + hh, 1), :] = nst[hh, d]
                m_out[d, pl.ds(h0 + hh, 1), :] = mst[hh, d]


def _mlstm_scan(q, k, v, gates, gb, ng, s0, *, nb, L, hp, emit_state):
    rb0 = 0 if s0 is None else ROWS_P // L
    in_specs = [
        pl.BlockSpec((L, hp * ML_DQK), lambda b, h: (rb0 + b, h)),
        pl.BlockSpec((L, hp * ML_DQK), lambda b, h: (rb0 + b, h)),
        pl.BlockSpec((L, hp * ML_DV), lambda b, h: (rb0 + b, h)),
        pl.BlockSpec((L, LANES), lambda b, h: (rb0 + b, 0)),
        pl.BlockSpec((1, LANES), lambda b, h: (0, 0)),
        pl.BlockSpec((1, hp * ML_DV), lambda b, h: (0, h)),
    ]
    args = [q, k, v, gates, gb, ng]
    c_spec = pl.BlockSpec((None, None, 2, hp, ML_DQK, ML_DV), lambda b, h: (b, 0, 0, h, 0, 0))
    n_spec = pl.BlockSpec((None, 2, ML_HEADS, LANES), lambda b, h: (b, 0, 0, 0))
    if s0 is not None:
        in_specs += [c_spec, n_spec, n_spec]
        args += list(s0)
    out_specs = [pl.BlockSpec((L, hp * ML_DV), lambda b, h: (b, h))]
    out_shape = [jax.ShapeDtypeStruct((nb * L, ML_DI), BF16)]
    if emit_state:
        out_specs += [c_spec, n_spec, n_spec]
        out_shape += [
            jax.ShapeDtypeStruct((nb, 1, 2, ML_HEADS, ML_DQK, ML_DV), F32),
            jax.ShapeDtypeStruct((nb, 2, ML_HEADS, LANES), F32),
            jax.ShapeDtypeStruct((nb, 2, ML_HEADS, LANES), F32),
        ]
    scratch = [
        pltpu.VMEM((3, L, LANES), F32),
        pltpu.VMEM((3, LANES, L), F32),
        pltpu.VMEM((hp, L // SCAN_T, SCAN_T, SCAN_T), F32),
        pltpu.VMEM((hp, L // SCAN_T, ML_DV, SCAN_T), BF16),
        pltpu.VMEM((hp, L // SCAN_T, ML_DQK, SCAN_T), BF16),
        pltpu.VMEM((hp, ML_DV, L), F32),
        pltpu.VMEM((hp, 2, ML_DQK, ML_DV), F32),
        pltpu.VMEM((hp, 2, 1, ML_DQK), F32),
        pltpu.VMEM((hp, 2, 1, LANES), F32),
    ]
    return pl.pallas_call(
        functools.partial(_mlstm_body, L=L, hp=hp, has_s0=s0 is not None, emit_state=emit_state),
        grid=(nb, ML_HEADS // hp),
        in_specs=in_specs,
        out_specs=out_specs,
        out_shape=out_shape,
        scratch_shapes=scratch,
        compiler_params=_cparams(("parallel", "arbitrary")),
        name="mlstm_scan_p" if s0 is None else "mlstm_scan_s",
    )(*args)


def _rope_blocks(x, ct, st):
    lane = lax.broadcasted_iota(jnp.int32, (x.shape[0], LANES), 1)
    first_half = lane % DF_D < DF_D // 2
    outs = []
    for blk in range(x.shape[1] // LANES):
        xb = x[:, LANES * blk:LANES * (blk + 1)]
        swapped = jnp.where(first_half, pltpu.roll(xb, LANES - DF_D // 2, 1), pltpu.roll(xb, DF_D // 2, 1))
        outs.append(xb * ct + swapped * st)
    return jnp.concatenate(outs, axis=1)


def _diff_body(*refs, L, sample, lam_init):
    it = iter(refs)
    q_ref, k_ref, v_ref = next(it), next(it), next(it)
    lq1, lk1, lq2, lk2, sg_ref = (next(it) for _ in range(5))
    if sample:
        ck_ref, cv_ref, cq_t, sq_t, ck_t, sk_t = (next(it) for _ in range(6))
    o_ref = next(it)
    if sample:
        ka, va = next(it), next(it)
    else:
        k_out, v_out = next(it), next(it)
    qt = pl.program_id(1)

    lam = (jnp.exp(jnp.sum(lq1[...] * lk1[...], axis=1, keepdims=True))
           - jnp.exp(jnp.sum(lq2[...] * lk2[...], axis=1, keepdims=True)) + lam_init)

    @pl.when(qt == 0)
    def _():
        if sample:
            ka[0:PAST, :] = ck_ref[...].astype(BF16)
            ka[PAST:PAST + L, :] = _rope_blocks(k_ref[...].astype(F32), ck_t[...], sk_t[...]).astype(BF16)
            va[0:PAST, :] = cv_ref[...].astype(BF16)
            va[PAST:PAST + L, :] = v_ref[...]
        else:
            k_out[...] = k_ref[...].astype(F32)
            v_out[...] = v_ref[...].astype(F32)

    q = q_ref[...].astype(F32)
    if sample:
        q = _rope_blocks(q, cq_t[...], sq_t[...])
    q = q * (DF_D ** -0.5)
    lo = lax.broadcasted_iota(jnp.int32, (TQ, LANES), 1) < DF_D

    def attend(qm, kh, vh):
        s = _dot_nt(qm, kh)
        e = jnp.exp(s - jnp.max(s, axis=1, keepdims=True))
        return _dot(e.astype(BF16), vh) / jnp.sum(e, axis=1, keepdims=True)

    for h in range(DF_HEADS):
        cols = slice(LANES * h, LANES * (h + 1))
        qh = q[:, cols]
        if sample:
            kh, vh = ka[:, cols], va[:, cols]
        else:
            kh, vh = k_ref[:, cols], v_ref[:, cols]
        a0 = attend(jnp.where(lo, qh, 0.0).astype(BF16), kh, vh)
        a1 = attend(jnp.where(lo, 0.0, qh).astype(BF16), kh, vh)
        o = a0 - lam * a1
        o_ref[:, cols] = (_rms(o, sg_ref[...]) * (1.0 - lam_init)).astype(BF16)


def _diff_attn(p3, lq1, lk1, lq2, lk2, sg, ctx, *, nb, L, lam_init):
    sample = ctx is not None
    nq = L // TQ
    rbq0 = ROWS_P // TQ if sample else 0
    rbs0 = ROWS_P // L if sample else 0

    def const(a):
        return pl.BlockSpec(a.shape, lambda b, q, nd=a.ndim: (0,) * nd)

    in_specs = [
        pl.BlockSpec((TQ, D), lambda b, q: (rbq0 + b * nq + q, 0)),
        pl.BlockSpec((L, D), lambda b, q: (rbs0 + b, 1)),
        pl.BlockSpec((L, D), lambda b, q: (rbs0 + b, 2)),
        const(lq1), const(lk1), const(lq2), const(lk2), const(sg),
    ]
    args = [p3, p3, p3, lq1, lk1, lq2, lk2, sg]
    kv_spec = pl.BlockSpec((None, None, PAST if sample else L, D), lambda b, q: (b, 0, 0, 0))
    if sample:
        ck, cv, c128, s128 = ctx
        in_specs += [
            kv_spec, kv_spec,
            pl.BlockSpec((TQ, LANES), lambda b, q: (q, 0)),
            pl.BlockSpec((TQ, LANES), lambda b, q: (q, 0)),
            const(c128), const(s128),
        ]
        args += [ck, cv, c128, s128, c128, s128]
    out_specs = [pl.BlockSpec((TQ, D), lambda b, q: (b * nq + q, 0))]
    out_shape = [jax.ShapeDtypeStruct((nb * L, D), BF16)]
    scratch = []
    if sample:
        scratch = [pltpu.VMEM((PAST + L, D), BF16), pltpu.VMEM((PAST + L, D), BF16)]
    else:
        out_specs += [kv_spec, kv_spec]
        out_shape += [jax.ShapeDtypeStruct((nb, 1, L, D), F32)] * 2
    return pl.pallas_call(
        functools.partial(_diff_body, L=L, sample=sample, lam_init=lam_init),
        grid=(nb, nq),
        in_specs=in_specs,
        out_specs=out_specs,
        out_shape=out_shape,
        scratch_shapes=scratch,
        compiler_params=_cparams(("parallel", "arbitrary")),
        name="diff_attn_s" if sample else "diff_attn_p",
    )(*args)


def _pro_mlstm_conv(p, rs):
    x_ref = p.rows[0]
    w_ref, b_ref = p.consts
    tm = x_ref.shape[0]
    seq = jnp.where(p.i < ROWS_P // tm, L_PROMPT, L_SAMPLE)
    cw = 256
    pos = lax.broadcasted_iota(jnp.int32, (tm, cw), 0) & (seq - 1)
    for cb in range(ML_DI // cw):
        cols = slice(cw * cb, cw * (cb + 1))
        x = x_ref[:, cols].astype(F32)
        acc = None
        for k in range(CONV_W):
            off = k - CONV_W // 2
            src = pos + off
            tap = x if off == 0 else pltpu.roll(x, (-off) % tm, 0)
            tap = jnp.where(jnp.logical_and(src >= 0, src < seq), tap, 0.0) * w_ref[k:k + 1, cols]
            acc = b_ref[:, cols] + tap if acc is None else acc + tap
        p.emit(_silu(acc), rs, cols)


def _pro_mlstm_gate(p, rs):
    hn, xc, z = (r[rs, :].astype(F32) for r in p.rows)
    p.emit((hn + p.consts[0][...] * xc) * _silu(z), rs)


def _rope_tables(d):
    rows = L_SAMPLE // GRID_W
    pos_r = jnp.repeat(jnp.arange(rows, dtype=F32), GRID_W)
    pos_c = jnp.tile(jnp.arange(GRID_W, dtype=F32), rows)
    nf = d // 4
    inv = ROPE_BASE ** (-jnp.arange(nf, dtype=F32) / nf)
    ang = jnp.concatenate([pos_r[:, None] * inv, pos_c[:, None] * inv], axis=-1)
    cos, sin = jnp.cos(ang), jnp.sin(ang)
    return jnp.concatenate([cos, cos], axis=-1), jnp.concatenate([-sin, sin], axis=-1)


def _pad_cols(a, n):
    return jnp.pad(a, ((0, 0), (0, n - a.shape[1])))


def kernel(x_prompt, x_sample, state_ssd, cache_mla_ckv, cache_mla_krope, state_mlstm_C, state_mlstm_n, state_mlstm_m, cache_diff_k, cache_diff_v, c, c_ctx, norm1_g, norm2_g, ada_w, ada_b, mlp_w1, mlp_w2, final_g, ssd_w_in, ssd_conv_w, ssd_conv_b, ssd_dt_bias, ssd_A_log, ssd_D, ssd_norm_g, ssd_w_out, mla_w_in, mla_q_norm_g, mla_kv_norm_g, mla_w_uq, mla_w_ukv, mla_w_o, mlstm_w_up, mlstm_conv_w, mlstm_conv_b, mlstm_gate_b, mlstm_w_q, mlstm_w_k, mlstm_w_v, mlstm_skip, mlstm_norm_g, mlstm_w_down, diff_w_qkv, diff_lq1, diff_lk1, diff_lq2, diff_lk2, diff_subln_g, diff_w_o):
    xp2, xs2 = x_prompt.reshape(ROWS_P, D), x_sample.reshape(ROWS_S, D)
    cvec = jnp.concatenate([c_ctx[None, :], c, jnp.zeros((5, D), F32)], axis=0)
    mod_all = _ada_mod(cvec, ada_w, ada_b)
    g2 = norm2_g.reshape(DEPTH, 1, D)

    def in_proj(xin, layer, w, n_cols, tn, extra_w=None, name="in_proj"):
        xrow = ("d", xin[0], xin[1], D, 0) if isinstance(xin, tuple) else ("u", xin, D, 0)
        return _fused_mm(rows=[xrow], consts=[norm1_g[layer][None, :]], mod=(mod_all, layer), w=w, k_dim=D,
                         n_cols=n_cols, tm=2048, tn=tn, prologue=_pro_normmod, extra_w=extra_w, name=name)

    def out_proj(xin, layer, rows, consts, prologue, w, k_dim, name):
        return _fused_mm(rows=rows, consts=consts, w=w, k_dim=k_dim, n_cols=D, tm=1024, tn=512, prologue=prologue,
                         epilogue=_epi_residual(2), erows=[xin], emod=(mod_all, layer), out_dtype=F32, name=name)

    p0, pdt = in_proj((xp2, xs2), 0, ssd_w_in[0], 3 * SSD_DI, 512, extra_w=_pad_cols(ssd_w_in[0][:, 3 * SSD_DI:], LANES),
                      name="ssd_in")
    dtb = _pad_cols(ssd_dt_bias[0].reshape(1, 2 * SSD_HEADS), LANES)
    alog = _pad_cols(ssd_A_log[0].reshape(1, 2 * SSD_HEADS), LANES)
    dl = jnp.repeat(ssd_D[0], SSD_P)[None, :]
    scan_args = (p0, pdt, ssd_conv_w[0], ssd_conv_b[0][None, :], dtb, alog, dl)
    yg_p, new_ssd = _ssd_scan(*scan_args, None, nb=N_PROMPT_SEQ, L=L_PROMPT, emit_state=True)
    (yg_s,) = _ssd_scan(*scan_args, state_ssd, nb=N_SAMPLE_SEQ, L=L_SAMPLE, emit_state=False)
    x = out_proj((xp2, xs2), 0, [("d", yg_p, yg_s, SSD_DI, 0)], [ssd_norm_g[0][None, :]], _pro_rms, ssd_w_out[0], SSD_DI,
                 "ssd_out")
    x = _mlp(x, 0, g2, mod_all, mlp_w1, mlp_w2)

    w_in = mla_w_in[0]
    kr0 = MLA_Q_RANK + MLA_KV_RANK
    half = MLA_ROPE // 2
    zk = jnp.zeros((D, MLA_NOPE), F32)
    zr = jnp.zeros((D, LANES - MLA_NOPE - MLA_ROPE), F32)
    w_kr = jnp.concatenate([zk, w_in[:, kr0:kr0 + MLA_ROPE], zr,
                            zk, w_in[:, kr0 + half:kr0 + MLA_ROPE], w_in[:, kr0:kr0 + half], zr], axis=1)
    p1, p1s = in_proj(x, 1, w_in, kr0, kr0, extra_w=w_kr, name="mla_in")
    wuq = mla_w_uq[0].reshape(MLA_Q_RANK, MLA_HEADS, MLA_NOPE + MLA_ROPE)
    zq = jnp.zeros((MLA_Q_RANK, MLA_HEADS, LANES - MLA_NOPE - MLA_ROPE), F32)
    zqn = jnp.zeros((MLA_Q_RANK, MLA_HEADS, MLA_NOPE), F32)
    wq = jnp.concatenate([wuq, zq], axis=-1).reshape(MLA_Q_RANK, MLA_HEADS * LANES).astype(BF16)
    wqs = jnp.concatenate([zqn, wuq[..., MLA_NOPE + half:], wuq[..., MLA_NOPE:MLA_NOPE + half], zq],
                          axis=-1).reshape(MLA_Q_RANK, MLA_HEADS * LANES).astype(BF16)
    wukv = mla_w_ukv[0].reshape(MLA_KV_RANK, MLA_HEADS, MLA_NOPE + MLA_V)
    zkv = jnp.zeros((MLA_KV_RANK, MLA_HEADS, MLA_NOPE), F32)
    wk = jnp.concatenate([wukv[..., :MLA_NOPE], zkv], axis=-1).reshape(MLA_KV_RANK, MLA_HEADS * LANES).astype(BF16)
    wv_own = wukv[..., MLA_NOPE:]
    odd = (jnp.arange(MLA_HEADS) % 2 == 1)[None, :, None]
    wv = jnp.where(odd, jnp.concatenate([zkv, wv_own], axis=-1), jnp.concatenate([wv_own, zkv], axis=-1))
    wv = wv.reshape(MLA_KV_RANK, MLA_HEADS * LANES).astype(BF16)
    c32, s32 = _rope_tables(MLA_ROPE)
    tz = jnp.zeros((L_SAMPLE, LANES - MLA_NOPE - MLA_ROPE), F32)
    cpad = jnp.concatenate([jnp.ones((L_SAMPLE, MLA_NOPE), F32), c32, tz], axis=1)
    spad = jnp.concatenate([jnp.zeros((L_SAMPLE, MLA_NOPE), F32), s32, tz], axis=1)
    ckr_pad = jnp.pad(cache_mla_krope, ((0, 0), (0, 0), (0, 0), (MLA_NOPE, LANES - MLA_NOPE - MLA_ROPE)))
    mla_w = (mla_q_norm_g[0][None, :], mla_kv_norm_g[0][None, :], wq, wqs, wk, wv)
    o_p, new_ckv, new_kr = _mla_attn(p1, p1s, *mla_w, None, nb=N_PROMPT_SEQ, L=L_PROMPT)
    (o_s,) = _mla_attn(p1, p1s, *mla_w, (cache_mla_ckv, ckr_pad, cpad, spad), nb=N_SAMPLE_SEQ, L=L_SAMPLE)
    x = out_proj(x, 1, [("d", o_p, o_s, D, 0)], [], _pro_cast, mla_w_o[0], D, "mla_out")
    x = _mlp(x, 1, g2, mod_all, mlp_w1, mlp_w2)

    p2, gates = in_proj(x, 2, mlstm_w_up[0], 2 * ML_DI, 1024, extra_w=_pad_cols(mlstm_w_up[0][:, 2 * ML_DI:], LANES),
                        name="mlstm_up")
    q, xc = _fused_mm(rows=[("u", p2, ML_DI, 0)], consts=[mlstm_conv_w[0], mlstm_conv_b[0][None, :]], w=mlstm_w_q[0],
                      k_dim=ML_DI, n_cols=ML_HEADS * ML_DQK, tm=1024, tn=256, prologue=_pro_mlstm_conv,
                      whole_tile_prologue=True, emit_lhs=True, name="mlstm_q")
    k = _fused_mm(rows=[("u", xc, ML_DI, 0)], w=mlstm_w_k[0], k_dim=ML_DI, n_cols=ML_HEADS * ML_DQK, tm=2048, tn=512,
                  epilogue=lambda acc, e, m, rs: acc * (ML_DQK ** -0.5), name="mlstm_k")
    v = _fused_mm(rows=[("u", p2, ML_DI, 0)], w=mlstm_w_v[0], k_dim=ML_DI, n_cols=ML_DI, tm=2048, tn=512,
                  name="mlstm_v")
    gb = _pad_cols(mlstm_gate_b[0].reshape(1, 4 * ML_HEADS), LANES)
    ng = mlstm_norm_g[0][None, :]
    n0 = _pad_cols(state_mlstm_n[:, 0].reshape(N_SAMPLE_SEQ * 2 * ML_HEADS, ML_DQK), LANES).reshape(
        N_SAMPLE_SEQ, 2, ML_HEADS, LANES)
    m0 = jnp.broadcast_to(state_mlstm_m[:, 0][..., None], (N_SAMPLE_SEQ, 2, ML_HEADS, LANES))
    hn_p, new_c, new_n, new_m = _mlstm_scan(q, k, v, gates, gb, ng, None, nb=N_PROMPT_SEQ, L=L_PROMPT, hp=HP_P,
                                            emit_state=True)
    (hn_s,) = _mlstm_scan(q, k, v, gates, gb, ng, (state_mlstm_C, n0, m0), nb=N_SAMPLE_SEQ, L=L_SAMPLE, hp=HP_S,
                          emit_state=False)
    x = out_proj(x, 2, [("d", hn_p, hn_s, ML_DI, 0), ("u", xc, ML_DI, 0), ("u", p2, ML_DI, 1)],
                 [mlstm_skip[0][None, :]], _pro_mlstm_gate, mlstm_w_down[0], ML_DI, "mlstm_down")
    x = _mlp(x, 2, g2, mod_all, mlp_w1, mlp_w2)

    lam_init = 0.8 - 0.6 * math.exp(-0.3 * 3)
    p3 = in_proj(x, 3, diff_w_qkv[0], 3 * D, 1024, name="diff_qkv")
    c64, s64 = _rope_tables(DF_D)
    c128 = jnp.concatenate([c64, c64], axis=-1)
    s128 = jnp.concatenate([s64, s64], axis=-1)
    dparams = (diff_lq1, diff_lk1, diff_lq2, diff_lk2, diff_subln_g)
    od_p, new_dk, new_dv = _diff_attn(p3, *dparams, None, nb=N_PROMPT_SEQ, L=L_PROMPT, lam_init=lam_init)
    ctx = (cache_diff_k.reshape(N_SAMPLE_SEQ, 1, PAST, D), cache_diff_v.reshape(N_SAMPLE_SEQ, 1, PAST, D), c128, s128)
    (od_s,) = _diff_attn(p3, *dparams, ctx, nb=N_SAMPLE_SEQ, L=L_SAMPLE, lam_init=lam_init)
    x = out_proj(x, 3, [("d", od_p, od_s, D, 0)], [], _pro_cast, diff_w_o[0], D, "diff_out")
    y_prompt, y_sample = _mlp(x, 3, g2, mod_all, mlp_w1, mlp_w2, final_g=final_g[None, :])
    y_prompt = y_prompt.reshape(N_PROMPT_SEQ, L_PROMPT, D)
    y_sample = y_sample.reshape(N_SAMPLE_SEQ, L_SAMPLE, D)
    return (y_prompt, y_sample, new_ssd, new_ckv, new_kr, new_c,
            new_n[None].reshape(N_PROMPT_SEQ, 1, 2, ML_HEADS, ML_DQK),
            new_m[..., 0].reshape(N_PROMPT_SEQ, 1, 2, ML_HEADS),
            new_dk.reshape(N_PROMPT_SEQ, 1, L_PROMPT, DF_HEADS, 2 * DF_D),
            new_dv.reshape(N_PROMPT_SEQ, 1, L_PROMPT, DF_HEADS, 2 * DF_D))
```

```python
import functools
import math

import jax
import jax.numpy as jnp
from jax import lax
from jax.experimental import pallas as pl
from jax.experimental.pallas import tpu as pltpu

F32 = jnp.float32
BF16 = jnp.bfloat16

D = 1024
DEPTH = 4
D_FF = 4 * D
EPS = 1e-6
ROPE_BASE = 10000.0
CONV_W = 5
GRID_W = 64

N_PROMPT_SEQ = 32
L_PROMPT = 256
N_SAMPLE_SEQ = 2
L_SAMPLE = 1024
PAST = 256
ROWS_P = N_PROMPT_SEQ * L_PROMPT
ROWS_S = N_SAMPLE_SEQ * L_SAMPLE
ROWS = ROWS_P + ROWS_S

SSD_DI = 2 * D
SSD_HEADS = 32
SSD_P = 64
SSD_GROUPS = 8
SSD_N = 128
SSD_HPG = SSD_HEADS // SSD_GROUPS

MLA_HEADS = 16
MLA_Q_RANK = 512
MLA_KV_RANK = 256
MLA_NOPE = 64
MLA_ROPE = 32
MLA_V = 64

ML_DI = 2 * D
ML_HEADS = 8
ML_DQK = 128
ML_DV = 256

DF_HEADS = 8
DF_D = 64

LANES = 128
VMEM_LIMIT = 56 * 1024 * 1024


def _cparams(sem):
    return pltpu.CompilerParams(dimension_semantics=sem, vmem_limit_bytes=VMEM_LIMIT)


def _silu(x):
    return x * jax.nn.sigmoid(x)


def _softplus(x):
    return jnp.maximum(x, 0.0) + jnp.log1p(jnp.exp(-jnp.abs(x)))


def _rms(x, g):
    r = lax.rsqrt(jnp.mean(x * x, axis=-1, keepdims=True) + EPS)
    return (x * r) * g


def _dot(a, b):
    return jnp.dot(a, b, preferred_element_type=F32)


def _dot_nt(a, b):
    return lax.dot_general(a, b, (((1,), (1,)), ((), ())), preferred_element_type=F32)


def _dot_tn(a, b):
    return lax.dot_general(a, b, (((0,), (0,)), ((), ())), preferred_element_type=F32)


MOD_ROWS = 1024


def _group_of_tile(i, tm, sub=0):
    row0 = i * tm + sub * MOD_ROWS
    return jnp.where(row0 < ROWS_P, 0, 1 + (row0 - ROWS_P) // L_SAMPLE)


def _ada_body(c_ref, w_ref, b_ref, o_ref):
    s = _silu(c_ref[...]).astype(BF16)
    o_ref[...] = _dot(s, w_ref[...].astype(BF16)) + b_ref[...]


def _ada_mod(cvec, ada_w, ada_b):
    tn = 1536
    out = pl.pallas_call(
        _ada_body,
        grid=(DEPTH, 6 * D // tn),
        in_specs=[
            pl.BlockSpec((8, D), lambda l, j: (0, 0)),
            pl.BlockSpec((None, D, tn), lambda l, j: (l, 0, j)),
            pl.BlockSpec((None, 1, tn), lambda l, j: (l, 0, j)),
        ],
        out_specs=pl.BlockSpec((None, 8, tn), lambda l, j: (l, 0, j)),
        out_shape=jax.ShapeDtypeStruct((DEPTH, 8, 6 * D), F32),
        compiler_params=_cparams(("parallel", "parallel")),
        name="ada_mod",
    )(cvec, ada_w, ada_b.reshape(DEPTH, 1, 6 * D))
    return out[:, :3].reshape(DEPTH, 3, 6, D)


class _Pro:
    def __init__(self, rows, consts, mod, i, hs, lhs_out):
        self.rows, self.consts, self.mod, self.i = rows, consts, mod, i
        self._hs, self._lhs_out = hs, lhs_out

    def emit(self, val, rows=slice(None), cols=slice(None)):
        vb = val.astype(BF16)
        self._hs[rows, cols] = vb
        if self._lhs_out is not None:
            self._lhs_out[rows, cols] = vb


def _seq_conv_silu(x, w_ref, b_ref, cols, seq):
    n_rows = x.shape[0]
    pos = lax.broadcasted_iota(jnp.int32, x.shape, 0) & (seq - 1)
    acc = None
    for k in range(CONV_W):
        off = k - CONV_W // 2
        src = pos + off
        tap = x if off == 0 else pltpu.roll(x, (-off) % n_rows, 0)
        tap = jnp.where(jnp.logical_and(src >= 0, src < seq), tap, 0.0) * w_ref[k:k + 1, cols]
        acc = b_ref[:, cols] + tap if acc is None else acc + tap
    return _silu(acc)


CONV_COLS = 256
ROW_CHUNK = 512


def _fused_mm(*, rows, w, k_dim, n_cols, tm, tn, prologue=None, consts=(), mod=None,
              epilogue=None, erows=(), emod=None, w_col0=0, emit_lhs=False, extra_w=None, econv=None, row_chunk=ROW_CHUNK, out_dtype=BF16,
              name):
    npt = ROWS_P // tm
    grid = (ROWS // tm, n_cols // tn)
    has_dual = any(r[0] == "d" for r in rows)
    dual_epi = any(isinstance(e, tuple) for e in erows)
    if prologue is None:
        assert len(rows) == 1 and rows[0][0] == "u" and not emit_lhs

    in_specs, args = [], []
    for r in rows:
        if r[0] == "u":
            _, arr, width, cb = r
            in_specs.append(pl.BlockSpec((tm, width), lambda i, j, cb=cb: (i, cb)))
            args.append(arr)
        else:
            _, arr_p, arr_s, width, cb = r
            in_specs.append(pl.BlockSpec((tm, width), lambda i, j, cb=cb: (jnp.minimum(i, npt - 1), cb)))
            in_specs.append(pl.BlockSpec((tm, width), lambda i, j, cb=cb: (jnp.maximum(i - npt, 0), cb)))
            args += [arr_p, arr_s]
    for c in consts:
        in_specs.append(pl.BlockSpec(c.shape, lambda i, j, nd=c.ndim: (0,) * nd))
        args.append(c)
    n_sub = max(tm // MOD_ROWS, 1)
    if mod is not None:
        mod_arr, mod_layer = mod
        for s in range(n_sub):
            in_specs.append(pl.BlockSpec((None, None, 6, D),
                                         lambda i, j, s=s: (mod_layer, _group_of_tile(i, tm, s), 0, 0)))
            args.append(mod_arr)
    in_specs.append(pl.BlockSpec((k_dim, tn), lambda i, j: (0, w_col0 // tn + j)))
    args.append(w)
    if extra_w is not None:
        in_specs.append(pl.BlockSpec(extra_w.shape, lambda i, j: (0, 0)))
        args.append(extra_w)
    for e in erows:
        if isinstance(e, tuple):
            in_specs.append(pl.BlockSpec((tm, tn), lambda i, j: (jnp.minimum(i, npt - 1), j)))
            in_specs.append(pl.BlockSpec((tm, tn), lambda i, j: (jnp.maximum(i - npt, 0), j)))
            args += list(e)
        else:
            in_specs.append(pl.BlockSpec((tm, tn), lambda i, j: (i, j)))
            args.append(e)
    if econv is not None:
        conv_w, conv_b, conv_j0, conv_j1, keep_raw = econv

        def conv_blk(j):
            return jnp.clip(j - conv_j0, 0, conv_j1 - conv_j0 - 1)

        in_specs.append(pl.BlockSpec((CONV_W, tn), lambda i, j: (0, conv_blk(j))))
        in_specs.append(pl.BlockSpec((1, tn), lambda i, j: (0, conv_blk(j))))
        args += [conv_w, conv_b]
    if emod is not None:
        emod_arr, emod_layer = emod
        in_specs.append(pl.BlockSpec((None, None, 6, tn), lambda i, j: (emod_layer, _group_of_tile(i, tm), 0, j)))
        args.append(emod_arr)

    out_specs = [pl.BlockSpec((tm, tn), lambda i, j: (i, j))]
    out_shape = [jax.ShapeDtypeStruct((ROWS, n_cols), out_dtype)]
    if emit_lhs:
        out_specs.append(pl.BlockSpec((tm, k_dim), lambda i, j: (i, 0)))
        out_shape.append(jax.ShapeDtypeStruct((ROWS, k_dim), BF16))
    if extra_w is not None:
        out_specs.append(pl.BlockSpec((tm, extra_w.shape[1]), lambda i, j: (i, 0)))
        out_shape.append(jax.ShapeDtypeStruct((ROWS, extra_w.shape[1]), F32))
    if econv is not None and keep_raw:
        out_specs.append(pl.BlockSpec((tm, tn), lambda i, j: (i, conv_blk(j))))
        out_shape.append(jax.ShapeDtypeStruct((ROWS, conv_w.shape[1]), out_dtype))
    scratch = [] if prologue is None else [pltpu.VMEM((tm, k_dim), BF16)]
    chunks = [slice(r, r + row_chunk) for r in range(0, tm, row_chunk)]

    def body(*refs):
        it = iter(refs)
        row_refs = [(next(it),) if r[0] == "u" else (next(it), next(it)) for r in rows]
        const_refs = [next(it) for _ in consts]
        mod_ref = [next(it) for _ in range(n_sub)] if mod is not None else None
        w_ref = next(it)
        extra_ref = next(it) if extra_w is not None else None
        erow_refs = [(next(it), next(it)) if isinstance(e, tuple) else (next(it),) for e in erows]
        cw_ref, cb_ref = (next(it), next(it)) if econv is not None else (None, None)
        emod_ref = next(it) if emod is not None else None
        out_ref = next(it)
        lhs_out = next(it) if emit_lhs else None
        extra_out = next(it) if extra_w is not None else None
        conv_out = next(it) if (econv is not None and keep_raw) else None
        hs = next(it) if prologue is not None else None
        i = pl.program_id(0)
        j = pl.program_id(1)

        def compute(first, use_prompt=True):
            chosen = [rr[0] if (len(rr) == 1 or use_prompt) else rr[1] for rr in row_refs]
            echosen = [er[0] if (len(er) == 1 or use_prompt) else er[1] for er in erow_refs]
            p = _Pro(chosen, const_refs, mod_ref, i, hs, lhs_out)
            wb = w_ref[...].astype(BF16)
            for rs in chunks:
                if first:
                    prologue(p, rs)
                lhs = chosen[0][rs, :] if prologue is None else hs[rs, :]
                acc = _dot(lhs, wb)
                if epilogue is not None:
                    acc = epilogue(acc, echosen, emod_ref, rs)
                if econv is None or keep_raw:
                    out_ref[rs, :] = acc.astype(out_dtype)
                if econv is not None:
                    seq = jnp.where(i * tm + rs.start < ROWS_P, L_PROMPT, L_SAMPLE)
                    in_conv = jnp.logical_and(j >= conv_j0, j < conv_j1)

                    def conv_store(acc=acc, rs=rs, seq=seq):
                        dst = conv_out if keep_raw else out_ref
                        for c0 in range(0, tn, CONV_COLS):
                            cols = slice(c0, c0 + CONV_COLS)
                            y = _seq_conv_silu(acc[:, cols], cw_ref, cb_ref, cols, seq)
                            dst[rs, cols] = y.astype(out_dtype)

                    def plain_store(acc=acc, rs=rs):
                        out_ref[rs, :] = acc.astype(out_dtype)

                    pl.when(in_conv)(conv_store)
                    if not keep_raw:
                        pl.when(jnp.logical_not(in_conv))(plain_store)
                if first and extra_w is not None:
                    extra_out[rs, :] = _dot(lhs, extra_ref[...].astype(BF16))

        if prologue is None:
            if extra_w is None:
                compute(False)
            else:
                pl.when(j == 0)(lambda: compute(True))
                pl.when(j > 0)(lambda: compute(False))
        else:
            if has_dual or dual_epi:
                pl.when(jnp.logical_and(j == 0, i < npt))(lambda: compute(True, True))
                pl.when(jnp.logical_and(j == 0, i >= npt))(lambda: compute(True, False))
            else:
                pl.when(j == 0)(lambda: compute(True))
            if dual_epi:
                pl.when(jnp.logical_and(j > 0, i < npt))(lambda: compute(False, True))
                pl.when(jnp.logical_and(j > 0, i >= npt))(lambda: compute(False, False))
            else:
                pl.when(j > 0)(lambda: compute(False))

    res = pl.pallas_call(
        body,
        grid=grid,
        in_specs=in_specs,
        out_specs=out_specs,
        out_shape=out_shape,
        scratch_shapes=scratch,
        compiler_params=_cparams(("parallel", "arbitrary")),
        name=name,
    )(*args)
    return res if len(res) > 1 else res[0]


def _pro_normmod(p, rs):
    m = p.mod[rs.start // MOD_ROWS]
    h = _rms(p.rows[0][rs, :], p.consts[0][...]) * (1.0 + m[1:2, :]) + m[0:1, :]
    p.emit(h, rs)


def _pro_cast(p, rs):
    p.emit(p.rows[0][rs, :], rs)


def _pro_rms(p, rs):
    p.emit(_rms(p.rows[0][rs, :].astype(F32), p.consts[0][...]), rs)


def _epi_residual(gate_row):
    def epi(acc, erows, emod, rs):
        return erows[0][rs, :] + emod[gate_row:gate_row + 1, :] * acc
    return epi


def _mlp_body(*refs, final):
    it = iter(refs)
    x_ref, g_ref, mod_ref, w1_ref, w2_ref = (next(it) for _ in range(5))
    fg_ref = next(it) if final else None
    outs = [next(it), next(it)] if final else [next(it)]
    hs, acc = next(it), next(it)
    i = pl.program_id(0)
    f = pl.program_id(1)
    nf = pl.num_programs(1)
    tm = x_ref.shape[0]
    chunks = [slice(r, r + ROW_CHUNK) for r in range(0, tm, ROW_CHUNK)]

    def step(first, last):
        w1b = w1_ref[...].astype(BF16)
        w2b = w2_ref[...].astype(BF16)
        for rs in chunks:
            if first:
                h = _rms(x_ref[rs, :], g_ref[...]) * (1.0 + mod_ref[4:5, :]) + mod_ref[3:4, :]
                hs[rs, :] = h.astype(BF16)
            u = jnp.square(jnp.maximum(_dot(hs[rs, :], w1b), 0.0))
            tot = _dot(u.astype(BF16), w2b)
            if not first:
                tot = acc[rs, :] + tot
            if not last:
                acc[rs, :] = tot
                continue
            y = x_ref[rs, :] + mod_ref[5:6, :] * tot
            if not final:
                outs[0][rs, :] = y
            else:
                y = _rms(y, fg_ref[...])
                npt = ROWS_P // tm

                @pl.when(i < npt)
                def _():
                    outs[0][rs, :] = y

                @pl.when(i >= npt)
                def _():
                    outs[1][rs, :] = y

    pl.when(f == 0)(lambda: step(True, False))
    pl.when(jnp.logical_and(f > 0, f < nf - 1))(lambda: step(False, False))
    pl.when(f == nf - 1)(lambda: step(False, True))


def _mlp(x, layer, g, mod, w1, w2, final_g=None):
    tm, tf = 1024, 1024
    npt = ROWS_P // tm
    final = final_g is not None
    in_specs = [
        pl.BlockSpec((tm, D), lambda i, f: (i, 0)),
        pl.BlockSpec((None, 1, D), lambda i, f: (layer, 0, 0)),
        pl.BlockSpec((None, None, 6, D), lambda i, f: (layer, _group_of_tile(i, tm), 0, 0)),
        pl.BlockSpec((None, D, tf), lambda i, f: (layer, 0, f)),
        pl.BlockSpec((None, tf, D), lambda i, f: (layer, f, 0)),
    ]
    args = [x, g, mod, w1, w2]
    if final:
        in_specs.append(pl.BlockSpec((1, D), lambda i, f: (0, 0)))
        args.append(final_g)
        out_specs = [pl.BlockSpec((tm, D), lambda i, f: (jnp.minimum(i, npt - 1), 0)),
                     pl.BlockSpec((tm, D), lambda i, f: (jnp.maximum(i - npt, 0), 0))]
        out_shape = [jax.ShapeDtypeStruct((ROWS_P, D), F32), jax.ShapeDtypeStruct((ROWS_S, D), F32)]
    else:
        out_specs = pl.BlockSpec((tm, D), lambda i, f: (i, 0))
        out_shape = jax.ShapeDtypeStruct((ROWS, D), F32)
    return pl.pallas_call(
        functools.partial(_mlp_body, final=final),
        grid=(ROWS // tm, D_FF // tf),
        in_specs=in_specs,
        out_specs=out_specs,
        out_shape=out_shape,
        scratch_shapes=[pltpu.VMEM((tm, D), BF16), pltpu.VMEM((tm, D), F32)],
        compiler_params=_cparams(("arbitrary", "arbitrary")),
        name="mlp_final" if final else "mlp",
    )(*args)


SCAN_T = 256


def _seg_cumsums(a):
    n = a.shape[0]
    ii = lax.broadcasted_iota(jnp.int32, (SCAN_T, SCAN_T), 0)
    jj = lax.broadcasted_iota(jnp.int32, (SCAN_T, SCAN_T), 1)
    lower = jnp.where(ii >= jj, 1.0, 0.0).astype(BF16)
    upper = jnp.where(ii <= jj, 1.0, 0.0).astype(BF16)
    hi = a.astype(BF16)
    rest = a - hi.astype(F32)
    mid = rest.astype(BF16)
    lo = (rest - mid.astype(F32)).astype(BF16)
    parts = jnp.concatenate([hi, mid, lo], axis=1)
    pre, suf = [], []
    for c in range(n // SCAN_T):
        pc = parts[SCAN_T * c:SCAN_T * (c + 1), :]
        for tri, out in ((lower, pre), (upper, suf)):
            s3 = _dot(tri, pc)
            out.append(s3[:, :LANES] + s3[:, LANES:2 * LANES] + s3[:, 2 * LANES:])
    return jnp.concatenate(pre, axis=0), jnp.concatenate(suf, axis=0)


LOG2E = 1.4426950408889634


def _ssd_body(*refs, L, has_s0, emit_state):
    it = iter(refs)
    z_ref, x_ref, b_ref, c_ref, dt_ref = (next(it) for _ in range(5))
    dtb_ref, alog_ref, dl_ref = next(it), next(it), next(it)
    s0_ref = next(it) if has_s0 else None
    y_ref = next(it)
    st_out = next(it) if emit_state else None
    xa, ba, ca, cbs, cumc, crp, yacc, st = (next(it) for _ in range(8))

    T = SCAN_T
    nc = L // T
    g0 = pl.program_id(0)

    xa[...] = x_ref[...].astype(F32)
    ba[...] = b_ref[...].astype(F32)
    ca[...] = c_ref[...].astype(F32)

    dt_all = _softplus(dt_ref[...] + dtb_ref[...])
    a_all = dt_all * (-jnp.exp(alog_ref[...]))
    shift = jnp.where(g0 == 0, 0, LANES - SSD_HPG * g0)
    l2dt = jnp.log2(pltpu.roll(dt_all, shift, 1))
    ar = pltpu.roll(a_all, shift, 1) * LOG2E
    cum_f, cum_b = _seg_cumsums(ar)
    cumc[0] = cum_f
    cumc[1] = cum_b
    crp[0] = (cum_f - l2dt).T
    crp[1] = (cum_b - l2dt).T

    yacc[...] = xa[...] * dl_ref[...]
    if has_s0:
        st[...] = s0_ref[...]

    ii = lax.broadcasted_iota(jnp.int32, (T, T), 0)
    jj = lax.broadcasted_iota(jnp.int32, (T, T), 1)

    def chunk(c, d):
        rows = slice(T * c, T * (c + 1))
        first = c == (0 if d == 0 else nc - 1)
        last = c == (nc - 1 if d == 0 else 0)
        zero_state = first and not has_s0
        need_state = emit_state or not last
        mask = (ii >= jj) if d == 0 else (ii <= jj)
        end = T * c + (T - 1 if d == 0 else 0)
        xav = xa[rows, :]
        xab = xav.astype(BF16)
        cab = ca[rows, :].astype(BF16)
        if need_state:
            xat = xav.T.astype(BF16)
            bat = ba[rows, :].T
        for r in range(SSD_HPG):
            lane = SSD_HEADS * d + r
            hs = slice(SSD_P * r, SSD_P * (r + 1))
            cc = cumc[d, rows, lane:lane + 1]
            cr = crp[d, lane:lane + 1, rows]
            dec = jnp.where(mask, jnp.exp2(cc - cr), 0.0)
            y = _dot((cbs[c] * dec).astype(BF16), xab[:, hs])
            if not zero_state:
                y = y + _dot_nt(cab, st[d, r].astype(BF16)) * jnp.exp2(cc)
            yacc[rows, hs] += y
            if need_state:
                tot = cumc[d, end:end + 1, lane:lane + 1]
                upd = _dot_nt(xat[hs, :], (bat * jnp.exp2(tot - cr)).astype(BF16))
                st[d, r] = upd if zero_state else jnp.exp2(tot) * st[d, r] + upd

    for c in range(nc):
        rows = slice(T * c, T * (c + 1))
        cbs[c] = _dot_nt(ca[rows, :].astype(BF16), ba[rows, :].astype(BF16))
    for c in range(nc):
        chunk(c, 0)
        chunk(nc - 1 - c, 1)

    y_ref[...] = (yacc[...] * _silu(z_ref[...].astype(F32))).astype(BF16)
    if emit_state:
        st_out[...] = st[...]


def _ssd_scan(p0, pdt, dtb, alog, dl, s0, *, nb, L, emit_state):
    rb0 = 0 if s0 is None else ROWS_P // L
    gw = SSD_HPG * SSD_P
    nc = L // SCAN_T
    x0 = SSD_DI // gw
    b0 = 2 * SSD_DI // SSD_N
    c0 = b0 + SSD_GROUPS
    in_specs = [
        pl.BlockSpec((L, gw), lambda g, b: (rb0 + b, g)),
        pl.BlockSpec((L, gw), lambda g, b: (rb0 + b, x0 + g)),
        pl.BlockSpec((L, SSD_N), lambda g, b: (rb0 + b, b0 + g)),
        pl.BlockSpec((L, SSD_N), lambda g, b: (rb0 + b, c0 + g)),
        pl.BlockSpec((L, LANES), lambda g, b: (rb0 + b, 0)),
        pl.BlockSpec((1, LANES), lambda g, b: (0, 0)),
        pl.BlockSpec((1, LANES), lambda g, b: (0, 0)),
        pl.BlockSpec((1, gw), lambda g, b: (0, g)),
    ]
    args = [p0, p0, p0, p0, pdt, dtb, alog, dl]
    state_spec = pl.BlockSpec((None, None, 2, SSD_HPG, SSD_P, SSD_N), lambda g, b: (b, 0, 0, g, 0, 0))
    if s0 is not None:
        in_specs.append(state_spec)
        args.append(s0)
    out_specs = [pl.BlockSpec((L, gw), lambda g, b: (b, g))]
    out_shape = [jax.ShapeDtypeStruct((nb * L, SSD_DI), BF16)]
    if emit_state:
        out_specs.append(state_spec)
        out_shape.append(jax.ShapeDtypeStruct((nb, 1, 2, SSD_HEADS, SSD_P, SSD_N), F32))
    scratch = [
        pltpu.VMEM((L, gw), F32),
        pltpu.VMEM((L, SSD_N), F32),
        pltpu.VMEM((L, SSD_N), F32),
        pltpu.VMEM((nc, SCAN_T, SCAN_T), F32),
        pltpu.VMEM((2, L, LANES), F32),
        pltpu.VMEM((2, LANES, L), F32),
        pltpu.VMEM((L, gw), F32),
        pltpu.VMEM((2, SSD_HPG, SSD_P, SSD_N), F32),
    ]
    res = pl.pallas_call(
        functools.partial(_ssd_body, L=L, has_s0=s0 is not None, emit_state=emit_state),
        grid=(SSD_GROUPS, nb),
        in_specs=in_specs,
        out_specs=out_specs,
        out_shape=out_shape,
        scratch_shapes=scratch,
        compiler_params=_cparams(("parallel", "parallel")),
        name="ssd_scan_p" if s0 is None else "ssd_scan_s",
    )(*args)
    return res


TQ = 256
HP_P = 4
HP_S = 2


def _mla_body(*refs, L, sample):
    it = iter(refs)
    cq_ref, ckv_ref, krs_ref = next(it), next(it), next(it)
    gq_ref, gkv_ref = next(it), next(it)
    wq, wqs, wk, wv = (next(it) for _ in range(4))
    if sample:
        cckv_ref, ckr_ref, cq_t, sq_t, ck_t, sk_t = (next(it) for _ in range(6))
    o_ref = next(it)
    if not sample:
        ckv_out, kr_out = next(it), next(it)
    kk, vv = next(it), next(it)
    qt = pl.program_id(1)
    scale = (MLA_NOPE + MLA_ROPE) ** -0.5
    hb = 4 * LANES

    @pl.when(qt == 0)
    def _():
        ckv = _rms(ckv_ref[...].astype(F32), gkv_ref[...])
        kr_own = krs_ref[:, 0:LANES]
        if sample:
            keys = jnp.concatenate([cckv_ref[...], ckv], axis=0)
            kr_own = kr_own * ck_t[...] + krs_ref[:, LANES:2 * LANES] * sk_t[...]
            kr_all = jnp.concatenate([ckr_ref[...], kr_own], axis=0)
        else:
            ckv_out[...] = ckv
            kr_out[...] = kr_own[:, MLA_NOPE:MLA_NOPE + MLA_ROPE]
            keys = ckv
            kr_all = kr_own
        kb = keys.astype(BF16)
        for blk in range(MLA_HEADS * LANES // hb):
            cols = slice(hb * blk, hb * (blk + 1))
            kn = _dot(kb, wk[:, cols])
            kk[:, cols] = (kn + jnp.concatenate([kr_all] * 4, axis=1)).astype(BF16)
            vv[:, cols] = _dot(kb, wv[:, cols]).astype(BF16)

    cq = _rms(cq_ref[...].astype(F32), gq_ref[...]).astype(BF16)
    for blk in range(MLA_HEADS * LANES // hb):
        qa = _dot(cq, wq[:, hb * blk:hb * (blk + 1)])
        if sample:
            qs = _dot(cq, wqs[:, hb * blk:hb * (blk + 1)])
        pair = None
        for hh in range(4):
            h = 4 * blk + hh
            cols = slice(LANES * h, LANES * (h + 1))
            qh = qa[:, LANES * hh:LANES * (hh + 1)]
            if sample:
                qh = qh * cq_t[...] + qs[:, LANES * hh:LANES * (hh + 1)] * sq_t[...]
            s = _dot_nt((qh * scale).astype(BF16), kk[:, cols])
            e = jnp.exp(s - jnp.max(s, axis=1, keepdims=True))
            o = _dot(e.astype(BF16), vv[:, cols]) / jnp.sum(e, axis=1, keepdims=True)
            if h % 2 == 0:
                pair = o
            else:
                o_ref[:, LANES * (h // 2):LANES * (h // 2 + 1)] = (pair + o).astype(BF16)


def _mla_attn(p1, p1s, gq, gkv, wq, wqs, wk, wv, ctx, *, nb, L):
    sample = ctx is not None
    nq = L // TQ
    tk = L + (PAST if sample else 0)
    rbq0 = ROWS_P // TQ if sample else 0
    rbs0 = ROWS_P // L if sample else 0
    ckv_blk = MLA_Q_RANK // MLA_KV_RANK

    def const(a):
        return pl.BlockSpec(a.shape, lambda b, q, nd=a.ndim: (0,) * nd)

    in_specs = [
        pl.BlockSpec((TQ, MLA_Q_RANK), lambda b, q: (rbq0 + b * nq + q, 0)),
        pl.BlockSpec((L, MLA_KV_RANK), lambda b, q: (rbs0 + b, ckv_blk)),
        pl.BlockSpec((L, 2 * LANES), lambda b, q: (rbs0 + b, 0)),
        const(gq), const(gkv), const(wq), const(wqs), const(wk), const(wv),
    ]
    args = [p1, p1, p1s, gq, gkv, wq, wqs, wk, wv]
    if sample:
        cckv, ckr, cpad, spad = ctx
        in_specs += [
            pl.BlockSpec((None, None, PAST, MLA_KV_RANK), lambda b, q: (b, 0, 0, 0)),
            pl.BlockSpec((None, None, PAST, LANES), lambda b, q: (b, 0, 0, 0)),
            pl.BlockSpec((TQ, LANES), lambda b, q: (q, 0)),
            pl.BlockSpec((TQ, LANES), lambda b, q: (q, 0)),
            const(cpad), const(spad),
        ]
        args += [cckv, ckr, cpad, spad, cpad, spad]
    out_specs = [pl.BlockSpec((TQ, MLA_HEADS * MLA_V), lambda b, q: (b * nq + q, 0))]
    out_shape = [jax.ShapeDtypeStruct((nb * L, MLA_HEADS * MLA_V), BF16)]
    if not sample:
        out_specs += [
            pl.BlockSpec((None, None, L, MLA_KV_RANK), lambda b, q: (b, 0, 0, 0)),
            pl.BlockSpec((None, None, L, MLA_ROPE), lambda b, q: (b, 0, 0, 0)),
        ]
        out_shape += [
            jax.ShapeDtypeStruct((nb, 1, L, MLA_KV_RANK), F32),
            jax.ShapeDtypeStruct((nb, 1, L, MLA_ROPE), F32),
        ]
    scratch = [
        pltpu.VMEM((tk, MLA_HEADS * LANES), BF16),
        pltpu.VMEM((tk, MLA_HEADS * LANES), BF16),
    ]
    return pl.pallas_call(
        functools.partial(_mla_body, L=L, sample=sample),
        grid=(nb, nq),
        in_specs=in_specs,
        out_specs=out_specs,
        out_shape=out_shape,
        scratch_shapes=scratch,
        compiler_params=_cparams(("parallel", "arbitrary")),
        name="mla_attn_s" if sample else "mla_attn_p",
    )(*args)


def _log_sigmoid(x):
    return -_softplus(-x)


def _mlstm_body(*refs, L, hp, has_s0, emit_state):
    it = iter(refs)
    q_ref, k_ref, v_ref, g_ref, gb_ref, ng_ref = (next(it) for _ in range(6))
    if has_s0:
        c0_ref, n0_ref, m0_ref = next(it), next(it), next(it)
    h_ref = next(it)
    if emit_state:
        c_out, n_out, m_out = next(it), next(it), next(it)
    gsc, gtr, kq, vts, kts, hacc, cst, nst, mst = (next(it) for _ in range(9))

    T = SCAN_T
    nc = L // T
    h0 = hp * pl.program_id(1)

    gts = g_ref[...] + gb_ref[...]
    gr = pltpu.roll(gts, jnp.where(h0 == 0, 0, LANES - h0), 1)
    b_f, b_b = _seg_cumsums(_log_sigmoid(gr))
    gsc[0] = gr
    gsc[1] = b_f
    gsc[2] = b_b
    gtr[0] = gr.T
    gtr[1] = b_f.T
    gtr[2] = b_b.T

    jj = lax.broadcasted_iota(jnp.int32, (T, T), 0)
    ii = lax.broadcasted_iota(jnp.int32, (T, T), 1)

    def chunk(hh, c, d):
        rows = slice(T * c, T * (c + 1))
        qcols = slice(ML_DQK * hh, ML_DQK * (hh + 1))
        first = c == (0 if d == 0 else nc - 1)
        last = c == (nc - 1 if d == 0 else 0)
        zero_state = first and not has_s0
        li = 2 * ML_HEADS * d + hh
        lb = li + ML_HEADS
        b_row = gtr[1 + d, lb:lb + 1, rows]
        logi_row = gtr[0, li:li + 1, rows]
        cj = gsc[1 + d, rows, lb:lb + 1] - gsc[0, rows, li:li + 1]
        mask = (jj <= ii) if d == 0 else (jj >= ii)
        dlog = jnp.where(mask, b_row - cj, -jnp.inf)
        m_prev = jnp.zeros((1, 1), F32) if zero_state else mst[hh, d][:, 0:1]
        inter = b_row + m_prev
        mcomb = jnp.maximum(inter, jnp.max(dlog, axis=0, keepdims=True))
        s = kq[hh, c] * jnp.exp(dlog - mcomb)
        vt = vts[hh, c]
        num = _dot(vt, s.astype(BF16))
        den = jnp.sum(s, axis=0, keepdims=True)
        if not zero_state:
            iw = jnp.exp(inter - mcomb)
            qt = q_ref[rows, qcols].T
            num = num + iw * _dot_tn(cst[hh, d].astype(BF16), qt)
            n8 = jnp.broadcast_to(nst[hh, d], (8, ML_DQK)).astype(BF16)
            den = den + iw * _dot(n8, qt)[0:1]
        hc = num / jnp.maximum(jnp.abs(den), jnp.exp(-mcomb))
        if d == 0:
            hacc[hh, :, rows] = hc
        else:
            hacc[hh, :, rows] += hc
        if emit_state or not last:
            end = T * c + (T - 1 if d == 0 else 0)
            bq = gtr[1 + d, lb:lb + 1, end:end + 1]
            wlog = bq - b_row + logi_row
            m_new = jnp.maximum(bq + m_prev, jnp.max(wlog, axis=1, keepdims=True))
            sw = jnp.exp(wlog - m_new)
            upd = _dot_nt((kts[hh, c].astype(F32) * sw).astype(BF16), vt)
            nsum = _dot(jnp.broadcast_to(sw, (8, T)).astype(BF16), k_ref[rows, qcols])[0:1]
            if zero_state:
                cst[hh, d] = upd
                nst[hh, d] = nsum
            else:
                cw = jnp.exp(bq + m_prev - m_new)
                cst[hh, d] = cw * cst[hh, d] + upd
                nst[hh, d] = cw * nst[hh, d] + nsum
            mst[hh, d] = jnp.broadcast_to(m_new, (1, LANES))

    for hh in range(hp):
        qcols = slice(ML_DQK * hh, ML_DQK * (hh + 1))
        vcols = slice(ML_DV * hh, ML_DV * (hh + 1))
        if has_s0:
            for d in range(2):
                cst[hh, d] = c0_ref[d, hh]
                nst[hh, d] = n0_ref[d, pl.ds(h0 + hh, 1), :]
                mst[hh, d] = m0_ref[d, pl.ds(h0 + hh, 1), :]
        for c in range(nc):
            rows = slice(T * c, T * (c + 1))
            kc = k_ref[rows, qcols]
            kq[hh, c] = _dot_nt(kc, q_ref[rows, qcols])
            kts[hh, c] = kc.T
            vts[hh, c] = v_ref[rows, vcols].T
        for c in range(nc):
            chunk(hh, c, 0)
        for c in range(nc):
            chunk(hh, nc - 1 - c, 1)
        ht = hacc[hh]
        r = lax.rsqrt(jnp.mean(ht * ht, axis=0, keepdims=True) + EPS)
        h_ref[:, vcols] = ((ht * r).T * ng_ref[:, vcols]).astype(BF16)
        if emit_state:
            for d in range(2):
                c_out[d, hh] = cst[hh, d]
                n_out[d, pl.ds(h0 + hh, 1), :] = nst[hh, d]
                m_out[d, pl.ds(h0 + hh, 1), :] = mst[hh, d]


def _mlstm_scan(q, k, v, gates, gb, ng, s0, *, nb, L, hp, emit_state):
    rb0 = 0 if s0 is None else ROWS_P // L
    in_specs = [
        pl.BlockSpec((L, hp * ML_DQK), lambda b, h: (rb0 + b, h)),
        pl.BlockSpec((L, hp * ML_DQK), lambda b, h: (rb0 + b, h)),
        pl.BlockSpec((L, hp * ML_DV), lambda b, h: (rb0 + b, h)),
        pl.BlockSpec((L, LANES), lambda b, h: (rb0 + b, 0)),
        pl.BlockSpec((1, LANES), lambda b, h: (0, 0)),
        pl.BlockSpec((1, hp * ML_DV), lambda b, h: (0, h)),
    ]
    args = [q, k, v, gates, gb, ng]
    c_spec = pl.BlockSpec((None, None, 2, hp, ML_DQK, ML_DV), lambda b, h: (b, 0, 0, h, 0, 0))
    n_spec = pl.BlockSpec((None, 2, ML_HEADS, LANES), lambda b, h: (b, 0, 0, 0))
    if s0 is not None:
        in_specs += [c_spec, n_spec, n_spec]
        args += list(s0)
    out_specs = [pl.BlockSpec((L, hp * ML_DV), lambda b, h: (b, h))]
    out_shape = [jax.ShapeDtypeStruct((nb * L, ML_DI), BF16)]
    if emit_state:
        out_specs += [c_spec, n_spec, n_spec]
        out_shape += [
            jax.ShapeDtypeStruct((nb, 1, 2, ML_HEADS, ML_DQK, ML_DV), F32),
            jax.ShapeDtypeStruct((nb, 2, ML_HEADS, LANES), F32),
            jax.ShapeDtypeStruct((nb, 2, ML_HEADS, LANES), F32),
        ]
    scratch = [
        pltpu.VMEM((3, L, LANES), F32),
        pltpu.VMEM((3, LANES, L), F32),
        pltpu.VMEM((hp, L // SCAN_T, SCAN_T, SCAN_T), F32),
        pltpu.VMEM((hp, L // SCAN_T, ML_DV, SCAN_T), BF16),
        pltpu.VMEM((hp, L // SCAN_T, ML_DQK, SCAN_T), BF16),
        pltpu.VMEM((hp, ML_DV, L), F32),
        pltpu.VMEM((hp, 2, ML_DQK, ML_DV), F32),
        pltpu.VMEM((hp, 2, 1, ML_DQK), F32),
        pltpu.VMEM((hp, 2, 1, LANES), F32),
    ]
    return pl.pallas_call(
        functools.partial(_mlstm_body, L=L, hp=hp, has_s0=s0 is not None, emit_state=emit_state),
        grid=(nb, ML_HEADS // hp),
        in_specs=in_specs,
        out_specs=out_specs,
        out_shape=out_shape,
        scratch_shapes=scratch,
        compiler_params=_cparams(("parallel", "arbitrary")),
        name="mlstm_scan_p" if s0 is None else "mlstm_scan_s",
    )(*args)


def _rope_blocks(x, ct, st):
    lane = lax.broadcasted_iota(jnp.int32, (x.shape[0], LANES), 1)
    first_half = lane % DF_D < DF_D // 2
    outs = []
    for blk in range(x.shape[1] // LANES):
        xb = x[:, LANES * blk:LANES * (blk + 1)]
        swapped = jnp.where(first_half, pltpu.roll(xb, LANES - DF_D // 2, 1), pltpu.roll(xb, DF_D // 2, 1))
        outs.append(xb * ct + swapped * st)
    return jnp.concatenate(outs, axis=1)


def _diff_body(*refs, L, sample, lam_init):
    it = iter(refs)
    q_ref, k_ref, v_ref = next(it), next(it), next(it)
    lq1, lk1, lq2, lk2, sg_ref = (next(it) for _ in range(5))
    if sample:
        ck_ref, cv_ref, cq_t, sq_t, ck_t, sk_t = (next(it) for _ in range(6))
    o_ref = next(it)
    if sample:
        ka, va = next(it), next(it)
    else:
        k_out, v_out = next(it), next(it)
    qt = pl.program_id(1)

    lam = (jnp.exp(jnp.sum(lq1[...] * lk1[...], axis=1, keepdims=True))
           - jnp.exp(jnp.sum(lq2[...] * lk2[...], axis=1, keepdims=True)) + lam_init)

    @pl.when(qt == 0)
    def _():
        if sample:
            ka[0:PAST, :] = ck_ref[...].astype(BF16)
            ka[PAST:PAST + L, :] = _rope_blocks(k_ref[...].astype(F32), ck_t[...], sk_t[...]).astype(BF16)
            va[0:PAST, :] = cv_ref[...].astype(BF16)
            va[PAST:PAST + L, :] = v_ref[...]
        else:
            k_out[...] = k_ref[...].astype(F32)
            v_out[...] = v_ref[...].astype(F32)

    q = q_ref[...].astype(F32)
    if sample:
        q = _rope_blocks(q, cq_t[...], sq_t[...])
    q = q * (DF_D ** -0.5)
    lo = lax.broadcasted_iota(jnp.int32, (TQ, LANES), 1) < DF_D

    def attend(qm, kh, vh):
        s = _dot_nt(qm, kh)
        e = jnp.exp(s - jnp.max(s, axis=1, keepdims=True))
        return _dot(e.astype(BF16), vh) / jnp.sum(e, axis=1, keepdims=True)

    for h in range(DF_HEADS):
        cols = slice(LANES * h, LANES * (h + 1))
        qh = q[:, cols]
        if sample:
            kh, vh = ka[:, cols], va[:, cols]
        else:
            kh, vh = k_ref[:, cols], v_ref[:, cols]
        a0 = attend(jnp.where(lo, qh, 0.0).astype(BF16), kh, vh)
        a1 = attend(jnp.where(lo, 0.0, qh).astype(BF16), kh, vh)
        o = a0 - lam * a1
        o_ref[:, cols] = (_rms(o, sg_ref[...]) * (1.0 - lam_init)).astype(BF16)


def _diff_attn(p3, lq1, lk1, lq2, lk2, sg, ctx, *, nb, L, lam_init):
    sample = ctx is not None
    nq = L // TQ
    rbq0 = ROWS_P // TQ if sample else 0
    rbs0 = ROWS_P // L if sample else 0

    def const(a):
        return pl.BlockSpec(a.shape, lambda b, q, nd=a.ndim: (0,) * nd)

    in_specs = [
        pl.BlockSpec((TQ, D), lambda b, q: (rbq0 + b * nq + q, 0)),
        pl.BlockSpec((L, D), lambda b, q: (rbs0 + b, 1)),
        pl.BlockSpec((L, D), lambda b, q: (rbs0 + b, 2)),
        const(lq1), const(lk1), const(lq2), const(lk2), const(sg),
    ]
    args = [p3, p3, p3, lq1, lk1, lq2, lk2, sg]
    kv_spec = pl.BlockSpec((None, None, PAST if sample else L, D), lambda b, q: (b, 0, 0, 0))
    if sample:
        ck, cv, c128, s128 = ctx
        in_specs += [
            kv_spec, kv_spec,
            pl.BlockSpec((TQ, LANES), lambda b, q: (q, 0)),
            pl.BlockSpec((TQ, LANES), lambda b, q: (q, 0)),
            const(c128), const(s128),
        ]
        args += [ck, cv, c128, s128, c128, s128]
    out_specs = [pl.BlockSpec((TQ, D), lambda b, q: (b * nq + q, 0))]
    out_shape = [jax.ShapeDtypeStruct((nb * L, D), BF16)]
    scratch = []
    if sample:
        scratch = [pltpu.VMEM((PAST + L, D), BF16), pltpu.VMEM((PAST + L, D), BF16)]
    else:
        out_specs += [kv_spec, kv_spec]
        out_shape += [jax.ShapeDtypeStruct((nb, 1, L, D), F32)] * 2
    return pl.pallas_call(
        functools.partial(_diff_body, L=L, sample=sample, lam_init=lam_init),
        grid=(nb, nq),
        in_specs=in_specs,
        out_specs=out_specs,
        out_shape=out_shape,
        scratch_shapes=scratch,
        compiler_params=_cparams(("parallel", "arbitrary")),
        name="diff_attn_s" if sample else "diff_attn_p",
    )(*args)


def _pro_mlstm_gate(p, rs):
    hn, xc, z = (r[rs, :].astype(F32) for r in p.rows)
    p.emit((hn + p.consts[0][...] * xc) * _silu(z), rs)


def _rope_tables(d):
    rows = L_SAMPLE // GRID_W
    pos_r = jnp.repeat(jnp.arange(rows, dtype=F32), GRID_W)
    pos_c = jnp.tile(jnp.arange(GRID_W, dtype=F32), rows)
    nf = d // 4
    inv = ROPE_BASE ** (-jnp.arange(nf, dtype=F32) / nf)
    ang = jnp.concatenate([pos_r[:, None] * inv, pos_c[:, None] * inv], axis=-1)
    cos, sin = jnp.cos(ang), jnp.sin(ang)
    return jnp.concatenate([cos, cos], axis=-1), jnp.concatenate([-sin, sin], axis=-1)


def _pad_cols(a, n):
    return jnp.pad(a, ((0, 0), (0, n - a.shape[1])))


def kernel(x_prompt, x_sample, state_ssd, cache_mla_ckv, cache_mla_krope, state_mlstm_C, state_mlstm_n, state_mlstm_m, cache_diff_k, cache_diff_v, c, c_ctx, norm1_g, norm2_g, ada_w, ada_b, mlp_w1, mlp_w2, final_g, ssd_w_in, ssd_conv_w, ssd_conv_b, ssd_dt_bias, ssd_A_log, ssd_D, ssd_norm_g, ssd_w_out, mla_w_in, mla_q_norm_g, mla_kv_norm_g, mla_w_uq, mla_w_ukv, mla_w_o, mlstm_w_up, mlstm_conv_w, mlstm_conv_b, mlstm_gate_b, mlstm_w_q, mlstm_w_k, mlstm_w_v, mlstm_skip, mlstm_norm_g, mlstm_w_down, diff_w_qkv, diff_lq1, diff_lk1, diff_lq2, diff_lk2, diff_subln_g, diff_w_o):
    xp2, xs2 = x_prompt.reshape(ROWS_P, D), x_sample.reshape(ROWS_S, D)
    cvec = jnp.concatenate([c_ctx[None, :], c, jnp.zeros((5, D), F32)], axis=0)
    mod_all = _ada_mod(cvec, ada_w, ada_b)
    g2 = norm2_g.reshape(DEPTH, 1, D)

    def in_proj(xin, layer, w, n_cols, tn, extra_w=None, name="in_proj", **kw):
        xrow = ("d", xin[0], xin[1], D, 0) if isinstance(xin, tuple) else ("u", xin, D, 0)
        return _fused_mm(rows=[xrow], consts=[norm1_g[layer][None, :]], mod=(mod_all, layer), w=w, k_dim=D,
                         n_cols=n_cols, tm=2048, tn=tn, prologue=_pro_normmod, extra_w=extra_w, name=name, **kw)

    def out_proj(xin, layer, rows, consts, prologue, w, k_dim, name):
        return _fused_mm(rows=rows, consts=consts, w=w, k_dim=k_dim, n_cols=D, tm=1024, tn=512, prologue=prologue,
                         epilogue=_epi_residual(2), erows=[xin], emod=(mod_all, layer), out_dtype=F32, name=name)

    p0, pdt = in_proj((xp2, xs2), 0, ssd_w_in[0], 3 * SSD_DI, 512, extra_w=_pad_cols(ssd_w_in[0][:, 3 * SSD_DI:], LANES),
                      econv=(ssd_conv_w[0], ssd_conv_b[0][None, :], SSD_DI // 512, 3 * SSD_DI // 512, False),
                      row_chunk=L_SAMPLE, name="ssd_in")
    dtb = _pad_cols(ssd_dt_bias[0].reshape(1, 2 * SSD_HEADS), LANES)
    alog = _pad_cols(ssd_A_log[0].reshape(1, 2 * SSD_HEADS), LANES)
    dl = jnp.repeat(ssd_D[0], SSD_P)[None, :]
    scan_args = (p0, pdt, dtb, alog, dl)
    yg_p, new_ssd = _ssd_scan(*scan_args, None, nb=N_PROMPT_SEQ, L=L_PROMPT, emit_state=True)
    (yg_s,) = _ssd_scan(*scan_args, state_ssd, nb=N_SAMPLE_SEQ, L=L_SAMPLE, emit_state=False)
    x = out_proj((xp2, xs2), 0, [("d", yg_p, yg_s, SSD_DI, 0)], [ssd_norm_g[0][None, :]], _pro_rms, ssd_w_out[0], SSD_DI,
                 "ssd_out")
    x = _mlp(x, 0, g2, mod_all, mlp_w1, mlp_w2)

    w_in = mla_w_in[0]
    kr0 = MLA_Q_RANK + MLA_KV_RANK
    half = MLA_ROPE // 2
    zk = jnp.zeros((D, MLA_NOPE), F32)
    zr = jnp.zeros((D, LANES - MLA_NOPE - MLA_ROPE), F32)
    w_kr = jnp.concatenate([zk, w_in[:, kr0:kr0 + MLA_ROPE], zr,
                            zk, w_in[:, kr0 + half:kr0 + MLA_ROPE], w_in[:, kr0:kr0 + half], zr], axis=1)
    p1, p1s = in_proj(x, 1, w_in, kr0, kr0, extra_w=w_kr, name="mla_in")
    wuq = mla_w_uq[0].reshape(MLA_Q_RANK, MLA_HEADS, MLA_NOPE + MLA_ROPE)
    zq = jnp.zeros((MLA_Q_RANK, MLA_HEADS, LANES - MLA_NOPE - MLA_ROPE), F32)
    zqn = jnp.zeros((MLA_Q_RANK, MLA_HEADS, MLA_NOPE), F32)
    wq = jnp.concatenate([wuq, zq], axis=-1).reshape(MLA_Q_RANK, MLA_HEADS * LANES).astype(BF16)
    wqs = jnp.concatenate([zqn, wuq[..., MLA_NOPE + half:], wuq[..., MLA_NOPE:MLA_NOPE + half], zq],
                          axis=-1).reshape(MLA_Q_RANK, MLA_HEADS * LANES).astype(BF16)
    wukv = mla_w_ukv[0].reshape(MLA_KV_RANK, MLA_HEADS, MLA_NOPE + MLA_V)
    zkv = jnp.zeros((MLA_KV_RANK, MLA_HEADS, MLA_NOPE), F32)
    wk = jnp.concatenate([wukv[..., :MLA_NOPE], zkv], axis=-1).reshape(MLA_KV_RANK, MLA_HEADS * LANES).astype(BF16)
    wv_own = wukv[..., MLA_NOPE:]
    odd = (jnp.arange(MLA_HEADS) % 2 == 1)[None, :, None]
    wv = jnp.where(odd, jnp.concatenate([zkv, wv_own], axis=-1), jnp.concatenate([wv_own, zkv], axis=-1))
    wv = wv.reshape(MLA_KV_RANK, MLA_HEADS * LANES).astype(BF16)
    c32, s32 = _rope_tables(MLA_ROPE)
    tz = jnp.zeros((L_SAMPLE, LANES - MLA_NOPE - MLA_ROPE), F32)
    cpad = jnp.concatenate([jnp.ones((L_SAMPLE, MLA_NOPE), F32), c32, tz], axis=1)
    spad = jnp.concatenate([jnp.zeros((L_SAMPLE, MLA_NOPE), F32), s32, tz], axis=1)
    ckr_pad = jnp.pad(cache_mla_krope, ((0, 0), (0, 0), (0, 0), (MLA_NOPE, LANES - MLA_NOPE - MLA_ROPE)))
    mla_w = (mla_q_norm_g[0][None, :], mla_kv_norm_g[0][None, :], wq, wqs, wk, wv)
    o_p, new_ckv, new_kr = _mla_attn(p1, p1s, *mla_w, None, nb=N_PROMPT_SEQ, L=L_PROMPT)
    (o_s,) = _mla_attn(p1, p1s, *mla_w, (cache_mla_ckv, ckr_pad, cpad, spad), nb=N_SAMPLE_SEQ, L=L_SAMPLE)
    x = out_proj(x, 1, [("d", o_p, o_s, D, 0)], [], _pro_cast, mla_w_o[0], D, "mla_out")
    x = _mlp(x, 1, g2, mod_all, mlp_w1, mlp_w2)

    p2, gates, xc = in_proj(x, 2, mlstm_w_up[0], 2 * ML_DI, 512,
                            extra_w=_pad_cols(mlstm_w_up[0][:, 2 * ML_DI:], LANES),
                            econv=(mlstm_conv_w[0], mlstm_conv_b[0][None, :], 0, ML_DI // 512, True),
                            row_chunk=L_SAMPLE, name="mlstm_up")
    q = _fused_mm(rows=[("u", xc, ML_DI, 0)], w=mlstm_w_q[0], k_dim=ML_DI, n_cols=ML_HEADS * ML_DQK, tm=2048, tn=512,
                  name="mlstm_q")
    k = _fused_mm(rows=[("u", xc, ML_DI, 0)], w=mlstm_w_k[0], k_dim=ML_DI, n_cols=ML_HEADS * ML_DQK, tm=2048, tn=512,
                  epilogue=lambda acc, e, m, rs: acc * (ML_DQK ** -0.5), name="mlstm_k")
    v = _fused_mm(rows=[("u", p2, ML_DI, 0)], w=mlstm_w_v[0], k_dim=ML_DI, n_cols=ML_DI, tm=2048, tn=512,
                  name="mlstm_v")
    gb = _pad_cols(mlstm_gate_b[0].reshape(1, 4 * ML_HEADS), LANES)
    ng = mlstm_norm_g[0][None, :]
    n0 = _pad_cols(state_mlstm_n[:, 0].reshape(N_SAMPLE_SEQ * 2 * ML_HEADS, ML_DQK), LANES).reshape(
        N_SAMPLE_SEQ, 2, ML_HEADS, LANES)
    m0 = jnp.broadcast_to(state_mlstm_m[:, 0][..., None], (N_SAMPLE_SEQ, 2, ML_HEADS, LANES))
    hn_p, new_c, new_n, new_m = _mlstm_scan(q, k, v, gates, gb, ng, None, nb=N_PROMPT_SEQ, L=L_PROMPT, hp=HP_P,
                                            emit_state=True)
    (hn_s,) = _mlstm_scan(q, k, v, gates, gb, ng, (state_mlstm_C, n0, m0), nb=N_SAMPLE_SEQ, L=L_SAMPLE, hp=HP_S,
                          emit_state=False)
    x = out_proj(x, 2, [("d", hn_p, hn_s, ML_DI, 0), ("u", xc, ML_DI, 0), ("u", p2, ML_DI, 1)],
                 [mlstm_skip[0][None, :]], _pro_mlstm_gate, mlstm_w_down[0], ML_DI, "mlstm_down")
    x = _mlp(x, 2, g2, mod_all, mlp_w1, mlp_w2)

    lam_init = 0.8 - 0.6 * math.exp(-0.3 * 3)
    p3 = in_proj(x, 3, diff_w_qkv[0], 3 * D, 1024, name="diff_qkv")
    c64, s64 = _rope_tables(DF_D)
    c128 = jnp.concatenate([c64, c64], axis=-1)
    s128 = jnp.concatenate([s64, s64], axis=-1)
    dparams = (diff_lq1, diff_lk1, diff_lq2, diff_lk2, diff_subln_g)
    od_p, new_dk, new_dv = _diff_attn(p3, *dparams, None, nb=N_PROMPT_SEQ, L=L_PROMPT, lam_init=lam_init)
    ctx = (cache_diff_k.reshape(N_SAMPLE_SEQ, 1, PAST, D), cache_diff_v.reshape(N_SAMPLE_SEQ, 1, PAST, D), c128, s128)
    (od_s,) = _diff_attn(p3, *dparams, ctx, nb=N_SAMPLE_SEQ, L=L_SAMPLE, lam_init=lam_init)
    x = out_proj(x, 3, [("d", od_p, od_s, D, 0)], [], _pro_cast, diff_w_o[0], D, "diff_out")
    y_prompt, y_sample = _mlp(x, 3, g2, mod_all, mlp_w1, mlp_w2, final_g=final_g[None, :])
    y_prompt = y_prompt.reshape(N_PROMPT_SEQ, L_PROMPT, D)
    y_sample = y_sample.reshape(N_SAMPLE_SEQ, L_SAMPLE, D)
    return (y_prompt, y_sample, new_ssd, new_ckv, new_kr, new_c,
            new_n[None].reshape(N_PROMPT_SEQ, 1, 2, ML_HEADS, ML_DQK),
            new_m[..., 0].reshape(N_PROMPT_SEQ, 1, 2, ML_HEADS),
            new_dk.reshape(N_PROMPT_SEQ, 1, L_PROMPT, DF_HEADS, 2 * DF_D),
            new_dv.reshape(N_PROMPT_SEQ, 1, L_PROMPT, DF_HEADS, 2 * DF_D))
```

```python
import functools
import math

import jax
import jax.numpy as jnp
from jax import lax
from jax.experimental import pallas as pl
from jax.experimental.pallas import tpu as pltpu

F32 = jnp.float32
BF16 = jnp.bfloat16

D = 1024
DEPTH = 4
D_FF = 4 * D
EPS = 1e-6
ROPE_BASE = 10000.0
CONV_W = 5
GRID_W = 64

N_PROMPT_SEQ = 32
L_PROMPT = 256
N_SAMPLE_SEQ = 2
L_SAMPLE = 1024
PAST = 256
ROWS_P = N_PROMPT_SEQ * L_PROMPT
ROWS_S = N_SAMPLE_SEQ * L_SAMPLE
ROWS = ROWS_P + ROWS_S

SSD_DI = 2 * D
SSD_HEADS = 32
SSD_P = 64
SSD_GROUPS = 8
SSD_N = 128
SSD_HPG = SSD_HEADS // SSD_GROUPS

MLA_HEADS = 16
MLA_Q_RANK = 512
MLA_KV_RANK = 256
MLA_NOPE = 64
MLA_ROPE = 32
MLA_V = 64

ML_DI = 2 * D
ML_HEADS = 8
ML_DQK = 128
ML_DV = 256

DF_HEADS = 8
DF_D = 64

LANES = 128
VMEM_LIMIT = 56 * 1024 * 1024


def _cparams(sem):
    return pltpu.CompilerParams(dimension_semantics=sem, vmem_limit_bytes=VMEM_LIMIT)


def _silu(x):
    return x * jax.nn.sigmoid(x)


def _softplus(x):
    return jnp.maximum(x, 0.0) + jnp.log1p(jnp.exp(-jnp.abs(x)))


def _rms(x, g):
    r = lax.rsqrt(jnp.mean(x * x, axis=-1, keepdims=True) + EPS)
    return (x * r) * g


def _dot(a, b):
    return jnp.dot(a, b, preferred_element_type=F32)


def _dot_nt(a, b):
    return lax.dot_general(a, b, (((1,), (1,)), ((), ())), preferred_element_type=F32)


def _dot_tn(a, b):
    return lax.dot_general(a, b, (((0,), (0,)), ((), ())), preferred_element_type=F32)


MOD_ROWS = 1024


def _group_of_tile(i, tm, sub=0):
    row0 = i * tm + sub * MOD_ROWS
    return jnp.where(row0 < ROWS_P, 0, 1 + (row0 - ROWS_P) // L_SAMPLE)


def _ada_body(c_ref, w_ref, b_ref, o_ref):
    s = _silu(c_ref[...]).astype(BF16)
    o_ref[...] = _dot(s, w_ref[...].astype(BF16)) + b_ref[...]


def _ada_mod(cvec, ada_w, ada_b):
    tn = 1536
    out = pl.pallas_call(
        _ada_body,
        grid=(DEPTH, 6 * D // tn),
        in_specs=[
            pl.BlockSpec((8, D), lambda l, j: (0, 0)),
            pl.BlockSpec((None, D, tn), lambda l, j: (l, 0, j)),
            pl.BlockSpec((None, 1, tn), lambda l, j: (l, 0, j)),
        ],
        out_specs=pl.BlockSpec((None, 8, tn), lambda l, j: (l, 0, j)),
        out_shape=jax.ShapeDtypeStruct((DEPTH, 8, 6 * D), F32),
        compiler_params=_cparams(("parallel", "parallel")),
        name="ada_mod",
    )(cvec, ada_w, ada_b.reshape(DEPTH, 1, 6 * D))
    return out[:, :3].reshape(DEPTH, 3, 6, D)


class _Pro:
    def __init__(self, rows, consts, mod, i, hs, lhs_out):
        self.rows, self.consts, self.mod, self.i = rows, consts, mod, i
        self._hs, self._lhs_out = hs, lhs_out

    def emit(self, val, rows=slice(None), cols=slice(None)):
        vb = val.astype(BF16)
        self._hs[rows, cols] = vb
        if self._lhs_out is not None:
            self._lhs_out[rows, cols] = vb


ROW_CHUNK = 512


def _fused_mm(*, rows, w, k_dim, n_cols, tm, tn, prologue=None, consts=(), mod=None,
              epilogue=None, erows=(), emod=None, w_col0=0, emit_lhs=False, extra_w=None, out_dtype=BF16, name):
    npt = ROWS_P // tm
    grid = (ROWS // tm, n_cols // tn)
    has_dual = any(r[0] == "d" for r in rows)
    dual_epi = any(isinstance(e, tuple) for e in erows)
    if prologue is None:
        assert len(rows) == 1 and rows[0][0] == "u" and not emit_lhs

    in_specs, args = [], []
    for r in rows:
        if r[0] == "u":
            _, arr, width, cb = r
            in_specs.append(pl.BlockSpec((tm, width), lambda i, j, cb=cb: (i, cb)))
            args.append(arr)
        else:
            _, arr_p, arr_s, width, cb = r
            in_specs.append(pl.BlockSpec((tm, width), lambda i, j, cb=cb: (jnp.minimum(i, npt - 1), cb)))
            in_specs.append(pl.BlockSpec((tm, width), lambda i, j, cb=cb: (jnp.maximum(i - npt, 0), cb)))
            args += [arr_p, arr_s]
    for c in consts:
        in_specs.append(pl.BlockSpec(c.shape, lambda i, j, nd=c.ndim: (0,) * nd))
        args.append(c)
    n_sub = max(tm // MOD_ROWS, 1)
    if mod is not None:
        mod_arr, mod_layer = mod
        for s in range(n_sub):
            in_specs.append(pl.BlockSpec((None, None, 6, D),
                                         lambda i, j, s=s: (mod_layer, _group_of_tile(i, tm, s), 0, 0)))
            args.append(mod_arr)
    in_specs.append(pl.BlockSpec((k_dim, tn), lambda i, j: (0, w_col0 // tn + j)))
    args.append(w)
    if extra_w is not None:
        in_specs.append(pl.BlockSpec(extra_w.shape, lambda i, j: (0, 0)))
        args.append(extra_w)
    for e in erows:
        if isinstance(e, tuple):
            in_specs.append(pl.BlockSpec((tm, tn), lambda i, j: (jnp.minimum(i, npt - 1), j)))
            in_specs.append(pl.BlockSpec((tm, tn), lambda i, j: (jnp.maximum(i - npt, 0), j)))
            args += list(e)
        else:
            in_specs.append(pl.BlockSpec((tm, tn), lambda i, j: (i, j)))
            args.append(e)
    if emod is not None:
        emod_arr, emod_layer = emod
        in_specs.append(pl.BlockSpec((None, None, 6, tn), lambda i, j: (emod_layer, _group_of_tile(i, tm), 0, j)))
        args.append(emod_arr)

    out_specs = [pl.BlockSpec((tm, tn), lambda i, j: (i, j))]
    out_shape = [jax.ShapeDtypeStruct((ROWS, n_cols), out_dtype)]
    if emit_lhs:
        out_specs.append(pl.BlockSpec((tm, k_dim), lambda i, j: (i, 0)))
        out_shape.append(jax.ShapeDtypeStruct((ROWS, k_dim), BF16))
    if extra_w is not None:
        out_specs.append(pl.BlockSpec((tm, extra_w.shape[1]), lambda i, j: (i, 0)))
        out_shape.append(jax.ShapeDtypeStruct((ROWS, extra_w.shape[1]), F32))
    scratch = [] if prologue is None else [pltpu.VMEM((tm, k_dim), BF16)]
    chunks = [slice(r, r + ROW_CHUNK) for r in range(0, tm, ROW_CHUNK)]

    def body(*refs):
        it = iter(refs)
        row_refs = [(next(it),) if r[0] == "u" else (next(it), next(it)) for r in rows]
        const_refs = [next(it) for _ in consts]
        mod_ref = [next(it) for _ in range(n_sub)] if mod is not None else None
        w_ref = next(it)
        extra_ref = next(it) if extra_w is not None else None
        erow_refs = [(next(it), next(it)) if isinstance(e, tuple) else (next(it),) for e in erows]
        emod_ref = next(it) if emod is not None else None
        out_ref = next(it)
        lhs_out = next(it) if emit_lhs else None
        extra_out = next(it) if extra_w is not None else None
        hs = next(it) if prologue is not None else None
        i = pl.program_id(0)
        j = pl.program_id(1)

        def compute(first, use_prompt=True):
            chosen = [rr[0] if (len(rr) == 1 or use_prompt) else rr[1] for rr in row_refs]
            echosen = [er[0] if (len(er) == 1 or use_prompt) else er[1] for er in erow_refs]
            p = _Pro(chosen, const_refs, mod_ref, i, hs, lhs_out)
            wb = w_ref[...].astype(BF16)
            for rs in chunks:
                if first:
                    prologue(p, rs)
                lhs = chosen[0][rs, :] if prologue is None else hs[rs, :]
                acc = _dot(lhs, wb)
                if epilogue is not None:
                    acc = epilogue(acc, echosen, emod_ref, rs)
                out_ref[rs, :] = acc.astype(out_dtype)
                if first and extra_w is not None:
                    extra_out[rs, :] = _dot(lhs, extra_ref[...].astype(BF16))

        if prologue is None:
            if extra_w is None:
                compute(False)
            else:
                pl.when(j == 0)(lambda: compute(True))
                pl.when(j > 0)(lambda: compute(False))
        else:
            if has_dual or dual_epi:
                pl.when(jnp.logical_and(j == 0, i < npt))(lambda: compute(True, True))
                pl.when(jnp.logical_and(j == 0, i >= npt))(lambda: compute(True, False))
            else:
                pl.when(j == 0)(lambda: compute(True))
            if dual_epi:
                pl.when(jnp.logical_and(j > 0, i < npt))(lambda: compute(False, True))
                pl.when(jnp.logical_and(j > 0, i >= npt))(lambda: compute(False, False))
            else:
                pl.when(j > 0)(lambda: compute(False))

    res = pl.pallas_call(
        body,
        grid=grid,
        in_specs=in_specs,
        out_specs=out_specs,
        out_shape=out_shape,
        scratch_shapes=scratch,
        compiler_params=_cparams(("parallel", "arbitrary")),
        name=name,
    )(*args)
    return res if len(res) > 1 else res[0]


def _pro_normmod(p, rs):
    m = p.mod[rs.start // MOD_ROWS]
    h = _rms(p.rows[0][rs, :], p.consts[0][...]) * (1.0 + m[1:2, :]) + m[0:1, :]
    p.emit(h, rs)


def _pro_cast(p, rs):
    p.emit(p.rows[0][rs, :], rs)


def _pro_rms(p, rs):
    p.emit(_rms(p.rows[0][rs, :].astype(F32), p.consts[0][...]), rs)


def _epi_residual(gate_row):
    def epi(acc, erows, emod, rs):
        return erows[0][rs, :] + emod[gate_row:gate_row + 1, :] * acc
    return epi


def _mlp_body(*refs, final):
    it = iter(refs)
    x_ref, g_ref, mod_ref, w1_ref, w2_ref = (next(it) for _ in range(5))
    fg_ref = next(it) if final else None
    outs = [next(it), next(it)] if final else [next(it)]
    hs, acc = next(it), next(it)
    i = pl.program_id(0)
    f = pl.program_id(1)
    nf = pl.num_programs(1)
    tm = x_ref.shape[0]
    chunks = [slice(r, r + ROW_CHUNK) for r in range(0, tm, ROW_CHUNK)]

    def step(first, last):
        w1b = w1_ref[...].astype(BF16)
        w2b = w2_ref[...].astype(BF16)
        for rs in chunks:
            if first:
                h = _rms(x_ref[rs, :], g_ref[...]) * (1.0 + mod_ref[4:5, :]) + mod_ref[3:4, :]
                hs[rs, :] = h.astype(BF16)
            u = jnp.square(jnp.maximum(_dot(hs[rs, :], w1b), 0.0))
            tot = _dot(u.astype(BF16), w2b)
            if not first:
                tot = acc[rs, :] + tot
            if not last:
                acc[rs, :] = tot
                continue
            y = x_ref[rs, :] + mod_ref[5:6, :] * tot
            if not final:
                outs[0][rs, :] = y
            else:
                y = _rms(y, fg_ref[...])
                npt = ROWS_P // tm

                @pl.when(i < npt)
                def _():
                    outs[0][rs, :] = y

                @pl.when(i >= npt)
                def _():
                    outs[1][rs, :] = y

    pl.when(f == 0)(lambda: step(True, False))
    pl.when(jnp.logical_and(f > 0, f < nf - 1))(lambda: step(False, False))
    pl.when(f == nf - 1)(lambda: step(False, True))


def _mlp(x, layer, g, mod, w1, w2, final_g=None):
    tm, tf = 1024, 1024
    npt = ROWS_P // tm
    final = final_g is not None
    in_specs = [
        pl.BlockSpec((tm, D), lambda i, f: (i, 0)),
        pl.BlockSpec((None, 1, D), lambda i, f: (layer, 0, 0)),
        pl.BlockSpec((None, None, 6, D), lambda i, f: (layer, _group_of_tile(i, tm), 0, 0)),
        pl.BlockSpec((None, D, tf), lambda i, f: (layer, 0, f)),
        pl.BlockSpec((None, tf, D), lambda i, f: (layer, f, 0)),
    ]
    args = [x, g, mod, w1, w2]
    if final:
        in_specs.append(pl.BlockSpec((1, D), lambda i, f: (0, 0)))
        args.append(final_g)
        out_specs = [pl.BlockSpec((tm, D), lambda i, f: (jnp.minimum(i, npt - 1), 0)),
                     pl.BlockSpec((tm, D), lambda i, f: (jnp.maximum(i - npt, 0), 0))]
        out_shape = [jax.ShapeDtypeStruct((ROWS_P, D), F32), jax.ShapeDtypeStruct((ROWS_S, D), F32)]
    else:
        out_specs = pl.BlockSpec((tm, D), lambda i, f: (i, 0))
        out_shape = jax.ShapeDtypeStruct((ROWS, D), F32)
    return pl.pallas_call(
        functools.partial(_mlp_body, final=final),
        grid=(ROWS // tm, D_FF // tf),
        in_specs=in_specs,
        out_specs=out_specs,
        out_shape=out_shape,
        scratch_shapes=[pltpu.VMEM((tm, D), BF16), pltpu.VMEM((tm, D), F32)],
        compiler_params=_cparams(("arbitrary", "arbitrary")),
        name="mlp_final" if final else "mlp",
    )(*args)


SCAN_T = 256


def _seg_cumsums(a):
    n = a.shape[0]
    ii = lax.broadcasted_iota(jnp.int32, (SCAN_T, SCAN_T), 0)
    jj = lax.broadcasted_iota(jnp.int32, (SCAN_T, SCAN_T), 1)
    lower = jnp.where(ii >= jj, 1.0, 0.0).astype(BF16)
    upper = jnp.where(ii <= jj, 1.0, 0.0).astype(BF16)
    hi = a.astype(BF16)
    rest = a - hi.astype(F32)
    mid = rest.astype(BF16)
    lo = (rest - mid.astype(F32)).astype(BF16)
    parts = jnp.concatenate([hi, mid, lo], axis=1)
    pre, suf = [], []
    for c in range(n // SCAN_T):
        pc = parts[SCAN_T * c:SCAN_T * (c + 1), :]
        for tri, out in ((lower, pre), (upper, suf)):
            s3 = _dot(tri, pc)
            out.append(s3[:, :LANES] + s3[:, LANES:2 * LANES] + s3[:, 2 * LANES:])
    return jnp.concatenate(pre, axis=0), jnp.concatenate(suf, axis=0)


LOG2E = 1.4426950408889634


def _ssd_body(*refs, L, has_s0, emit_state):
    it = iter(refs)
    z_ref, x_ref, b_ref, c_ref, dt_ref = (next(it) for _ in range(5))
    cwx, cbx, cwb, cbb, cwc, cbc = (next(it) for _ in range(6))
    dtb_ref, alog_ref, dl_ref = next(it), next(it), next(it)
    s0_ref = next(it) if has_s0 else None
    y_ref = next(it)
    st_out = next(it) if emit_state else None
    padx, padb, xa, ba, ca, cbs, cumc, crp, yacc, st = (next(it) for _ in range(10))

    T = SCAN_T
    nc = L // T
    g0 = pl.program_id(0)

    def conv(in_ref, w_ref, bias_ref, pad):
        width = in_ref.shape[1]
        pad[0:8, :] = jnp.zeros((8, width), F32)
        pad[L + 8:L + 16, :] = jnp.zeros((8, width), F32)
        pad[8:L + 8, :] = in_ref[...].astype(F32)
        acc = bias_ref[...] + pad[6:6 + L, :] * w_ref[0:1, :]
        for k in range(1, CONV_W):
            acc = acc + pad[6 + k:6 + k + L, :] * w_ref[k:k + 1, :]
        return _silu(acc)

    xa[...] = conv(x_ref, cwx, cbx, padx)
    ba[...] = conv(b_ref, cwb, cbb, padb)
    ca[...] = conv(c_ref, cwc, cbc, padb)

    dt_all = _softplus(dt_ref[...] + dtb_ref[...])
    a_all = dt_all * (-jnp.exp(alog_ref[...]))
    shift = jnp.where(g0 == 0, 0, LANES - SSD_HPG * g0)
    l2dt = jnp.log2(pltpu.roll(dt_all, shift, 1))
    ar = pltpu.roll(a_all, shift, 1) * LOG2E
    cum_f, cum_b = _seg_cumsums(ar)
    cumc[0] = cum_f
    cumc[1] = cum_b
    crp[0] = (cum_f - l2dt).T
    crp[1] = (cum_b - l2dt).T

    yacc[...] = xa[...] * dl_ref[...]
    if has_s0:
        st[...] = s0_ref[...]

    ii = lax.broadcasted_iota(jnp.int32, (T, T), 0)
    jj = lax.broadcasted_iota(jnp.int32, (T, T), 1)

    def chunk(c, d):
        rows = slice(T * c, T * (c + 1))
        first = c == (0 if d == 0 else nc - 1)
        last = c == (nc - 1 if d == 0 else 0)
        zero_state = first and not has_s0
        need_state = emit_state or not last
        mask = (ii >= jj) if d == 0 else (ii <= jj)
        end = T * c + (T - 1 if d == 0 else 0)
        xav = xa[rows, :]
        xab = xav.astype(BF16)
        cab = ca[rows, :].astype(BF16)
        if need_state:
            xat = xav.T.astype(BF16)
            bat = ba[rows, :].T
        for r in range(SSD_HPG):
            lane = SSD_HEADS * d + r
            hs = slice(SSD_P * r, SSD_P * (r + 1))
            cc = cumc[d, rows, lane:lane + 1]
            cr = crp[d, lane:lane + 1, rows]
            dec = jnp.where(mask, jnp.exp2(cc - cr), 0.0)
            y = _dot((cbs[c] * dec).astype(BF16), xab[:, hs])
            if not zero_state:
                y = y + _dot_nt(cab, st[d, r].astype(BF16)) * jnp.exp2(cc)
            yacc[rows, hs] += y
            if need_state:
                tot = cumc[d, end:end + 1, lane:lane + 1]
                upd = _dot_nt(xat[hs, :], (bat * jnp.exp2(tot - cr)).astype(BF16))
                st[d, r] = upd if zero_state else jnp.exp2(tot) * st[d, r] + upd

    for c in range(nc):
        rows = slice(T * c, T * (c + 1))
        cbs[c] = _dot_nt(ca[rows, :].astype(BF16), ba[rows, :].astype(BF16))
    for c in range(nc):
        chunk(c, 0)
        chunk(nc - 1 - c, 1)

    y_ref[...] = (yacc[...] * _silu(z_ref[...].astype(F32))).astype(BF16)
    if emit_state:
        st_out[...] = st[...]


def _ssd_scan(p0, pdt, conv_w, conv_b, dtb, alog, dl, s0, *, nb, L, emit_state):
    rb0 = 0 if s0 is None else ROWS_P // L
    gw = SSD_HPG * SSD_P
    nc = L // SCAN_T
    x0 = SSD_DI // gw
    b0 = 2 * SSD_DI // SSD_N
    c0 = b0 + SSD_GROUPS
    wb0 = SSD_DI // SSD_N
    wc0 = wb0 + SSD_GROUPS
    in_specs = [
        pl.BlockSpec((L, gw), lambda g, b: (rb0 + b, g)),
        pl.BlockSpec((L, gw), lambda g, b: (rb0 + b, x0 + g)),
        pl.BlockSpec((L, SSD_N), lambda g, b: (rb0 + b, b0 + g)),
        pl.BlockSpec((L, SSD_N), lambda g, b: (rb0 + b, c0 + g)),
        pl.BlockSpec((L, LANES), lambda g, b: (rb0 + b, 0)),
        pl.BlockSpec((CONV_W, gw), lambda g, b: (0, g)),
        pl.BlockSpec((1, gw), lambda g, b: (0, g)),
        pl.BlockSpec((CONV_W, SSD_N), lambda g, b: (0, wb0 + g)),
        pl.BlockSpec((1, SSD_N), lambda g, b: (0, wb0 + g)),
        pl.BlockSpec((CONV_W, SSD_N), lambda g, b: (0, wc0 + g)),
        pl.BlockSpec((1, SSD_N), lambda g, b: (0, wc0 + g)),
        pl.BlockSpec((1, LANES), lambda g, b: (0, 0)),
        pl.BlockSpec((1, LANES), lambda g, b: (0, 0)),
        pl.BlockSpec((1, gw), lambda g, b: (0, g)),
    ]
    args = [p0, p0, p0, p0, pdt, conv_w, conv_b, conv_w, conv_b, conv_w, conv_b, dtb, alog, dl]
    state_spec = pl.BlockSpec((None, None, 2, SSD_HPG, SSD_P, SSD_N), lambda g, b: (b, 0, 0, g, 0, 0))
    if s0 is not None:
        in_specs.append(state_spec)
        args.append(s0)
    out_specs = [pl.BlockSpec((L, gw), lambda g, b: (b, g))]
    out_shape = [jax.ShapeDtypeStruct((nb * L, SSD_DI), BF16)]
    if emit_state:
        out_specs.append(state_spec)
        out_shape.append(jax.ShapeDtypeStruct((nb, 1, 2, SSD_HEADS, SSD_P, SSD_N), F32))
    scratch = [
        pltpu.VMEM((L + 16, gw), F32),
        pltpu.VMEM((L + 16, SSD_N), F32),
        pltpu.VMEM((L, gw), F32),
        pltpu.VMEM((L, SSD_N), F32),
        pltpu.VMEM((L, SSD_N), F32),
        pltpu.VMEM((nc, SCAN_T, SCAN_T), F32),
        pltpu.VMEM((2, L, LANES), F32),
        pltpu.VMEM((2, LANES, L), F32),
        pltpu.VMEM((L, gw), F32),
        pltpu.VMEM((2, SSD_HPG, SSD_P, SSD_N), F32),
    ]
    res = pl.pallas_call(
        functools.partial(_ssd_body, L=L, has_s0=s0 is not None, emit_state=emit_state),
        grid=(SSD_GROUPS, nb),
        in_specs=in_specs,
        out_specs=out_specs,
        out_shape=out_shape,
        scratch_shapes=scratch,
        compiler_params=_cparams(("parallel", "parallel")),
        name="ssd_scan_p" if s0 is None else "ssd_scan_s",
    )(*args)
    return res


TQ = 256
HP_P = 4
HP_S = 2


def _mla_body(*refs, L, sample):
    it = iter(refs)
    cq_ref, ckv_ref, krs_ref = next(it), next(it), next(it)
    gq_ref, gkv_ref = next(it), next(it)
    wq, wqs, wk, wv = (next(it) for _ in range(4))
    if sample:
        cckv_ref, ckr_ref, cq_t, sq_t, ck_t, sk_t = (next(it) for _ in range(6))
    o_ref = next(it)
    if not sample:
        ckv_out, kr_out = next(it), next(it)
    kk, vv = next(it), next(it)
    qt = pl.program_id(1)
    scale = (MLA_NOPE + MLA_ROPE) ** -0.5
    hb = 4 * LANES

    @pl.when(qt == 0)
    def _():
        ckv = _rms(ckv_ref[...].astype(F32), gkv_ref[...])
        kr_own = krs_ref[:, 0:LANES]
        if sample:
            keys = jnp.concatenate([cckv_ref[...], ckv], axis=0)
            kr_own = kr_own * ck_t[...] + krs_ref[:, LANES:2 * LANES] * sk_t[...]
            kr_all = jnp.concatenate([ckr_ref[...], kr_own], axis=0)
        else:
            ckv_out[...] = ckv
            kr_out[...] = kr_own[:, MLA_NOPE:MLA_NOPE + MLA_ROPE]
            keys = ckv
            kr_all = kr_own
        kb = keys.astype(BF16)
        for blk in range(MLA_HEADS * LANES // hb):
            cols = slice(hb * blk, hb * (blk + 1))
            kn = _dot(kb, wk[:, cols])
            kk[:, cols] = (kn + jnp.concatenate([kr_all] * 4, axis=1)).astype(BF16)
            vv[:, cols] = _dot(kb, wv[:, cols]).astype(BF16)

    cq = _rms(cq_ref[...].astype(F32), gq_ref[...]).astype(BF16)
    for blk in range(MLA_HEADS * LANES // hb):
        qa = _dot(cq, wq[:, hb * blk:hb * (blk + 1)])
        if sample:
            qs = _dot(cq, wqs[:, hb * blk:hb * (blk + 1)])
        pair = None
        for hh in range(4):
            h = 4 * blk + hh
            cols = slice(LANES * h, LANES * (h + 1))
            qh = qa[:, LANES * hh:LANES * (hh + 1)]
            if sample:
                qh = qh * cq_t[...] + qs[:, LANES * hh:LANES * (hh + 1)] * sq_t[...]
            s = _dot_nt((qh * scale).astype(BF16), kk[:, cols])
            e = jnp.exp(s - jnp.max(s, axis=1, keepdims=True))
            o = _dot(e.astype(BF16), vv[:, cols]) / jnp.sum(e, axis=1, keepdims=True)
            if h % 2 == 0:
                pair = o
            else:
                o_ref[:, LANES * (h // 2):LANES * (h // 2 + 1)] = (pair + o).astype(BF16)


def _mla_attn(p1, p1s, gq, gkv, wq, wqs, wk, wv, ctx, *, nb, L):
    sample = ctx is not None
    nq = L // TQ
    tk = L + (PAST if sample else 0)
    rbq0 = ROWS_P // TQ if sample else 0
    rbs0 = ROWS_P // L if sample else 0
    ckv_blk = MLA_Q_RANK // MLA_KV_RANK

    def const(a):
        return pl.BlockSpec(a.shape, lambda b, q, nd=a.ndim: (0,) * nd)

    in_specs = [
        pl.BlockSpec((TQ, MLA_Q_RANK), lambda b, q: (rbq0 + b * nq + q, 0)),
        pl.BlockSpec((L, MLA_KV_RANK), lambda b, q: (rbs0 + b, ckv_blk)),
        pl.BlockSpec((L, 2 * LANES), lambda b, q: (rbs0 + b, 0)),
        const(gq), const(gkv), const(wq), const(wqs), const(wk), const(wv),
    ]
    args = [p1, p1, p1s, gq, gkv, wq, wqs, wk, wv]
    if sample:
        cckv, ckr, cpad, spad = ctx
        in_specs += [
            pl.BlockSpec((None, None, PAST, MLA_KV_RANK), lambda b, q: (b, 0, 0, 0)),
            pl.BlockSpec((None, None, PAST, LANES), lambda b, q: (b, 0, 0, 0)),
            pl.BlockSpec((TQ, LANES), lambda b, q: (q, 0)),
            pl.BlockSpec((TQ, LANES), lambda b, q: (q, 0)),
            const(cpad), const(spad),
        ]
        args += [cckv, ckr, cpad, spad, cpad, spad]
    out_specs = [pl.BlockSpec((TQ, MLA_HEADS * MLA_V), lambda b, q: (b * nq + q, 0))]
    out_shape = [jax.ShapeDtypeStruct((nb * L, MLA_HEADS * MLA_V), BF16)]
    if not sample:
        out_specs += [
            pl.BlockSpec((None, None, L, MLA_KV_RANK), lambda b, q: (b, 0, 0, 0)),
            pl.BlockSpec((None, None, L, MLA_ROPE), lambda b, q: (b, 0, 0, 0)),
        ]
        out_shape += [
            jax.ShapeDtypeStruct((nb, 1, L, MLA_KV_RANK), F32),
            jax.ShapeDtypeStruct((nb, 1, L, MLA_ROPE), F32),
        ]
    scratch = [
        pltpu.VMEM((tk, MLA_HEADS * LANES), BF16),
        pltpu.VMEM((tk, MLA_HEADS * LANES), BF16),
    ]
    return pl.pallas_call(
        functools.partial(_mla_body, L=L, sample=sample),
        grid=(nb, nq),
        in_specs=in_specs,
        out_specs=out_specs,
        out_shape=out_shape,
        scratch_shapes=scratch,
        compiler_params=_cparams(("parallel", "arbitrary")),
        name="mla_attn_s" if sample else "mla_attn_p",
    )(*args)


def _log_sigmoid(x):
    return -_softplus(-x)


def _mlstm_body(*refs, L, hp, has_s0, emit_state):
    it = iter(refs)
    q_ref, k_ref, v_ref, g_ref, gb_ref, ng_ref = (next(it) for _ in range(6))
    if has_s0:
        c0_ref, n0_ref, m0_ref = next(it), next(it), next(it)
    h_ref = next(it)
    if emit_state:
        c_out, n_out, m_out = next(it), next(it), next(it)
    gsc, gtr, kq, vts, kts, hacc, cst, nst, mst = (next(it) for _ in range(9))

    T = SCAN_T
    nc = L // T
    h0 = hp * pl.program_id(1)

    gts = g_ref[...] + gb_ref[...]
    gr = pltpu.roll(gts, jnp.where(h0 == 0, 0, LANES - h0), 1)
    b_f, b_b = _seg_cumsums(_log_sigmoid(gr))
    gsc[0] = gr
    gsc[1] = b_f
    gsc[2] = b_b
    gtr[0] = gr.T
    gtr[1] = b_f.T
    gtr[2] = b_b.T

    jj = lax.broadcasted_iota(jnp.int32, (T, T), 0)
    ii = lax.broadcasted_iota(jnp.int32, (T, T), 1)

    def chunk(hh, c, d):
        rows = slice(T * c, T * (c + 1))
        qcols = slice(ML_DQK * hh, ML_DQK * (hh + 1))
        first = c == (0 if d == 0 else nc - 1)
        last = c == (nc - 1 if d == 0 else 0)
        zero_state = first and not has_s0
        li = 2 * ML_HEADS * d + hh
        lb = li + ML_HEADS
        b_row = gtr[1 + d, lb:lb + 1, rows]
        logi_row = gtr[0, li:li + 1, rows]
        cj = gsc[1 + d, rows, lb:lb + 1] - gsc[0, rows, li:li + 1]
        mask = (jj <= ii) if d == 0 else (jj >= ii)
        dlog = jnp.where(mask, b_row - cj, -jnp.inf)
        m_prev = jnp.zeros((1, 1), F32) if zero_state else mst[hh, d][:, 0:1]
        inter = b_row + m_prev
        mcomb = jnp.maximum(inter, jnp.max(dlog, axis=0, keepdims=True))
        s = kq[hh, c] * jnp.exp(dlog - mcomb)
        vt = vts[hh, c]
        num = _dot(vt, s.astype(BF16))
        den = jnp.sum(s, axis=0, keepdims=True)
        if not zero_state:
            iw = jnp.exp(inter - mcomb)
            qt = q_ref[rows, qcols].T
            num = num + iw * _dot_tn(cst[hh, d].astype(BF16), qt)
            n8 = jnp.broadcast_to(nst[hh, d], (8, ML_DQK)).astype(BF16)
            den = den + iw * _dot(n8, qt)[0:1]
        hc = num / jnp.maximum(jnp.abs(den), jnp.exp(-mcomb))
        if d == 0:
            hacc[hh, :, rows] = hc
        else:
            hacc[hh, :, rows] += hc
        if emit_state or not last:
            end = T * c + (T - 1 if d == 0 else 0)
            bq = gtr[1 + d, lb:lb + 1, end:end + 1]
            wlog = bq - b_row + logi_row
            m_new = jnp.maximum(bq + m_prev, jnp.max(wlog, axis=1, keepdims=True))
            sw = jnp.exp(wlog - m_new)
            upd = _dot_nt((kts[hh, c].astype(F32) * sw).astype(BF16), vt)
            nsum = _dot(jnp.broadcast_to(sw, (8, T)).astype(BF16), k_ref[rows, qcols])[0:1]
            if zero_state:
                cst[hh, d] = upd
                nst[hh, d] = nsum
            else:
                cw = jnp.exp(bq + m_prev - m_new)
                cst[hh, d] = cw * cst[hh, d] + upd
                nst[hh, d] = cw * nst[hh, d] + nsum
            mst[hh, d] = jnp.broadcast_to(m_new, (1, LANES))

    for hh in range(hp):
        qcols = slice(ML_DQK * hh, ML_DQK * (hh + 1))
        vcols = slice(ML_DV * hh, ML_DV * (hh + 1))
        if has_s0:
            for d in range(2):
                cst[hh, d] = c0_ref[d, hh]
                nst[hh, d] = n0_ref[d, pl.ds(h0 + hh, 1), :]
                mst[hh, d] = m0_ref[d, pl.ds(h0 + hh, 1), :]
        for c in range(nc):
            rows = slice(T * c, T * (c + 1))
            kc = k_ref[rows, qcols]
            kq[hh, c] = _dot_nt(kc, q_ref[rows, qcols])
            kts[hh, c] = kc.T
            vts[hh, c] = v_ref[rows, vcols].T
        for c in range(nc):
            chunk(hh, c, 0)
        for c in range(nc):
            chunk(hh, nc - 1 - c, 1)
        ht = hacc[hh]
        r = lax.rsqrt(jnp.mean(ht * ht, axis=0, keepdims=True) + EPS)
        h_ref[:, vcols] = ((ht * r).T * ng_ref[:, vcols]).astype(BF16)
        if emit_state:
            for d in range(2):
                c_out[d, hh] = cst[hh, d]
                n_out[d, pl.ds(h0 + hh, 1), :] = nst[hh, d]
                m_out[d, pl.ds(h0 + hh, 1), :] = mst[hh, d]


def _mlstm_scan(q, k, v, gates, gb, ng, s0, *, nb, L, hp, emit_state):
    rb0 = 0 if s0 is None else ROWS_P // L
    in_specs = [
        pl.BlockSpec((L, hp * ML_DQK), lambda b, h: (rb0 + b, h)),
        pl.BlockSpec((L, hp * ML_DQK), lambda b, h: (rb0 + b, h)),
        pl.BlockSpec((L, hp * ML_DV), lambda b, h: (rb0 + b, h)),
        pl.BlockSpec((L, LANES), lambda b, h: (rb0 + b, 0)),
        pl.BlockSpec((1, LANES), lambda b, h: (0, 0)),
        pl.BlockSpec((1, hp * ML_DV), lambda b, h: (0, h)),
    ]
    args = [q, k, v, gates, gb, ng]
    c_spec = pl.BlockSpec((None, None, 2, hp, ML_DQK, ML_DV), lambda b, h: (b, 0, 0, h, 0, 0))
    n_spec = pl.BlockSpec((None, 2, ML_HEADS, LANES), lambda b, h: (b, 0, 0, 0))
    if s0 is not None:
        in_specs += [c_spec, n_spec, n_spec]
        args += list(s0)
    out_specs = [pl.BlockSpec((L, hp * ML_DV), lambda b, h: (b, h))]
    out_shape = [jax.ShapeDtypeStruct((nb * L, ML_DI), BF16)]
    if emit_state:
        out_specs += [c_spec, n_spec, n_spec]
        out_shape += [
            jax.ShapeDtypeStruct((nb, 1, 2, ML_HEADS, ML_DQK, ML_DV), F32),
            jax.ShapeDtypeStruct((nb, 2, ML_HEADS, LANES), F32),
            jax.ShapeDtypeStruct((nb, 2, ML_HEADS, LANES), F32),
        ]
    scratch = [
        pltpu.VMEM((3, L, LANES), F32),
        pltpu.VMEM((3, LANES, L), F32),
        pltpu.VMEM((hp, L // SCAN_T, SCAN_T, SCAN_T), F32),
        pltpu.VMEM((hp, L // SCAN_T, ML_DV, SCAN_T), BF16),
        pltpu.VMEM((hp, L // SCAN_T, ML_DQK, SCAN_T), BF16),
        pltpu.VMEM((hp, ML_DV, L), F32),
        pltpu.VMEM((hp, 2, ML_DQK, ML_DV), F32),
        pltpu.VMEM((hp, 2, 1, ML_DQK), F32),
        pltpu.VMEM((hp, 2, 1, LANES), F32),
    ]
    return pl.pallas_call(
        functools.partial(_mlstm_body, L=L, hp=hp, has_s0=s0 is not None, emit_state=emit_state),
        grid=(nb, ML_HEADS // hp),
        in_specs=in_specs,
        out_specs=out_specs,
        out_shape=out_shape,
        scratch_shapes=scratch,
        compiler_params=_cparams(("parallel", "arbitrary")),
        name="mlstm_scan_p" if s0 is None else "mlstm_scan_s",
    )(*args)


def _rope_blocks(x, ct, st):
    lane = lax.broadcasted_iota(jnp.int32, (x.shape[0], LANES), 1)
    first_half = lane % DF_D < DF_D // 2
    outs = []
    for blk in range(x.shape[1] // LANES):
        xb = x[:, LANES * blk:LANES * (blk + 1)]
        swapped = jnp.where(first_half, pltpu.roll(xb, LANES - DF_D // 2, 1), pltpu.roll(xb, DF_D // 2, 1))
        outs.append(xb * ct + swapped * st)
    return jnp.concatenate(outs, axis=1)


def _diff_body(*refs, L, sample, lam_init):
    it = iter(refs)
    q_ref, k_ref, v_ref = next(it), next(it), next(it)
    lq1, lk1, lq2, lk2, sg_ref = (next(it) for _ in range(5))
    if sample:
        ck_ref, cv_ref, cq_t, sq_t, ck_t, sk_t = (next(it) for _ in range(6))
    o_ref = next(it)
    if sample:
        ka, va = next(it), next(it)
    else:
        k_out, v_out = next(it), next(it)
    qt = pl.program_id(1)

    lam = (jnp.exp(jnp.sum(lq1[...] * lk1[...], axis=1, keepdims=True))
           - jnp.exp(jnp.sum(lq2[...] * lk2[...], axis=1, keepdims=True)) + lam_init)

    @pl.when(qt == 0)
    def _():
        if sample:
            ka[0:PAST, :] = ck_ref[...].astype(BF16)
            ka[PAST:PAST + L, :] = _rope_blocks(k_ref[...].astype(F32), ck_t[...], sk_t[...]).astype(BF16)
            va[0:PAST, :] = cv_ref[...].astype(BF16)
            va[PAST:PAST + L, :] = v_ref[...]
        else:
            k_out[...] = k_ref[...].astype(F32)
            v_out[...] = v_ref[...].astype(F32)

    q = q_ref[...].astype(F32)
    if sample:
        q = _rope_blocks(q, cq_t[...], sq_t[...])
    q = q * (DF_D ** -0.5)
    lo = lax.broadcasted_iota(jnp.int32, (TQ, LANES), 1) < DF_D

    def attend(qm, kh, vh):
        s = _dot_nt(qm, kh)
        e = jnp.exp(s - jnp.max(s, axis=1, keepdims=True))
        return _dot(e.astype(BF16), vh) / jnp.sum(e, axis=1, keepdims=True)

    for h in range(DF_HEADS):
        cols = slice(LANES * h, LANES * (h + 1))
        qh = q[:, cols]
        if sample:
            kh, vh = ka[:, cols], va[:, cols]
        else:
            kh, vh = k_ref[:, cols], v_ref[:, cols]
        a0 = attend(jnp.where(lo, qh, 0.0).astype(BF16), kh, vh)
        a1 = attend(jnp.where(lo, 0.0, qh).astype(BF16), kh, vh)
        o = a0 - lam * a1
        o_ref[:, cols] = (_rms(o, sg_ref[...]) * (1.0 - lam_init)).astype(BF16)


def _diff_attn(p3, lq1, lk1, lq2, lk2, sg, ctx, *, nb, L, lam_init):
    sample = ctx is not None
    nq = L // TQ
    rbq0 = ROWS_P // TQ if sample else 0
    rbs0 = ROWS_P // L if sample else 0

    def const(a):
        return pl.BlockSpec(a.shape, lambda b, q, nd=a.ndim: (0,) * nd)

    in_specs = [
        pl.BlockSpec((TQ, D), lambda b, q: (rbq0 + b * nq + q, 0)),
        pl.BlockSpec((L, D), lambda b, q: (rbs0 + b, 1)),
        pl.BlockSpec((L, D), lambda b, q: (rbs0 + b, 2)),
        const(lq1), const(lk1), const(lq2), const(lk2), const(sg),
    ]
    args = [p3, p3, p3, lq1, lk1, lq2, lk2, sg]
    kv_spec = pl.BlockSpec((None, None, PAST if sample else L, D), lambda b, q: (b, 0, 0, 0))
    if sample:
        ck, cv, c128, s128 = ctx
        in_specs += [
            kv_spec, kv_spec,
            pl.BlockSpec((TQ, LANES), lambda b, q: (q, 0)),
            pl.BlockSpec((TQ, LANES), lambda b, q: (q, 0)),
            const(c128), const(s128),
        ]
        args += [ck, cv, c128, s128, c128, s128]
    out_specs = [pl.BlockSpec((TQ, D), lambda b, q: (b * nq + q, 0))]
    out_shape = [jax.ShapeDtypeStruct((nb * L, D), BF16)]
    scratch = []
    if sample:
        scratch = [pltpu.VMEM((PAST + L, D), BF16), pltpu.VMEM((PAST + L, D), BF16)]
    else:
        out_specs += [kv_spec, kv_spec]
        out_shape += [jax.ShapeDtypeStruct((nb, 1, L, D), F32)] * 2
    return pl.pallas_call(
        functools.partial(_diff_body, L=L, sample=sample, lam_init=lam_init),
        grid=(nb, nq),
        in_specs=in_specs,
        out_specs=out_specs,
        out_shape=out_shape,
        scratch_shapes=scratch,
        compiler_params=_cparams(("parallel", "arbitrary")),
        name="diff_attn_s" if sample else "diff_attn_p",
    )(*args)


CONV_COLS = 256


def _mlstm_q_body(x_ref, cw_ref, cb_ref, w_ref, q_ref, xc_ref, msk):
    tm = x_ref.shape[0]
    seq = jnp.where(pl.program_id(0) < ROWS_P // tm, L_PROMPT, L_SAMPLE)
    pos = lax.broadcasted_iota(jnp.int32, (tm, CONV_COLS), 0) & (seq - 1)
    offs = [k - CONV_W // 2 for k in range(CONV_W)]
    for k, off in enumerate(offs):
        if off != 0:
            dst = pos - off
            msk[k] = jnp.where(jnp.logical_and(dst >= 0, dst < seq), 1.0, 0.0)
    acc = None
    kslab = 2 * CONV_COLS
    for kb in range(ML_DI // kslab):
        for cb in range(kslab // CONV_COLS):
            cols = slice(kslab * kb + CONV_COLS * cb, kslab * kb + CONV_COLS * (cb + 1))
            x = x_ref[:, cols].astype(F32)
            y = None
            for k, off in enumerate(offs):
                tap = x if off == 0 else pltpu.roll(x * msk[k], (-off) % tm, 0)
                tap = tap * cw_ref[k:k + 1, cols]
                y = cb_ref[:, cols] + tap if y is None else y + tap
            xc_ref[:, cols] = _silu(y).astype(BF16)
        kcols = slice(kslab * kb, kslab * (kb + 1))
        part = _dot(xc_ref[:, kcols], w_ref[kcols, :].astype(BF16))
        acc = part if acc is None else part + acc
    q_ref[...] = acc.astype(BF16)


def _mlstm_q(xm_src, conv_w, conv_b, w_q):
    tm = 1024
    nq = ML_HEADS * ML_DQK
    return pl.pallas_call(
        _mlstm_q_body,
        grid=(ROWS // tm,),
        in_specs=[
            pl.BlockSpec((tm, ML_DI), lambda i: (i, 0)),
            pl.BlockSpec((CONV_W, ML_DI), lambda i: (0, 0)),
            pl.BlockSpec((1, ML_DI), lambda i: (0, 0)),
            pl.BlockSpec((ML_DI, nq), lambda i: (0, 0)),
        ],
        out_specs=[pl.BlockSpec((tm, nq), lambda i: (i, 0)), pl.BlockSpec((tm, ML_DI), lambda i: (i, 0))],
        out_shape=[jax.ShapeDtypeStruct((ROWS, nq), BF16), jax.ShapeDtypeStruct((ROWS, ML_DI), BF16)],
        scratch_shapes=[pltpu.VMEM((CONV_W, tm, CONV_COLS), F32)],
        compiler_params=_cparams(("parallel",)),
        name="mlstm_q",
    )(xm_src, conv_w, conv_b, w_q)


def _pro_mlstm_gate(p, rs):
    hn, xc, z = (r[rs, :].astype(F32) for r in p.rows)
    p.emit((hn + p.consts[0][...] * xc) * _silu(z), rs)


def _rope_tables(d):
    rows = L_SAMPLE // GRID_W
    pos_r = jnp.repeat(jnp.arange(rows, dtype=F32), GRID_W)
    pos_c = jnp.tile(jnp.arange(GRID_W, dtype=F32), rows)
    nf = d // 4
    inv = ROPE_BASE ** (-jnp.arange(nf, dtype=F32) / nf)
    ang = jnp.concatenate([pos_r[:, None] * inv, pos_c[:, None] * inv], axis=-1)
    cos, sin = jnp.cos(ang), jnp.sin(ang)
    return jnp.concatenate([cos, cos], axis=-1), jnp.concatenate([-sin, sin], axis=-1)


def _pad_cols(a, n):
    return jnp.pad(a, ((0, 0), (0, n - a.shape[1])))


def kernel(x_prompt, x_sample, state_ssd, cache_mla_ckv, cache_mla_krope, state_mlstm_C, state_mlstm_n, state_mlstm_m, cache_diff_k, cache_diff_v, c, c_ctx, norm1_g, norm2_g, ada_w, ada_b, mlp_w1, mlp_w2, final_g, ssd_w_in, ssd_conv_w, ssd_conv_b, ssd_dt_bias, ssd_A_log, ssd_D, ssd_norm_g, ssd_w_out, mla_w_in, mla_q_norm_g, mla_kv_norm_g, mla_w_uq, mla_w_ukv, mla_w_o, mlstm_w_up, mlstm_conv_w, mlstm_conv_b, mlstm_gate_b, mlstm_w_q, mlstm_w_k, mlstm_w_v, mlstm_skip, mlstm_norm_g, mlstm_w_down, diff_w_qkv, diff_lq1, diff_lk1, diff_lq2, diff_lk2, diff_subln_g, diff_w_o):
    xp2, xs2 = x_prompt.reshape(ROWS_P, D), x_sample.reshape(ROWS_S, D)
    cvec = jnp.concatenate([c_ctx[None, :], c, jnp.zeros((5, D), F32)], axis=0)
    mod_all = _ada_mod(cvec, ada_w, ada_b)
    g2 = norm2_g.reshape(DEPTH, 1, D)

    def in_proj(xin, layer, w, n_cols, tn, extra_w=None, name="in_proj"):
        xrow = ("d", xin[0], xin[1], D, 0) if isinstance(xin, tuple) else ("u", xin, D, 0)
        return _fused_mm(rows=[xrow], consts=[norm1_g[layer][None, :]], mod=(mod_all, layer), w=w, k_dim=D,
                         n_cols=n_cols, tm=2048, tn=tn, prologue=_pro_normmod, extra_w=extra_w, name=name)

    def out_proj(xin, layer, rows, consts, prologue, w, k_dim, name):
        return _fused_mm(rows=rows, consts=consts, w=w, k_dim=k_dim, n_cols=D, tm=1024, tn=512, prologue=prologue,
                         epilogue=_epi_residual(2), erows=[xin], emod=(mod_all, layer), out_dtype=F32, name=name)

    p0, pdt = in_proj((xp2, xs2), 0, ssd_w_in[0], 3 * SSD_DI, 512, extra_w=_pad_cols(ssd_w_in[0][:, 3 * SSD_DI:], LANES),
                      name="ssd_in")
    dtb = _pad_cols(ssd_dt_bias[0].reshape(1, 2 * SSD_HEADS), LANES)
    alog = _pad_cols(ssd_A_log[0].reshape(1, 2 * SSD_HEADS), LANES)
    dl = jnp.repeat(ssd_D[0], SSD_P)[None, :]
    scan_args = (p0, pdt, ssd_conv_w[0], ssd_conv_b[0][None, :], dtb, alog, dl)
    yg_p, new_ssd = _ssd_scan(*scan_args, None, nb=N_PROMPT_SEQ, L=L_PROMPT, emit_state=True)
    (yg_s,) = _ssd_scan(*scan_args, state_ssd, nb=N_SAMPLE_SEQ, L=L_SAMPLE, emit_state=False)
    x = out_proj((xp2, xs2), 0, [("d", yg_p, yg_s, SSD_DI, 0)], [ssd_norm_g[0][None, :]], _pro_rms, ssd_w_out[0], SSD_DI,
                 "ssd_out")
    x = _mlp(x, 0, g2, mod_all, mlp_w1, mlp_w2)

    w_in = mla_w_in[0]
    kr0 = MLA_Q_RANK + MLA_KV_RANK
    half = MLA_ROPE // 2
    zk = jnp.zeros((D, MLA_NOPE), F32)
    zr = jnp.zeros((D, LANES - MLA_NOPE - MLA_ROPE), F32)
    w_kr = jnp.concatenate([zk, w_in[:, kr0:kr0 + MLA_ROPE], zr,
                            zk, w_in[:, kr0 + half:kr0 + MLA_ROPE], w_in[:, kr0:kr0 + half], zr], axis=1)
    p1, p1s = in_proj(x, 1, w_in, kr0, kr0, extra_w=w_kr, name="mla_in")
    wuq = mla_w_uq[0].reshape(MLA_Q_RANK, MLA_HEADS, MLA_NOPE + MLA_ROPE)
    zq = jnp.zeros((MLA_Q_RANK, MLA_HEADS, LANES - MLA_NOPE - MLA_ROPE), F32)
    zqn = jnp.zeros((MLA_Q_RANK, MLA_HEADS, MLA_NOPE), F32)
    wq = jnp.concatenate([wuq, zq], axis=-1).reshape(MLA_Q_RANK, MLA_HEADS * LANES).astype(BF16)
    wqs = jnp.concatenate([zqn, wuq[..., MLA_NOPE + half:], wuq[..., MLA_NOPE:MLA_NOPE + half], zq],
                          axis=-1).reshape(MLA_Q_RANK, MLA_HEADS * LANES).astype(BF16)
    wukv = mla_w_ukv[0].reshape(MLA_KV_RANK, MLA_HEADS, MLA_NOPE + MLA_V)
    zkv = jnp.zeros((MLA_KV_RANK, MLA_HEADS, MLA_NOPE), F32)
    wk = jnp.concatenate([wukv[..., :MLA_NOPE], zkv], axis=-1).reshape(MLA_KV_RANK, MLA_HEADS * LANES).astype(BF16)
    wv_own = wukv[..., MLA_NOPE:]
    odd = (jnp.arange(MLA_HEADS) % 2 == 1)[None, :, None]
    wv = jnp.where(odd, jnp.concatenate([zkv, wv_own], axis=-1), jnp.concatenate([wv_own, zkv], axis=-1))
    wv = wv.reshape(MLA_KV_RANK, MLA_HEADS * LANES).astype(BF16)
    c32, s32 = _rope_tables(MLA_ROPE)
    tz = jnp.zeros((L_SAMPLE, LANES - MLA_NOPE - MLA_ROPE), F32)
    cpad = jnp.concatenate([jnp.ones((L_SAMPLE, MLA_NOPE), F32), c32, tz], axis=1)
    spad = jnp.concatenate([jnp.zeros((L_SAMPLE, MLA_NOPE), F32), s32, tz], axis=1)
    ckr_pad = jnp.pad(cache_mla_krope, ((0, 0), (0, 0), (0, 0), (MLA_NOPE, LANES - MLA_NOPE - MLA_ROPE)))
    mla_w = (mla_q_norm_g[0][None, :], mla_kv_norm_g[0][None, :], wq, wqs, wk, wv)
    o_p, new_ckv, new_kr = _mla_attn(p1, p1s, *mla_w, None, nb=N_PROMPT_SEQ, L=L_PROMPT)
    (o_s,) = _mla_attn(p1, p1s, *mla_w, (cache_mla_ckv, ckr_pad, cpad, spad), nb=N_SAMPLE_SEQ, L=L_SAMPLE)
    x = out_proj(x, 1, [("d", o_p, o_s, D, 0)], [], _pro_cast, mla_w_o[0], D, "mla_out")
    x = _mlp(x, 1, g2, mod_all, mlp_w1, mlp_w2)

    p2, gates = in_proj(x, 2, mlstm_w_up[0], 2 * ML_DI, 1024, extra_w=_pad_cols(mlstm_w_up[0][:, 2 * ML_DI:], LANES),
                        name="mlstm_up")
    q, xc = _mlstm_q(p2, mlstm_conv_w[0], mlstm_conv_b[0][None, :], mlstm_w_q[0])
    k = _fused_mm(rows=[("u", xc, ML_DI, 0)], w=mlstm_w_k[0], k_dim=ML_DI, n_cols=ML_HEADS * ML_DQK, tm=2048, tn=512,
                  epilogue=lambda acc, e, m, rs: acc * (ML_DQK ** -0.5), name="mlstm_k")
    v = _fused_mm(rows=[("u", p2, ML_DI, 0)], w=mlstm_w_v[0], k_dim=ML_DI, n_cols=ML_DI, tm=2048, tn=512,
                  name="mlstm_v")
    gb = _pad_cols(mlstm_gate_b[0].reshape(1, 4 * ML_HEADS), LANES)
    ng = mlstm_norm_g[0][None, :]
    n0 = _pad_cols(state_mlstm_n[:, 0].reshape(N_SAMPLE_SEQ * 2 * ML_HEADS, ML_DQK), LANES).reshape(
        N_SAMPLE_SEQ, 2, ML_HEADS, LANES)
    m0 = jnp.broadcast_to(state_mlstm_m[:, 0][..., None], (N_SAMPLE_SEQ, 2, ML_HEADS, LANES))
    hn_p, new_c, new_n, new_m = _mlstm_scan(q, k, v, gates, gb, ng, None, nb=N_PROMPT_SEQ, L=L_PROMPT, hp=HP_P,
                                            emit_state=True)
    (hn_s,) = _mlstm_scan(q, k, v, gates, gb, ng, (state_mlstm_C, n0, m0), nb=N_SAMPLE_SEQ, L=L_SAMPLE, hp=HP_S,
                          emit_state=False)
    x = out_proj(x, 2, [("d", hn_p, hn_s, ML_DI, 0), ("u", xc, ML_DI, 0), ("u", p2, ML_DI, 1)],
                 [mlstm_skip[0][None, :]], _pro_mlstm_gate, mlstm_w_down[0], ML_DI, "mlstm_down")
    x = _mlp(x, 2, g2, mod_all, mlp_w1, mlp_w2)

    lam_init = 0.8 - 0.6 * math.exp(-0.3 * 3)
    p3 = in_proj(x, 3, diff_w_qkv[0], 3 * D, 1024, name="diff_qkv")
    c64, s64 = _rope_tables(DF_D)
    c128 = jnp.concatenate([c64, c64], axis=-1)
    s128 = jnp.concatenate([s64, s64], axis=-1)
    dparams = (diff_lq1, diff_lk1, diff_lq2, diff_lk2, diff_subln_g)
    od_p, new_dk, new_dv = _diff_attn(p3, *dparams, None, nb=N_PROMPT_SEQ, L=L_PROMPT, lam_init=lam_init)
    ctx = (cache_diff_k.reshape(N_SAMPLE_SEQ, 1, PAST, D), cache_diff_v.reshape(N_SAMPLE_SEQ, 1, PAST, D), c128, s128)
    (od_s,) = _diff_attn(p3, *dparams, ctx, nb=N_SAMPLE_SEQ, L=L_SAMPLE, lam_init=lam_init)
    x = out_proj(x, 3, [("d", od_p, od_s, D, 0)], [], _pro_cast, diff_w_o[0], D, "diff_out")
    y_prompt, y_sample = _mlp(x, 3, g2, mod_all, mlp_w1, mlp_w2, final_g=final_g[None, :])
    y_prompt = y_prompt.reshape(N_PROMPT_SEQ, L_PROMPT, D)
    y_sample = y_sample.reshape(N_SAMPLE_SEQ, L_SAMPLE, D)
    return (y_prompt, y_sample, new_ssd, new_ckv, new_kr, new_c,
            new_n[None].reshape(N_PROMPT_SEQ, 1, 2, ML_HEADS, ML_DQK),
            new_m[..., 0].reshape(N_PROMPT_SEQ, 1, 2, ML_HEADS),
            new_dk.reshape(N_PROMPT_SEQ, 1, L_PROMPT, DF_HEADS, 2 * DF_D),
            new_dv.reshape(N_PROMPT_SEQ, 1, L_PROMPT, DF_HEADS, 2 * DF_D))
```

```python
import functools
import math

import jax
import jax.numpy as jnp
from jax import lax
from jax.experimental import pallas as pl
from jax.experimental.pallas import tpu as pltpu

F32 = jnp.float32
BF16 = jnp.bfloat16

D = 1024
DEPTH = 4
D_FF = 4 * D
EPS = 1e-6
ROPE_BASE = 10000.0
CONV_W = 5
GRID_W = 64

N_PROMPT_SEQ = 32
L_PROMPT = 256
N_SAMPLE_SEQ = 2
L_SAMPLE = 1024
PAST = 256
ROWS_P = N_PROMPT_SEQ * L_PROMPT
ROWS_S = N_SAMPLE_SEQ * L_SAMPLE
ROWS = ROWS_P + ROWS_S

SSD_DI = 2 * D
SSD_HEADS = 32
SSD_P = 64
SSD_GROUPS = 8
SSD_N = 128
SSD_HPG = SSD_HEADS // SSD_GROUPS

MLA_HEADS = 16
MLA_Q_RANK = 512
MLA_KV_RANK = 256
MLA_NOPE = 64
MLA_ROPE = 32
MLA_V = 64

ML_DI = 2 * D
ML_HEADS = 8
ML_DQK = 128
ML_DV = 256

DF_HEADS = 8
DF_D = 64

LANES = 128
VMEM_LIMIT = 56 * 1024 * 1024

TM_IN = 2048
TN_IN = 1024
TN_SSD_IN = 512
TM_OUT = 1024
TM_WIDE = 2048
TN_PROJ = 512
TM_MLP = 1024
TF_MLP = 1024
TM_CONVQ = 1024
TN_ADA = 1536


def _cparams(sem):
    return pltpu.CompilerParams(dimension_semantics=sem, vmem_limit_bytes=VMEM_LIMIT)


def _silu(x):
    return x * jax.nn.sigmoid(x)


def _softplus(x):
    return jnp.maximum(x, 0.0) + jnp.log1p(jnp.exp(-jnp.abs(x)))


def _rms(x, g):
    r = lax.rsqrt(jnp.mean(x * x, axis=-1, keepdims=True) + EPS)
    return (x * r) * g


def _dot(a, b):
    return jnp.dot(a, b, preferred_element_type=F32)


def _dot_nt(a, b):
    return lax.dot_general(a, b, (((1,), (1,)), ((), ())), preferred_element_type=F32)


def _dot_tn(a, b):
    return lax.dot_general(a, b, (((0,), (0,)), ((), ())), preferred_element_type=F32)


MOD_ROWS = 1024


def _group_of_tile(i, tm, sub=0):
    row0 = i * tm + sub * MOD_ROWS
    return jnp.where(row0 < ROWS_P, 0, 1 + (row0 - ROWS_P) // L_SAMPLE)


def _ada_body(c_ref, w_ref, b_ref, o_ref):
    s = _silu(c_ref[...]).astype(BF16)
    o_ref[...] = _dot(s, w_ref[...].astype(BF16)) + b_ref[...]


def _ada_mod(cvec, ada_w, ada_b):
    tn = TN_ADA
    out = pl.pallas_call(
        _ada_body,
        grid=(DEPTH, 6 * D // tn),
        in_specs=[
            pl.BlockSpec((8, D), lambda l, j: (0, 0)),
            pl.BlockSpec((None, D, tn), lambda l, j: (l, 0, j)),
            pl.BlockSpec((None, 1, tn), lambda l, j: (l, 0, j)),
        ],
        out_specs=pl.BlockSpec((None, 8, tn), lambda l, j: (l, 0, j)),
        out_shape=jax.ShapeDtypeStruct((DEPTH, 8, 6 * D), F32),
        compiler_params=_cparams(("parallel", "parallel")),
        name="ada_mod",
    )(cvec, ada_w, ada_b.reshape(DEPTH, 1, 6 * D))
    return out[:, :3].reshape(DEPTH, 3, 6, D)


class _Pro:
    def __init__(self, rows, consts, mod, i, hs, lhs_out):
        self.rows, self.consts, self.mod, self.i = rows, consts, mod, i
        self._hs, self._lhs_out = hs, lhs_out

    def emit(self, val, rows=slice(None), cols=slice(None)):
        vb = val.astype(BF16)
        self._hs[rows, cols] = vb
        if self._lhs_out is not None:
            self._lhs_out[rows, cols] = vb


ROW_CHUNK = 512


def _fused_mm(*, rows, w, k_dim, n_cols, tm, tn, prologue=None, consts=(), mod=None,
              epilogue=None, erows=(), emod=None, w_col0=0, emit_lhs=False, extra_w=None, out_dtype=BF16, name):
    npt = ROWS_P // tm
    grid = (ROWS // tm, n_cols // tn)
    has_dual = any(r[0] == "d" for r in rows)
    dual_epi = any(isinstance(e, tuple) for e in erows)
    if prologue is None:
        assert len(rows) == 1 and rows[0][0] == "u" and not emit_lhs

    in_specs, args = [], []
    for r in rows:
        if r[0] == "u":
            _, arr, width, cb = r
            in_specs.append(pl.BlockSpec((tm, width), lambda i, j, cb=cb: (i, cb)))
            args.append(arr)
        else:
            _, arr_p, arr_s, width, cb = r
            in_specs.append(pl.BlockSpec((tm, width), lambda i, j, cb=cb: (jnp.minimum(i, npt - 1), cb)))
            in_specs.append(pl.BlockSpec((tm, width), lambda i, j, cb=cb: (jnp.maximum(i - npt, 0), cb)))
            args += [arr_p, arr_s]
    for c in consts:
        in_specs.append(pl.BlockSpec(c.shape, lambda i, j, nd=c.ndim: (0,) * nd))
        args.append(c)
    n_sub = max(tm // MOD_ROWS, 1)
    if mod is not None:
        mod_arr, mod_layer = mod
        for s in range(n_sub):
            in_specs.append(pl.BlockSpec((None, None, 6, D),
                                         lambda i, j, s=s: (mod_layer, _group_of_tile(i, tm, s), 0, 0)))
            args.append(mod_arr)
    in_specs.append(pl.BlockSpec((k_dim, tn), lambda i, j: (0, w_col0 // tn + j)))
    args.append(w)
    if extra_w is not None:
        in_specs.append(pl.BlockSpec(extra_w.shape, lambda i, j: (0, 0)))
        args.append(extra_w)
    for e in erows:
        if isinstance(e, tuple):
            in_specs.append(pl.BlockSpec((tm, tn), lambda i, j: (jnp.minimum(i, npt - 1), j)))
            in_specs.append(pl.BlockSpec((tm, tn), lambda i, j: (jnp.maximum(i - npt, 0), j)))
            args += list(e)
        else:
            in_specs.append(pl.BlockSpec((tm, tn), lambda i, j: (i, j)))
            args.append(e)
    if emod is not None:
        emod_arr, emod_layer = emod
        in_specs.append(pl.BlockSpec((None, None, 6, tn), lambda i, j: (emod_layer, _group_of_tile(i, tm), 0, j)))
        args.append(emod_arr)

    out_specs = [pl.BlockSpec((tm, tn), lambda i, j: (i, j))]
    out_shape = [jax.ShapeDtypeStruct((ROWS, n_cols), out_dtype)]
    if emit_lhs:
        out_specs.append(pl.BlockSpec((tm, k_dim), lambda i, j: (i, 0)))
        out_shape.append(jax.ShapeDtypeStruct((ROWS, k_dim), BF16))
    if extra_w is not None:
        out_specs.append(pl.BlockSpec((tm, extra_w.shape[1]), lambda i, j: (i, 0)))
        out_shape.append(jax.ShapeDtypeStruct((ROWS, extra_w.shape[1]), F32))
    scratch = [] if prologue is None else [pltpu.VMEM((tm, k_dim), BF16)]
    chunks = [slice(r, r + ROW_CHUNK) for r in range(0, tm, ROW_CHUNK)]

    def body(*refs):
        it = iter(refs)
        row_refs = [(next(it),) if r[0] == "u" else (next(it), next(it)) for r in rows]
        const_refs = [next(it) for _ in consts]
        mod_ref = [next(it) for _ in range(n_sub)] if mod is not None else None
        w_ref = next(it)
        extra_ref = next(it) if extra_w is not None else None
        erow_refs = [(next(it), next(it)) if isinstance(e, tuple) else (next(it),) for e in erows]
        emod_ref = next(it) if emod is not None else None
        out_ref = next(it)
        lhs_out = next(it) if emit_lhs else None
        extra_out = next(it) if extra_w is not None else None
        hs = next(it) if prologue is not None else None
        i = pl.program_id(0)
        j = pl.program_id(1)

        def compute(first, use_prompt=True):
            chosen = [rr[0] if (len(rr) == 1 or use_prompt) else rr[1] for rr in row_refs]
            echosen = [er[0] if (len(er) == 1 or use_prompt) else er[1] for er in erow_refs]
            p = _Pro(chosen, const_refs, mod_ref, i, hs, lhs_out)
            wb = w_ref[...].astype(BF16)
            for rs in chunks:
                if first:
                    prologue(p, rs)
                lhs = chosen[0][rs, :] if prologue is None else hs[rs, :]
                acc = _dot(lhs, wb)
                if epilogue is not None:
                    acc = epilogue(acc, echosen, emod_ref, rs)
                out_ref[rs, :] = acc.astype(out_dtype)
                if first and extra_w is not None:
                    extra_out[rs, :] = _dot(lhs, extra_ref[...].astype(BF16))

        if prologue is None:
            if extra_w is None:
                compute(False)
            else:
                pl.when(j == 0)(lambda: compute(True))
                pl.when(j > 0)(lambda: compute(False))
        else:
            if has_dual or dual_epi:
                pl.when(jnp.logical_and(j == 0, i < npt))(lambda: compute(True, True))
                pl.when(jnp.logical_and(j == 0, i >= npt))(lambda: compute(True, False))
            else:
                pl.when(j == 0)(lambda: compute(True))
            if dual_epi:
                pl.when(jnp.logical_and(j > 0, i < npt))(lambda: compute(False, True))
                pl.when(jnp.logical_and(j > 0, i >= npt))(lambda: compute(False, False))
            else:
                pl.when(j > 0)(lambda: compute(False))

    res = pl.pallas_call(
        body,
        grid=grid,
        in_specs=in_specs,
        out_specs=out_specs,
        out_shape=out_shape,
        scratch_shapes=scratch,
        compiler_params=_cparams(("parallel", "arbitrary")),
        name=name,
    )(*args)
    return res if len(res) > 1 else res[0]


def _pro_normmod(p, rs):
    m = p.mod[rs.start // MOD_ROWS]
    h = _rms(p.rows[0][rs, :], p.consts[0][...]) * (1.0 + m[1:2, :]) + m[0:1, :]
    p.emit(h, rs)


def _pro_cast(p, rs):
    p.emit(p.rows[0][rs, :], rs)


def _pro_rms(p, rs):
    p.emit(_rms(p.rows[0][rs, :].astype(F32), p.consts[0][...]), rs)


def _epi_residual(gate_row):
    def epi(acc, erows, emod, rs):
        return erows[0][rs, :] + emod[gate_row:gate_row + 1, :] * acc
    return epi


def _mlp_body(*refs, final):
    it = iter(refs)
    x_ref, g_ref, mod_ref, w1_ref, w2_ref = (next(it) for _ in range(5))
    fg_ref = next(it) if final else None
    outs = [next(it), next(it)] if final else [next(it)]
    hs, acc = next(it), next(it)
    i = pl.program_id(0)
    f = pl.program_id(1)
    nf = pl.num_programs(1)
    tm = x_ref.shape[0]
    chunks = [slice(r, r + ROW_CHUNK) for r in range(0, tm, ROW_CHUNK)]

    def step(first, last):
        w1b = w1_ref[...].astype(BF16)
        w2b = w2_ref[...].astype(BF16)
        for rs in chunks:
            if first:
                h = _rms(x_ref[rs, :], g_ref[...]) * (1.0 + mod_ref[4:5, :]) + mod_ref[3:4, :]
                hs[rs, :] = h.astype(BF16)
            u = jnp.square(jnp.maximum(_dot(hs[rs, :], w1b), 0.0))
            tot = _dot(u.astype(BF16), w2b)
            if not first:
                tot = acc[rs, :] + tot
            if not last:
                acc[rs, :] = tot
                continue
            y = x_ref[rs, :] + mod_ref[5:6, :] * tot
            if not final:
                outs[0][rs, :] = y
            else:
                y = _rms(y, fg_ref[...])
                npt = ROWS_P // tm

                @pl.when(i < npt)
                def _():
                    outs[0][rs, :] = y

                @pl.when(i >= npt)
                def _():
                    outs[1][rs, :] = y

    pl.when(f == 0)(lambda: step(True, False))
    pl.when(jnp.logical_and(f > 0, f < nf - 1))(lambda: step(False, False))
    pl.when(f == nf - 1)(lambda: step(False, True))


def _mlp(x, layer, g, mod, w1, w2, final_g=None):
    tm, tf = TM_MLP, TF_MLP
    npt = ROWS_P // tm
    final = final_g is not None
    in_specs = [
        pl.BlockSpec((tm, D), lambda i, f: (i, 0)),
        pl.BlockSpec((None, 1, D), lambda i, f: (layer, 0, 0)),
        pl.BlockSpec((None, None, 6, D), lambda i, f: (layer, _group_of_tile(i, tm), 0, 0)),
        pl.BlockSpec((None, D, tf), lambda i, f: (layer, 0, f)),
        pl.BlockSpec((None, tf, D), lambda i, f: (layer, f, 0)),
    ]
    args = [x, g, mod, w1, w2]
    if final:
        in_specs.append(pl.BlockSpec((1, D), lambda i, f: (0, 0)))
        args.append(final_g)
        out_specs = [pl.BlockSpec((tm, D), lambda i, f: (jnp.minimum(i, npt - 1), 0)),
                     pl.BlockSpec((tm, D), lambda i, f: (jnp.maximum(i - npt, 0), 0))]
        out_shape = [jax.ShapeDtypeStruct((ROWS_P, D), F32), jax.ShapeDtypeStruct((ROWS_S, D), F32)]
    else:
        out_specs = pl.BlockSpec((tm, D), lambda i, f: (i, 0))
        out_shape = jax.ShapeDtypeStruct((ROWS, D), F32)
    return pl.pallas_call(
        functools.partial(_mlp_body, final=final),
        grid=(ROWS // tm, D_FF // tf),
        in_specs=in_specs,
        out_specs=out_specs,
        out_shape=out_shape,
        scratch_shapes=[pltpu.VMEM((tm, D), BF16), pltpu.VMEM((tm, D), F32)],
        compiler_params=_cparams(("arbitrary", "arbitrary")),
        name="mlp_final" if final else "mlp",
    )(*args)


SCAN_T = 256


def _seg_cumsums(a):
    n = a.shape[0]
    ii = lax.broadcasted_iota(jnp.int32, (SCAN_T, SCAN_T), 0)
    jj = lax.broadcasted_iota(jnp.int32, (SCAN_T, SCAN_T), 1)
    lower = jnp.where(ii >= jj, 1.0, 0.0).astype(BF16)
    upper = jnp.where(ii <= jj, 1.0, 0.0).astype(BF16)
    hi = a.astype(BF16)
    rest = a - hi.astype(F32)
    mid = rest.astype(BF16)
    lo = (rest - mid.astype(F32)).astype(BF16)
    parts = jnp.concatenate([hi, mid, lo], axis=1)
    pre, suf = [], []
    for c in range(n // SCAN_T):
        pc = parts[SCAN_T * c:SCAN_T * (c + 1), :]
        for tri, out in ((lower, pre), (upper, suf)):
            s3 = _dot(tri, pc)
            out.append(s3[:, :LANES] + s3[:, LANES:2 * LANES] + s3[:, 2 * LANES:])
    return jnp.concatenate(pre, axis=0), jnp.concatenate(suf, axis=0)


LOG2E = 1.4426950408889634


def _ssd_body(*refs, L, has_s0, emit_state):
    it = iter(refs)
    z_ref, x_ref, b_ref, c_ref, dt_ref = (next(it) for _ in range(5))
    cwx, cbx, cwb, cbb, cwc, cbc = (next(it) for _ in range(6))
    dtb_ref, alog_ref, dl_ref = next(it), next(it), next(it)
    s0_ref = next(it) if has_s0 else None
    y_ref = next(it)
    st_out = next(it) if emit_state else None
    padx, padb, xa, ba, ca, cbs, cumc, crp, yacc, st = (next(it) for _ in range(10))

    T = SCAN_T
    nc = L // T
    g0 = pl.program_id(0)

    def conv(in_ref, w_ref, bias_ref, pad):
        width = in_ref.shape[1]
        pad[0:8, :] = jnp.zeros((8, width), F32)
        pad[L + 8:L + 16, :] = jnp.zeros((8, width), F32)
        pad[8:L + 8, :] = in_ref[...].astype(F32)
        acc = bias_ref[...] + pad[6:6 + L, :] * w_ref[0:1, :]
        for k in range(1, CONV_W):
            acc = acc + pad[6 + k:6 + k + L, :] * w_ref[k:k + 1, :]
        return _silu(acc)

    xa[...] = conv(x_ref, cwx, cbx, padx)
    ba[...] = conv(b_ref, cwb, cbb, padb)
    ca[...] = conv(c_ref, cwc, cbc, padb)

    dt_all = _softplus(dt_ref[...] + dtb_ref[...])
    a_all = dt_all * (-jnp.exp(alog_ref[...]))
    shift = jnp.where(g0 == 0, 0, LANES - SSD_HPG * g0)
    l2dt = jnp.log2(pltpu.roll(dt_all, shift, 1))
    ar = pltpu.roll(a_all, shift, 1) * LOG2E
    cum_f, cum_b = _seg_cumsums(ar)
    cumc[0] = cum_f
    cumc[1] = cum_b
    crp[0] = (cum_f - l2dt).T
    crp[1] = (cum_b - l2dt).T

    yacc[...] = xa[...] * dl_ref[...]
    if has_s0:
        st[...] = s0_ref[...]

    ii = lax.broadcasted_iota(jnp.int32, (T, T), 0)
    jj = lax.broadcasted_iota(jnp.int32, (T, T), 1)

    def chunk(c, d):
        rows = slice(T * c, T * (c + 1))
        first = c == (0 if d == 0 else nc - 1)
        last = c == (nc - 1 if d == 0 else 0)
        zero_state = first and not has_s0
        need_state = emit_state or not last
        mask = (ii >= jj) if d == 0 else (ii <= jj)
        end = T * c + (T - 1 if d == 0 else 0)
        xav = xa[rows, :]
        xab = xav.astype(BF16)
        cab = ca[rows, :].astype(BF16)
        if need_state:
            xat = xav.T.astype(BF16)
            bat = ba[rows, :].T
        for r in range(SSD_HPG):
            lane = SSD_HEADS * d + r
            hs = slice(SSD_P * r, SSD_P * (r + 1))
            cc = cumc[d, rows, lane:lane + 1]
            cr = crp[d, lane:lane + 1, rows]
            dec = jnp.where(mask, jnp.exp2(cc - cr), 0.0)
            y = _dot((cbs[c] * dec).astype(BF16), xab[:, hs])
            if not zero_state:
                y = y + _dot_nt(cab, st[d, r].astype(BF16)) * jnp.exp2(cc)
            yacc[rows, hs] += y
            if need_state:
                tot = cumc[d, end:end + 1, lane:lane + 1]
                upd = _dot_nt(xat[hs, :], (bat * jnp.exp2(tot - cr)).astype(BF16))
                st[d, r] = upd if zero_state else jnp.exp2(tot) * st[d, r] + upd

    for c in range(nc):
        rows = slice(T * c, T * (c + 1))
        cbs[c] = _dot_nt(ca[rows, :].astype(BF16), ba[rows, :].astype(BF16))
    for c in range(nc):
        chunk(c, 0)
        chunk(nc - 1 - c, 1)

    y_ref[...] = (yacc[...] * _silu(z_ref[...].astype(F32))).astype(BF16)
    if emit_state:
        st_out[...] = st[...]


def _ssd_scan(p0, pdt, conv_w, conv_b, dtb, alog, dl, s0, *, nb, L, emit_state):
    rb0 = 0 if s0 is None else ROWS_P // L
    gw = SSD_HPG * SSD_P
    nc = L // SCAN_T
    x0 = SSD_DI // gw
    b0 = 2 * SSD_DI // SSD_N
    c0 = b0 + SSD_GROUPS
    wb0 = SSD_DI // SSD_N
    wc0 = wb0 + SSD_GROUPS
    in_specs = [
        pl.BlockSpec((L, gw), lambda g, b: (rb0 + b, g)),
        pl.BlockSpec((L, gw), lambda g, b: (rb0 + b, x0 + g)),
        pl.BlockSpec((L, SSD_N), lambda g, b: (rb0 + b, b0 + g)),
        pl.BlockSpec((L, SSD_N), lambda g, b: (rb0 + b, c0 + g)),
        pl.BlockSpec((L, LANES), lambda g, b: (rb0 + b, 0)),
        pl.BlockSpec((CONV_W, gw), lambda g, b: (0, g)),
        pl.BlockSpec((1, gw), lambda g, b: (0, g)),
        pl.BlockSpec((CONV_W, SSD_N), lambda g, b: (0, wb0 + g)),
        pl.BlockSpec((1, SSD_N), lambda g, b: (0, wb0 + g)),
        pl.BlockSpec((CONV_W, SSD_N), lambda g, b: (0, wc0 + g)),
        pl.BlockSpec((1, SSD_N), lambda g, b: (0, wc0 + g)),
        pl.BlockSpec((1, LANES), lambda g, b: (0, 0)),
        pl.BlockSpec((1, LANES), lambda g, b: (0, 0)),
        pl.BlockSpec((1, gw), lambda g, b: (0, g)),
    ]
    args = [p0, p0, p0, p0, pdt, conv_w, conv_b, conv_w, conv_b, conv_w, conv_b, dtb, alog, dl]
    state_spec = pl.BlockSpec((None, None, 2, SSD_HPG, SSD_P, SSD_N), lambda g, b: (b, 0, 0, g, 0, 0))
    if s0 is not None:
        in_specs.append(state_spec)
        args.append(s0)
    out_specs = [pl.BlockSpec((L, gw), lambda g, b: (b, g))]
    out_shape = [jax.ShapeDtypeStruct((nb * L, SSD_DI), BF16)]
    if emit_state:
        out_specs.append(state_spec)
        out_shape.append(jax.ShapeDtypeStruct((nb, 1, 2, SSD_HEADS, SSD_P, SSD_N), F32))
    scratch = [
        pltpu.VMEM((L + 16, gw), F32),
        pltpu.VMEM((L + 16, SSD_N), F32),
        pltpu.VMEM((L, gw), F32),
        pltpu.VMEM((L, SSD_N), F32),
        pltpu.VMEM((L, SSD_N), F32),
        pltpu.VMEM((nc, SCAN_T, SCAN_T), F32),
        pltpu.VMEM((2, L, LANES), F32),
        pltpu.VMEM((2, LANES, L), F32),
        pltpu.VMEM((L, gw), F32),
        pltpu.VMEM((2, SSD_HPG, SSD_P, SSD_N), F32),
    ]
    res = pl.pallas_call(
        functools.partial(_ssd_body, L=L, has_s0=s0 is not None, emit_state=emit_state),
        grid=(SSD_GROUPS, nb),
        in_specs=in_specs,
        out_specs=out_specs,
        out_shape=out_shape,
        scratch_shapes=scratch,
        compiler_params=_cparams(("parallel", "parallel")),
        name="ssd_scan_p" if s0 is None else "ssd_scan_s",
    )(*args)
    return res


TQ = 256
HP_P = 4
HP_S = 2


def _mla_body(*refs, L, sample):
    it = iter(refs)
    cq_ref, ckv_ref, krs_ref = next(it), next(it), next(it)
    gq_ref, gkv_ref = next(it), next(it)
    wq, wqs, wk, wv = (next(it) for _ in range(4))
    if sample:
        cckv_ref, ckr_ref, cq_t, sq_t, ck_t, sk_t = (next(it) for _ in range(6))
    o_ref = next(it)
    if not sample:
        ckv_out, kr_out = next(it), next(it)
    kk, vv = next(it), next(it)
    qt = pl.program_id(1)
    scale = (MLA_NOPE + MLA_ROPE) ** -0.5
    hb = 4 * LANES

    @pl.when(qt == 0)
    def _():
        ckv = _rms(ckv_ref[...].astype(F32), gkv_ref[...])
        kr_own = krs_ref[:, 0:LANES]
        if sample:
            keys = jnp.concatenate([cckv_ref[...], ckv], axis=0)
            kr_own = kr_own * ck_t[...] + krs_ref[:, LANES:2 * LANES] * sk_t[...]
            kr_all = jnp.concatenate([ckr_ref[...], kr_own], axis=0)
        else:
            ckv_out[...] = ckv
            kr_out[...] = kr_own[:, MLA_NOPE:MLA_NOPE + MLA_ROPE]
            keys = ckv
            kr_all = kr_own
        kb = keys.astype(BF16)
        for blk in range(MLA_HEADS * LANES // hb):
            cols = slice(hb * blk, hb * (blk + 1))
            kn = _dot(kb, wk[:, cols])
            kk[:, cols] = (kn + jnp.concatenate([kr_all] * 4, axis=1)).astype(BF16)
            vv[:, cols] = _dot(kb, wv[:, cols]).astype(BF16)

    cq = _rms(cq_ref[...].astype(F32), gq_ref[...]).astype(BF16)
    for blk in range(MLA_HEADS * LANES // hb):
        qa = _dot(cq, wq[:, hb * blk:hb * (blk + 1)])
        if sample:
            qs = _dot(cq, wqs[:, hb * blk:hb * (blk + 1)])
        pair = None
        for hh in range(4):
            h = 4 * blk + hh
            cols = slice(LANES * h, LANES * (h + 1))
            qh = qa[:, LANES * hh:LANES * (hh + 1)]
            if sample:
                qh = qh * cq_t[...] + qs[:, LANES * hh:LANES * (hh + 1)] * sq_t[...]
            s = _dot_nt((qh * scale).astype(BF16), kk[:, cols])
            e = jnp.exp(s - jnp.max(s, axis=1, keepdims=True))
            o = _dot(e.astype(BF16), vv[:, cols]) / jnp.sum(e, axis=1, keepdims=True)
            if h % 2 == 0:
                pair = o
            else:
                o_ref[:, LANES * (h // 2):LANES * (h // 2 + 1)] = (pair + o).astype(BF16)


def _mla_attn(p1, p1s, gq, gkv, wq, wqs, wk, wv, ctx, *, nb, L):
    sample = ctx is not None
    nq = L // TQ
    tk = L + (PAST if sample else 0)
    rbq0 = ROWS_P // TQ if sample else 0
    rbs0 = ROWS_P // L if sample else 0
    ckv_blk = MLA_Q_RANK // MLA_KV_RANK

    def const(a):
        return pl.BlockSpec(a.shape, lambda b, q, nd=a.ndim: (0,) * nd)

    in_specs = [
        pl.BlockSpec((TQ, MLA_Q_RANK), lambda b, q: (rbq0 + b * nq + q, 0)),
        pl.BlockSpec((L, MLA_KV_RANK), lambda b, q: (rbs0 + b, ckv_blk)),
        pl.BlockSpec((L, 2 * LANES), lambda b, q: (rbs0 + b, 0)),
        const(gq), const(gkv), const(wq), const(wqs), const(wk), const(wv),
    ]
    args = [p1, p1, p1s, gq, gkv, wq, wqs, wk, wv]
    if sample:
        cckv, ckr, cpad, spad = ctx
        in_specs += [
            pl.BlockSpec((None, None, PAST, MLA_KV_RANK), lambda b, q: (b, 0, 0, 0)),
            pl.BlockSpec((None, None, PAST, LANES), lambda b, q: (b, 0, 0, 0)),
            pl.BlockSpec((TQ, LANES), lambda b, q: (q, 0)),
            pl.BlockSpec((TQ, LANES), lambda b, q: (q, 0)),
            const(cpad), const(spad),
        ]
        args += [cckv, ckr, cpad, spad, cpad, spad]
    out_specs = [pl.BlockSpec((TQ, MLA_HEADS * MLA_V), lambda b, q: (b * nq + q, 0))]
    out_shape = [jax.ShapeDtypeStruct((nb * L, MLA_HEADS * MLA_V), BF16)]
    if not sample:
        out_specs += [
            pl.BlockSpec((None, None, L, MLA_KV_RANK), lambda b, q: (b, 0, 0, 0)),
            pl.BlockSpec((None, None, L, MLA_ROPE), lambda b, q: (b, 0, 0, 0)),
        ]
        out_shape += [
            jax.ShapeDtypeStruct((nb, 1, L, MLA_KV_RANK), F32),
            jax.ShapeDtypeStruct((nb, 1, L, MLA_ROPE), F32),
        ]
    scratch = [
        pltpu.VMEM((tk, MLA_HEADS * LANES), BF16),
        pltpu.VMEM((tk, MLA_HEADS * LANES), BF16),
    ]
    return pl.pallas_call(
        functools.partial(_mla_body, L=L, sample=sample),
        grid=(nb, nq),
        in_specs=in_specs,
        out_specs=out_specs,
        out_shape=out_shape,
        scratch_shapes=scratch,
        compiler_params=_cparams(("parallel", "arbitrary")),
        name="mla_attn_s" if sample else "mla_attn_p",
    )(*args)


def _log_sigmoid(x):
    return -_softplus(-x)


def _mlstm_body(*refs, L, hp, has_s0, emit_state):
    it = iter(refs)
    q_ref, k_ref, v_ref, g_ref, gb_ref, ng_ref = (next(it) for _ in range(6))
    if has_s0:
        c0_ref, n0_ref, m0_ref = next(it), next(it), next(it)
    h_ref = next(it)
    if emit_state:
        c_out, n_out, m_out = next(it), next(it), next(it)
    gsc, gtr, kq, vts, kts, hacc, cst, nst, mst = (next(it) for _ in range(9))

    T = SCAN_T
    nc = L // T
    h0 = hp * pl.program_id(1)

    gts = g_ref[...] + gb_ref[...]
    gr = pltpu.roll(gts, jnp.where(h0 == 0, 0, LANES - h0), 1)
    b_f, b_b = _seg_cumsums(_log_sigmoid(gr))
    gsc[0] = gr
    gsc[1] = b_f
    gsc[2] = b_b
    gtr[0] = gr.T
    gtr[1] = b_f.T
    gtr[2] = b_b.T

    jj = lax.broadcasted_iota(jnp.int32, (T, T), 0)
    ii = lax.broadcasted_iota(jnp.int32, (T, T), 1)

    def chunk(hh, c, d):
        rows = slice(T * c, T * (c + 1))
        qcols = slice(ML_DQK * hh, ML_DQK * (hh + 1))
        first = c == (0 if d == 0 else nc - 1)
        last = c == (nc - 1 if d == 0 else 0)
        zero_state = first and not has_s0
        li = 2 * ML_HEADS * d + hh
        lb = li + ML_HEADS
        b_row = gtr[1 + d, lb:lb + 1, rows]
        logi_row = gtr[0, li:li + 1, rows]
        cj = gsc[1 + d, rows, lb:lb + 1] - gsc[0, rows, li:li + 1]
        mask = (jj <= ii) if d == 0 else (jj >= ii)
        dlog = jnp.where(mask, b_row - cj, -jnp.inf)
        m_prev = jnp.zeros((1, 1), F32) if zero_state else mst[hh, d][:, 0:1]
        inter = b_row + m_prev
        mcomb = jnp.maximum(inter, jnp.max(dlog, axis=0, keepdims=True))
        s = kq[hh, c] * jnp.exp(dlog - mcomb)
        vt = vts[hh, c]
        num = _dot(vt, s.astype(BF16))
        den = jnp.sum(s, axis=0, keepdims=True)
        if not zero_state:
            iw = jnp.exp(inter - mcomb)
            qt = q_ref[rows, qcols].T
            num = num + iw * _dot_tn(cst[hh, d].astype(BF16), qt)
            n8 = jnp.broadcast_to(nst[hh, d], (8, ML_DQK)).astype(BF16)
            den = den + iw * _dot(n8, qt)[0:1]
        hc = num / jnp.maximum(jnp.abs(den), jnp.exp(-mcomb))
        if d == 0:
            hacc[hh, :, rows] = hc
        else:
            hacc[hh, :, rows] += hc
        if emit_state or not last:
            end = T * c + (T - 1 if d == 0 else 0)
            bq = gtr[1 + d, lb:lb + 1, end:end + 1]
            wlog = bq - b_row + logi_row
            m_new = jnp.maximum(bq + m_prev, jnp.max(wlog, axis=1, keepdims=True))
            sw = jnp.exp(wlog - m_new)
            upd = _dot_nt((kts[hh, c].astype(F32) * sw).astype(BF16), vt)
            nsum = _dot(jnp.broadcast_to(sw, (8, T)).astype(BF16), k_ref[rows, qcols])[0:1]
            if zero_state:
                cst[hh, d] = upd
                nst[hh, d] = nsum
            else:
                cw = jnp.exp(bq + m_prev - m_new)
                cst[hh, d] = cw * cst[hh, d] + upd
                nst[hh, d] = cw * nst[hh, d] + nsum
            mst[hh, d] = jnp.broadcast_to(m_new, (1, LANES))

    for hh in range(hp):
        qcols = slice(ML_DQK * hh, ML_DQK * (hh + 1))
        vcols = slice(ML_DV * hh, ML_DV * (hh + 1))
        if has_s0:
            for d in range(2):
                cst[hh, d] = c0_ref[d, hh]
                nst[hh, d] = n0_ref[d, pl.ds(h0 + hh, 1), :]
                mst[hh, d] = m0_ref[d, pl.ds(h0 + hh, 1), :]
        for c in range(nc):
            rows = slice(T * c, T * (c + 1))
            kc = k_ref[rows, qcols]
            kq[hh, c] = _dot_nt(kc, q_ref[rows, qcols])
            kts[hh, c] = kc.T
            vts[hh, c] = v_ref[rows, vcols].T
        for c in range(nc):
            chunk(hh, c, 0)
        for c in range(nc):
            chunk(hh, nc - 1 - c, 1)
        ht = hacc[hh]
        r = lax.rsqrt(jnp.mean(ht * ht, axis=0, keepdims=True) + EPS)
        h_ref[:, vcols] = ((ht * r).T * ng_ref[:, vcols]).astype(BF16)
        if emit_state:
            for d in range(2):
                c_out[d, hh] = cst[hh, d]
                n_out[d, pl.ds(h0 + hh, 1), :] = nst[hh, d]
                m_out[d, pl.ds(h0 + hh, 1), :] = mst[hh, d]


def _mlstm_scan(q, k, v, gates, gb, ng, s0, *, nb, L, hp, emit_state):
    rb0 = 0 if s0 is None else ROWS_P // L
    in_specs = [
        pl.BlockSpec((L, hp * ML_DQK), lambda b, h: (rb0 + b, h)),
        pl.BlockSpec((L, hp * ML_DQK), lambda b, h: (rb0 + b, h)),
        pl.BlockSpec((L, hp * ML_DV), lambda b, h: (rb0 + b, h)),
        pl.BlockSpec((L, LANES), lambda b, h: (rb0 + b, 0)),
        pl.BlockSpec((1, LANES), lambda b, h: (0, 0)),
        pl.BlockSpec((1, hp * ML_DV), lambda b, h: (0, h)),
    ]
    args = [q, k, v, gates, gb, ng]
    c_spec = pl.BlockSpec((None, None, 2, hp, ML_DQK, ML_DV), lambda b, h: (b, 0, 0, h, 0, 0))
    n_spec = pl.BlockSpec((None, 2, ML_HEADS, LANES), lambda b, h: (b, 0, 0, 0))
    if s0 is not None:
        in_specs += [c_spec, n_spec, n_spec]
        args += list(s0)
    out_specs = [pl.BlockSpec((L, hp * ML_DV), lambda b, h: (b, h))]
    out_shape = [jax.ShapeDtypeStruct((nb * L, ML_DI), BF16)]
    if emit_state:
        out_specs += [c_spec, n_spec, n_spec]
        out_shape += [
            jax.ShapeDtypeStruct((nb, 1, 2, ML_HEADS, ML_DQK, ML_DV), F32),
            jax.ShapeDtypeStruct((nb, 2, ML_HEADS, LANES), F32),
            jax.ShapeDtypeStruct((nb, 2, ML_HEADS, LANES), F32),
        ]
    scratch = [
        pltpu.VMEM((3, L, LANES), F32),
        pltpu.VMEM((3, LANES, L), F32),
        pltpu.VMEM((hp, L // SCAN_T, SCAN_T, SCAN_T), F32),
        pltpu.VMEM((hp, L // SCAN_T, ML_DV, SCAN_T), BF16),
        pltpu.VMEM((hp, L // SCAN_T, ML_DQK, SCAN_T), BF16),
        pltpu.VMEM((hp, ML_DV, L), F32),
        pltpu.VMEM((hp, 2, ML_DQK, ML_DV), F32),
        pltpu.VMEM((hp, 2, 1, ML_DQK), F32),
        pltpu.VMEM((hp, 2, 1, LANES), F32),
    ]
    return pl.pallas_call(
        functools.partial(_mlstm_body, L=L, hp=hp, has_s0=s0 is not None, emit_state=emit_state),
        grid=(nb, ML_HEADS // hp),
        in_specs=in_specs,
        out_specs=out_specs,
        out_shape=out_shape,
        scratch_shapes=scratch,
        compiler_params=_cparams(("parallel", "arbitrary")),
        name="mlstm_scan_p" if s0 is None else "mlstm_scan_s",
    )(*args)


def _rope_blocks(x, ct, st):
    lane = lax.broadcasted_iota(jnp.int32, (x.shape[0], LANES), 1)
    first_half = lane % DF_D < DF_D // 2
    outs = []
    for blk in range(x.shape[1] // LANES):
        xb = x[:, LANES * blk:LANES * (blk + 1)]
        swapped = jnp.where(first_half, pltpu.roll(xb, LANES - DF_D // 2, 1), pltpu.roll(xb, DF_D // 2, 1))
        outs.append(xb * ct + swapped * st)
    return jnp.concatenate(outs, axis=1)


def _diff_body(*refs, L, sample, lam_init):
    it = iter(refs)
    q_ref, k_ref, v_ref = next(it), next(it), next(it)
    lq1, lk1, lq2, lk2, sg_ref = (next(it) for _ in range(5))
    if sample:
        ck_ref, cv_ref, cq_t, sq_t, ck_t, sk_t = (next(it) for _ in range(6))
    o_ref = next(it)
    if sample:
        ka, va = next(it), next(it)
    else:
        k_out, v_out = next(it), next(it)
    qt = pl.program_id(1)

    lam = (jnp.exp(jnp.sum(lq1[...] * lk1[...], axis=1, keepdims=True))
           - jnp.exp(jnp.sum(lq2[...] * lk2[...], axis=1, keepdims=True)) + lam_init)

    @pl.when(qt == 0)
    def _():
        if sample:
            ka[0:PAST, :] = ck_ref[...].astype(BF16)
            ka[PAST:PAST + L, :] = _rope_blocks(k_ref[...].astype(F32), ck_t[...], sk_t[...]).astype(BF16)
            va[0:PAST, :] = cv_ref[...].astype(BF16)
            va[PAST:PAST + L, :] = v_ref[...]
        else:
            k_out[...] = k_ref[...].astype(F32)
            v_out[...] = v_ref[...].astype(F32)

    q = q_ref[...].astype(F32)
    if sample:
        q = _rope_blocks(q, cq_t[...], sq_t[...])
    q = q * (DF_D ** -0.5)
    lo = lax.broadcasted_iota(jnp.int32, (TQ, LANES), 1) < DF_D

    def attend(qm, kh, vh):
        s = _dot_nt(qm, kh)
        e = jnp.exp(s - jnp.max(s, axis=1, keepdims=True))
        return _dot(e.astype(BF16), vh) / jnp.sum(e, axis=1, keepdims=True)

    for h in range(DF_HEADS):
        cols = slice(LANES * h, LANES * (h + 1))
        qh = q[:, cols]
        if sample:
            kh, vh = ka[:, cols], va[:, cols]
        else:
            kh, vh = k_ref[:, cols], v_ref[:, cols]
        a0 = attend(jnp.where(lo, qh, 0.0).astype(BF16), kh, vh)
        a1 = attend(jnp.where(lo, 0.0, qh).astype(BF16), kh, vh)
        o = a0 - lam * a1
        o_ref[:, cols] = (_rms(o, sg_ref[...]) * (1.0 - lam_init)).astype(BF16)


def _diff_attn(p3, lq1, lk1, lq2, lk2, sg, ctx, *, nb, L, lam_init):
    sample = ctx is not None
    nq = L // TQ
    rbq0 = ROWS_P // TQ if sample else 0
    rbs0 = ROWS_P // L if sample else 0

    def const(a):
        return pl.BlockSpec(a.shape, lambda b, q, nd=a.ndim: (0,) * nd)

    in_specs = [
        pl.BlockSpec((TQ, D), lambda b, q: (rbq0 + b * nq + q, 0)),
        pl.BlockSpec((L, D), lambda b, q: (rbs0 + b, 1)),
        pl.BlockSpec((L, D), lambda b, q: (rbs0 + b, 2)),
        const(lq1), const(lk1), const(lq2), const(lk2), const(sg),
    ]
    args = [p3, p3, p3, lq1, lk1, lq2, lk2, sg]
    kv_spec = pl.BlockSpec((None, None, PAST if sample else L, D), lambda b, q: (b, 0, 0, 0))
    if sample:
        ck, cv, c128, s128 = ctx
        in_specs += [
            kv_spec, kv_spec,
            pl.BlockSpec((TQ, LANES), lambda b, q: (q, 0)),
            pl.BlockSpec((TQ, LANES), lambda b, q: (q, 0)),
            const(c128), const(s128),
        ]
        args += [ck, cv, c128, s128, c128, s128]
    out_specs = [pl.BlockSpec((TQ, D), lambda b, q: (b * nq + q, 0))]
    out_shape = [jax.ShapeDtypeStruct((nb * L, D), BF16)]
    scratch = []
    if sample:
        scratch = [pltpu.VMEM((PAST + L, D), BF16), pltpu.VMEM((PAST + L, D), BF16)]
    else:
        out_specs += [kv_spec, kv_spec]
        out_shape += [jax.ShapeDtypeStruct((nb, 1, L, D), F32)] * 2
    return pl.pallas_call(
        functools.partial(_diff_body, L=L, sample=sample, lam_init=lam_init),
        grid=(nb, nq),
        in_specs=in_specs,
        out_specs=out_specs,
        out_shape=out_shape,
        scratch_shapes=scratch,
        compiler_params=_cparams(("parallel", "arbitrary")),
        name="diff_attn_s" if sample else "diff_attn_p",
    )(*args)


CONV_COLS = 256


def _mlstm_q_body(x_ref, cw_ref, cb_ref, w_ref, q_ref, xc_ref, msk):
    tm = x_ref.shape[0]
    seq = jnp.where(pl.program_id(0) < ROWS_P // tm, L_PROMPT, L_SAMPLE)
    pos = lax.broadcasted_iota(jnp.int32, (tm, CONV_COLS), 0) & (seq - 1)
    offs = [k - CONV_W // 2 for k in range(CONV_W)]
    for k, off in enumerate(offs):
        if off != 0:
            dst = pos - off
            msk[k] = jnp.where(jnp.logical_and(dst >= 0, dst < seq), 1.0, 0.0)
    acc = None
    kslab = 2 * CONV_COLS
    for kb in range(ML_DI // kslab):
        for cb in range(kslab // CONV_COLS):
            cols = slice(kslab * kb + CONV_COLS * cb, kslab * kb + CONV_COLS * (cb + 1))
            x = x_ref[:, cols].astype(F32)
            y = None
            for k, off in enumerate(offs):
                tap = x if off == 0 else pltpu.roll(x * msk[k], (-off) % tm, 0)
                tap = tap * cw_ref[k:k + 1, cols]
                y = cb_ref[:, cols] + tap if y is None else y + tap
            xc_ref[:, cols] = _silu(y).astype(BF16)
        kcols = slice(kslab * kb, kslab * (kb + 1))
        part = _dot(xc_ref[:, kcols], w_ref[kcols, :].astype(BF16))
        acc = part if acc is None else part + acc
    q_ref[...] = acc.astype(BF16)


def _mlstm_q(xm_src, conv_w, conv_b, w_q):
    tm = TM_CONVQ
    nq = ML_HEADS * ML_DQK
    return pl.pallas_call(
        _mlstm_q_body,
        grid=(ROWS // tm,),
        in_specs=[
            pl.BlockSpec((tm, ML_DI), lambda i: (i, 0)),
            pl.BlockSpec((CONV_W, ML_DI), lambda i: (0, 0)),
            pl.BlockSpec((1, ML_DI), lambda i: (0, 0)),
            pl.BlockSpec((ML_DI, nq), lambda i: (0, 0)),
        ],
        out_specs=[pl.BlockSpec((tm, nq), lambda i: (i, 0)), pl.BlockSpec((tm, ML_DI), lambda i: (i, 0))],
        out_shape=[jax.ShapeDtypeStruct((ROWS, nq), BF16), jax.ShapeDtypeStruct((ROWS, ML_DI), BF16)],
        scratch_shapes=[pltpu.VMEM((CONV_W, tm, CONV_COLS), F32)],
        compiler_params=_cparams(("parallel",)),
        name="mlstm_q",
    )(xm_src, conv_w, conv_b, w_q)


def _pro_mlstm_gate(p, rs):
    hn, xc, z = (r[rs, :].astype(F32) for r in p.rows)
    p.emit((hn + p.consts[0][...] * xc) * _silu(z), rs)


def _rope_tables(d):
    rows = L_SAMPLE // GRID_W
    pos_r = jnp.repeat(jnp.arange(rows, dtype=F32), GRID_W)
    pos_c = jnp.tile(jnp.arange(GRID_W, dtype=F32), rows)
    nf = d // 4
    inv = ROPE_BASE ** (-jnp.arange(nf, dtype=F32) / nf)
    ang = jnp.concatenate([pos_r[:, None] * inv, pos_c[:, None] * inv], axis=-1)
    cos, sin = jnp.cos(ang), jnp.sin(ang)
    return jnp.concatenate([cos, cos], axis=-1), jnp.concatenate([-sin, sin], axis=-1)


def _pad_cols(a, n):
    return jnp.pad(a, ((0, 0), (0, n - a.shape[1])))


def kernel(x_prompt, x_sample, state_ssd, cache_mla_ckv, cache_mla_krope, state_mlstm_C, state_mlstm_n, state_mlstm_m, cache_diff_k, cache_diff_v, c, c_ctx, norm1_g, norm2_g, ada_w, ada_b, mlp_w1, mlp_w2, final_g, ssd_w_in, ssd_conv_w, ssd_conv_b, ssd_dt_bias, ssd_A_log, ssd_D, ssd_norm_g, ssd_w_out, mla_w_in, mla_q_norm_g, mla_kv_norm_g, mla_w_uq, mla_w_ukv, mla_w_o, mlstm_w_up, mlstm_conv_w, mlstm_conv_b, mlstm_gate_b, mlstm_w_q, mlstm_w_k, mlstm_w_v, mlstm_skip, mlstm_norm_g, mlstm_w_down, diff_w_qkv, diff_lq1, diff_lk1, diff_lq2, diff_lk2, diff_subln_g, diff_w_o):
    xp2, xs2 = x_prompt.reshape(ROWS_P, D), x_sample.reshape(ROWS_S, D)
    cvec = jnp.concatenate([c_ctx[None, :], c, jnp.zeros((5, D), F32)], axis=0)
    mod_all = _ada_mod(cvec, ada_w, ada_b)
    g2 = norm2_g.reshape(DEPTH, 1, D)

    def in_proj(xin, layer, w, n_cols, tn, extra_w=None, name="in_proj"):
        xrow = ("d", xin[0], xin[1], D, 0) if isinstance(xin, tuple) else ("u", xin, D, 0)
        return _fused_mm(rows=[xrow], consts=[norm1_g[layer][None, :]], mod=(mod_all, layer), w=w, k_dim=D,
                         n_cols=n_cols, tm=TM_IN, tn=tn, prologue=_pro_normmod, extra_w=extra_w, name=name)

    def out_proj(xin, layer, rows, consts, prologue, w, k_dim, name):
        return _fused_mm(rows=rows, consts=consts, w=w, k_dim=k_dim, n_cols=D, tm=TM_OUT, tn=TN_PROJ, prologue=prologue,
                         epilogue=_epi_residual(2), erows=[xin], emod=(mod_all, layer), out_dtype=F32, name=name)

    p0, pdt = in_proj((xp2, xs2), 0, ssd_w_in[0], 3 * SSD_DI, TN_SSD_IN, extra_w=_pad_cols(ssd_w_in[0][:, 3 * SSD_DI:], LANES),
                      name="ssd_in")
    dtb = _pad_cols(ssd_dt_bias[0].reshape(1, 2 * SSD_HEADS), LANES)
    alog = _pad_cols(ssd_A_log[0].reshape(1, 2 * SSD_HEADS), LANES)
    dl = jnp.repeat(ssd_D[0], SSD_P)[None, :]
    scan_args = (p0, pdt, ssd_conv_w[0], ssd_conv_b[0][None, :], dtb, alog, dl)
    yg_p, new_ssd = _ssd_scan(*scan_args, None, nb=N_PROMPT_SEQ, L=L_PROMPT, emit_state=True)
    (yg_s,) = _ssd_scan(*scan_args, state_ssd, nb=N_SAMPLE_SEQ, L=L_SAMPLE, emit_state=False)
    x = out_proj((xp2, xs2), 0, [("d", yg_p, yg_s, SSD_DI, 0)], [ssd_norm_g[0][None, :]], _pro_rms, ssd_w_out[0], SSD_DI,
                 "ssd_out")
    x = _mlp(x, 0, g2, mod_all, mlp_w1, mlp_w2)

    w_in = mla_w_in[0]
    kr0 = MLA_Q_RANK + MLA_KV_RANK
    half = MLA_ROPE // 2
    zk = jnp.zeros((D, MLA_NOPE), F32)
    zr = jnp.zeros((D, LANES - MLA_NOPE - MLA_ROPE), F32)
    w_kr = jnp.concatenate([zk, w_in[:, kr0:kr0 + MLA_ROPE], zr,
                            zk, w_in[:, kr0 + half:kr0 + MLA_ROPE], w_in[:, kr0:kr0 + half], zr], axis=1)
    p1, p1s = in_proj(x, 1, w_in, kr0, kr0, extra_w=w_kr, name="mla_in")
    wuq = mla_w_uq[0].reshape(MLA_Q_RANK, MLA_HEADS, MLA_NOPE + MLA_ROPE)
    zq = jnp.zeros((MLA_Q_RANK, MLA_HEADS, LANES - MLA_NOPE - MLA_ROPE), F32)
    zqn = jnp.zeros((MLA_Q_RANK, MLA_HEADS, MLA_NOPE), F32)
    wq = jnp.concatenate([wuq, zq], axis=-1).reshape(MLA_Q_RANK, MLA_HEADS * LANES).astype(BF16)
    wqs = jnp.concatenate([zqn, wuq[..., MLA_NOPE + half:], wuq[..., MLA_NOPE:MLA_NOPE + half], zq],
                          axis=-1).reshape(MLA_Q_RANK, MLA_HEADS * LANES).astype(BF16)
    wukv = mla_w_ukv[0].reshape(MLA_KV_RANK, MLA_HEADS, MLA_NOPE + MLA_V)
    zkv = jnp.zeros((MLA_KV_RANK, MLA_HEADS, MLA_NOPE), F32)
    wk = jnp.concatenate([wukv[..., :MLA_NOPE], zkv], axis=-1).reshape(MLA_KV_RANK, MLA_HEADS * LANES).astype(BF16)
    wv_own = wukv[..., MLA_NOPE:]
    odd = (jnp.arange(MLA_HEADS) % 2 == 1)[None, :, None]
    wv = jnp.where(odd, jnp.concatenate([zkv, wv_own], axis=-1), jnp.concatenate([wv_own, zkv], axis=-1))
    wv = wv.reshape(MLA_KV_RANK, MLA_HEADS * LANES).astype(BF16)
    c32, s32 = _rope_tables(MLA_ROPE)
    tz = jnp.zeros((L_SAMPLE, LANES - MLA_NOPE - MLA_ROPE), F32)
    cpad = jnp.concatenate([jnp.ones((L_SAMPLE, MLA_NOPE), F32), c32, tz], axis=1)
    spad = jnp.concatenate([jnp.zeros((L_SAMPLE, MLA_NOPE), F32), s32, tz], axis=1)
    ckr_pad = jnp.pad(cache_mla_krope, ((0, 0), (0, 0), (0, 0), (MLA_NOPE, LANES - MLA_NOPE - MLA_ROPE)))
    mla_w = (mla_q_norm_g[0][None, :], mla_kv_norm_g[0][None, :], wq, wqs, wk, wv)
    o_p, new_ckv, new_kr = _mla_attn(p1, p1s, *mla_w, None, nb=N_PROMPT_SEQ, L=L_PROMPT)
    (o_s,) = _mla_attn(p1, p1s, *mla_w, (cache_mla_ckv, ckr_pad, cpad, spad), nb=N_SAMPLE_SEQ, L=L_SAMPLE)
    x = out_proj(x, 1, [("d", o_p, o_s, D, 0)], [], _pro_cast, mla_w_o[0], D, "mla_out")
    x = _mlp(x, 1, g2, mod_all, mlp_w1, mlp_w2)

    p2, gates = in_proj(x, 2, mlstm_w_up[0], 2 * ML_DI, TN_IN, extra_w=_pad_cols(mlstm_w_up[0][:, 2 * ML_DI:], LANES),
                        name="mlstm_up")
    q, xc = _mlstm_q(p2, mlstm_conv_w[0], mlstm_conv_b[0][None, :], mlstm_w_q[0])
    k = _fused_mm(rows=[("u", xc, ML_DI, 0)], w=mlstm_w_k[0], k_dim=ML_DI, n_cols=ML_HEADS * ML_DQK, tm=TM_WIDE,
                  tn=TN_PROJ, epilogue=lambda acc, e, m, rs: acc * (ML_DQK ** -0.5), name="mlstm_k")
    v = _fused_mm(rows=[("u", p2, ML_DI, 0)], w=mlstm_w_v[0], k_dim=ML_DI, n_cols=ML_DI, tm=TM_WIDE, tn=TN_PROJ,
                  name="mlstm_v")
    gb = _pad_cols(mlstm_gate_b[0].reshape(1, 4 * ML_HEADS), LANES)
    ng = mlstm_norm_g[0][None, :]
    n0 = _pad_cols(state_mlstm_n[:, 0].reshape(N_SAMPLE_SEQ * 2 * ML_HEADS, ML_DQK), LANES).reshape(
        N_SAMPLE_SEQ, 2, ML_HEADS, LANES)
    m0 = jnp.broadcast_to(state_mlstm_m[:, 0][..., None], (N_SAMPLE_SEQ, 2, ML_HEADS, LANES))
    hn_p, new_c, new_n, new_m = _mlstm_scan(q, k, v, gates, gb, ng, None, nb=N_PROMPT_SEQ, L=L_PROMPT, hp=HP_P,
                                            emit_state=True)
    (hn_s,) = _mlstm_scan(q, k, v, gates, gb, ng, (state_mlstm_C, n0, m0), nb=N_SAMPLE_SEQ, L=L_SAMPLE, hp=HP_S,
                          emit_state=False)
    x = out_proj(x, 2, [("d", hn_p, hn_s, ML_DI, 0), ("u", xc, ML_DI, 0), ("u", p2, ML_DI, 1)],
                 [mlstm_skip[0][None, :]], _pro_mlstm_gate, mlstm_w_down[0], ML_DI, "mlstm_down")
    x = _mlp(x, 2, g2, mod_all, mlp_w1, mlp_w2)

    lam_init = 0.8 - 0.6 * math.exp(-0.3 * 3)
    p3 = in_proj(x, 3, diff_w_qkv[0], 3 * D, TN_IN, name="diff_qkv")
    c64, s64 = _rope_tables(DF_D)
    c128 = jnp.concatenate([c64, c64], axis=-1)
    s128 = jnp.concatenate([s64, s64], axis=-1)
    dparams = (diff_lq1, diff_lk1, diff_lq2, diff_lk2, diff_subln_g)
    od_p, new_dk, new_dv = _diff_attn(p3, *dparams, None, nb=N_PROMPT_SEQ, L=L_PROMPT, lam_init=lam_init)
    ctx = (cache_diff_k.reshape(N_SAMPLE_SEQ, 1, PAST, D), cache_diff_v.reshape(N_SAMPLE_SEQ, 1, PAST, D), c128, s128)
    (od_s,) = _diff_attn(p3, *dparams, ctx, nb=N_SAMPLE_SEQ, L=L_SAMPLE, lam_init=lam_init)
    x = out_proj(x, 3, [("d", od_p, od_s, D, 0)], [], _pro_cast, diff_w_o[0], D, "diff_out")
    y_prompt, y_sample = _mlp(x, 3, g2, mod_all, mlp_w1, mlp_w2, final_g=final_g[None, :])
    y_prompt = y_prompt.reshape(N_PROMPT_SEQ, L_PROMPT, D)
    y_sample = y_sample.reshape(N_SAMPLE_SEQ, L_SAMPLE, D)
    return (y_prompt, y_sample, new_ssd, new_ckv, new_kr, new_c,
            new_n[None].reshape(N_PROMPT_SEQ, 1, 2, ML_HEADS, ML_DQK),
            new_m[..., 0].reshape(N_PROMPT_SEQ, 1, 2, ML_HEADS),
            new_dk.reshape(N_PROMPT_SEQ, 1, L_PROMPT, DF_HEADS, 2 * DF_D),
            new_dv.reshape(N_PROMPT_SEQ, 1, L_PROMPT, DF_HEADS, 2 * DF_D))
```

```python
import functools
import math

import jax
import jax.numpy as jnp
from jax import lax
from jax.experimental import pallas as pl
from jax.experimental.pallas import tpu as pltpu

F32 = jnp.float32
BF16 = jnp.bfloat16

D = 1024
DEPTH = 4
D_FF = 4 * D
EPS = 1e-6
ROPE_BASE = 10000.0
CONV_W = 5
GRID_W = 64

N_PROMPT_SEQ = 32
L_PROMPT = 256
N_SAMPLE_SEQ = 2
L_SAMPLE = 1024
PAST = 256
ROWS_P = N_PROMPT_SEQ * L_PROMPT
ROWS_S = N_SAMPLE_SEQ * L_SAMPLE
ROWS = ROWS_P + ROWS_S

SSD_DI = 2 * D
SSD_HEADS = 32
SSD_P = 64
SSD_GROUPS = 8
SSD_N = 128
SSD_HPG = SSD_HEADS // SSD_GROUPS

MLA_HEADS = 16
MLA_Q_RANK = 512
MLA_KV_RANK = 256
MLA_NOPE = 64
MLA_ROPE = 32
MLA_V = 64

ML_DI = 2 * D
ML_HEADS = 8
ML_DQK = 128
ML_DV = 256

DF_HEADS = 8
DF_D = 64

LANES = 128
VMEM_LIMIT = 56 * 1024 * 1024

TM_IN = 2048
TN_IN = 1024
TN_SSD_IN = 512
TM_OUT = 1024
TM_WIDE = 2048
TN_PROJ = 512
TM_MLP = 1024
TF_MLP = 1024
TM_CONVQ = 1024
TN_ADA = 1536


def _cparams(sem):
    return pltpu.CompilerParams(dimension_semantics=sem, vmem_limit_bytes=VMEM_LIMIT)


def _silu(x):
    return x * jax.nn.sigmoid(x)


def _softplus(x):
    return jnp.maximum(x, 0.0) + jnp.log1p(jnp.exp(-jnp.abs(x)))


def _rms(x, g):
    r = lax.rsqrt(jnp.mean(x * x, axis=-1, keepdims=True) + EPS)
    return (x * r) * g


def _dot(a, b):
    return jnp.dot(a, b, preferred_element_type=F32)


def _dot_nt(a, b):
    return lax.dot_general(a, b, (((1,), (1,)), ((), ())), preferred_element_type=F32)


def _dot_tn(a, b):
    return lax.dot_general(a, b, (((0,), (0,)), ((), ())), preferred_element_type=F32)


MOD_ROWS = 1024


def _group_of_tile(i, tm, sub=0):
    row0 = i * tm + sub * MOD_ROWS
    return jnp.where(row0 < ROWS_P, 0, 1 + (row0 - ROWS_P) // L_SAMPLE)


def _ada_body(c_ref, w_ref, b_ref, o_ref):
    s = _silu(c_ref[...]).astype(BF16)
    o_ref[...] = _dot(s, w_ref[...].astype(BF16)) + b_ref[...]


def _ada_mod(cvec, ada_w, ada_b):
    tn = TN_ADA
    out = pl.pallas_call(
        _ada_body,
        grid=(DEPTH, 6 * D // tn),
        in_specs=[
            pl.BlockSpec((8, D), lambda l, j: (0, 0)),
            pl.BlockSpec((None, D, tn), lambda l, j: (l, 0, j)),
            pl.BlockSpec((None, 1, tn), lambda l, j: (l, 0, j)),
        ],
        out_specs=pl.BlockSpec((None, 8, tn), lambda l, j: (l, 0, j)),
        out_shape=jax.ShapeDtypeStruct((DEPTH, 8, 6 * D), F32),
        compiler_params=_cparams(("parallel", "parallel")),
        name="ada_mod",
    )(cvec, ada_w, ada_b.reshape(DEPTH, 1, 6 * D))
    return out[:, :3].reshape(DEPTH, 3, 6, D)


class _Pro:
    def __init__(self, rows, consts, mod, i, hs, lhs_out):
        self.rows, self.consts, self.mod, self.i = rows, consts, mod, i
        self._hs, self._lhs_out = hs, lhs_out

    def emit(self, val, rows=slice(None), cols=slice(None)):
        vb = val.astype(BF16)
        self._hs[rows, cols] = vb
        if self._lhs_out is not None:
            self._lhs_out[rows, cols] = vb


ROW_CHUNK = 512


def _fused_mm(*, rows, w, k_dim, n_cols, tm, tn, prologue=None, consts=(), mod=None,
              epilogue=None, erows=(), emod=None, w_col0=0, emit_lhs=False, extra_w=None, out_dtype=BF16, name):
    npt = ROWS_P // tm
    grid = (ROWS // tm, n_cols // tn)
    has_dual = any(r[0] == "d" for r in rows)
    dual_epi = any(isinstance(e, tuple) for e in erows)
    if prologue is None:
        assert len(rows) == 1 and rows[0][0] == "u" and not emit_lhs

    in_specs, args = [], []
    for r in rows:
        if r[0] == "u":
            _, arr, width, cb = r
            in_specs.append(pl.BlockSpec((tm, width), lambda i, j, cb=cb: (i, cb)))
            args.append(arr)
        else:
            _, arr_p, arr_s, width, cb = r
            in_specs.append(pl.BlockSpec((tm, width), lambda i, j, cb=cb: (jnp.minimum(i, npt - 1), cb)))
            in_specs.append(pl.BlockSpec((tm, width), lambda i, j, cb=cb: (jnp.maximum(i - npt, 0), cb)))
            args += [arr_p, arr_s]
    for c in consts:
        in_specs.append(pl.BlockSpec(c.shape, lambda i, j, nd=c.ndim: (0,) * nd))
        args.append(c)
    n_sub = max(tm // MOD_ROWS, 1)
    if mod is not None:
        mod_arr, mod_layer = mod
        for s in range(n_sub):
            in_specs.append(pl.BlockSpec((None, None, 6, D),
                                         lambda i, j, s=s: (mod_layer, _group_of_tile(i, tm, s), 0, 0)))
            args.append(mod_arr)
    in_specs.append(pl.BlockSpec((k_dim, tn), lambda i, j: (0, w_col0 // tn + j)))
    args.append(w)
    if extra_w is not None:
        in_specs.append(pl.BlockSpec(extra_w.shape, lambda i, j: (0, 0)))
        args.append(extra_w)
    for e in erows:
        if isinstance(e, tuple):
            in_specs.append(pl.BlockSpec((tm, tn), lambda i, j: (jnp.minimum(i, npt - 1), j)))
            in_specs.append(pl.BlockSpec((tm, tn), lambda i, j: (jnp.maximum(i - npt, 0), j)))
            args += list(e)
        else:
            in_specs.append(pl.BlockSpec((tm, tn), lambda i, j: (i, j)))
            args.append(e)
    if emod is not None:
        emod_arr, emod_layer = emod
        in_specs.append(pl.BlockSpec((None, None, 6, tn), lambda i, j: (emod_layer, _group_of_tile(i, tm), 0, j)))
        args.append(emod_arr)

    out_specs = [pl.BlockSpec((tm, tn), lambda i, j: (i, j))]
    out_shape = [jax.ShapeDtypeStruct((ROWS, n_cols), out_dtype)]
    if emit_lhs:
        out_specs.append(pl.BlockSpec((tm, k_dim), lambda i, j: (i, 0)))
        out_shape.append(jax.ShapeDtypeStruct((ROWS, k_dim), BF16))
    if extra_w is not None:
        out_specs.append(pl.BlockSpec((tm, extra_w.shape[1]), lambda i, j: (i, 0)))
        out_shape.append(jax.ShapeDtypeStruct((ROWS, extra_w.shape[1]), F32))
    scratch = [] if prologue is None else [pltpu.VMEM((tm, k_dim), BF16)]
    chunks = [slice(r, r + ROW_CHUNK) for r in range(0, tm, ROW_CHUNK)]

    def body(*refs):
        it = iter(refs)
        row_refs = [(next(it),) if r[0] == "u" else (next(it), next(it)) for r in rows]
        const_refs = [next(it) for _ in consts]
        mod_ref = [next(it) for _ in range(n_sub)] if mod is not None else None
        w_ref = next(it)
        extra_ref = next(it) if extra_w is not None else None
        erow_refs = [(next(it), next(it)) if isinstance(e, tuple) else (next(it),) for e in erows]
        emod_ref = next(it) if emod is not None else None
        out_ref = next(it)
        lhs_out = next(it) if emit_lhs else None
        extra_out = next(it) if extra_w is not None else None
        hs = next(it) if prologue is not None else None
        i = pl.program_id(0)
        j = pl.program_id(1)

        def compute(first, use_prompt=True):
            chosen = [rr[0] if (len(rr) == 1 or use_prompt) else rr[1] for rr in row_refs]
            echosen = [er[0] if (len(er) == 1 or use_prompt) else er[1] for er in erow_refs]
            p = _Pro(chosen, const_refs, mod_ref, i, hs, lhs_out)
            wb = w_ref[...].astype(BF16)
            for rs in chunks:
                if first:
                    prologue(p, rs)
                lhs = chosen[0][rs, :] if prologue is None else hs[rs, :]
                acc = _dot(lhs, wb)
                if epilogue is not None:
                    acc = epilogue(acc, echosen, emod_ref, rs)
                out_ref[rs, :] = acc.astype(out_dtype)
                if first and extra_w is not None:
                    extra_out[rs, :] = _dot(lhs, extra_ref[...].astype(BF16))

        if prologue is None:
            if extra_w is None:
                compute(False)
            else:
                pl.when(j == 0)(lambda: compute(True))
                pl.when(j > 0)(lambda: compute(False))
        else:
            if has_dual or dual_epi:
                pl.when(jnp.logical_and(j == 0, i < npt))(lambda: compute(True, True))
                pl.when(jnp.logical_and(j == 0, i >= npt))(lambda: compute(True, False))
            else:
                pl.when(j == 0)(lambda: compute(True))
            if dual_epi:
                pl.when(jnp.logical_and(j > 0, i < npt))(lambda: compute(False, True))
                pl.when(jnp.logical_and(j > 0, i >= npt))(lambda: compute(False, False))
            else:
                pl.when(j > 0)(lambda: compute(False))

    res = pl.pallas_call(
        body,
        grid=grid,
        in_specs=in_specs,
        out_specs=out_specs,
        out_shape=out_shape,
        scratch_shapes=scratch,
        compiler_params=_cparams(("parallel", "arbitrary")),
        name=name,
    )(*args)
    return res if len(res) > 1 else res[0]


def _pro_normmod(p, rs):
    m = p.mod[rs.start // MOD_ROWS]
    h = _rms(p.rows[0][rs, :], p.consts[0][...]) * (1.0 + m[1:2, :]) + m[0:1, :]
    p.emit(h, rs)


def _pro_rms(p, rs):
    p.emit(_rms(p.rows[0][rs, :].astype(F32), p.consts[0][...]), rs)


def _epi_residual(gate_row):
    def epi(acc, erows, emod, rs):
        return erows[0][rs, :] + emod[gate_row:gate_row + 1, :] * acc
    return epi


def _mlp_body(*refs, final):
    it = iter(refs)
    x_ref, g_ref, mod_ref, w1_ref, w2_ref = (next(it) for _ in range(5))
    fg_ref = next(it) if final else None
    outs = [next(it), next(it)] if final else [next(it)]
    hs, acc = next(it), next(it)
    i = pl.program_id(0)
    f = pl.program_id(1)
    nf = pl.num_programs(1)
    tm = x_ref.shape[0]
    chunks = [slice(r, r + ROW_CHUNK) for r in range(0, tm, ROW_CHUNK)]

    def step(first, last):
        w1b = w1_ref[...].astype(BF16)
        w2b = w2_ref[...].astype(BF16)
        for rs in chunks:
            if first:
                h = _rms(x_ref[rs, :], g_ref[...]) * (1.0 + mod_ref[4:5, :]) + mod_ref[3:4, :]
                hs[rs, :] = h.astype(BF16)
            u = jnp.square(jnp.maximum(_dot(hs[rs, :], w1b), 0.0))
            tot = _dot(u.astype(BF16), w2b)
            if not first:
                tot = acc[rs, :] + tot
            if not last:
                acc[rs, :] = tot
                continue
            y = x_ref[rs, :] + mod_ref[5:6, :] * tot
            if not final:
                outs[0][rs, :] = y
            else:
                y = _rms(y, fg_ref[...])
                npt = ROWS_P // tm

                @pl.when(i < npt)
                def _():
                    outs[0][rs, :] = y

                @pl.when(i >= npt)
                def _():
                    outs[1][rs, :] = y

    pl.when(f == 0)(lambda: step(True, False))
    pl.when(jnp.logical_and(f > 0, f < nf - 1))(lambda: step(False, False))
    pl.when(f == nf - 1)(lambda: step(False, True))


def _mlp(x, layer, g, mod, w1, w2, final_g=None):
    tm, tf = TM_MLP, TF_MLP
    npt = ROWS_P // tm
    final = final_g is not None
    in_specs = [
        pl.BlockSpec((tm, D), lambda i, f: (i, 0)),
        pl.BlockSpec((None, 1, D), lambda i, f: (layer, 0, 0)),
        pl.BlockSpec((None, None, 6, D), lambda i, f: (layer, _group_of_tile(i, tm), 0, 0)),
        pl.BlockSpec((None, D, tf), lambda i, f: (layer, 0, f)),
        pl.BlockSpec((None, tf, D), lambda i, f: (layer, f, 0)),
    ]
    args = [x, g, mod, w1, w2]
    if final:
        in_specs.append(pl.BlockSpec((1, D), lambda i, f: (0, 0)))
        args.append(final_g)
        out_specs = [pl.BlockSpec((tm, D), lambda i, f: (jnp.minimum(i, npt - 1), 0)),
                     pl.BlockSpec((tm, D), lambda i, f: (jnp.maximum(i - npt, 0), 0))]
        out_shape = [jax.ShapeDtypeStruct((ROWS_P, D), F32), jax.ShapeDtypeStruct((ROWS_S, D), F32)]
    else:
        out_specs = pl.BlockSpec((tm, D), lambda i, f: (i, 0))
        out_shape = jax.ShapeDtypeStruct((ROWS, D), F32)
    return pl.pallas_call(
        functools.partial(_mlp_body, final=final),
        grid=(ROWS // tm, D_FF // tf),
        in_specs=in_specs,
        out_specs=out_specs,
        out_shape=out_shape,
        scratch_shapes=[pltpu.VMEM((tm, D), BF16), pltpu.VMEM((tm, D), F32)],
        compiler_params=_cparams(("arbitrary", "arbitrary")),
        name="mlp_final" if final else "mlp",
    )(*args)


SCAN_T = 256


def _seg_cumsums(a):
    n = a.shape[0]
    ii = lax.broadcasted_iota(jnp.int32, (SCAN_T, SCAN_T), 0)
    jj = lax.broadcasted_iota(jnp.int32, (SCAN_T, SCAN_T), 1)
    lower = jnp.where(ii >= jj, 1.0, 0.0).astype(BF16)
    upper = jnp.where(ii <= jj, 1.0, 0.0).astype(BF16)
    hi = a.astype(BF16)
    rest = a - hi.astype(F32)
    mid = rest.astype(BF16)
    lo = (rest - mid.astype(F32)).astype(BF16)
    parts = jnp.concatenate([hi, mid, lo], axis=1)
    pre, suf = [], []
    for c in range(n // SCAN_T):
        pc = parts[SCAN_T * c:SCAN_T * (c + 1), :]
        for tri, out in ((lower, pre), (upper, suf)):
            s3 = _dot(tri, pc)
            out.append(s3[:, :LANES] + s3[:, LANES:2 * LANES] + s3[:, 2 * LANES:])
    return jnp.concatenate(pre, axis=0), jnp.concatenate(suf, axis=0)


LOG2E = 1.4426950408889634


def _ssd_body(*refs, L, has_s0, emit_state):
    it = iter(refs)
    z_ref, x_ref, b_ref, c_ref, dt_ref = (next(it) for _ in range(5))
    cwx, cbx, cwb, cbb, cwc, cbc = (next(it) for _ in range(6))
    dtb_ref, alog_ref, dl_ref = next(it), next(it), next(it)
    s0_ref = next(it) if has_s0 else None
    y_ref = next(it)
    st_out = next(it) if emit_state else None
    padx, padb, xa, ba, ca, cbs, cumc, crp, yacc, st = (next(it) for _ in range(10))

    T = SCAN_T
    nc = L // T
    g0 = pl.program_id(0)

    def conv(in_ref, w_ref, bias_ref, pad):
        width = in_ref.shape[1]
        pad[0:8, :] = jnp.zeros((8, width), F32)
        pad[L + 8:L + 16, :] = jnp.zeros((8, width), F32)
        pad[8:L + 8, :] = in_ref[...].astype(F32)
        acc = bias_ref[...] + pad[6:6 + L, :] * w_ref[0:1, :]
        for k in range(1, CONV_W):
            acc = acc + pad[6 + k:6 + k + L, :] * w_ref[k:k + 1, :]
        return _silu(acc)

    xa[...] = conv(x_ref, cwx, cbx, padx)
    ba[...] = conv(b_ref, cwb, cbb, padb)
    ca[...] = conv(c_ref, cwc, cbc, padb)

    dt_all = _softplus(dt_ref[...] + dtb_ref[...])
    a_all = dt_all * (-jnp.exp(alog_ref[...]))
    shift = jnp.where(g0 == 0, 0, LANES - SSD_HPG * g0)
    l2dt = jnp.log2(pltpu.roll(dt_all, shift, 1))
    ar = pltpu.roll(a_all, shift, 1) * LOG2E
    cum_f, cum_b = _seg_cumsums(ar)
    cumc[0] = cum_f
    cumc[1] = cum_b
    crp[0] = (cum_f - l2dt).T
    crp[1] = (cum_b - l2dt).T

    yacc[...] = xa[...] * dl_ref[...]
    if has_s0:
        st[...] = s0_ref[...]

    ii = lax.broadcasted_iota(jnp.int32, (T, T), 0)
    jj = lax.broadcasted_iota(jnp.int32, (T, T), 1)

    def chunk(c, d):
        rows = slice(T * c, T * (c + 1))
        first = c == (0 if d == 0 else nc - 1)
        last = c == (nc - 1 if d == 0 else 0)
        zero_state = first and not has_s0
        need_state = emit_state or not last
        mask = (ii >= jj) if d == 0 else (ii <= jj)
        end = T * c + (T - 1 if d == 0 else 0)
        xav = xa[rows, :]
        xab = xav.astype(BF16)
        cab = ca[rows, :].astype(BF16)
        if need_state:
            xat = xav.T.astype(BF16)
            bat = ba[rows, :].T
        for r in range(SSD_HPG):
            lane = SSD_HEADS * d + r
            hs = slice(SSD_P * r, SSD_P * (r + 1))
            cc = cumc[d, rows, lane:lane + 1]
            cr = crp[d, lane:lane + 1, rows]
            dec = jnp.where(mask, jnp.exp2(cc - cr), 0.0)
            y = _dot((cbs[c] * dec).astype(BF16), xab[:, hs])
            if not zero_state:
                y = y + _dot_nt(cab, st[d, r].astype(BF16)) * jnp.exp2(cc)
            yacc[rows, hs] += y
            if need_state:
                tot = cumc[d, end:end + 1, lane:lane + 1]
                upd = _dot_nt(xat[hs, :], (bat * jnp.exp2(tot - cr)).astype(BF16))
                st[d, r] = upd if zero_state else jnp.exp2(tot) * st[d, r] + upd

    for c in range(nc):
        rows = slice(T * c, T * (c + 1))
        cbs[c] = _dot_nt(ca[rows, :].astype(BF16), ba[rows, :].astype(BF16))
    for c in range(nc):
        chunk(c, 0)
        chunk(nc - 1 - c, 1)

    y_ref[...] = (yacc[...] * _silu(z_ref[...].astype(F32))).astype(BF16)
    if emit_state:
        st_out[...] = st[...]


def _ssd_scan(p0, pdt, conv_w, conv_b, dtb, alog, dl, s0, *, nb, L, emit_state):
    rb0 = 0 if s0 is None else ROWS_P // L
    gw = SSD_HPG * SSD_P
    nc = L // SCAN_T
    x0 = SSD_DI // gw
    b0 = 2 * SSD_DI // SSD_N
    c0 = b0 + SSD_GROUPS
    wb0 = SSD_DI // SSD_N
    wc0 = wb0 + SSD_GROUPS
    in_specs = [
        pl.BlockSpec((L, gw), lambda g, b: (rb0 + b, g)),
        pl.BlockSpec((L, gw), lambda g, b: (rb0 + b, x0 + g)),
        pl.BlockSpec((L, SSD_N), lambda g, b: (rb0 + b, b0 + g)),
        pl.BlockSpec((L, SSD_N), lambda g, b: (rb0 + b, c0 + g)),
        pl.BlockSpec((L, LANES), lambda g, b: (rb0 + b, 0)),
        pl.BlockSpec((CONV_W, gw), lambda g, b: (0, g)),
        pl.BlockSpec((1, gw), lambda g, b: (0, g)),
        pl.BlockSpec((CONV_W, SSD_N), lambda g, b: (0, wb0 + g)),
        pl.BlockSpec((1, SSD_N), lambda g, b: (0, wb0 + g)),
        pl.BlockSpec((CONV_W, SSD_N), lambda g, b: (0, wc0 + g)),
        pl.BlockSpec((1, SSD_N), lambda g, b: (0, wc0 + g)),
        pl.BlockSpec((1, LANES), lambda g, b: (0, 0)),
        pl.BlockSpec((1, LANES), lambda g, b: (0, 0)),
        pl.BlockSpec((1, gw), lambda g, b: (0, g)),
    ]
    args = [p0, p0, p0, p0, pdt, conv_w, conv_b, conv_w, conv_b, conv_w, conv_b, dtb, alog, dl]
    state_spec = pl.BlockSpec((None, None, 2, SSD_HPG, SSD_P, SSD_N), lambda g, b: (b, 0, 0, g, 0, 0))
    if s0 is not None:
        in_specs.append(state_spec)
        args.append(s0)
    out_specs = [pl.BlockSpec((L, gw), lambda g, b: (b, g))]
    out_shape = [jax.ShapeDtypeStruct((nb * L, SSD_DI), BF16)]
    if emit_state:
        out_specs.append(state_spec)
        out_shape.append(jax.ShapeDtypeStruct((nb, 1, 2, SSD_HEADS, SSD_P, SSD_N), F32))
    scratch = [
        pltpu.VMEM((L + 16, gw), F32),
        pltpu.VMEM((L + 16, SSD_N), F32),
        pltpu.VMEM((L, gw), F32),
        pltpu.VMEM((L, SSD_N), F32),
        pltpu.VMEM((L, SSD_N), F32),
        pltpu.VMEM((nc, SCAN_T, SCAN_T), F32),
        pltpu.VMEM((2, L, LANES), F32),
        pltpu.VMEM((2, LANES, L), F32),
        pltpu.VMEM((L, gw), F32),
        pltpu.VMEM((2, SSD_HPG, SSD_P, SSD_N), F32),
    ]
    res = pl.pallas_call(
        functools.partial(_ssd_body, L=L, has_s0=s0 is not None, emit_state=emit_state),
        grid=(SSD_GROUPS, nb),
        in_specs=in_specs,
        out_specs=out_specs,
        out_shape=out_shape,
        scratch_shapes=scratch,
        compiler_params=_cparams(("parallel", "parallel")),
        name="ssd_scan_p" if s0 is None else "ssd_scan_s",
    )(*args)
    return res


TQ = 256
HP_P = 4
HP_S = 2


def _mla_body(*refs, L, sample):
    it = iter(refs)
    cq_ref, ckv_ref, krs_ref = next(it), next(it), next(it)
    gq_ref, gkv_ref = next(it), next(it)
    wq, wqs, wk, wv = (next(it) for _ in range(4))
    if sample:
        cckv_ref, ckr_ref, cq_t, sq_t, ck_t, sk_t = (next(it) for _ in range(6))
    x_ref, wo_ref, mod_ref = next(it), next(it), next(it)
    x_out = next(it)
    if not sample:
        ckv_out, kr_out = next(it), next(it)
    kk, vv, o_scr = next(it), next(it), next(it)
    qt = pl.program_id(1)
    scale = (MLA_NOPE + MLA_ROPE) ** -0.5
    hb = 4 * LANES

    @pl.when(qt == 0)
    def _():
        ckv = _rms(ckv_ref[...].astype(F32), gkv_ref[...])
        kr_own = krs_ref[:, 0:LANES]
        if sample:
            keys = jnp.concatenate([cckv_ref[...], ckv], axis=0)
            kr_own = kr_own * ck_t[...] + krs_ref[:, LANES:2 * LANES] * sk_t[...]
            kr_all = jnp.concatenate([ckr_ref[...], kr_own], axis=0)
        else:
            ckv_out[...] = ckv
            kr_out[...] = kr_own[:, MLA_NOPE:MLA_NOPE + MLA_ROPE]
            keys = ckv
            kr_all = kr_own
        kb = keys.astype(BF16)
        for blk in range(MLA_HEADS * LANES // hb):
            cols = slice(hb * blk, hb * (blk + 1))
            kn = _dot(kb, wk[:, cols])
            kk[:, cols] = (kn + jnp.concatenate([kr_all] * 4, axis=1)).astype(BF16)
            vv[:, cols] = _dot(kb, wv[:, cols]).astype(BF16)

    cq = _rms(cq_ref[...].astype(F32), gq_ref[...]).astype(BF16)
    for blk in range(MLA_HEADS * LANES // hb):
        qa = _dot(cq, wq[:, hb * blk:hb * (blk + 1)])
        if sample:
            qs = _dot(cq, wqs[:, hb * blk:hb * (blk + 1)])
        pair = None
        for hh in range(4):
            h = 4 * blk + hh
            cols = slice(LANES * h, LANES * (h + 1))
            qh = qa[:, LANES * hh:LANES * (hh + 1)]
            if sample:
                qh = qh * cq_t[...] + qs[:, LANES * hh:LANES * (hh + 1)] * sq_t[...]
            s = _dot_nt((qh * scale).astype(BF16), kk[:, cols])
            e = jnp.exp(s - jnp.max(s, axis=1, keepdims=True))
            o = _dot(e.astype(BF16), vv[:, cols]) / jnp.sum(e, axis=1, keepdims=True)
            if h % 2 == 0:
                pair = o
            else:
                o_scr[:, LANES * (h // 2):LANES * (h // 2 + 1)] = (pair + o).astype(BF16)
    x_out[...] = x_ref[...] + mod_ref[2:3, :] * _dot(o_scr[...], wo_ref[...].astype(BF16))


def _mla_attn(p1, p1s, gq, gkv, wq, wqs, wk, wv, ctx, x, wo, mod, *, nb, L):
    sample = ctx is not None
    nq = L // TQ
    tk = L + (PAST if sample else 0)
    rbq0 = ROWS_P // TQ if sample else 0
    rbs0 = ROWS_P // L if sample else 0
    ckv_blk = MLA_Q_RANK // MLA_KV_RANK

    def const(a):
        return pl.BlockSpec(a.shape, lambda b, q, nd=a.ndim: (0,) * nd)

    in_specs = [
        pl.BlockSpec((TQ, MLA_Q_RANK), lambda b, q: (rbq0 + b * nq + q, 0)),
        pl.BlockSpec((L, MLA_KV_RANK), lambda b, q: (rbs0 + b, ckv_blk)),
        pl.BlockSpec((L, 2 * LANES), lambda b, q: (rbs0 + b, 0)),
        const(gq), const(gkv), const(wq), const(wqs), const(wk), const(wv),
    ]
    args = [p1, p1, p1s, gq, gkv, wq, wqs, wk, wv]
    if sample:
        cckv, ckr, cpad, spad = ctx
        in_specs += [
            pl.BlockSpec((None, None, PAST, MLA_KV_RANK), lambda b, q: (b, 0, 0, 0)),
            pl.BlockSpec((None, None, PAST, LANES), lambda b, q: (b, 0, 0, 0)),
            pl.BlockSpec((TQ, LANES), lambda b, q: (q, 0)),
            pl.BlockSpec((TQ, LANES), lambda b, q: (q, 0)),
            const(cpad), const(spad),
        ]
        args += [cckv, ckr, cpad, spad, cpad, spad]
    mod_arr, mod_layer = mod
    x_spec = pl.BlockSpec((TQ, D), lambda b, q: (rbq0 + b * nq + q, 0))
    in_specs += [
        x_spec, const(wo),
        pl.BlockSpec((None, None, 6, D), lambda b, q: (mod_layer, (1 + b) if sample else 0, 0, 0)),
    ]
    x_index = len(args)
    args += [x, wo, mod_arr]
    out_specs = [x_spec]
    out_shape = [jax.ShapeDtypeStruct((ROWS, D), F32)]
    if not sample:
        out_specs += [
            pl.BlockSpec((None, None, L, MLA_KV_RANK), lambda b, q: (b, 0, 0, 0)),
            pl.BlockSpec((None, None, L, MLA_ROPE), lambda b, q: (b, 0, 0, 0)),
        ]
        out_shape += [
            jax.ShapeDtypeStruct((nb, 1, L, MLA_KV_RANK), F32),
            jax.ShapeDtypeStruct((nb, 1, L, MLA_ROPE), F32),
        ]
    scratch = [
        pltpu.VMEM((tk, MLA_HEADS * LANES), BF16),
        pltpu.VMEM((tk, MLA_HEADS * LANES), BF16),
        pltpu.VMEM((TQ, MLA_HEADS * MLA_V), BF16),
    ]
    return pl.pallas_call(
        functools.partial(_mla_body, L=L, sample=sample),
        grid=(nb, nq),
        in_specs=in_specs,
        out_specs=out_specs,
        out_shape=out_shape,
        scratch_shapes=scratch,
        input_output_aliases={x_index: 0},
        compiler_params=_cparams(("parallel", "arbitrary")),
        name="mla_attn_s" if sample else "mla_attn_p",
    )(*args)


def _log_sigmoid(x):
    return -_softplus(-x)


def _mlstm_body(*refs, L, hp, has_s0, emit_state):
    it = iter(refs)
    q_ref, k_ref, v_ref, g_ref, gb_ref, ng_ref = (next(it) for _ in range(6))
    if has_s0:
        c0_ref, n0_ref, m0_ref = next(it), next(it), next(it)
    h_ref = next(it)
    if emit_state:
        c_out, n_out, m_out = next(it), next(it), next(it)
    gsc, gtr, kq, vts, kts, hacc, cst, nst, mst = (next(it) for _ in range(9))

    T = SCAN_T
    nc = L // T
    h0 = hp * pl.program_id(1)

    gts = g_ref[...] + gb_ref[...]
    gr = pltpu.roll(gts, jnp.where(h0 == 0, 0, LANES - h0), 1)
    b_f, b_b = _seg_cumsums(_log_sigmoid(gr))
    gsc[0] = gr
    gsc[1] = b_f
    gsc[2] = b_b
    gtr[0] = gr.T
    gtr[1] = b_f.T
    gtr[2] = b_b.T

    jj = lax.broadcasted_iota(jnp.int32, (T, T), 0)
    ii = lax.broadcasted_iota(jnp.int32, (T, T), 1)

    def chunk(hh, c, d):
        rows = slice(T * c, T * (c + 1))
        qcols = slice(ML_DQK * hh, ML_DQK * (hh + 1))
        first = c == (0 if d == 0 else nc - 1)
        last = c == (nc - 1 if d == 0 else 0)
        zero_state = first and not has_s0
        li = 2 * ML_HEADS * d + hh
        lb = li + ML_HEADS
        b_row = gtr[1 + d, lb:lb + 1, rows]
        logi_row = gtr[0, li:li + 1, rows]
        cj = gsc[1 + d, rows, lb:lb + 1] - gsc[0, rows, li:li + 1]
        mask = (jj <= ii) if d == 0 else (jj >= ii)
        dlog = jnp.where(mask, b_row - cj, -jnp.inf)
        m_prev = jnp.zeros((1, 1), F32) if zero_state else mst[hh, d][:, 0:1]
        inter = b_row + m_prev
        mcomb = jnp.maximum(inter, jnp.max(dlog, axis=0, keepdims=True))
        s = kq[hh, c] * jnp.exp(dlog - mcomb)
        vt = vts[hh, c]
        num = _dot(vt, s.astype(BF16))
        den = jnp.sum(s, axis=0, keepdims=True)
        if not zero_state:
            iw = jnp.exp(inter - mcomb)
            qt = q_ref[rows, qcols].T
            num = num + iw * _dot_tn(cst[hh, d].astype(BF16), qt)
            n8 = jnp.broadcast_to(nst[hh, d], (8, ML_DQK)).astype(BF16)
            den = den + iw * _dot(n8, qt)[0:1]
        hc = num / jnp.maximum(jnp.abs(den), jnp.exp(-mcomb))
        if d == 0:
            hacc[hh, :, rows] = hc
        else:
            hacc[hh, :, rows] += hc
        if emit_state or not last:
            end = T * c + (T - 1 if d == 0 else 0)
            bq = gtr[1 + d, lb:lb + 1, end:end + 1]
            wlog = bq - b_row + logi_row
            m_new = jnp.maximum(bq + m_prev, jnp.max(wlog, axis=1, keepdims=True))
            sw = jnp.exp(wlog - m_new)
            upd = _dot_nt((kts[hh, c].astype(F32) * sw).astype(BF16), vt)
            nsum = _dot(jnp.broadcast_to(sw, (8, T)).astype(BF16), k_ref[rows, qcols])[0:1]
            if zero_state:
                cst[hh, d] = upd
                nst[hh, d] = nsum
            else:
                cw = jnp.exp(bq + m_prev - m_new)
                cst[hh, d] = cw * cst[hh, d] + upd
                nst[hh, d] = cw * nst[hh, d] + nsum
            mst[hh, d] = jnp.broadcast_to(m_new, (1, LANES))

    for hh in range(hp):
        qcols = slice(ML_DQK * hh, ML_DQK * (hh + 1))
        vcols = slice(ML_DV * hh, ML_DV * (hh + 1))
        if has_s0:
            for d in range(2):
                cst[hh, d] = c0_ref[d, hh]
                nst[hh, d] = n0_ref[d, pl.ds(h0 + hh, 1), :]
                mst[hh, d] = m0_ref[d, pl.ds(h0 + hh, 1), :]
        for c in range(nc):
            rows = slice(T * c, T * (c + 1))
            kc = k_ref[rows, qcols]
            kq[hh, c] = _dot_nt(kc, q_ref[rows, qcols])
            kts[hh, c] = kc.T
            vts[hh, c] = v_ref[rows, vcols].T
        for c in range(nc):
            chunk(hh, c, 0)
        for c in range(nc):
            chunk(hh, nc - 1 - c, 1)
        ht = hacc[hh]
        r = lax.rsqrt(jnp.mean(ht * ht, axis=0, keepdims=True) + EPS)
        h_ref[:, vcols] = ((ht * r).T * ng_ref[:, vcols]).astype(BF16)
        if emit_state:
            for d in range(2):
                c_out[d, hh] = cst[hh, d]
                n_out[d, pl.ds(h0 + hh, 1), :] = nst[hh, d]
                m_out[d, pl.ds(h0 + hh, 1), :] = mst[hh, d]


def _mlstm_scan(q, k, v, gates, gb, ng, s0, *, nb, L, hp, emit_state):
    rb0 = 0 if s0 is None else ROWS_P // L
    in_specs = [
        pl.BlockSpec((L, hp * ML_DQK), lambda b, h: (rb0 + b, h)),
        pl.BlockSpec((L, hp * ML_DQK), lambda b, h: (rb0 + b, h)),
        pl.BlockSpec((L, hp * ML_DV), lambda b, h: (rb0 + b, h)),
        pl.BlockSpec((L, LANES), lambda b, h: (rb0 + b, 0)),
        pl.BlockSpec((1, LANES), lambda b, h: (0, 0)),
        pl.BlockSpec((1, hp * ML_DV), lambda b, h: (0, h)),
    ]
    args = [q, k, v, gates, gb, ng]
    c_spec = pl.BlockSpec((None, None, 2, hp, ML_DQK, ML_DV), lambda b, h: (b, 0, 0, h, 0, 0))
    n_spec = pl.BlockSpec((None, 2, ML_HEADS, LANES), lambda b, h: (b, 0, 0, 0))
    if s0 is not None:
        in_specs += [c_spec, n_spec, n_spec]
        args += list(s0)
    out_specs = [pl.BlockSpec((L, hp * ML_DV), lambda b, h: (b, h))]
    out_shape = [jax.ShapeDtypeStruct((nb * L, ML_DI), BF16)]
    if emit_state:
        out_specs += [c_spec, n_spec, n_spec]
        out_shape += [
            jax.ShapeDtypeStruct((nb, 1, 2, ML_HEADS, ML_DQK, ML_DV), F32),
            jax.ShapeDtypeStruct((nb, 2, ML_HEADS, LANES), F32),
            jax.ShapeDtypeStruct((nb, 2, ML_HEADS, LANES), F32),
        ]
    scratch = [
        pltpu.VMEM((3, L, LANES), F32),
        pltpu.VMEM((3, LANES, L), F32),
        pltpu.VMEM((hp, L // SCAN_T, SCAN_T, SCAN_T), F32),
        pltpu.VMEM((hp, L // SCAN_T, ML_DV, SCAN_T), BF16),
        pltpu.VMEM((hp, L // SCAN_T, ML_DQK, SCAN_T), BF16),
        pltpu.VMEM((hp, ML_DV, L), F32),
        pltpu.VMEM((hp, 2, ML_DQK, ML_DV), F32),
        pltpu.VMEM((hp, 2, 1, ML_DQK), F32),
        pltpu.VMEM((hp, 2, 1, LANES), F32),
    ]
    return pl.pallas_call(
        functools.partial(_mlstm_body, L=L, hp=hp, has_s0=s0 is not None, emit_state=emit_state),
        grid=(nb, ML_HEADS // hp),
        in_specs=in_specs,
        out_specs=out_specs,
        out_shape=out_shape,
        scratch_shapes=scratch,
        compiler_params=_cparams(("parallel", "arbitrary")),
        name="mlstm_scan_p" if s0 is None else "mlstm_scan_s",
    )(*args)


def _rope_blocks(x, ct, st):
    lane = lax.broadcasted_iota(jnp.int32, (x.shape[0], LANES), 1)
    first_half = lane % DF_D < DF_D // 2
    outs = []
    for blk in range(x.shape[1] // LANES):
        xb = x[:, LANES * blk:LANES * (blk + 1)]
        swapped = jnp.where(first_half, pltpu.roll(xb, LANES - DF_D // 2, 1), pltpu.roll(xb, DF_D // 2, 1))
        outs.append(xb * ct + swapped * st)
    return jnp.concatenate(outs, axis=1)


def _diff_body(*refs, L, sample, lam_init):
    it = iter(refs)
    q_ref, k_ref, v_ref = next(it), next(it), next(it)
    lq1, lk1, lq2, lk2, sg_ref = (next(it) for _ in range(5))
    if sample:
        ck_ref, cv_ref, cq_t, sq_t, ck_t, sk_t = (next(it) for _ in range(6))
    x_ref, wo_ref, mod_ref = next(it), next(it), next(it)
    x_out = next(it)
    if sample:
        ka, va = next(it), next(it)
    else:
        k_out, v_out = next(it), next(it)
    o_scr = next(it)
    qt = pl.program_id(1)

    lam = (jnp.exp(jnp.sum(lq1[...] * lk1[...], axis=1, keepdims=True))
           - jnp.exp(jnp.sum(lq2[...] * lk2[...], axis=1, keepdims=True)) + lam_init)

    @pl.when(qt == 0)
    def _():
        if sample:
            ka[0:PAST, :] = ck_ref[...].astype(BF16)
            ka[PAST:PAST + L, :] = _rope_blocks(k_ref[...].astype(F32), ck_t[...], sk_t[...]).astype(BF16)
            va[0:PAST, :] = cv_ref[...].astype(BF16)
            va[PAST:PAST + L, :] = v_ref[...]
        else:
            k_out[...] = k_ref[...].astype(F32)
            v_out[...] = v_ref[...].astype(F32)

    q = q_ref[...].astype(F32)
    if sample:
        q = _rope_blocks(q, cq_t[...], sq_t[...])
    q = q * (DF_D ** -0.5)
    lo = lax.broadcasted_iota(jnp.int32, (TQ, LANES), 1) < DF_D

    def attend(qm, kh, vh):
        s = _dot_nt(qm, kh)
        e = jnp.exp(s - jnp.max(s, axis=1, keepdims=True))
        return _dot(e.astype(BF16), vh) / jnp.sum(e, axis=1, keepdims=True)

    for h in range(DF_HEADS):
        cols = slice(LANES * h, LANES * (h + 1))
        qh = q[:, cols]
        if sample:
            kh, vh = ka[:, cols], va[:, cols]
        else:
            kh, vh = k_ref[:, cols], v_ref[:, cols]
        a0 = attend(jnp.where(lo, qh, 0.0).astype(BF16), kh, vh)
        a1 = attend(jnp.where(lo, 0.0, qh).astype(BF16), kh, vh)
        o = a0 - lam * a1
        o_scr[:, cols] = (_rms(o, sg_ref[...]) * (1.0 - lam_init)).astype(BF16)
    x_out[...] = x_ref[...] + mod_ref[2:3, :] * _dot(o_scr[...], wo_ref[...].astype(BF16))


def _diff_attn(p3, lq1, lk1, lq2, lk2, sg, ctx, x, wo, mod, *, nb, L, lam_init):
    sample = ctx is not None
    nq = L // TQ
    rbq0 = ROWS_P // TQ if sample else 0
    rbs0 = ROWS_P // L if sample else 0

    def const(a):
        return pl.BlockSpec(a.shape, lambda b, q, nd=a.ndim: (0,) * nd)

    in_specs = [
        pl.BlockSpec((TQ, D), lambda b, q: (rbq0 + b * nq + q, 0)),
        pl.BlockSpec((L, D), lambda b, q: (rbs0 + b, 1)),
        pl.BlockSpec((L, D), lambda b, q: (rbs0 + b, 2)),
        const(lq1), const(lk1), const(lq2), const(lk2), const(sg),
    ]
    args = [p3, p3, p3, lq1, lk1, lq2, lk2, sg]
    kv_spec = pl.BlockSpec((None, None, PAST if sample else L, D), lambda b, q: (b, 0, 0, 0))
    if sample:
        ck, cv, c128, s128 = ctx
        in_specs += [
            kv_spec, kv_spec,
            pl.BlockSpec((TQ, LANES), lambda b, q: (q, 0)),
            pl.BlockSpec((TQ, LANES), lambda b, q: (q, 0)),
            const(c128), const(s128),
        ]
        args += [ck, cv, c128, s128, c128, s128]
    mod_arr, mod_layer = mod
    x_spec = pl.BlockSpec((TQ, D), lambda b, q: (rbq0 + b * nq + q, 0))
    in_specs += [
        x_spec, const(wo),
        pl.BlockSpec((None, None, 6, D), lambda b, q: (mod_layer, (1 + b) if sample else 0, 0, 0)),
    ]
    x_index = len(args)
    args += [x, wo, mod_arr]
    out_specs = [x_spec]
    out_shape = [jax.ShapeDtypeStruct((ROWS, D), F32)]
    scratch = []
    if sample:
        scratch = [pltpu.VMEM((PAST + L, D), BF16), pltpu.VMEM((PAST + L, D), BF16)]
    else:
        out_specs += [kv_spec, kv_spec]
        out_shape += [jax.ShapeDtypeStruct((nb, 1, L, D), F32)] * 2
    scratch = scratch + [pltpu.VMEM((TQ, D), BF16)]
    return pl.pallas_call(
        functools.partial(_diff_body, L=L, sample=sample, lam_init=lam_init),
        grid=(nb, nq),
        in_specs=in_specs,
        out_specs=out_specs,
        out_shape=out_shape,
        scratch_shapes=scratch,
        input_output_aliases={x_index: 0},
        compiler_params=_cparams(("parallel", "arbitrary")),
        name="diff_attn_s" if sample else "diff_attn_p",
    )(*args)


CONV_COLS = 256


def _mlstm_q_body(x_ref, cw_ref, cb_ref, w_ref, q_ref, xc_ref, msk):
    tm = x_ref.shape[0]
    seq = jnp.where(pl.program_id(0) < ROWS_P // tm, L_PROMPT, L_SAMPLE)
    pos = lax.broadcasted_iota(jnp.int32, (tm, CONV_COLS), 0) & (seq - 1)
    offs = [k - CONV_W // 2 for k in range(CONV_W)]
    for k, off in enumerate(offs):
        if off != 0:
            dst = pos - off
            msk[k] = jnp.where(jnp.logical_and(dst >= 0, dst < seq), 1.0, 0.0)
    acc = None
    kslab = 2 * CONV_COLS
    for kb in range(ML_DI // kslab):
        for cb in range(kslab // CONV_COLS):
            cols = slice(kslab * kb + CONV_COLS * cb, kslab * kb + CONV_COLS * (cb + 1))
            x = x_ref[:, cols].astype(F32)
            y = None
            for k, off in enumerate(offs):
                tap = x if off == 0 else pltpu.roll(x * msk[k], (-off) % tm, 0)
                tap = tap * cw_ref[k:k + 1, cols]
                y = cb_ref[:, cols] + tap if y is None else y + tap
            xc_ref[:, cols] = _silu(y).astype(BF16)
        kcols = slice(kslab * kb, kslab * (kb + 1))
        part = _dot(xc_ref[:, kcols], w_ref[kcols, :].astype(BF16))
        acc = part if acc is None else part + acc
    q_ref[...] = acc.astype(BF16)


def _mlstm_q(xm_src, conv_w, conv_b, w_q):
    tm = TM_CONVQ
    nq = ML_HEADS * ML_DQK
    return pl.pallas_call(
        _mlstm_q_body,
        grid=(ROWS // tm,),
        in_specs=[
            pl.BlockSpec((tm, ML_DI), lambda i: (i, 0)),
            pl.BlockSpec((CONV_W, ML_DI), lambda i: (0, 0)),
            pl.BlockSpec((1, ML_DI), lambda i: (0, 0)),
            pl.BlockSpec((ML_DI, nq), lambda i: (0, 0)),
        ],
        out_specs=[pl.BlockSpec((tm, nq), lambda i: (i, 0)), pl.BlockSpec((tm, ML_DI), lambda i: (i, 0))],
        out_shape=[jax.ShapeDtypeStruct((ROWS, nq), BF16), jax.ShapeDtypeStruct((ROWS, ML_DI), BF16)],
        scratch_shapes=[pltpu.VMEM((CONV_W, tm, CONV_COLS), F32)],
        compiler_params=_cparams(("parallel",)),
        name="mlstm_q",
    )(xm_src, conv_w, conv_b, w_q)


def _pro_mlstm_gate(p, rs):
    hn, xc, z = (r[rs, :].astype(F32) for r in p.rows)
    p.emit((hn + p.consts[0][...] * xc) * _silu(z), rs)


def _rope_tables(d):
    rows = L_SAMPLE // GRID_W
    pos_r = jnp.repeat(jnp.arange(rows, dtype=F32), GRID_W)
    pos_c = jnp.tile(jnp.arange(GRID_W, dtype=F32), rows)
    nf = d // 4
    inv = ROPE_BASE ** (-jnp.arange(nf, dtype=F32) / nf)
    ang = jnp.concatenate([pos_r[:, None] * inv, pos_c[:, None] * inv], axis=-1)
    cos, sin = jnp.cos(ang), jnp.sin(ang)
    return jnp.concatenate([cos, cos], axis=-1), jnp.concatenate([-sin, sin], axis=-1)


def _pad_cols(a, n):
    return jnp.pad(a, ((0, 0), (0, n - a.shape[1])))


def kernel(x_prompt, x_sample, state_ssd, cache_mla_ckv, cache_mla_krope, state_mlstm_C, state_mlstm_n, state_mlstm_m, cache_diff_k, cache_diff_v, c, c_ctx, norm1_g, norm2_g, ada_w, ada_b, mlp_w1, mlp_w2, final_g, ssd_w_in, ssd_conv_w, ssd_conv_b, ssd_dt_bias, ssd_A_log, ssd_D, ssd_norm_g, ssd_w_out, mla_w_in, mla_q_norm_g, mla_kv_norm_g, mla_w_uq, mla_w_ukv, mla_w_o, mlstm_w_up, mlstm_conv_w, mlstm_conv_b, mlstm_gate_b, mlstm_w_q, mlstm_w_k, mlstm_w_v, mlstm_skip, mlstm_norm_g, mlstm_w_down, diff_w_qkv, diff_lq1, diff_lk1, diff_lq2, diff_lk2, diff_subln_g, diff_w_o):
    xp2, xs2 = x_prompt.reshape(ROWS_P, D), x_sample.reshape(ROWS_S, D)
    cvec = jnp.concatenate([c_ctx[None, :], c, jnp.zeros((5, D), F32)], axis=0)
    mod_all = _ada_mod(cvec, ada_w, ada_b)
    g2 = norm2_g.reshape(DEPTH, 1, D)

    def in_proj(xin, layer, w, n_cols, tn, extra_w=None, name="in_proj"):
        xrow = ("d", xin[0], xin[1], D, 0) if isinstance(xin, tuple) else ("u", xin, D, 0)
        return _fused_mm(rows=[xrow], consts=[norm1_g[layer][None, :]], mod=(mod_all, layer), w=w, k_dim=D,
                         n_cols=n_cols, tm=TM_IN, tn=tn, prologue=_pro_normmod, extra_w=extra_w, name=name)

    def out_proj(xin, layer, rows, consts, prologue, w, k_dim, name):
        return _fused_mm(rows=rows, consts=consts, w=w, k_dim=k_dim, n_cols=D, tm=TM_OUT, tn=TN_PROJ, prologue=prologue,
                         epilogue=_epi_residual(2), erows=[xin], emod=(mod_all, layer), out_dtype=F32, name=name)

    p0, pdt = in_proj((xp2, xs2), 0, ssd_w_in[0], 3 * SSD_DI, TN_SSD_IN, extra_w=_pad_cols(ssd_w_in[0][:, 3 * SSD_DI:], LANES),
                      name="ssd_in")
    dtb = _pad_cols(ssd_dt_bias[0].reshape(1, 2 * SSD_HEADS), LANES)
    alog = _pad_cols(ssd_A_log[0].reshape(1, 2 * SSD_HEADS), LANES)
    dl = jnp.repeat(ssd_D[0], SSD_P)[None, :]
    scan_args = (p0, pdt, ssd_conv_w[0], ssd_conv_b[0][None, :], dtb, alog, dl)
    yg_p, new_ssd = _ssd_scan(*scan_args, None, nb=N_PROMPT_SEQ, L=L_PROMPT, emit_state=True)
    (yg_s,) = _ssd_scan(*scan_args, state_ssd, nb=N_SAMPLE_SEQ, L=L_SAMPLE, emit_state=False)
    x = out_proj((xp2, xs2), 0, [("d", yg_p, yg_s, SSD_DI, 0)], [ssd_norm_g[0][None, :]], _pro_rms, ssd_w_out[0], SSD_DI,
                 "ssd_out")
    x = _mlp(x, 0, g2, mod_all, mlp_w1, mlp_w2)

    w_in = mla_w_in[0]
    kr0 = MLA_Q_RANK + MLA_KV_RANK
    half = MLA_ROPE // 2
    zk = jnp.zeros((D, MLA_NOPE), F32)
    zr = jnp.zeros((D, LANES - MLA_NOPE - MLA_ROPE), F32)
    w_kr = jnp.concatenate([zk, w_in[:, kr0:kr0 + MLA_ROPE], zr,
                            zk, w_in[:, kr0 + half:kr0 + MLA_ROPE], w_in[:, kr0:kr0 + half], zr], axis=1)
    p1, p1s = in_proj(x, 1, w_in, kr0, kr0, extra_w=w_kr, name="mla_in")
    wuq = mla_w_uq[0].reshape(MLA_Q_RANK, MLA_HEADS, MLA_NOPE + MLA_ROPE)
    zq = jnp.zeros((MLA_Q_RANK, MLA_HEADS, LANES - MLA_NOPE - MLA_ROPE), F32)
    zqn = jnp.zeros((MLA_Q_RANK, MLA_HEADS, MLA_NOPE), F32)
    wq = jnp.concatenate([wuq, zq], axis=-1).reshape(MLA_Q_RANK, MLA_HEADS * LANES).astype(BF16)
    wqs = jnp.concatenate([zqn, wuq[..., MLA_NOPE + half:], wuq[..., MLA_NOPE:MLA_NOPE + half], zq],
                          axis=-1).reshape(MLA_Q_RANK, MLA_HEADS * LANES).astype(BF16)
    wukv = mla_w_ukv[0].reshape(MLA_KV_RANK, MLA_HEADS, MLA_NOPE + MLA_V)
    zkv = jnp.zeros((MLA_KV_RANK, MLA_HEADS, MLA_NOPE), F32)
    wk = jnp.concatenate([wukv[..., :MLA_NOPE], zkv], axis=-1).reshape(MLA_KV_RANK, MLA_HEADS * LANES).astype(BF16)
    wv_own = wukv[..., MLA_NOPE:]
    odd = (jnp.arange(MLA_HEADS) % 2 == 1)[None, :, None]
    wv = jnp.where(odd, jnp.concatenate([zkv, wv_own], axis=-1), jnp.concatenate([wv_own, zkv], axis=-1))
    wv = wv.reshape(MLA_KV_RANK, MLA_HEADS * LANES).astype(BF16)
    c32, s32 = _rope_tables(MLA_ROPE)
    tz = jnp.zeros((L_SAMPLE, LANES - MLA_NOPE - MLA_ROPE), F32)
    cpad = jnp.concatenate([jnp.ones((L_SAMPLE, MLA_NOPE), F32), c32, tz], axis=1)
    spad = jnp.concatenate([jnp.zeros((L_SAMPLE, MLA_NOPE), F32), s32, tz], axis=1)
    ckr_pad = jnp.pad(cache_mla_krope, ((0, 0), (0, 0), (0, 0), (MLA_NOPE, LANES - MLA_NOPE - MLA_ROPE)))
    mla_w = (mla_q_norm_g[0][None, :], mla_kv_norm_g[0][None, :], wq, wqs, wk, wv)
    x, new_ckv, new_kr = _mla_attn(p1, p1s, *mla_w, None, x, mla_w_o[0], (mod_all, 1), nb=N_PROMPT_SEQ, L=L_PROMPT)
    (x,) = _mla_attn(p1, p1s, *mla_w, (cache_mla_ckv, ckr_pad, cpad, spad), x, mla_w_o[0], (mod_all, 1),
                     nb=N_SAMPLE_SEQ, L=L_SAMPLE)
    x = _mlp(x, 1, g2, mod_all, mlp_w1, mlp_w2)

    p2, gates = in_proj(x, 2, mlstm_w_up[0], 2 * ML_DI, TN_IN, extra_w=_pad_cols(mlstm_w_up[0][:, 2 * ML_DI:], LANES),
                        name="mlstm_up")
    q, xc = _mlstm_q(p2, mlstm_conv_w[0], mlstm_conv_b[0][None, :], mlstm_w_q[0])
    k = _fused_mm(rows=[("u", xc, ML_DI, 0)], w=mlstm_w_k[0], k_dim=ML_DI, n_cols=ML_HEADS * ML_DQK, tm=TM_WIDE,
                  tn=TN_PROJ, epilogue=lambda acc, e, m, rs: acc * (ML_DQK ** -0.5), name="mlstm_k")
    v = _fused_mm(rows=[("u", p2, ML_DI, 0)], w=mlstm_w_v[0], k_dim=ML_DI, n_cols=ML_DI, tm=TM_WIDE, tn=TN_PROJ,
                  name="mlstm_v")
    gb = _pad_cols(mlstm_gate_b[0].reshape(1, 4 * ML_HEADS), LANES)
    ng = mlstm_norm_g[0][None, :]
    n0 = _pad_cols(state_mlstm_n[:, 0].reshape(N_SAMPLE_SEQ * 2 * ML_HEADS, ML_DQK), LANES).reshape(
        N_SAMPLE_SEQ, 2, ML_HEADS, LANES)
    m0 = jnp.broadcast_to(state_mlstm_m[:, 0][..., None], (N_SAMPLE_SEQ, 2, ML_HEADS, LANES))
    hn_p, new_c, new_n, new_m = _mlstm_scan(q, k, v, gates, gb, ng, None, nb=N_PROMPT_SEQ, L=L_PROMPT, hp=HP_P,
                                            emit_state=True)
    (hn_s,) = _mlstm_scan(q, k, v, gates, gb, ng, (state_mlstm_C, n0, m0), nb=N_SAMPLE_SEQ, L=L_SAMPLE, hp=HP_S,
                          emit_state=False)
    x = out_proj(x, 2, [("d", hn_p, hn_s, ML_DI, 0), ("u", xc, ML_DI, 0), ("u", p2, ML_DI, 1)],
                 [mlstm_skip[0][None, :]], _pro_mlstm_gate, mlstm_w_down[0], ML_DI, "mlstm_down")
    x = _mlp(x, 2, g2, mod_all, mlp_w1, mlp_w2)

    lam_init = 0.8 - 0.6 * math.exp(-0.3 * 3)
    p3 = in_proj(x, 3, diff_w_qkv[0], 3 * D, TN_IN, name="diff_qkv")
    c64, s64 = _rope_tables(DF_D)
    c128 = jnp.concatenate([c64, c64], axis=-1)
    s128 = jnp.concatenate([s64, s64], axis=-1)
    dparams = (diff_lq1, diff_lk1, diff_lq2, diff_lk2, diff_subln_g)
    x, new_dk, new_dv = _diff_attn(p3, *dparams, None, x, diff_w_o[0], (mod_all, 3), nb=N_PROMPT_SEQ, L=L_PROMPT,
                                   lam_init=lam_init)
    ctx = (cache_diff_k.reshape(N_SAMPLE_SEQ, 1, PAST, D), cache_diff_v.reshape(N_SAMPLE_SEQ, 1, PAST, D), c128, s128)
    (x,) = _diff_attn(p3, *dparams, ctx, x, diff_w_o[0], (mod_all, 3), nb=N_SAMPLE_SEQ, L=L_SAMPLE, lam_init=lam_init)
    y_prompt, y_sample = _mlp(x, 3, g2, mod_all, mlp_w1, mlp_w2, final_g=final_g[None, :])
    y_prompt = y_prompt.reshape(N_PROMPT_SEQ, L_PROMPT, D)
    y_sample = y_sample.reshape(N_SAMPLE_SEQ, L_SAMPLE, D)
    return (y_prompt, y_sample, new_ssd, new_ckv, new_kr, new_c,
            new_n[None].reshape(N_PROMPT_SEQ, 1, 2, ML_HEADS, ML_DQK),
            new_m[..., 0].reshape(N_PROMPT_SEQ, 1, 2, ML_HEADS),
            new_dk.reshape(N_PROMPT_SEQ, 1, L_PROMPT, DF_HEADS, 2 * DF_D),
            new_dv.reshape(N_PROMPT_SEQ, 1, L_PROMPT, DF_HEADS, 2 * DF_D))
```

```python
import functools
import math

import jax
import jax.numpy as jnp
from jax import lax
from jax.experimental import pallas as pl
from jax.experimental.pallas import tpu as pltpu

F32 = jnp.float32
BF16 = jnp.bfloat16

D = 1024
DEPTH = 4
D_FF = 4 * D
EPS = 1e-6
ROPE_BASE = 10000.0
CONV_W = 5
GRID_W = 64

N_PROMPT_SEQ = 32
L_PROMPT = 256
N_SAMPLE_SEQ = 2
L_SAMPLE = 1024
PAST = 256
ROWS_P = N_PROMPT_SEQ * L_PROMPT
ROWS_S = N_SAMPLE_SEQ * L_SAMPLE
ROWS = ROWS_P + ROWS_S

SSD_DI = 2 * D
SSD_HEADS = 32
SSD_P = 64
SSD_GROUPS = 8
SSD_N = 128
SSD_HPG = SSD_HEADS // SSD_GROUPS

MLA_HEADS = 16
MLA_Q_RANK = 512
MLA_KV_RANK = 256
MLA_NOPE = 64
MLA_ROPE = 32
MLA_V = 64

ML_DI = 2 * D
ML_HEADS = 8
ML_DQK = 128
ML_DV = 256

DF_HEADS = 8
DF_D = 64

LANES = 128
VMEM_LIMIT = 56 * 1024 * 1024

TM_IN = 2048
TN_IN = 1024
TN_SSD_IN = 512
TM_OUT = 1024
TM_WIDE = 2048
TN_PROJ = 512
TM_MLP = 1024
TF_MLP = 1024
TM_CONVQ = 1024
TN_ADA = 1536


def _cparams(sem):
    return pltpu.CompilerParams(dimension_semantics=sem, vmem_limit_bytes=VMEM_LIMIT)


def _silu(x):
    return x * jax.nn.sigmoid(x)


def _softplus(x):
    return jnp.maximum(x, 0.0) + jnp.log1p(jnp.exp(-jnp.abs(x)))


def _rms(x, g):
    r = lax.rsqrt(jnp.mean(x * x, axis=-1, keepdims=True) + EPS)
    return (x * r) * g


def _dot(a, b):
    return jnp.dot(a, b, preferred_element_type=F32)


def _dot_nt(a, b):
    return lax.dot_general(a, b, (((1,), (1,)), ((), ())), preferred_element_type=F32)


def _dot_tn(a, b):
    return lax.dot_general(a, b, (((0,), (0,)), ((), ())), preferred_element_type=F32)


MOD_ROWS = 1024


def _group_of_tile(i, tm, sub=0):
    row0 = i * tm + sub * MOD_ROWS
    return jnp.where(row0 < ROWS_P, 0, 1 + (row0 - ROWS_P) // L_SAMPLE)


def _ada_body(c_ref, w_ref, b_ref, o_ref):
    s = _silu(c_ref[...]).astype(BF16)
    o_ref[...] = _dot(s, w_ref[...].astype(BF16)) + b_ref[...]


def _ada_mod(cvec, ada_w, ada_b):
    tn = TN_ADA
    out = pl.pallas_call(
        _ada_body,
        grid=(DEPTH, 6 * D // tn),
        in_specs=[
            pl.BlockSpec((8, D), lambda l, j: (0, 0)),
            pl.BlockSpec((None, D, tn), lambda l, j: (l, 0, j)),
            pl.BlockSpec((None, 1, tn), lambda l, j: (l, 0, j)),
        ],
        out_specs=pl.BlockSpec((None, 8, tn), lambda l, j: (l, 0, j)),
        out_shape=jax.ShapeDtypeStruct((DEPTH, 8, 6 * D), F32),
        compiler_params=_cparams(("parallel", "parallel")),
        name="ada_mod",
    )(cvec, ada_w, ada_b.reshape(DEPTH, 1, 6 * D))
    return out[:, :3].reshape(DEPTH, 3, 6, D)


class _Pro:
    def __init__(self, rows, consts, mod, i, hs, lhs_out):
        self.rows, self.consts, self.mod, self.i = rows, consts, mod, i
        self._hs, self._lhs_out = hs, lhs_out

    def emit(self, val, rows=slice(None), cols=slice(None)):
        vb = val.astype(BF16)
        self._hs[rows, cols] = vb
        if self._lhs_out is not None:
            self._lhs_out[rows, cols] = vb


ROW_CHUNK = 512


def _fused_mm(*, rows, w, k_dim, n_cols, tm, tn, prologue=None, consts=(), mod=None,
              epilogue=None, erows=(), emod=None, w_col0=0, emit_lhs=False, extra_w=None, out_dtype=BF16, name):
    npt = ROWS_P // tm
    grid = (ROWS // tm, n_cols // tn)
    has_dual = any(r[0] == "d" for r in rows)
    dual_epi = any(isinstance(e, tuple) for e in erows)
    if prologue is None:
        assert len(rows) == 1 and rows[0][0] == "u" and not emit_lhs

    in_specs, args = [], []
    for r in rows:
        if r[0] == "u":
            _, arr, width, cb = r
            in_specs.append(pl.BlockSpec((tm, width), lambda i, j, cb=cb: (i, cb)))
            args.append(arr)
        else:
            _, arr_p, arr_s, width, cb = r
            in_specs.append(pl.BlockSpec((tm, width), lambda i, j, cb=cb: (jnp.minimum(i, npt - 1), cb)))
            in_specs.append(pl.BlockSpec((tm, width), lambda i, j, cb=cb: (jnp.maximum(i - npt, 0), cb)))
            args += [arr_p, arr_s]
    for c in consts:
        in_specs.append(pl.BlockSpec(c.shape, lambda i, j, nd=c.ndim: (0,) * nd))
        args.append(c)
    n_sub = max(tm // MOD_ROWS, 1)
    if mod is not None:
        mod_arr, mod_layer = mod
        for s in range(n_sub):
            in_specs.append(pl.BlockSpec((None, None, 6, D),
                                         lambda i, j, s=s: (mod_layer, _group_of_tile(i, tm, s), 0, 0)))
            args.append(mod_arr)
    in_specs.append(pl.BlockSpec((k_dim, tn), lambda i, j: (0, w_col0 // tn + j)))
    args.append(w)
    if extra_w is not None:
        in_specs.append(pl.BlockSpec(extra_w.shape, lambda i, j: (0, 0)))
        args.append(extra_w)
    for e in erows:
        if isinstance(e, tuple):
            in_specs.append(pl.BlockSpec((tm, tn), lambda i, j: (jnp.minimum(i, npt - 1), j)))
            in_specs.append(pl.BlockSpec((tm, tn), lambda i, j: (jnp.maximum(i - npt, 0), j)))
            args += list(e)
        else:
            in_specs.append(pl.BlockSpec((tm, tn), lambda i, j: (i, j)))
            args.append(e)
    if emod is not None:
        emod_arr, emod_layer = emod
        in_specs.append(pl.BlockSpec((None, None, 6, tn), lambda i, j: (emod_layer, _group_of_tile(i, tm), 0, j)))
        args.append(emod_arr)

    out_specs = [pl.BlockSpec((tm, tn), lambda i, j: (i, j))]
    out_shape = [jax.ShapeDtypeStruct((ROWS, n_cols), out_dtype)]
    if emit_lhs:
        out_specs.append(pl.BlockSpec((tm, k_dim), lambda i, j: (i, 0)))
        out_shape.append(jax.ShapeDtypeStruct((ROWS, k_dim), BF16))
    if extra_w is not None:
        out_specs.append(pl.BlockSpec((tm, extra_w.shape[1]), lambda i, j: (i, 0)))
        out_shape.append(jax.ShapeDtypeStruct((ROWS, extra_w.shape[1]), F32))
    scratch = [] if prologue is None else [pltpu.VMEM((tm, k_dim), BF16)]
    chunks = [slice(r, r + ROW_CHUNK) for r in range(0, tm, ROW_CHUNK)]

    def body(*refs):
        it = iter(refs)
        row_refs = [(next(it),) if r[0] == "u" else (next(it), next(it)) for r in rows]
        const_refs = [next(it) for _ in consts]
        mod_ref = [next(it) for _ in range(n_sub)] if mod is not None else None
        w_ref = next(it)
        extra_ref = next(it) if extra_w is not None else None
        erow_refs = [(next(it), next(it)) if isinstance(e, tuple) else (next(it),) for e in erows]
        emod_ref = next(it) if emod is not None else None
        out_ref = next(it)
        lhs_out = next(it) if emit_lhs else None
        extra_out = next(it) if extra_w is not None else None
        hs = next(it) if prologue is not None else None
        i = pl.program_id(0)
        j = pl.program_id(1)

        def compute(first, use_prompt=True):
            chosen = [rr[0] if (len(rr) == 1 or use_prompt) else rr[1] for rr in row_refs]
            echosen = [er[0] if (len(er) == 1 or use_prompt) else er[1] for er in erow_refs]
            p = _Pro(chosen, const_refs, mod_ref, i, hs, lhs_out)
            wb = w_ref[...].astype(BF16)
            for rs in chunks:
                if first:
                    prologue(p, rs)
                lhs = chosen[0][rs, :] if prologue is None else hs[rs, :]
                acc = _dot(lhs, wb)
                if epilogue is not None:
                    acc = epilogue(acc, echosen, emod_ref, rs)
                out_ref[rs, :] = acc.astype(out_dtype)
                if first and extra_w is not None:
                    extra_out[rs, :] = _dot(lhs, extra_ref[...].astype(BF16))

        if prologue is None:
            if extra_w is None:
                compute(False)
            else:
                pl.when(j == 0)(lambda: compute(True))
                pl.when(j > 0)(lambda: compute(False))
        else:
            if has_dual or dual_epi:
                pl.when(jnp.logical_and(j == 0, i < npt))(lambda: compute(True, True))
                pl.when(jnp.logical_and(j == 0, i >= npt))(lambda: compute(True, False))
            else:
                pl.when(j == 0)(lambda: compute(True))
            if dual_epi:
                pl.when(jnp.logical_and(j > 0, i < npt))(lambda: compute(False, True))
                pl.when(jnp.logical_and(j > 0, i >= npt))(lambda: compute(False, False))
            else:
                pl.when(j > 0)(lambda: compute(False))

    res = pl.pallas_call(
        body,
        grid=grid,
        in_specs=in_specs,
        out_specs=out_specs,
        out_shape=out_shape,
        scratch_shapes=scratch,
        compiler_params=_cparams(("parallel", "arbitrary")),
        name=name,
    )(*args)
    return res if len(res) > 1 else res[0]


def _pro_normmod(p, rs):
    m = p.mod[rs.start // MOD_ROWS]
    h = _rms(p.rows[0][rs, :], p.consts[0][...]) * (1.0 + m[1:2, :]) + m[0:1, :]
    p.emit(h, rs)


def _pro_rms(p, rs):
    p.emit(_rms(p.rows[0][rs, :].astype(F32), p.consts[0][...]), rs)


def _epi_residual(gate_row):
    def epi(acc, erows, emod, rs):
        return erows[0][rs, :] + emod[gate_row:gate_row + 1, :] * acc
    return epi


def _mlp_body(*refs, final):
    it = iter(refs)
    x_ref, g_ref, mod_ref, w1_ref, w2_ref = (next(it) for _ in range(5))
    fg_ref = next(it) if final else None
    outs = [next(it), next(it)] if final else [next(it)]
    hs, acc = next(it), next(it)
    i = pl.program_id(0)
    f = pl.program_id(1)
    nf = pl.num_programs(1)
    tm = x_ref.shape[0]
    chunks = [slice(r, r + ROW_CHUNK) for r in range(0, tm, ROW_CHUNK)]

    def step(first, last):
        w1b = w1_ref[...].astype(BF16)
        w2b = w2_ref[...].astype(BF16)
        for rs in chunks:
            if first:
                h = _rms(x_ref[rs, :], g_ref[...]) * (1.0 + mod_ref[4:5, :]) + mod_ref[3:4, :]
                hs[rs, :] = h.astype(BF16)
            u = jnp.square(jnp.maximum(_dot(hs[rs, :], w1b), 0.0))
            tot = _dot(u.astype(BF16), w2b)
            if not first:
                tot = acc[rs, :] + tot
            if not last:
                acc[rs, :] = tot
                continue
            y = x_ref[rs, :] + mod_ref[5:6, :] * tot
            if not final:
                outs[0][rs, :] = y
            else:
                y = _rms(y, fg_ref[...])
                npt = ROWS_P // tm

                @pl.when(i < npt)
                def _():
                    outs[0][rs, :] = y

                @pl.when(i >= npt)
                def _():
                    outs[1][rs, :] = y

    pl.when(f == 0)(lambda: step(True, False))
    pl.when(jnp.logical_and(f > 0, f < nf - 1))(lambda: step(False, False))
    pl.when(f == nf - 1)(lambda: step(False, True))


def _mlp(x, layer, g, mod, w1, w2, final_g=None):
    tm, tf = TM_MLP, TF_MLP
    npt = ROWS_P // tm
    final = final_g is not None
    in_specs = [
        pl.BlockSpec((tm, D), lambda i, f: (i, 0)),
        pl.BlockSpec((None, 1, D), lambda i, f: (layer, 0, 0)),
        pl.BlockSpec((None, None, 6, D), lambda i, f: (layer, _group_of_tile(i, tm), 0, 0)),
        pl.BlockSpec((None, D, tf), lambda i, f: (layer, 0, f)),
        pl.BlockSpec((None, tf, D), lambda i, f: (layer, f, 0)),
    ]
    args = [x, g, mod, w1, w2]
    if final:
        in_specs.append(pl.BlockSpec((1, D), lambda i, f: (0, 0)))
        args.append(final_g)
        out_specs = [pl.BlockSpec((tm, D), lambda i, f: (jnp.minimum(i, npt - 1), 0)),
                     pl.BlockSpec((tm, D), lambda i, f: (jnp.maximum(i - npt, 0), 0))]
        out_shape = [jax.ShapeDtypeStruct((ROWS_P, D), F32), jax.ShapeDtypeStruct((ROWS_S, D), F32)]
    else:
        out_specs = pl.BlockSpec((tm, D), lambda i, f: (i, 0))
        out_shape = jax.ShapeDtypeStruct((ROWS, D), F32)
    return pl.pallas_call(
        functools.partial(_mlp_body, final=final),
        grid=(ROWS // tm, D_FF // tf),
        in_specs=in_specs,
        out_specs=out_specs,
        out_shape=out_shape,
        scratch_shapes=[pltpu.VMEM((tm, D), BF16), pltpu.VMEM((tm, D), F32)],
        compiler_params=_cparams(("arbitrary", "arbitrary")),
        name="mlp_final" if final else "mlp",
    )(*args)


SCAN_T = 256


def _seg_cumsums(a):
    n = a.shape[0]
    ii = lax.broadcasted_iota(jnp.int32, (SCAN_T, SCAN_T), 0)
    jj = lax.broadcasted_iota(jnp.int32, (SCAN_T, SCAN_T), 1)
    lower = jnp.where(ii >= jj, 1.0, 0.0).astype(BF16)
    upper = jnp.where(ii <= jj, 1.0, 0.0).astype(BF16)
    hi = a.astype(BF16)
    rest = a - hi.astype(F32)
    mid = rest.astype(BF16)
    lo = (rest - mid.astype(F32)).astype(BF16)
    parts = jnp.concatenate([hi, mid, lo], axis=1)
    pre, suf = [], []
    for c in range(n // SCAN_T):
        pc = parts[SCAN_T * c:SCAN_T * (c + 1), :]
        for tri, out in ((lower, pre), (upper, suf)):
            s3 = _dot(tri, pc)
            out.append(s3[:, :LANES] + s3[:, LANES:2 * LANES] + s3[:, 2 * LANES:])
    return jnp.concatenate(pre, axis=0), jnp.concatenate(suf, axis=0)


LOG2E = 1.4426950408889634


def _ssd_body(*refs, L, has_s0, emit_state):
    it = iter(refs)
    z_ref, x_ref, b_ref, c_ref, dt_ref = (next(it) for _ in range(5))
    cwx, cbx, cwb, cbb, cwc, cbc = (next(it) for _ in range(6))
    dtb_ref, alog_ref, dl_ref = next(it), next(it), next(it)
    s0_ref = next(it) if has_s0 else None
    y_ref = next(it)
    st_out = next(it) if emit_state else None
    padx, padb, xa, ba, ca, cbs, cumc, crp, yacc, st = (next(it) for _ in range(10))

    T = SCAN_T
    nc = L // T
    g0 = pl.program_id(0)

    def conv(in_ref, w_ref, bias_ref, pad):
        width = in_ref.shape[1]
        pad[0:8, :] = jnp.zeros((8, width), F32)
        pad[L + 8:L + 16, :] = jnp.zeros((8, width), F32)
        pad[8:L + 8, :] = in_ref[...].astype(F32)
        acc = bias_ref[...] + pad[6:6 + L, :] * w_ref[0:1, :]
        for k in range(1, CONV_W):
            acc = acc + pad[6 + k:6 + k + L, :] * w_ref[k:k + 1, :]
        return _silu(acc)

    xa[...] = conv(x_ref, cwx, cbx, padx)
    ba[...] = conv(b_ref, cwb, cbb, padb)
    ca[...] = conv(c_ref, cwc, cbc, padb)

    dt_all = _softplus(dt_ref[...] + dtb_ref[...])
    a_all = dt_all * (-jnp.exp(alog_ref[...]))
    shift = jnp.where(g0 == 0, 0, LANES - SSD_HPG * g0)
    l2dt = jnp.log2(pltpu.roll(dt_all, shift, 1))
    ar = pltpu.roll(a_all, shift, 1) * LOG2E
    cum_f, cum_b = _seg_cumsums(ar)
    cumc[0] = cum_f
    cumc[1] = cum_b
    crp[0] = (cum_f - l2dt).T
    crp[1] = (cum_b - l2dt).T

    yacc[...] = xa[...] * dl_ref[...]
    if has_s0:
        st[...] = s0_ref[...]

    ii = lax.broadcasted_iota(jnp.int32, (T, T), 0)
    jj = lax.broadcasted_iota(jnp.int32, (T, T), 1)

    def chunk(c, d):
        rows = slice(T * c, T * (c + 1))
        first = c == (0 if d == 0 else nc - 1)
        last = c == (nc - 1 if d == 0 else 0)
        zero_state = first and not has_s0
        need_state = emit_state or not last
        mask = (ii >= jj) if d == 0 else (ii <= jj)
        end = T * c + (T - 1 if d == 0 else 0)
        xav = xa[rows, :]
        xab = xav.astype(BF16)
        cab = ca[rows, :].astype(BF16)
        if need_state:
            xat = xav.T.astype(BF16)
            bat = ba[rows, :].T
        for r in range(SSD_HPG):
            lane = SSD_HEADS * d + r
            hs = slice(SSD_P * r, SSD_P * (r + 1))
            cc = cumc[d, rows, lane:lane + 1]
            cr = crp[d, lane:lane + 1, rows]
            dec = jnp.where(mask, jnp.exp2(cc - cr), 0.0)
            y = _dot((cbs[c] * dec).astype(BF16), xab[:, hs])
            if not zero_state:
                y = y + _dot_nt(cab, st[d, r].astype(BF16)) * jnp.exp2(cc)
            yacc[rows, hs] += y
            if need_state:
                tot = cumc[d, end:end + 1, lane:lane + 1]
                upd = _dot_nt(xat[hs, :], (bat * jnp.exp2(tot - cr)).astype(BF16))
                st[d, r] = upd if zero_state else jnp.exp2(tot) * st[d, r] + upd

    for c in range(nc):
        rows = slice(T * c, T * (c + 1))
        cbs[c] = _dot_nt(ca[rows, :].astype(BF16), ba[rows, :].astype(BF16))
    for c in range(nc):
        chunk(c, 0)
        chunk(nc - 1 - c, 1)

    y_ref[...] = (yacc[...] * _silu(z_ref[...].astype(F32))).astype(BF16)
    if emit_state:
        st_out[...] = st[...]


def _ssd_scan(p0, pdt, conv_w, conv_b, dtb, alog, dl, s0, *, nb, L, emit_state):
    rb0 = 0 if s0 is None else ROWS_P // L
    gw = SSD_HPG * SSD_P
    nc = L // SCAN_T
    x0 = SSD_DI // gw
    b0 = 2 * SSD_DI // SSD_N
    c0 = b0 + SSD_GROUPS
    wb0 = SSD_DI // SSD_N
    wc0 = wb0 + SSD_GROUPS
    in_specs = [
        pl.BlockSpec((L, gw), lambda g, b: (rb0 + b, g)),
        pl.BlockSpec((L, gw), lambda g, b: (rb0 + b, x0 + g)),
        pl.BlockSpec((L, SSD_N), lambda g, b: (rb0 + b, b0 + g)),
        pl.BlockSpec((L, SSD_N), lambda g, b: (rb0 + b, c0 + g)),
        pl.BlockSpec((L, LANES), lambda g, b: (rb0 + b, 0)),
        pl.BlockSpec((CONV_W, gw), lambda g, b: (0, g)),
        pl.BlockSpec((1, gw), lambda g, b: (0, g)),
        pl.BlockSpec((CONV_W, SSD_N), lambda g, b: (0, wb0 + g)),
        pl.BlockSpec((1, SSD_N), lambda g, b: (0, wb0 + g)),
        pl.BlockSpec((CONV_W, SSD_N), lambda g, b: (0, wc0 + g)),
        pl.BlockSpec((1, SSD_N), lambda g, b: (0, wc0 + g)),
        pl.BlockSpec((1, LANES), lambda g, b: (0, 0)),
        pl.BlockSpec((1, LANES), lambda g, b: (0, 0)),
        pl.BlockSpec((1, gw), lambda g, b: (0, g)),
    ]
    args = [p0, p0, p0, p0, pdt, conv_w, conv_b, conv_w, conv_b, conv_w, conv_b, dtb, alog, dl]
    state_spec = pl.BlockSpec((None, None, 2, SSD_HPG, SSD_P, SSD_N), lambda g, b: (b, 0, 0, g, 0, 0))
    if s0 is not None:
        in_specs.append(state_spec)
        args.append(s0)
    out_specs = [pl.BlockSpec((L, gw), lambda g, b: (b, g))]
    out_shape = [jax.ShapeDtypeStruct((nb * L, SSD_DI), BF16)]
    if emit_state:
        out_specs.append(state_spec)
        out_shape.append(jax.ShapeDtypeStruct((nb, 1, 2, SSD_HEADS, SSD_P, SSD_N), F32))
    scratch = [
        pltpu.VMEM((L + 16, gw), F32),
        pltpu.VMEM((L + 16, SSD_N), F32),
        pltpu.VMEM((L, gw), F32),
        pltpu.VMEM((L, SSD_N), F32),
        pltpu.VMEM((L, SSD_N), F32),
        pltpu.VMEM((nc, SCAN_T, SCAN_T), F32),
        pltpu.VMEM((2, L, LANES), F32),
        pltpu.VMEM((2, LANES, L), F32),
        pltpu.VMEM((L, gw), F32),
        pltpu.VMEM((2, SSD_HPG, SSD_P, SSD_N), F32),
    ]
    res = pl.pallas_call(
        functools.partial(_ssd_body, L=L, has_s0=s0 is not None, emit_state=emit_state),
        grid=(SSD_GROUPS, nb),
        in_specs=in_specs,
        out_specs=out_specs,
        out_shape=out_shape,
        scratch_shapes=scratch,
        compiler_params=_cparams(("parallel", "parallel")),
        name="ssd_scan_p" if s0 is None else "ssd_scan_s",
    )(*args)
    return res


TQ = 256
HP_P = 4
HP_S = 2


def _mla_body(*refs, L, sample):
    it = iter(refs)
    cq_ref, ckv_ref, krs_ref = next(it), next(it), next(it)
    gq_ref, gkv_ref = next(it), next(it)
    wq, wqs, wk, wv = (next(it) for _ in range(4))
    if sample:
        cckv_ref, ckr_ref, cq_t, sq_t, ck_t, sk_t = (next(it) for _ in range(6))
    x_ref, wo_ref, mod_ref = next(it), next(it), next(it)
    x_out = next(it)
    if not sample:
        ckv_out, kr_out = next(it), next(it)
    kk, vv, o_scr = next(it), next(it), next(it)
    qt = pl.program_id(1)
    scale = (MLA_NOPE + MLA_ROPE) ** -0.5
    hb = 4 * LANES

    @pl.when(qt == 0)
    def _():
        ckv = _rms(ckv_ref[...].astype(F32), gkv_ref[...])
        kr_own = krs_ref[:, 0:LANES]
        if sample:
            keys = jnp.concatenate([cckv_ref[...], ckv], axis=0)
            kr_own = kr_own * ck_t[...] + krs_ref[:, LANES:2 * LANES] * sk_t[...]
            kr_all = jnp.concatenate([ckr_ref[...], kr_own], axis=0)
        else:
            ckv_out[...] = ckv
            kr_out[...] = kr_own[:, MLA_NOPE:MLA_NOPE + MLA_ROPE]
            keys = ckv
            kr_all = kr_own
        kb = keys.astype(BF16)
        for blk in range(MLA_HEADS * LANES // hb):
            cols = slice(hb * blk, hb * (blk + 1))
            kn = _dot(kb, wk[:, cols])
            kk[:, cols] = (kn + jnp.concatenate([kr_all] * 4, axis=1)).astype(BF16)
            vv[:, cols] = _dot(kb, wv[:, cols]).astype(BF16)

    cq = _rms(cq_ref[...].astype(F32), gq_ref[...]).astype(BF16)
    for blk in range(MLA_HEADS * LANES // hb):
        qa = _dot(cq, wq[:, hb * blk:hb * (blk + 1)])
        if sample:
            qs = _dot(cq, wqs[:, hb * blk:hb * (blk + 1)])
        pair = None
        for hh in range(4):
            h = 4 * blk + hh
            cols = slice(LANES * h, LANES * (h + 1))
            qh = qa[:, LANES * hh:LANES * (hh + 1)]
            if sample:
                qh = qh * cq_t[...] + qs[:, LANES * hh:LANES * (hh + 1)] * sq_t[...]
            s = _dot_nt((qh * scale).astype(BF16), kk[:, cols])
            e = jnp.exp(s - jnp.max(s, axis=1, keepdims=True))
            o = _dot(e.astype(BF16), vv[:, cols]) / jnp.sum(e, axis=1, keepdims=True)
            if h % 2 == 0:
                pair = o
            else:
                o_scr[:, LANES * (h // 2):LANES * (h // 2 + 1)] = (pair + o).astype(BF16)
    x_out[...] = x_ref[...] + mod_ref[2:3, :] * _dot(o_scr[...], wo_ref[...].astype(BF16))


def _mla_attn(p1, p1s, gq, gkv, wq, wqs, wk, wv, ctx, x, wo, mod, *, nb, L):
    sample = ctx is not None
    nq = L // TQ
    tk = L + (PAST if sample else 0)
    rbq0 = ROWS_P // TQ if sample else 0
    rbs0 = ROWS_P // L if sample else 0
    ckv_blk = MLA_Q_RANK // MLA_KV_RANK

    def const(a):
        return pl.BlockSpec(a.shape, lambda b, q, nd=a.ndim: (0,) * nd)

    in_specs = [
        pl.BlockSpec((TQ, MLA_Q_RANK), lambda b, q: (rbq0 + b * nq + q, 0)),
        pl.BlockSpec((L, MLA_KV_RANK), lambda b, q: (rbs0 + b, ckv_blk)),
        pl.BlockSpec((L, 2 * LANES), lambda b, q: (rbs0 + b, 0)),
        const(gq), const(gkv), const(wq), const(wqs), const(wk), const(wv),
    ]
    args = [p1, p1, p1s, gq, gkv, wq, wqs, wk, wv]
    if sample:
        cckv, ckr, cpad, spad = ctx
        in_specs += [
            pl.BlockSpec((None, None, PAST, MLA_KV_RANK), lambda b, q: (b, 0, 0, 0)),
            pl.BlockSpec((None, None, PAST, LANES), lambda b, q: (b, 0, 0, 0)),
            pl.BlockSpec((TQ, LANES), lambda b, q: (q, 0)),
            pl.BlockSpec((TQ, LANES), lambda b, q: (q, 0)),
            const(cpad), const(spad),
        ]
        args += [cckv, ckr, cpad, spad, cpad, spad]
    mod_arr, mod_layer = mod
    x_spec = pl.BlockSpec((TQ, D), lambda b, q: (rbq0 + b * nq + q, 0))
    in_specs += [
        x_spec, const(wo),
        pl.BlockSpec((None, None, 6, D), lambda b, q: (mod_layer, (1 + b) if sample else 0, 0, 0)),
    ]
    x_index = len(args)
    args += [x, wo, mod_arr]
    out_specs = [x_spec]
    out_shape = [jax.ShapeDtypeStruct((ROWS, D), F32)]
    if not sample:
        out_specs += [
            pl.BlockSpec((None, None, L, MLA_KV_RANK), lambda b, q: (b, 0, 0, 0)),
            pl.BlockSpec((None, None, L, MLA_ROPE), lambda b, q: (b, 0, 0, 0)),
        ]
        out_shape += [
            jax.ShapeDtypeStruct((nb, 1, L, MLA_KV_RANK), F32),
            jax.ShapeDtypeStruct((nb, 1, L, MLA_ROPE), F32),
        ]
    scratch = [
        pltpu.VMEM((tk, MLA_HEADS * LANES), BF16),
        pltpu.VMEM((tk, MLA_HEADS * LANES), BF16),
        pltpu.VMEM((TQ, MLA_HEADS * MLA_V), BF16),
    ]
    return pl.pallas_call(
        functools.partial(_mla_body, L=L, sample=sample),
        grid=(nb, nq),
        in_specs=in_specs,
        out_specs=out_specs,
        out_shape=out_shape,
        scratch_shapes=scratch,
        input_output_aliases={x_index: 0},
        compiler_params=_cparams(("parallel", "arbitrary")),
        name="mla_attn_s" if sample else "mla_attn_p",
    )(*args)


def _log_sigmoid(x):
    return -_softplus(-x)


def _mlstm_body(*refs, L, hp, has_s0, emit_state):
    it = iter(refs)
    q_ref, k_ref, v_ref, g_ref, gb_ref, ng_ref = (next(it) for _ in range(6))
    if has_s0:
        c0_ref, n0_ref, m0_ref = next(it), next(it), next(it)
    h_ref = next(it)
    if emit_state:
        c_out, n_out, m_out = next(it), next(it), next(it)
    gsc, gtr, kq, vts, kts, hacc, cst, nst, mst = (next(it) for _ in range(9))

    T = SCAN_T
    nc = L // T
    h0 = hp * pl.program_id(1)

    gts = g_ref[...] + gb_ref[...]
    gr = pltpu.roll(gts, jnp.where(h0 == 0, 0, LANES - h0), 1)
    b_f, b_b = _seg_cumsums(_log_sigmoid(gr))
    gsc[0] = gr
    gsc[1] = b_f
    gsc[2] = b_b
    gtr[0] = gr.T
    gtr[1] = b_f.T
    gtr[2] = b_b.T

    jj = lax.broadcasted_iota(jnp.int32, (T, T), 0)
    ii = lax.broadcasted_iota(jnp.int32, (T, T), 1)

    def chunk(hh, c, d):
        rows = slice(T * c, T * (c + 1))
        qcols = slice(ML_DQK * hh, ML_DQK * (hh + 1))
        first = c == (0 if d == 0 else nc - 1)
        last = c == (nc - 1 if d == 0 else 0)
        zero_state = first and not has_s0
        li = 2 * ML_HEADS * d + hh
        lb = li + ML_HEADS
        b_row = gtr[1 + d, lb:lb + 1, rows]
        logi_row = gtr[0, li:li + 1, rows]
        cj = gsc[1 + d, rows, lb:lb + 1] - gsc[0, rows, li:li + 1]
        mask = (jj <= ii) if d == 0 else (jj >= ii)
        dlog = jnp.where(mask, b_row - cj, -jnp.inf)
        m_prev = jnp.zeros((1, 1), F32) if zero_state else mst[hh, d][:, 0:1]
        inter = b_row + m_prev
        mcomb = jnp.maximum(inter, jnp.max(dlog, axis=0, keepdims=True))
        s = kq[hh, c] * jnp.exp(dlog - mcomb)
        vt = vts[hh, c]
        num = _dot(vt, s.astype(BF16))
        den = jnp.sum(s, axis=0, keepdims=True)
        if not zero_state:
            iw = jnp.exp(inter - mcomb)
            qt = q_ref[rows, qcols].T
            num = num + iw * _dot_tn(cst[hh, d].astype(BF16), qt)
            n8 = jnp.broadcast_to(nst[hh, d], (8, ML_DQK)).astype(BF16)
            den = den + iw * _dot(n8, qt)[0:1]
        hc = num / jnp.maximum(jnp.abs(den), jnp.exp(-mcomb))
        if d == 0:
            hacc[hh, :, rows] = hc
        else:
            hacc[hh, :, rows] += hc
        if emit_state or not last:
            end = T * c + (T - 1 if d == 0 else 0)
            bq = gtr[1 + d, lb:lb + 1, end:end + 1]
            wlog = bq - b_row + logi_row
            m_new = jnp.maximum(bq + m_prev, jnp.max(wlog, axis=1, keepdims=True))
            sw = jnp.exp(wlog - m_new)
            upd = _dot_nt((kts[hh, c].astype(F32) * sw).astype(BF16), vt)
            nsum = _dot(jnp.broadcast_to(sw, (8, T)).astype(BF16), k_ref[rows, qcols])[0:1]
            if zero_state:
                cst[hh, d] = upd
                nst[hh, d] = nsum
            else:
                cw = jnp.exp(bq + m_prev - m_new)
                cst[hh, d] = cw * cst[hh, d] + upd
                nst[hh, d] = cw * nst[hh, d] + nsum
            mst[hh, d] = jnp.broadcast_to(m_new, (1, LANES))

    for hh in range(hp):
        qcols = slice(ML_DQK * hh, ML_DQK * (hh + 1))
        vcols = slice(ML_DV * hh, ML_DV * (hh + 1))
        if has_s0:
            for d in range(2):
                cst[hh, d] = c0_ref[d, hh]
                nst[hh, d] = n0_ref[d, pl.ds(h0 + hh, 1), :]
                mst[hh, d] = m0_ref[d, pl.ds(h0 + hh, 1), :]
        for c in range(nc):
            rows = slice(T * c, T * (c + 1))
            kc = k_ref[rows, qcols]
            kq[hh, c] = _dot_nt(kc, q_ref[rows, qcols])
            kts[hh, c] = kc.T
            vts[hh, c] = v_ref[rows, vcols].T
        for c in range(nc):
            chunk(hh, c, 0)
        for c in range(nc):
            chunk(hh, nc - 1 - c, 1)
        ht = hacc[hh]
        r = lax.rsqrt(jnp.mean(ht * ht, axis=0, keepdims=True) + EPS)
        h_ref[:, vcols] = ((ht * r).T * ng_ref[:, vcols]).astype(BF16)
        if emit_state:
            for d in range(2):
                c_out[d, hh] = cst[hh, d]
                n_out[d, pl.ds(h0 + hh, 1), :] = nst[hh, d]
                m_out[d, pl.ds(h0 + hh, 1), :] = mst[hh, d]


def _mlstm_scan(q, k, v, gates, gb, ng, s0, *, nb, L, hp, emit_state):
    rb0 = 0 if s0 is None else ROWS_P // L
    in_specs = [
        pl.BlockSpec((L, hp * ML_DQK), lambda b, h: (rb0 + b, h)),
        pl.BlockSpec((L, hp * ML_DQK), lambda b, h: (rb0 + b, h)),
        pl.BlockSpec((L, hp * ML_DV), lambda b, h: (rb0 + b, h)),
        pl.BlockSpec((L, LANES), lambda b, h: (rb0 + b, 0)),
        pl.BlockSpec((1, LANES), lambda b, h: (0, 0)),
        pl.BlockSpec((1, hp * ML_DV), lambda b, h: (0, h)),
    ]
    args = [q, k, v, gates, gb, ng]
    c_spec = pl.BlockSpec((None, None, 2, hp, ML_DQK, ML_DV), lambda b, h: (b, 0, 0, h, 0, 0))
    n_spec = pl.BlockSpec((None, 2, ML_HEADS, LANES), lambda b, h: (b, 0, 0, 0))
    if s0 is not None:
        in_specs += [c_spec, n_spec, n_spec]
        args += list(s0)
    out_specs = [pl.BlockSpec((L, hp * ML_DV), lambda b, h: (b, h))]
    out_shape = [jax.ShapeDtypeStruct((nb * L, ML_DI), BF16)]
    if emit_state:
        out_specs += [c_spec, n_spec, n_spec]
        out_shape += [
            jax.ShapeDtypeStruct((nb, 1, 2, ML_HEADS, ML_DQK, ML_DV), F32),
            jax.ShapeDtypeStruct((nb, 2, ML_HEADS, LANES), F32),
            jax.ShapeDtypeStruct((nb, 2, ML_HEADS, LANES), F32),
        ]
    scratch = [
        pltpu.VMEM((3, L, LANES), F32),
        pltpu.VMEM((3, LANES, L), F32),
        pltpu.VMEM((hp, L // SCAN_T, SCAN_T, SCAN_T), F32),
        pltpu.VMEM((hp, L // SCAN_T, ML_DV, SCAN_T), BF16),
        pltpu.VMEM((hp, L // SCAN_T, ML_DQK, SCAN_T), BF16),
        pltpu.VMEM((hp, ML_DV, L), F32),
        pltpu.VMEM((hp, 2, ML_DQK, ML_DV), F32),
        pltpu.VMEM((hp, 2, 1, ML_DQK), F32),
        pltpu.VMEM((hp, 2, 1, LANES), F32),
    ]
    return pl.pallas_call(
        functools.partial(_mlstm_body, L=L, hp=hp, has_s0=s0 is not None, emit_state=emit_state),
        grid=(nb, ML_HEADS // hp),
        in_specs=in_specs,
        out_specs=out_specs,
        out_shape=out_shape,
        scratch_shapes=scratch,
        compiler_params=_cparams(("parallel", "arbitrary")),
        name="mlstm_scan_p" if s0 is None else "mlstm_scan_s",
    )(*args)


def _rope_blocks(x, ct, st):
    lane = lax.broadcasted_iota(jnp.int32, (x.shape[0], LANES), 1)
    first_half = lane % DF_D < DF_D // 2
    outs = []
    for blk in range(x.shape[1] // LANES):
        xb = x[:, LANES * blk:LANES * (blk + 1)]
        swapped = jnp.where(first_half, pltpu.roll(xb, LANES - DF_D // 2, 1), pltpu.roll(xb, DF_D // 2, 1))
        outs.append(xb * ct + swapped * st)
    return jnp.concatenate(outs, axis=1)


def _diff_body(*refs, L, sample, lam_init):
    it = iter(refs)
    q_ref, k_ref, v_ref = next(it), next(it), next(it)
    lq1, lk1, lq2, lk2, sg_ref = (next(it) for _ in range(5))
    if sample:
        ck_ref, cv_ref, cq_t, sq_t, ck_t, sk_t = (next(it) for _ in range(6))
    x_ref, wo_ref, mod_ref = next(it), next(it), next(it)
    x_out = next(it)
    if sample:
        ka, va = next(it), next(it)
    else:
        k_out, v_out = next(it), next(it)
    o_scr = next(it)
    qt = pl.program_id(1)

    lam = (jnp.exp(jnp.sum(lq1[...] * lk1[...], axis=1, keepdims=True))
           - jnp.exp(jnp.sum(lq2[...] * lk2[...], axis=1, keepdims=True)) + lam_init)

    @pl.when(qt == 0)
    def _():
        if sample:
            ka[0:PAST, :] = ck_ref[...].astype(BF16)
            ka[PAST:PAST + L, :] = _rope_blocks(k_ref[...].astype(F32), ck_t[...], sk_t[...]).astype(BF16)
            va[0:PAST, :] = cv_ref[...].astype(BF16)
            va[PAST:PAST + L, :] = v_ref[...]
        else:
            k_out[...] = k_ref[...].astype(F32).reshape(L, DF_HEADS, 2 * DF_D)
            v_out[...] = v_ref[...].astype(F32).reshape(L, DF_HEADS, 2 * DF_D)

    q = q_ref[...].astype(F32)
    if sample:
        q = _rope_blocks(q, cq_t[...], sq_t[...])
    q = q * (DF_D ** -0.5)
    lo = lax.broadcasted_iota(jnp.int32, (TQ, LANES), 1) < DF_D

    def attend(qm, kh, vh):
        s = _dot_nt(qm, kh)
        e = jnp.exp(s - jnp.max(s, axis=1, keepdims=True))
        return _dot(e.astype(BF16), vh) / jnp.sum(e, axis=1, keepdims=True)

    for h in range(DF_HEADS):
        cols = slice(LANES * h, LANES * (h + 1))
        qh = q[:, cols]
        if sample:
            kh, vh = ka[:, cols], va[:, cols]
        else:
            kh, vh = k_ref[:, cols], v_ref[:, cols]
        a0 = attend(jnp.where(lo, qh, 0.0).astype(BF16), kh, vh)
        a1 = attend(jnp.where(lo, 0.0, qh).astype(BF16), kh, vh)
        o = a0 - lam * a1
        o_scr[:, cols] = (_rms(o, sg_ref[...]) * (1.0 - lam_init)).astype(BF16)
    x_out[...] = x_ref[...] + mod_ref[2:3, :] * _dot(o_scr[...], wo_ref[...].astype(BF16))


def _diff_attn(p3, lq1, lk1, lq2, lk2, sg, ctx, x, wo, mod, *, nb, L, lam_init):
    sample = ctx is not None
    nq = L // TQ
    rbq0 = ROWS_P // TQ if sample else 0
    rbs0 = ROWS_P // L if sample else 0

    def const(a):
        return pl.BlockSpec(a.shape, lambda b, q, nd=a.ndim: (0,) * nd)

    in_specs = [
        pl.BlockSpec((TQ, D), lambda b, q: (rbq0 + b * nq + q, 0)),
        pl.BlockSpec((L, D), lambda b, q: (rbs0 + b, 1)),
        pl.BlockSpec((L, D), lambda b, q: (rbs0 + b, 2)),
        const(lq1), const(lk1), const(lq2), const(lk2), const(sg),
    ]
    args = [p3, p3, p3, lq1, lk1, lq2, lk2, sg]
    kv_spec = pl.BlockSpec((None, None, PAST if sample else L, D), lambda b, q: (b, 0, 0, 0))
    if sample:
        ck, cv, c128, s128 = ctx
        in_specs += [
            kv_spec, kv_spec,
            pl.BlockSpec((TQ, LANES), lambda b, q: (q, 0)),
            pl.BlockSpec((TQ, LANES), lambda b, q: (q, 0)),
            const(c128), const(s128),
        ]
        args += [ck, cv, c128, s128, c128, s128]
    mod_arr, mod_layer = mod
    x_spec = pl.BlockSpec((TQ, D), lambda b, q: (rbq0 + b * nq + q, 0))
    in_specs += [
        x_spec, const(wo),
        pl.BlockSpec((None, None, 6, D), lambda b, q: (mod_layer, (1 + b) if sample else 0, 0, 0)),
    ]
    x_index = len(args)
    args += [x, wo, mod_arr]
    out_specs = [x_spec]
    out_shape = [jax.ShapeDtypeStruct((ROWS, D), F32)]
    scratch = []
    if sample:
        scratch = [pltpu.VMEM((PAST + L, D), BF16), pltpu.VMEM((PAST + L, D), BF16)]
    else:
        kv_out = pl.BlockSpec((None, None, L, DF_HEADS, 2 * DF_D), lambda b, q: (b, 0, 0, 0, 0))
        out_specs += [kv_out, kv_out]
        out_shape += [jax.ShapeDtypeStruct((nb, 1, L, DF_HEADS, 2 * DF_D), F32)] * 2
    scratch = scratch + [pltpu.VMEM((TQ, D), BF16)]
    return pl.pallas_call(
        functools.partial(_diff_body, L=L, sample=sample, lam_init=lam_init),
        grid=(nb, nq),
        in_specs=in_specs,
        out_specs=out_specs,
        out_shape=out_shape,
        scratch_shapes=scratch,
        input_output_aliases={x_index: 0},
        compiler_params=_cparams(("parallel", "arbitrary")),
        name="diff_attn_s" if sample else "diff_attn_p",
    )(*args)


CONV_COLS = 256


def _mlstm_q_body(x_ref, cw_ref, cb_ref, w_ref, q_ref, xc_ref, msk):
    tm = x_ref.shape[0]
    seq = jnp.where(pl.program_id(0) < ROWS_P // tm, L_PROMPT, L_SAMPLE)
    pos = lax.broadcasted_iota(jnp.int32, (tm, CONV_COLS), 0) & (seq - 1)
    offs = [k - CONV_W // 2 for k in range(CONV_W)]
    for k, off in enumerate(offs):
        if off != 0:
            dst = pos - off
            msk[k] = jnp.where(jnp.logical_and(dst >= 0, dst < seq), 1.0, 0.0)
    acc = None
    kslab = 2 * CONV_COLS
    for kb in range(ML_DI // kslab):
        for cb in range(kslab // CONV_COLS):
            cols = slice(kslab * kb + CONV_COLS * cb, kslab * kb + CONV_COLS * (cb + 1))
            x = x_ref[:, cols].astype(F32)
            y = None
            for k, off in enumerate(offs):
                tap = x if off == 0 else pltpu.roll(x * msk[k], (-off) % tm, 0)
                tap = tap * cw_ref[k:k + 1, cols]
                y = cb_ref[:, cols] + tap if y is None else y + tap
            xc_ref[:, cols] = _silu(y).astype(BF16)
        kcols = slice(kslab * kb, kslab * (kb + 1))
        part = _dot(xc_ref[:, kcols], w_ref[kcols, :].astype(BF16))
        acc = part if acc is None else part + acc
    q_ref[...] = acc.astype(BF16)


def _mlstm_q(xm_src, conv_w, conv_b, w_q):
    tm = TM_CONVQ
    nq = ML_HEADS * ML_DQK
    return pl.pallas_call(
        _mlstm_q_body,
        grid=(ROWS // tm,),
        in_specs=[
            pl.BlockSpec((tm, ML_DI), lambda i: (i, 0)),
            pl.BlockSpec((CONV_W, ML_DI), lambda i: (0, 0)),
            pl.BlockSpec((1, ML_DI), lambda i: (0, 0)),
            pl.BlockSpec((ML_DI, nq), lambda i: (0, 0)),
        ],
        out_specs=[pl.BlockSpec((tm, nq), lambda i: (i, 0)), pl.BlockSpec((tm, ML_DI), lambda i: (i, 0))],
        out_shape=[jax.ShapeDtypeStruct((ROWS, nq), BF16), jax.ShapeDtypeStruct((ROWS, ML_DI), BF16)],
        scratch_shapes=[pltpu.VMEM((CONV_W, tm, CONV_COLS), F32)],
        compiler_params=_cparams(("parallel",)),
        name="mlstm_q",
    )(xm_src, conv_w, conv_b, w_q)


def _pro_mlstm_gate(p, rs):
    hn, xc, z = (r[rs, :].astype(F32) for r in p.rows)
    p.emit((hn + p.consts[0][...] * xc) * _silu(z), rs)


def _rope_tables(d):
    rows = L_SAMPLE // GRID_W
    pos_r = jnp.repeat(jnp.arange(rows, dtype=F32), GRID_W)
    pos_c = jnp.tile(jnp.arange(GRID_W, dtype=F32), rows)
    nf = d // 4
    inv = ROPE_BASE ** (-jnp.arange(nf, dtype=F32) / nf)
    ang = jnp.concatenate([pos_r[:, None] * inv, pos_c[:, None] * inv], axis=-1)
    cos, sin = jnp.cos(ang), jnp.sin(ang)
    return jnp.concatenate([cos, cos], axis=-1), jnp.concatenate([-sin, sin], axis=-1)


def _pad_cols(a, n):
    return jnp.pad(a, ((0, 0), (0, n - a.shape[1])))


def kernel(x_prompt, x_sample, state_ssd, cache_mla_ckv, cache_mla_krope, state_mlstm_C, state_mlstm_n, state_mlstm_m, cache_diff_k, cache_diff_v, c, c_ctx, norm1_g, norm2_g, ada_w, ada_b, mlp_w1, mlp_w2, final_g, ssd_w_in, ssd_conv_w, ssd_conv_b, ssd_dt_bias, ssd_A_log, ssd_D, ssd_norm_g, ssd_w_out, mla_w_in, mla_q_norm_g, mla_kv_norm_g, mla_w_uq, mla_w_ukv, mla_w_o, mlstm_w_up, mlstm_conv_w, mlstm_conv_b, mlstm_gate_b, mlstm_w_q, mlstm_w_k, mlstm_w_v, mlstm_skip, mlstm_norm_g, mlstm_w_down, diff_w_qkv, diff_lq1, diff_lk1, diff_lq2, diff_lk2, diff_subln_g, diff_w_o):
    xp2, xs2 = x_prompt.reshape(ROWS_P, D), x_sample.reshape(ROWS_S, D)
    cvec = jnp.concatenate([c_ctx[None, :], c, jnp.zeros((5, D), F32)], axis=0)
    mod_all = _ada_mod(cvec, ada_w, ada_b)
    g2 = norm2_g.reshape(DEPTH, 1, D)

    def in_proj(xin, layer, w, n_cols, tn, extra_w=None, name="in_proj"):
        xrow = ("d", xin[0], xin[1], D, 0) if isinstance(xin, tuple) else ("u", xin, D, 0)
        return _fused_mm(rows=[xrow], consts=[norm1_g[layer][None, :]], mod=(mod_all, layer), w=w, k_dim=D,
                         n_cols=n_cols, tm=TM_IN, tn=tn, prologue=_pro_normmod, extra_w=extra_w, name=name)

    def out_proj(xin, layer, rows, consts, prologue, w, k_dim, name):
        return _fused_mm(rows=rows, consts=consts, w=w, k_dim=k_dim, n_cols=D, tm=TM_OUT, tn=TN_PROJ, prologue=prologue,
                         epilogue=_epi_residual(2), erows=[xin], emod=(mod_all, layer), out_dtype=F32, name=name)

    p0, pdt = in_proj((xp2, xs2), 0, ssd_w_in[0], 3 * SSD_DI, TN_SSD_IN, extra_w=_pad_cols(ssd_w_in[0][:, 3 * SSD_DI:], LANES),
                      name="ssd_in")
    dtb = _pad_cols(ssd_dt_bias[0].reshape(1, 2 * SSD_HEADS), LANES)
    alog = _pad_cols(ssd_A_log[0].reshape(1, 2 * SSD_HEADS), LANES)
    dl = jnp.repeat(ssd_D[0], SSD_P)[None, :]
    scan_args = (p0, pdt, ssd_conv_w[0], ssd_conv_b[0][None, :], dtb, alog, dl)
    yg_p, new_ssd = _ssd_scan(*scan_args, None, nb=N_PROMPT_SEQ, L=L_PROMPT, emit_state=True)
    (yg_s,) = _ssd_scan(*scan_args, state_ssd, nb=N_SAMPLE_SEQ, L=L_SAMPLE, emit_state=False)
    x = out_proj((xp2, xs2), 0, [("d", yg_p, yg_s, SSD_DI, 0)], [ssd_norm_g[0][None, :]], _pro_rms, ssd_w_out[0], SSD_DI,
                 "ssd_out")
    x = _mlp(x, 0, g2, mod_all, mlp_w1, mlp_w2)

    w_in = mla_w_in[0]
    kr0 = MLA_Q_RANK + MLA_KV_RANK
    half = MLA_ROPE // 2
    zk = jnp.zeros((D, MLA_NOPE), F32)
    zr = jnp.zeros((D, LANES - MLA_NOPE - MLA_ROPE), F32)
    w_kr = jnp.concatenate([zk, w_in[:, kr0:kr0 + MLA_ROPE], zr,
                            zk, w_in[:, kr0 + half:kr0 + MLA_ROPE], w_in[:, kr0:kr0 + half], zr], axis=1)
    p1, p1s = in_proj(x, 1, w_in, kr0, kr0, extra_w=w_kr, name="mla_in")
    wuq = mla_w_uq[0].reshape(MLA_Q_RANK, MLA_HEADS, MLA_NOPE + MLA_ROPE)
    zq = jnp.zeros((MLA_Q_RANK, MLA_HEADS, LANES - MLA_NOPE - MLA_ROPE), F32)
    zqn = jnp.zeros((MLA_Q_RANK, MLA_HEADS, MLA_NOPE), F32)
    wq = jnp.concatenate([wuq, zq], axis=-1).reshape(MLA_Q_RANK, MLA_HEADS * LANES).astype(BF16)
    wqs = jnp.concatenate([zqn, wuq[..., MLA_NOPE + half:], wuq[..., MLA_NOPE:MLA_NOPE + half], zq],
                          axis=-1).reshape(MLA_Q_RANK, MLA_HEADS * LANES).astype(BF16)
    wukv = mla_w_ukv[0].reshape(MLA_KV_RANK, MLA_HEADS, MLA_NOPE + MLA_V)
    zkv = jnp.zeros((MLA_KV_RANK, MLA_HEADS, MLA_NOPE), F32)
    wk = jnp.concatenate([wukv[..., :MLA_NOPE], zkv], axis=-1).reshape(MLA_KV_RANK, MLA_HEADS * LANES).astype(BF16)
    wv_own = wukv[..., MLA_NOPE:]
    odd = (jnp.arange(MLA_HEADS) % 2 == 1)[None, :, None]
    wv = jnp.where(odd, jnp.concatenate([zkv, wv_own], axis=-1), jnp.concatenate([wv_own, zkv], axis=-1))
    wv = wv.reshape(MLA_KV_RANK, MLA_HEADS * LANES).astype(BF16)
    c32, s32 = _rope_tables(MLA_ROPE)
    tz = jnp.zeros((L_SAMPLE, LANES - MLA_NOPE - MLA_ROPE), F32)
    cpad = jnp.concatenate([jnp.ones((L_SAMPLE, MLA_NOPE), F32), c32, tz], axis=1)
    spad = jnp.concatenate([jnp.zeros((L_SAMPLE, MLA_NOPE), F32), s32, tz], axis=1)
    ckr_pad = jnp.pad(cache_mla_krope, ((0, 0), (0, 0), (0, 0), (MLA_NOPE, LANES - MLA_NOPE - MLA_ROPE)))
    mla_w = (mla_q_norm_g[0][None, :], mla_kv_norm_g[0][None, :], wq, wqs, wk, wv)
    x, new_ckv, new_kr = _mla_attn(p1, p1s, *mla_w, None, x, mla_w_o[0], (mod_all, 1), nb=N_PROMPT_SEQ, L=L_PROMPT)
    (x,) = _mla_attn(p1, p1s, *mla_w, (cache_mla_ckv, ckr_pad, cpad, spad), x, mla_w_o[0], (mod_all, 1),
                     nb=N_SAMPLE_SEQ, L=L_SAMPLE)
    x = _mlp(x, 1, g2, mod_all, mlp_w1, mlp_w2)

    p2, gates = in_proj(x, 2, mlstm_w_up[0], 2 * ML_DI, TN_IN, extra_w=_pad_cols(mlstm_w_up[0][:, 2 * ML_DI:], LANES),
                        name="mlstm_up")
    q, xc = _mlstm_q(p2, mlstm_conv_w[0], mlstm_conv_b[0][None, :], mlstm_w_q[0])
    k = _fused_mm(rows=[("u", xc, ML_DI, 0)], w=mlstm_w_k[0], k_dim=ML_DI, n_cols=ML_HEADS * ML_DQK, tm=TM_WIDE,
                  tn=TN_PROJ, epilogue=lambda acc, e, m, rs: acc * (ML_DQK ** -0.5), name="mlstm_k")
    v = _fused_mm(rows=[("u", p2, ML_DI, 0)], w=mlstm_w_v[0], k_dim=ML_DI, n_cols=ML_DI, tm=TM_WIDE, tn=TN_PROJ,
                  name="mlstm_v")
    gb = _pad_cols(mlstm_gate_b[0].reshape(1, 4 * ML_HEADS), LANES)
    ng = mlstm_norm_g[0][None, :]
    n0 = _pad_cols(state_mlstm_n[:, 0].reshape(N_SAMPLE_SEQ * 2 * ML_HEADS, ML_DQK), LANES).reshape(
        N_SAMPLE_SEQ, 2, ML_HEADS, LANES)
    m0 = jnp.broadcast_to(state_mlstm_m[:, 0][..., None], (N_SAMPLE_SEQ, 2, ML_HEADS, LANES))
    hn_p, new_c, new_n, new_m = _mlstm_scan(q, k, v, gates, gb, ng, None, nb=N_PROMPT_SEQ, L=L_PROMPT, hp=HP_P,
                                            emit_state=True)
    (hn_s,) = _mlstm_scan(q, k, v, gates, gb, ng, (state_mlstm_C, n0, m0), nb=N_SAMPLE_SEQ, L=L_SAMPLE, hp=HP_S,
                          emit_state=False)
    x = out_proj(x, 2, [("d", hn_p, hn_s, ML_DI, 0), ("u", xc, ML_DI, 0), ("u", p2, ML_DI, 1)],
                 [mlstm_skip[0][None, :]], _pro_mlstm_gate, mlstm_w_down[0], ML_DI, "mlstm_down")
    x = _mlp(x, 2, g2, mod_all, mlp_w1, mlp_w2)

    lam_init = 0.8 - 0.6 * math.exp(-0.3 * 3)
    p3 = in_proj(x, 3, diff_w_qkv[0], 3 * D, TN_IN, name="diff_qkv")
    c64, s64 = _rope_tables(DF_D)
    c128 = jnp.concatenate([c64, c64], axis=-1)
    s128 = jnp.concatenate([s64, s64], axis=-1)
    dparams = (diff_lq1, diff_lk1, diff_lq2, diff_lk2, diff_subln_g)
    x, new_dk, new_dv = _diff_attn(p3, *dparams, None, x, diff_w_o[0], (mod_all, 3), nb=N_PROMPT_SEQ, L=L_PROMPT,
                                   lam_init=lam_init)
    ctx = (cache_diff_k.reshape(N_SAMPLE_SEQ, 1, PAST, D), cache_diff_v.reshape(N_SAMPLE_SEQ, 1, PAST, D), c128, s128)
    (x,) = _diff_attn(p3, *dparams, ctx, x, diff_w_o[0], (mod_all, 3), nb=N_SAMPLE_SEQ, L=L_SAMPLE, lam_init=lam_init)
    y_prompt, y_sample = _mlp(x, 3, g2, mod_all, mlp_w1, mlp_w2, final_g=final_g[None, :])
    y_prompt = y_prompt.reshape(N_PROMPT_SEQ, L_PROMPT, D)
    y_sample = y_sample.reshape(N_SAMPLE_SEQ, L_SAMPLE, D)
    return (y_prompt, y_sample, new_ssd, new_ckv, new_kr, new_c,
            new_n[None].reshape(N_PROMPT_SEQ, 1, 2, ML_HEADS, ML_DQK),
            new_m[..., 0].reshape(N_PROMPT_SEQ, 1, 2, ML_HEADS),
            new_dk, new_dv)
```

```python
import functools
import math

import jax
import jax.numpy as jnp
from jax import lax
from jax.experimental import pallas as pl
from jax.experimental.pallas import tpu as pltpu

F32 = jnp.float32
BF16 = jnp.bfloat16

D = 1024
DEPTH = 4
D_FF = 4 * D
EPS = 1e-6
ROPE_BASE = 10000.0
CONV_W = 5
GRID_W = 64

N_PROMPT_SEQ = 32
L_PROMPT = 256
N_SAMPLE_SEQ = 2
L_SAMPLE = 1024
PAST = 256
ROWS_P = N_PROMPT_SEQ * L_PROMPT
ROWS_S = N_SAMPLE_SEQ * L_SAMPLE
ROWS = ROWS_P + ROWS_S

SSD_DI = 2 * D
SSD_HEADS = 32
SSD_P = 64
SSD_GROUPS = 8
SSD_N = 128
SSD_HPG = SSD_HEADS // SSD_GROUPS

MLA_HEADS = 16
MLA_Q_RANK = 512
MLA_KV_RANK = 256
MLA_NOPE = 64
MLA_ROPE = 32
MLA_V = 64

ML_DI = 2 * D
ML_HEADS = 8
ML_DQK = 128
ML_DV = 256

DF_HEADS = 8
DF_D = 64

LANES = 128
VMEM_LIMIT = 56 * 1024 * 1024

TM_IN = 2048
TN_IN = 1024
TN_SSD_IN = 512
TM_OUT = 1024
TM_WIDE = 2048
TN_PROJ = 512
TM_MLP = 1024
TF_MLP = 1024
TM_CONVQ = 1024
TN_ADA = 1536


def _cparams(sem):
    return pltpu.CompilerParams(dimension_semantics=sem, vmem_limit_bytes=VMEM_LIMIT)


def _silu(x):
    return x * jax.nn.sigmoid(x)


def _softplus(x):
    return jnp.maximum(x, 0.0) + jnp.log1p(jnp.exp(-jnp.abs(x)))


def _rms(x, g):
    r = lax.rsqrt(jnp.mean(x * x, axis=-1, keepdims=True) + EPS)
    return (x * r) * g


def _dot(a, b):
    return jnp.dot(a, b, preferred_element_type=F32)


def _dot_nt(a, b):
    return lax.dot_general(a, b, (((1,), (1,)), ((), ())), preferred_element_type=F32)


def _dot_tn(a, b):
    return lax.dot_general(a, b, (((0,), (0,)), ((), ())), preferred_element_type=F32)


MOD_ROWS = 1024


def _group_of_tile(i, tm, sub=0):
    row0 = i * tm + sub * MOD_ROWS
    return jnp.where(row0 < ROWS_P, 0, 1 + (row0 - ROWS_P) // L_SAMPLE)


def _ada_body(c_ref, w_ref, b_ref, o_ref):
    s = _silu(c_ref[...]).astype(BF16)
    o_ref[...] = _dot(s, w_ref[...].astype(BF16)) + b_ref[...]


def _ada_mod(cvec, ada_w, ada_b):
    tn = TN_ADA
    out = pl.pallas_call(
        _ada_body,
        grid=(DEPTH, 6 * D // tn),
        in_specs=[
            pl.BlockSpec((8, D), lambda l, j: (0, 0)),
            pl.BlockSpec((None, D, tn), lambda l, j: (l, 0, j)),
            pl.BlockSpec((None, 1, tn), lambda l, j: (l, 0, j)),
        ],
        out_specs=pl.BlockSpec((None, 8, tn), lambda l, j: (l, 0, j)),
        out_shape=jax.ShapeDtypeStruct((DEPTH, 8, 6 * D), F32),
        compiler_params=_cparams(("parallel", "parallel")),
        name="ada_mod",
    )(cvec, ada_w, ada_b.reshape(DEPTH, 1, 6 * D))
    return out[:, :3].reshape(DEPTH, 3, 6, D)


class _Pro:
    def __init__(self, rows, consts, mod, i, hs, lhs_out):
        self.rows, self.consts, self.mod, self.i = rows, consts, mod, i
        self._hs, self._lhs_out = hs, lhs_out

    def emit(self, val, rows=slice(None), cols=slice(None)):
        vb = val.astype(BF16)
        self._hs[rows, cols] = vb
        if self._lhs_out is not None:
            self._lhs_out[rows, cols] = vb


ROW_CHUNK = 512


def _fused_mm(*, rows, w, k_dim, n_cols, tm, tn, prologue=None, consts=(), mod=None,
              epilogue=None, erows=(), emod=None, w_col0=0, w_t=False, emit_lhs=False, extra_w=None, out_dtype=BF16,
              name):
    npt = ROWS_P // tm
    grid = (ROWS // tm, n_cols // tn)
    has_dual = any(r[0] == "d" for r in rows)
    dual_epi = any(isinstance(e, tuple) for e in erows)
    if prologue is None:
        assert len(rows) == 1 and rows[0][0] == "u" and not emit_lhs

    in_specs, args = [], []
    for r in rows:
        if r[0] == "u":
            _, arr, width, cb = r
            in_specs.append(pl.BlockSpec((tm, width), lambda i, j, cb=cb: (i, cb)))
            args.append(arr)
        else:
            _, arr_p, arr_s, width, cb = r
            in_specs.append(pl.BlockSpec((tm, width), lambda i, j, cb=cb: (jnp.minimum(i, npt - 1), cb)))
            in_specs.append(pl.BlockSpec((tm, width), lambda i, j, cb=cb: (jnp.maximum(i - npt, 0), cb)))
            args += [arr_p, arr_s]
    for c in consts:
        in_specs.append(pl.BlockSpec(c.shape, lambda i, j, nd=c.ndim: (0,) * nd))
        args.append(c)
    n_sub = max(tm // MOD_ROWS, 1)
    if mod is not None:
        mod_arr, mod_layer = mod
        for s in range(n_sub):
            in_specs.append(pl.BlockSpec((None, None, 6, D),
                                         lambda i, j, s=s: (mod_layer, _group_of_tile(i, tm, s), 0, 0)))
            args.append(mod_arr)
    if w_t:
        in_specs.append(pl.BlockSpec((tn, k_dim), lambda i, j: (w_col0 // tn + j, 0)))
    else:
        in_specs.append(pl.BlockSpec((k_dim, tn), lambda i, j: (0, w_col0 // tn + j)))
    args.append(w)
    if extra_w is not None:
        in_specs.append(pl.BlockSpec(extra_w.shape, lambda i, j: (0, 0)))
        args.append(extra_w)
    for e in erows:
        if isinstance(e, tuple):
            in_specs.append(pl.BlockSpec((tm, tn), lambda i, j: (jnp.minimum(i, npt - 1), j)))
            in_specs.append(pl.BlockSpec((tm, tn), lambda i, j: (jnp.maximum(i - npt, 0), j)))
            args += list(e)
        else:
            in_specs.append(pl.BlockSpec((tm, tn), lambda i, j: (i, j)))
            args.append(e)
    if emod is not None:
        emod_arr, emod_layer = emod
        in_specs.append(pl.BlockSpec((None, None, 6, tn), lambda i, j: (emod_layer, _group_of_tile(i, tm), 0, j)))
        args.append(emod_arr)

    out_specs = [pl.BlockSpec((tm, tn), lambda i, j: (i, j))]
    out_shape = [jax.ShapeDtypeStruct((ROWS, n_cols), out_dtype)]
    if emit_lhs:
        out_specs.append(pl.BlockSpec((tm, k_dim), lambda i, j: (i, 0)))
        out_shape.append(jax.ShapeDtypeStruct((ROWS, k_dim), BF16))
    if extra_w is not None:
        n_extra = extra_w.shape[0] if w_t else extra_w.shape[1]
        out_specs.append(pl.BlockSpec((tm, n_extra), lambda i, j: (i, 0)))
        out_shape.append(jax.ShapeDtypeStruct((ROWS, n_extra), F32))
    scratch = [] if prologue is None else [pltpu.VMEM((tm, k_dim), BF16)]
    chunks = [slice(r, r + ROW_CHUNK) for r in range(0, tm, ROW_CHUNK)]

    def body(*refs):
        it = iter(refs)
        row_refs = [(next(it),) if r[0] == "u" else (next(it), next(it)) for r in rows]
        const_refs = [next(it) for _ in consts]
        mod_ref = [next(it) for _ in range(n_sub)] if mod is not None else None
        w_ref = next(it)
        extra_ref = next(it) if extra_w is not None else None
        erow_refs = [(next(it), next(it)) if isinstance(e, tuple) else (next(it),) for e in erows]
        emod_ref = next(it) if emod is not None else None
        out_ref = next(it)
        lhs_out = next(it) if emit_lhs else None
        extra_out = next(it) if extra_w is not None else None
        hs = next(it) if prologue is not None else None
        i = pl.program_id(0)
        j = pl.program_id(1)

        def compute(first, use_prompt=True):
            chosen = [rr[0] if (len(rr) == 1 or use_prompt) else rr[1] for rr in row_refs]
            echosen = [er[0] if (len(er) == 1 or use_prompt) else er[1] for er in erow_refs]
            p = _Pro(chosen, const_refs, mod_ref, i, hs, lhs_out)
            wb = w_ref[...].astype(BF16)
            for rs in chunks:
                if first:
                    prologue(p, rs)
                lhs = chosen[0][rs, :] if prologue is None else hs[rs, :]
                acc = _dot_nt(lhs, wb) if w_t else _dot(lhs, wb)
                if epilogue is not None:
                    acc = epilogue(acc, echosen, emod_ref, rs)
                out_ref[rs, :] = acc.astype(out_dtype)
                if first and extra_w is not None:
                    eb = extra_ref[...].astype(BF16)
                    extra_out[rs, :] = _dot_nt(lhs, eb) if w_t else _dot(lhs, eb)

        if prologue is None:
            if extra_w is None:
                compute(False)
            else:
                pl.when(j == 0)(lambda: compute(True))
                pl.when(j > 0)(lambda: compute(False))
        else:
            if has_dual or dual_epi:
                pl.when(jnp.logical_and(j == 0, i < npt))(lambda: compute(True, True))
                pl.when(jnp.logical_and(j == 0, i >= npt))(lambda: compute(True, False))
            else:
                pl.when(j == 0)(lambda: compute(True))
            if dual_epi:
                pl.when(jnp.logical_and(j > 0, i < npt))(lambda: compute(False, True))
                pl.when(jnp.logical_and(j > 0, i >= npt))(lambda: compute(False, False))
            else:
                pl.when(j > 0)(lambda: compute(False))

    res = pl.pallas_call(
        body,
        grid=grid,
        in_specs=in_specs,
        out_specs=out_specs,
        out_shape=out_shape,
        scratch_shapes=scratch,
        compiler_params=_cparams(("parallel", "arbitrary")),
        name=name,
    )(*args)
    return res if len(res) > 1 else res[0]


def _pro_normmod(p, rs):
    m = p.mod[rs.start // MOD_ROWS]
    h = _rms(p.rows[0][rs, :], p.consts[0][...]) * (1.0 + m[1:2, :]) + m[0:1, :]
    p.emit(h, rs)


def _pro_rms(p, rs):
    p.emit(_rms(p.rows[0][rs, :].astype(F32), p.consts[0][...]), rs)


def _epi_residual(gate_row):
    def epi(acc, erows, emod, rs):
        return erows[0][rs, :] + emod[gate_row:gate_row + 1, :] * acc
    return epi


def _mlp_body(*refs, final):
    it = iter(refs)
    x_ref, g_ref, mod_ref, w1_ref, w2_ref = (next(it) for _ in range(5))
    fg_ref = next(it) if final else None
    outs = [next(it), next(it)] if final else [next(it)]
    hs, acc = next(it), next(it)
    i = pl.program_id(0)
    f = pl.program_id(1)
    nf = pl.num_programs(1)
    tm = x_ref.shape[0]
    chunks = [slice(r, r + ROW_CHUNK) for r in range(0, tm, ROW_CHUNK)]

    def step(first, last):
        w1b = w1_ref[...].astype(BF16)
        w2b = w2_ref[...].astype(BF16)
        for rs in chunks:
            if first:
                h = _rms(x_ref[rs, :], g_ref[...]) * (1.0 + mod_ref[4:5, :]) + mod_ref[3:4, :]
                hs[rs, :] = h.astype(BF16)
            u = jnp.square(jnp.maximum(_dot(hs[rs, :], w1b), 0.0))
            tot = _dot(u.astype(BF16), w2b)
            if not first:
                tot = acc[rs, :] + tot
            if not last:
                acc[rs, :] = tot
                continue
            y = x_ref[rs, :] + mod_ref[5:6, :] * tot
            if not final:
                outs[0][rs, :] = y
            else:
                y = _rms(y, fg_ref[...])
                npt = ROWS_P // tm

                @pl.when(i < npt)
                def _():
                    outs[0][rs, :] = y

                @pl.when(i >= npt)
                def _():
                    outs[1][rs, :] = y

    pl.when(f == 0)(lambda: step(True, False))
    pl.when(jnp.logical_and(f > 0, f < nf - 1))(lambda: step(False, False))
    pl.when(f == nf - 1)(lambda: step(False, True))


def _mlp(x, layer, g, mod, w1, w2, final_g=None):
    tm, tf = TM_MLP, TF_MLP
    npt = ROWS_P // tm
    final = final_g is not None
    in_specs = [
        pl.BlockSpec((tm, D), lambda i, f: (i, 0)),
        pl.BlockSpec((None, 1, D), lambda i, f: (layer, 0, 0)),
        pl.BlockSpec((None, None, 6, D), lambda i, f: (layer, _group_of_tile(i, tm), 0, 0)),
        pl.BlockSpec((None, D, tf), lambda i, f: (layer, 0, f)),
        pl.BlockSpec((None, tf, D), lambda i, f: (layer, f, 0)),
    ]
    args = [x, g, mod, w1, w2]
    if final:
        in_specs.append(pl.BlockSpec((1, D), lambda i, f: (0, 0)))
        args.append(final_g)
        out_specs = [pl.BlockSpec((tm, D), lambda i, f: (jnp.minimum(i, npt - 1), 0)),
                     pl.BlockSpec((tm, D), lambda i, f: (jnp.maximum(i - npt, 0), 0))]
        out_shape = [jax.ShapeDtypeStruct((ROWS_P, D), F32), jax.ShapeDtypeStruct((ROWS_S, D), F32)]
    else:
        out_specs = pl.BlockSpec((tm, D), lambda i, f: (i, 0))
        out_shape = jax.ShapeDtypeStruct((ROWS, D), F32)
    return pl.pallas_call(
        functools.partial(_mlp_body, final=final),
        grid=(ROWS // tm, D_FF // tf),
        in_specs=in_specs,
        out_specs=out_specs,
        out_shape=out_shape,
        scratch_shapes=[pltpu.VMEM((tm, D), BF16), pltpu.VMEM((tm, D), F32)],
        compiler_params=_cparams(("arbitrary", "arbitrary")),
        name="mlp_final" if final else "mlp",
    )(*args)


SCAN_T = 256


def _seg_cumsums(a):
    n = a.shape[0]
    ii = lax.broadcasted_iota(jnp.int32, (SCAN_T, SCAN_T), 0)
    jj = lax.broadcasted_iota(jnp.int32, (SCAN_T, SCAN_T), 1)
    lower = jnp.where(ii >= jj, 1.0, 0.0).astype(BF16)
    upper = jnp.where(ii <= jj, 1.0, 0.0).astype(BF16)
    hi = a.astype(BF16)
    rest = a - hi.astype(F32)
    mid = rest.astype(BF16)
    lo = (rest - mid.astype(F32)).astype(BF16)
    parts = jnp.concatenate([hi, mid, lo], axis=1)
    pre, suf = [], []
    for c in range(n // SCAN_T):
        pc = parts[SCAN_T * c:SCAN_T * (c + 1), :]
        for tri, out in ((lower, pre), (upper, suf)):
            s3 = _dot(tri, pc)
            out.append(s3[:, :LANES] + s3[:, LANES:2 * LANES] + s3[:, 2 * LANES:])
    return jnp.concatenate(pre, axis=0), jnp.concatenate(suf, axis=0)


LOG2E = 1.4426950408889634


def _ssd_body(*refs, L, has_s0, emit_state):
    it = iter(refs)
    z_ref, x_ref, b_ref, c_ref, dt_ref = (next(it) for _ in range(5))
    cwx, cbx, cwb, cbb, cwc, cbc = (next(it) for _ in range(6))
    dtb_ref, alog_ref, dl_ref = next(it), next(it), next(it)
    s0_ref = next(it) if has_s0 else None
    y_ref = next(it)
    st_out = next(it) if emit_state else None
    padx, padb, xa, ba, ca, cbs, cumc, crp, yacc, st = (next(it) for _ in range(10))

    T = SCAN_T
    nc = L // T
    g0 = pl.program_id(0)

    def conv(in_ref, w_ref, bias_ref, pad):
        width = in_ref.shape[1]
        pad[0:8, :] = jnp.zeros((8, width), F32)
        pad[L + 8:L + 16, :] = jnp.zeros((8, width), F32)
        pad[8:L + 8, :] = in_ref[...].astype(F32)
        acc = bias_ref[...] + pad[6:6 + L, :] * w_ref[0:1, :]
        for k in range(1, CONV_W):
            acc = acc + pad[6 + k:6 + k + L, :] * w_ref[k:k + 1, :]
        return _silu(acc)

    xa[...] = conv(x_ref, cwx, cbx, padx)
    ba[...] = conv(b_ref, cwb, cbb, padb)
    ca[...] = conv(c_ref, cwc, cbc, padb)

    dt_all = _softplus(dt_ref[...] + dtb_ref[...])
    a_all = dt_all * (-jnp.exp(alog_ref[...]))
    shift = jnp.where(g0 == 0, 0, LANES - SSD_HPG * g0)
    l2dt = jnp.log2(pltpu.roll(dt_all, shift, 1))
    ar = pltpu.roll(a_all, shift, 1) * LOG2E
    cum_f, cum_b = _seg_cumsums(ar)
    cumc[0] = cum_f
    cumc[1] = cum_b
    crp[0] = (cum_f - l2dt).T
    crp[1] = (cum_b - l2dt).T

    yacc[...] = xa[...] * dl_ref[...]
    if has_s0:
        st[...] = s0_ref[...]

    ii = lax.broadcasted_iota(jnp.int32, (T, T), 0)
    jj = lax.broadcasted_iota(jnp.int32, (T, T), 1)

    def chunk(c, d):
        rows = slice(T * c, T * (c + 1))
        first = c == (0 if d == 0 else nc - 1)
        last = c == (nc - 1 if d == 0 else 0)
        zero_state = first and not has_s0
        need_state = emit_state or not last
        mask = (ii >= jj) if d == 0 else (ii <= jj)
        end = T * c + (T - 1 if d == 0 else 0)
        xav = xa[rows, :]
        xab = xav.astype(BF16)
        cab = ca[rows, :].astype(BF16)
        if need_state:
            xat = xav.T.astype(BF16)
            bat = ba[rows, :].T
        for r in range(SSD_HPG):
            lane = SSD_HEADS * d + r
            hs = slice(SSD_P * r, SSD_P * (r + 1))
            cc = cumc[d, rows, lane:lane + 1]
            cr = crp[d, lane:lane + 1, rows]
            dec = jnp.where(mask, jnp.exp2(cc - cr), 0.0)
            y = _dot((cbs[c] * dec).astype(BF16), xab[:, hs])
            if not zero_state:
                y = y + _dot_nt(cab, st[d, r].astype(BF16)) * jnp.exp2(cc)
            yacc[rows, hs] += y
            if need_state:
                tot = cumc[d, end:end + 1, lane:lane + 1]
                upd = _dot_nt(xat[hs, :], (bat * jnp.exp2(tot - cr)).astype(BF16))
                st[d, r] = upd if zero_state else jnp.exp2(tot) * st[d, r] + upd

    for c in range(nc):
        rows = slice(T * c, T * (c + 1))
        cbs[c] = _dot_nt(ca[rows, :].astype(BF16), ba[rows, :].astype(BF16))
    for c in range(nc):
        chunk(c, 0)
        chunk(nc - 1 - c, 1)

    y_ref[...] = (yacc[...] * _silu(z_ref[...].astype(F32))).astype(BF16)
    if emit_state:
        st_out[...] = st[...]


def _ssd_scan(p0, pdt, conv_w, conv_b, dtb, alog, dl, s0, *, nb, L, emit_state):
    rb0 = 0 if s0 is None else ROWS_P // L
    gw = SSD_HPG * SSD_P
    nc = L // SCAN_T
    x0 = SSD_DI // gw
    b0 = 2 * SSD_DI // SSD_N
    c0 = b0 + SSD_GROUPS
    wb0 = SSD_DI // SSD_N
    wc0 = wb0 + SSD_GROUPS
    in_specs = [
        pl.BlockSpec((L, gw), lambda g, b: (rb0 + b, g)),
        pl.BlockSpec((L, gw), lambda g, b: (rb0 + b, x0 + g)),
        pl.BlockSpec((L, SSD_N), lambda g, b: (rb0 + b, b0 + g)),
        pl.BlockSpec((L, SSD_N), lambda g, b: (rb0 + b, c0 + g)),
        pl.BlockSpec((L, LANES), lambda g, b: (rb0 + b, 0)),
        pl.BlockSpec((CONV_W, gw), lambda g, b: (0, g)),
        pl.BlockSpec((1, gw), lambda g, b: (0, g)),
        pl.BlockSpec((CONV_W, SSD_N), lambda g, b: (0, wb0 + g)),
        pl.BlockSpec((1, SSD_N), lambda g, b: (0, wb0 + g)),
        pl.BlockSpec((CONV_W, SSD_N), lambda g, b: (0, wc0 + g)),
        pl.BlockSpec((1, SSD_N), lambda g, b: (0, wc0 + g)),
        pl.BlockSpec((1, LANES), lambda g, b: (0, 0)),
        pl.BlockSpec((1, LANES), lambda g, b: (0, 0)),
        pl.BlockSpec((1, gw), lambda g, b: (0, g)),
    ]
    args = [p0, p0, p0, p0, pdt, conv_w, conv_b, conv_w, conv_b, conv_w, conv_b, dtb, alog, dl]
    state_spec = pl.BlockSpec((None, None, 2, SSD_HPG, SSD_P, SSD_N), lambda g, b: (b, 0, 0, g, 0, 0))
    if s0 is not None:
        in_specs.append(state_spec)
        args.append(s0)
    out_specs = [pl.BlockSpec((L, gw), lambda g, b: (b, g))]
    out_shape = [jax.ShapeDtypeStruct((nb * L, SSD_DI), BF16)]
    if emit_state:
        out_specs.append(state_spec)
        out_shape.append(jax.ShapeDtypeStruct((nb, 1, 2, SSD_HEADS, SSD_P, SSD_N), F32))
    scratch = [
        pltpu.VMEM((L + 16, gw), F32),
        pltpu.VMEM((L + 16, SSD_N), F32),
        pltpu.VMEM((L, gw), F32),
        pltpu.VMEM((L, SSD_N), F32),
        pltpu.VMEM((L, SSD_N), F32),
        pltpu.VMEM((nc, SCAN_T, SCAN_T), F32),
        pltpu.VMEM((2, L, LANES), F32),
        pltpu.VMEM((2, LANES, L), F32),
        pltpu.VMEM((L, gw), F32),
        pltpu.VMEM((2, SSD_HPG, SSD_P, SSD_N), F32),
    ]
    res = pl.pallas_call(
        functools.partial(_ssd_body, L=L, has_s0=s0 is not None, emit_state=emit_state),
        grid=(SSD_GROUPS, nb),
        in_specs=in_specs,
        out_specs=out_specs,
        out_shape=out_shape,
        scratch_shapes=scratch,
        compiler_params=_cparams(("parallel", "parallel")),
        name="ssd_scan_p" if s0 is None else "ssd_scan_s",
    )(*args)
    return res


TQ = 256
HP_P = 4
HP_S = 2


def _mla_body(*refs, L, sample):
    it = iter(refs)
    cq_ref, ckv_ref, krs_ref = next(it), next(it), next(it)
    gq_ref, gkv_ref = next(it), next(it)
    wq, wqs, wk, wv = (next(it) for _ in range(4))
    if sample:
        cckv_ref, ckr_ref, cq_t, sq_t, ck_t, sk_t = (next(it) for _ in range(6))
    x_ref, wo_ref, mod_ref = next(it), next(it), next(it)
    x_out = next(it)
    if not sample:
        ckv_out, kr_out = next(it), next(it)
    kk, vv, o_scr = next(it), next(it), next(it)
    qt = pl.program_id(1)
    scale = (MLA_NOPE + MLA_ROPE) ** -0.5
    hb = 4 * LANES

    @pl.when(qt == 0)
    def _():
        ckv = _rms(ckv_ref[...].astype(F32), gkv_ref[...])
        kr_own = krs_ref[:, 0:LANES]
        if sample:
            keys = jnp.concatenate([cckv_ref[...], ckv], axis=0)
            kr_own = kr_own * ck_t[...] + krs_ref[:, LANES:2 * LANES] * sk_t[...]
            kr_all = jnp.concatenate([ckr_ref[...], kr_own], axis=0)
        else:
            ckv_out[...] = ckv
            kr_out[...] = kr_own[:, MLA_NOPE:MLA_NOPE + MLA_ROPE]
            keys = ckv
            kr_all = kr_own
        kb = keys.astype(BF16)
        for blk in range(MLA_HEADS * LANES // hb):
            cols = slice(hb * blk, hb * (blk + 1))
            kn = _dot(kb, wk[:, cols])
            kk[:, cols] = (kn + jnp.concatenate([kr_all] * 4, axis=1)).astype(BF16)
            vv[:, cols] = _dot(kb, wv[:, cols]).astype(BF16)

    cq = _rms(cq_ref[...].astype(F32), gq_ref[...]).astype(BF16)
    for blk in range(MLA_HEADS * LANES // hb):
        qa = _dot(cq, wq[:, hb * blk:hb * (blk + 1)])
        if sample:
            qs = _dot(cq, wqs[:, hb * blk:hb * (blk + 1)])
        pair = None
        for hh in range(4):
            h = 4 * blk + hh
            cols = slice(LANES * h, LANES * (h + 1))
            qh = qa[:, LANES * hh:LANES * (hh + 1)]
            if sample:
                qh = qh * cq_t[...] + qs[:, LANES * hh:LANES * (hh + 1)] * sq_t[...]
            s = _dot_nt((qh * scale).astype(BF16), kk[:, cols])
            e = jnp.exp(s - jnp.max(s, axis=1, keepdims=True))
            o = _dot(e.astype(BF16), vv[:, cols]) / jnp.sum(e, axis=1, keepdims=True)
            if h % 2 == 0:
                pair = o
            else:
                o_scr[:, LANES * (h // 2):LANES * (h // 2 + 1)] = (pair + o).astype(BF16)
    x_out[...] = x_ref[...] + mod_ref[2:3, :] * _dot(o_scr[...], wo_ref[...].astype(BF16))


def _mla_attn(p1, p1s, gq, gkv, wq, wqs, wk, wv, ctx, x, wo, mod, *, nb, L):
    sample = ctx is not None
    nq = L // TQ
    tk = L + (PAST if sample else 0)
    rbq0 = ROWS_P // TQ if sample else 0
    rbs0 = ROWS_P // L if sample else 0
    ckv_blk = MLA_Q_RANK // MLA_KV_RANK

    def const(a):
        return pl.BlockSpec(a.shape, lambda b, q, nd=a.ndim: (0,) * nd)

    in_specs = [
        pl.BlockSpec((TQ, MLA_Q_RANK), lambda b, q: (rbq0 + b * nq + q, 0)),
        pl.BlockSpec((L, MLA_KV_RANK), lambda b, q: (rbs0 + b, ckv_blk)),
        pl.BlockSpec((L, 2 * LANES), lambda b, q: (rbs0 + b, 0)),
        const(gq), const(gkv), const(wq), const(wqs), const(wk), const(wv),
    ]
    args = [p1, p1, p1s, gq, gkv, wq, wqs, wk, wv]
    if sample:
        cckv, ckr, cpad, spad = ctx
        in_specs += [
            pl.BlockSpec((None, None, PAST, MLA_KV_RANK), lambda b, q: (b, 0, 0, 0)),
            pl.BlockSpec((None, None, PAST, LANES), lambda b, q: (b, 0, 0, 0)),
            pl.BlockSpec((TQ, LANES), lambda b, q: (q, 0)),
            pl.BlockSpec((TQ, LANES), lambda b, q: (q, 0)),
            const(cpad), const(spad),
        ]
        args += [cckv, ckr, cpad, spad, cpad, spad]
    mod_arr, mod_layer = mod
    x_spec = pl.BlockSpec((TQ, D), lambda b, q: (rbq0 + b * nq + q, 0))
    in_specs += [
        x_spec, const(wo),
        pl.BlockSpec((None, None, 6, D), lambda b, q: (mod_layer, (1 + b) if sample else 0, 0, 0)),
    ]
    x_index = len(args)
    args += [x, wo, mod_arr]
    out_specs = [x_spec]
    out_shape = [jax.ShapeDtypeStruct((ROWS, D), F32)]
    if not sample:
        out_specs += [
            pl.BlockSpec((None, None, L, MLA_KV_RANK), lambda b, q: (b, 0, 0, 0)),
            pl.BlockSpec((None, None, L, MLA_ROPE), lambda b, q: (b, 0, 0, 0)),
        ]
        out_shape += [
            jax.ShapeDtypeStruct((nb, 1, L, MLA_KV_RANK), F32),
            jax.ShapeDtypeStruct((nb, 1, L, MLA_ROPE), F32),
        ]
    scratch = [
        pltpu.VMEM((tk, MLA_HEADS * LANES), BF16),
        pltpu.VMEM((tk, MLA_HEADS * LANES), BF16),
        pltpu.VMEM((TQ, MLA_HEADS * MLA_V), BF16),
    ]
    return pl.pallas_call(
        functools.partial(_mla_body, L=L, sample=sample),
        grid=(nb, nq),
        in_specs=in_specs,
        out_specs=out_specs,
        out_shape=out_shape,
        scratch_shapes=scratch,
        input_output_aliases={x_index: 0},
        compiler_params=_cparams(("parallel", "arbitrary")),
        name="mla_attn_s" if sample else "mla_attn_p",
    )(*args)


def _log_sigmoid(x):
    return -_softplus(-x)


def _mlstm_body(*refs, L, hp, has_s0, emit_state):
    it = iter(refs)
    q_ref, k_ref, v_ref, g_ref, gb_ref, ng_ref = (next(it) for _ in range(6))
    if has_s0:
        c0_ref, n0_ref, m0_ref = next(it), next(it), next(it)
    h_ref = next(it)
    if emit_state:
        c_out, n_out, m_out = next(it), next(it), next(it)
    gsc, gtr, kq, vts, kts, hacc, cst, nst, mst = (next(it) for _ in range(9))

    T = SCAN_T
    nc = L // T
    h0 = hp * pl.program_id(1)

    gts = g_ref[...] + gb_ref[...]
    gr = pltpu.roll(gts, jnp.where(h0 == 0, 0, LANES - h0), 1)
    b_f, b_b = _seg_cumsums(_log_sigmoid(gr))
    gsc[0] = gr
    gsc[1] = b_f
    gsc[2] = b_b
    gtr[0] = gr.T
    gtr[1] = b_f.T
    gtr[2] = b_b.T

    jj = lax.broadcasted_iota(jnp.int32, (T, T), 0)
    ii = lax.broadcasted_iota(jnp.int32, (T, T), 1)

    def chunk(hh, c, d):
        rows = slice(T * c, T * (c + 1))
        qcols = slice(ML_DQK * hh, ML_DQK * (hh + 1))
        first = c == (0 if d == 0 else nc - 1)
        last = c == (nc - 1 if d == 0 else 0)
        zero_state = first and not has_s0
        li = 2 * ML_HEADS * d + hh
        lb = li + ML_HEADS
        b_row = gtr[1 + d, lb:lb + 1, rows]
        logi_row = gtr[0, li:li + 1, rows]
        cj = gsc[1 + d, rows, lb:lb + 1] - gsc[0, rows, li:li + 1]
        mask = (jj <= ii) if d == 0 else (jj >= ii)
        dlog = jnp.where(mask, b_row - cj, -jnp.inf)
        m_prev = jnp.zeros((1, 1), F32) if zero_state else mst[hh, d][:, 0:1]
        inter = b_row + m_prev
        mcomb = jnp.maximum(inter, jnp.max(dlog, axis=0, keepdims=True))
        s = kq[hh, c] * jnp.exp(dlog - mcomb)
        vt = vts[hh, c]
        num = _dot(vt, s.astype(BF16))
        den = jnp.sum(s, axis=0, keepdims=True)
        if not zero_state:
            iw = jnp.exp(inter - mcomb)
            qt = q_ref[rows, qcols].T
            num = num + iw * _dot_tn(cst[hh, d].astype(BF16), qt)
            n8 = jnp.broadcast_to(nst[hh, d], (8, ML_DQK)).astype(BF16)
            den = den + iw * _dot(n8, qt)[0:1]
        hc = num / jnp.maximum(jnp.abs(den), jnp.exp(-mcomb))
        if d == 0:
            hacc[hh, :, rows] = hc
        else:
            hacc[hh, :, rows] += hc
        if emit_state or not last:
            end = T * c + (T - 1 if d == 0 else 0)
            bq = gtr[1 + d, lb:lb + 1, end:end + 1]
            wlog = bq - b_row + logi_row
            m_new = jnp.maximum(bq + m_prev, jnp.max(wlog, axis=1, keepdims=True))
            sw = jnp.exp(wlog - m_new)
            upd = _dot_nt((kts[hh, c].astype(F32) * sw).astype(BF16), vt)
            nsum = _dot(jnp.broadcast_to(sw, (8, T)).astype(BF16), k_ref[rows, qcols])[0:1]
            if zero_state:
                cst[hh, d] = upd
                nst[hh, d] = nsum
            else:
                cw = jnp.exp(bq + m_prev - m_new)
                cst[hh, d] = cw * cst[hh, d] + upd
                nst[hh, d] = cw * nst[hh, d] + nsum
            mst[hh, d] = jnp.broadcast_to(m_new, (1, LANES))

    for hh in range(hp):
        qcols = slice(ML_DQK * hh, ML_DQK * (hh + 1))
        vcols = slice(ML_DV * hh, ML_DV * (hh + 1))
        if has_s0:
            for d in range(2):
                cst[hh, d] = c0_ref[d, hh]
                nst[hh, d] = n0_ref[d, pl.ds(h0 + hh, 1), :]
                mst[hh, d] = m0_ref[d, pl.ds(h0 + hh, 1), :]
        for c in range(nc):
            rows = slice(T * c, T * (c + 1))
            kc = k_ref[rows, qcols]
            kq[hh, c] = _dot_nt(kc, q_ref[rows, qcols])
            kts[hh, c] = kc.T
            vts[hh, c] = v_ref[rows, vcols].T
        for c in range(nc):
            chunk(hh, c, 0)
        for c in range(nc):
            chunk(hh, nc - 1 - c, 1)
        ht = hacc[hh]
        r = lax.rsqrt(jnp.mean(ht * ht, axis=0, keepdims=True) + EPS)
        h_ref[:, vcols] = ((ht * r).T * ng_ref[:, vcols]).astype(BF16)
        if emit_state:
            for d in range(2):
                c_out[d, hh] = cst[hh, d]
                n_out[d, pl.ds(h0 + hh, 1), :] = nst[hh, d]
                m_out[d, pl.ds(h0 + hh, 1), :] = mst[hh, d]


def _mlstm_scan(q, k, v, gates, gb, ng, s0, *, nb, L, hp, emit_state):
    rb0 = 0 if s0 is None else ROWS_P // L
    in_specs = [
        pl.BlockSpec((L, hp * ML_DQK), lambda b, h: (rb0 + b, h)),
        pl.BlockSpec((L, hp * ML_DQK), lambda b, h: (rb0 + b, h)),
        pl.BlockSpec((L, hp * ML_DV), lambda b, h: (rb0 + b, h)),
        pl.BlockSpec((L, LANES), lambda b, h: (rb0 + b, 0)),
        pl.BlockSpec((1, LANES), lambda b, h: (0, 0)),
        pl.BlockSpec((1, hp * ML_DV), lambda b, h: (0, h)),
    ]
    args = [q, k, v, gates, gb, ng]
    c_spec = pl.BlockSpec((None, None, 2, hp, ML_DQK, ML_DV), lambda b, h: (b, 0, 0, h, 0, 0))
    n_spec = pl.BlockSpec((None, 2, ML_HEADS, LANES), lambda b, h: (b, 0, 0, 0))
    if s0 is not None:
        in_specs += [c_spec, n_spec, n_spec]
        args += list(s0)
    out_specs = [pl.BlockSpec((L, hp * ML_DV), lambda b, h: (b, h))]
    out_shape = [jax.ShapeDtypeStruct((nb * L, ML_DI), BF16)]
    if emit_state:
        out_specs += [c_spec, n_spec, n_spec]
        out_shape += [
            jax.ShapeDtypeStruct((nb, 1, 2, ML_HEADS, ML_DQK, ML_DV), F32),
            jax.ShapeDtypeStruct((nb, 2, ML_HEADS, LANES), F32),
            jax.ShapeDtypeStruct((nb, 2, ML_HEADS, LANES), F32),
        ]
    scratch = [
        pltpu.VMEM((3, L, LANES), F32),
        pltpu.VMEM((3, LANES, L), F32),
        pltpu.VMEM((hp, L // SCAN_T, SCAN_T, SCAN_T), F32),
        pltpu.VMEM((hp, L // SCAN_T, ML_DV, SCAN_T), BF16),
        pltpu.VMEM((hp, L // SCAN_T, ML_DQK, SCAN_T), BF16),
        pltpu.VMEM((hp, ML_DV, L), F32),
        pltpu.VMEM((hp, 2, ML_DQK, ML_DV), F32),
        pltpu.VMEM((hp, 2, 1, ML_DQK), F32),
        pltpu.VMEM((hp, 2, 1, LANES), F32),
    ]
    return pl.pallas_call(
        functools.partial(_mlstm_body, L=L, hp=hp, has_s0=s0 is not None, emit_state=emit_state),
        grid=(nb, ML_HEADS // hp),
        in_specs=in_specs,
        out_specs=out_specs,
        out_shape=out_shape,
        scratch_shapes=scratch,
        compiler_params=_cparams(("parallel", "arbitrary")),
        name="mlstm_scan_p" if s0 is None else "mlstm_scan_s",
    )(*args)


def _rope_blocks(x, ct, st):
    lane = lax.broadcasted_iota(jnp.int32, (x.shape[0], LANES), 1)
    first_half = lane % DF_D < DF_D // 2
    outs = []
    for blk in range(x.shape[1] // LANES):
        xb = x[:, LANES * blk:LANES * (blk + 1)]
        swapped = jnp.where(first_half, pltpu.roll(xb, LANES - DF_D // 2, 1), pltpu.roll(xb, DF_D // 2, 1))
        outs.append(xb * ct + swapped * st)
    return jnp.concatenate(outs, axis=1)


def _diff_body(*refs, L, sample, lam_init):
    it = iter(refs)
    q_ref, k_ref, v_ref = next(it), next(it), next(it)
    lq1, lk1, lq2, lk2, sg_ref = (next(it) for _ in range(5))
    if sample:
        ck_ref, cv_ref, cq_t, sq_t, ck_t, sk_t = (next(it) for _ in range(6))
    x_ref, wo_ref, mod_ref = next(it), next(it), next(it)
    x_out = next(it)
    if sample:
        ka, va = next(it), next(it)
    else:
        k_out, v_out = next(it), next(it)
    o_scr = next(it)
    qt = pl.program_id(1)

    lam = (jnp.exp(jnp.sum(lq1[...] * lk1[...], axis=1, keepdims=True))
           - jnp.exp(jnp.sum(lq2[...] * lk2[...], axis=1, keepdims=True)) + lam_init)

    @pl.when(qt == 0)
    def _():
        if sample:
            ka[0:PAST, :] = ck_ref[...].astype(BF16)
            ka[PAST:PAST + L, :] = _rope_blocks(k_ref[...].astype(F32), ck_t[...], sk_t[...]).astype(BF16)
            va[0:PAST, :] = cv_ref[...].astype(BF16)
            va[PAST:PAST + L, :] = v_ref[...]
        else:
            k_out[...] = k_ref[...].astype(F32).reshape(L, DF_HEADS, 2 * DF_D)
            v_out[...] = v_ref[...].astype(F32).reshape(L, DF_HEADS, 2 * DF_D)

    q = q_ref[...].astype(F32)
    if sample:
        q = _rope_blocks(q, cq_t[...], sq_t[...])
    q = q * (DF_D ** -0.5)
    lo = lax.broadcasted_iota(jnp.int32, (TQ, LANES), 1) < DF_D

    def attend(qm, kh, vh):
        s = _dot_nt(qm, kh)
        e = jnp.exp(s - jnp.max(s, axis=1, keepdims=True))
        return _dot(e.astype(BF16), vh) / jnp.sum(e, axis=1, keepdims=True)

    for h in range(DF_HEADS):
        cols = slice(LANES * h, LANES * (h + 1))
        qh = q[:, cols]
        if sample:
            kh, vh = ka[:, cols], va[:, cols]
        else:
            kh, vh = k_ref[:, cols], v_ref[:, cols]
        a0 = attend(jnp.where(lo, qh, 0.0).astype(BF16), kh, vh)
        a1 = attend(jnp.where(lo, 0.0, qh).astype(BF16), kh, vh)
        o = a0 - lam * a1
        o_scr[:, cols] = (_rms(o, sg_ref[...]) * (1.0 - lam_init)).astype(BF16)
    x_out[...] = x_ref[...] + mod_ref[2:3, :] * _dot(o_scr[...], wo_ref[...].astype(BF16))


def _diff_attn(p3, lq1, lk1, lq2, lk2, sg, ctx, x, wo, mod, *, nb, L, lam_init):
    sample = ctx is not None
    nq = L // TQ
    rbq0 = ROWS_P // TQ if sample else 0
    rbs0 = ROWS_P // L if sample else 0

    def const(a):
        return pl.BlockSpec(a.shape, lambda b, q, nd=a.ndim: (0,) * nd)

    in_specs = [
        pl.BlockSpec((TQ, D), lambda b, q: (rbq0 + b * nq + q, 0)),
        pl.BlockSpec((L, D), lambda b, q: (rbs0 + b, 1)),
        pl.BlockSpec((L, D), lambda b, q: (rbs0 + b, 2)),
        const(lq1), const(lk1), const(lq2), const(lk2), const(sg),
    ]
    args = [p3, p3, p3, lq1, lk1, lq2, lk2, sg]
    kv_spec = pl.BlockSpec((None, None, PAST if sample else L, D), lambda b, q: (b, 0, 0, 0))
    if sample:
        ck, cv, c128, s128 = ctx
        in_specs += [
            kv_spec, kv_spec,
            pl.BlockSpec((TQ, LANES), lambda b, q: (q, 0)),
            pl.BlockSpec((TQ, LANES), lambda b, q: (q, 0)),
            const(c128), const(s128),
        ]
        args += [ck, cv, c128, s128, c128, s128]
    mod_arr, mod_layer = mod
    x_spec = pl.BlockSpec((TQ, D), lambda b, q: (rbq0 + b * nq + q, 0))
    in_specs += [
        x_spec, const(wo),
        pl.BlockSpec((None, None, 6, D), lambda b, q: (mod_layer, (1 + b) if sample else 0, 0, 0)),
    ]
    x_index = len(args)
    args += [x, wo, mod_arr]
    out_specs = [x_spec]
    out_shape = [jax.ShapeDtypeStruct((ROWS, D), F32)]
    scratch = []
    if sample:
        scratch = [pltpu.VMEM((PAST + L, D), BF16), pltpu.VMEM((PAST + L, D), BF16)]
    else:
        kv_out = pl.BlockSpec((None, None, L, DF_HEADS, 2 * DF_D), lambda b, q: (b, 0, 0, 0, 0))
        out_specs += [kv_out, kv_out]
        out_shape += [jax.ShapeDtypeStruct((nb, 1, L, DF_HEADS, 2 * DF_D), F32)] * 2
    scratch = scratch + [pltpu.VMEM((TQ, D), BF16)]
    return pl.pallas_call(
        functools.partial(_diff_body, L=L, sample=sample, lam_init=lam_init),
        grid=(nb, nq),
        in_specs=in_specs,
        out_specs=out_specs,
        out_shape=out_shape,
        scratch_shapes=scratch,
        input_output_aliases={x_index: 0},
        compiler_params=_cparams(("parallel", "arbitrary")),
        name="diff_attn_s" if sample else "diff_attn_p",
    )(*args)


CONV_COLS = 256


def _mlstm_q_body(x_ref, cw_ref, cb_ref, w_ref, q_ref, xc_ref, msk):
    tm = x_ref.shape[0]
    seq = jnp.where(pl.program_id(0) < ROWS_P // tm, L_PROMPT, L_SAMPLE)
    pos = lax.broadcasted_iota(jnp.int32, (tm, CONV_COLS), 0) & (seq - 1)
    offs = [k - CONV_W // 2 for k in range(CONV_W)]
    for k, off in enumerate(offs):
        if off != 0:
            dst = pos - off
            msk[k] = jnp.where(jnp.logical_and(dst >= 0, dst < seq), 1.0, 0.0)
    acc = None
    kslab = 2 * CONV_COLS
    for kb in range(ML_DI // kslab):
        for cb in range(kslab // CONV_COLS):
            cols = slice(kslab * kb + CONV_COLS * cb, kslab * kb + CONV_COLS * (cb + 1))
            x = x_ref[:, cols].astype(F32)
            y = None
            for k, off in enumerate(offs):
                tap = x if off == 0 else pltpu.roll(x * msk[k], (-off) % tm, 0)
                tap = tap * cw_ref[k:k + 1, cols]
                y = cb_ref[:, cols] + tap if y is None else y + tap
            xc_ref[:, cols] = _silu(y).astype(BF16)
        kcols = slice(kslab * kb, kslab * (kb + 1))
        part = _dot(xc_ref[:, kcols], w_ref[kcols, :].astype(BF16))
        acc = part if acc is None else part + acc
    q_ref[...] = acc.astype(BF16)


def _mlstm_q(xm_src, conv_w, conv_b, w_q):
    tm = TM_CONVQ
    nq = ML_HEADS * ML_DQK
    return pl.pallas_call(
        _mlstm_q_body,
        grid=(ROWS // tm,),
        in_specs=[
            pl.BlockSpec((tm, ML_DI), lambda i: (i, 0)),
            pl.BlockSpec((CONV_W, ML_DI), lambda i: (0, 0)),
            pl.BlockSpec((1, ML_DI), lambda i: (0, 0)),
            pl.BlockSpec((ML_DI, nq), lambda i: (0, 0)),
        ],
        out_specs=[pl.BlockSpec((tm, nq), lambda i: (i, 0)), pl.BlockSpec((tm, ML_DI), lambda i: (i, 0))],
        out_shape=[jax.ShapeDtypeStruct((ROWS, nq), BF16), jax.ShapeDtypeStruct((ROWS, ML_DI), BF16)],
        scratch_shapes=[pltpu.VMEM((CONV_W, tm, CONV_COLS), F32)],
        compiler_params=_cparams(("parallel",)),
        name="mlstm_q",
    )(xm_src, conv_w, conv_b, w_q)


def _pro_mlstm_gate(p, rs):
    hn, xc, z = (r[rs, :].astype(F32) for r in p.rows)
    p.emit((hn + p.consts[0][...] * xc) * _silu(z), rs)


def _rope_tables(d):
    rows = L_SAMPLE // GRID_W
    pos_r = jnp.repeat(jnp.arange(rows, dtype=F32), GRID_W)
    pos_c = jnp.tile(jnp.arange(GRID_W, dtype=F32), rows)
    nf = d // 4
    inv = ROPE_BASE ** (-jnp.arange(nf, dtype=F32) / nf)
    ang = jnp.concatenate([pos_r[:, None] * inv, pos_c[:, None] * inv], axis=-1)
    cos, sin = jnp.cos(ang), jnp.sin(ang)
    return jnp.concatenate([cos, cos], axis=-1), jnp.concatenate([-sin, sin], axis=-1)


def _pad_cols(a, n):
    return jnp.pad(a, ((0, 0), (0, n - a.shape[1])))


def _pad_rows(a, n):
    return jnp.pad(a, ((0, n - a.shape[0]), (0, 0)))


def kernel(x_prompt, x_sample, state_ssd, cache_mla_ckv, cache_mla_krope, state_mlstm_C, state_mlstm_n, state_mlstm_m, cache_diff_k, cache_diff_v, c, c_ctx, norm1_g, norm2_g, ada_w, ada_b, mlp_w1, mlp_w2, final_g, ssd_w_in, ssd_conv_w, ssd_conv_b, ssd_dt_bias, ssd_A_log, ssd_D, ssd_norm_g, ssd_w_out, mla_w_in, mla_q_norm_g, mla_kv_norm_g, mla_w_uq, mla_w_ukv, mla_w_o, mlstm_w_up, mlstm_conv_w, mlstm_conv_b, mlstm_gate_b, mlstm_w_q, mlstm_w_k, mlstm_w_v, mlstm_skip, mlstm_norm_g, mlstm_w_down, diff_w_qkv, diff_lq1, diff_lk1, diff_lq2, diff_lk2, diff_subln_g, diff_w_o):
    xp2, xs2 = x_prompt.reshape(ROWS_P, D), x_sample.reshape(ROWS_S, D)
    cvec = jnp.concatenate([c_ctx[None, :], c, jnp.zeros((5, D), F32)], axis=0)
    mod_all = _ada_mod(cvec, ada_w, ada_b)
    g2 = norm2_g.reshape(DEPTH, 1, D)

    def in_proj(xin, layer, w, n_cols, tn, extra_w=None, w_t=False, name="in_proj"):
        xrow = ("d", xin[0], xin[1], D, 0) if isinstance(xin, tuple) else ("u", xin, D, 0)
        return _fused_mm(rows=[xrow], consts=[norm1_g[layer][None, :]], mod=(mod_all, layer), w=w, k_dim=D,
                         n_cols=n_cols, tm=TM_IN, tn=tn, prologue=_pro_normmod, extra_w=extra_w, w_t=w_t, name=name)

    def out_proj(xin, layer, rows, consts, prologue, w, k_dim, name):
        return _fused_mm(rows=rows, consts=consts, w=w, k_dim=k_dim, n_cols=D, tm=TM_OUT, tn=TN_PROJ, prologue=prologue,
                         epilogue=_epi_residual(2), erows=[xin], emod=(mod_all, layer), out_dtype=F32, name=name)

    w0t = ssd_w_in[0].T
    p0, pdt = in_proj((xp2, xs2), 0, w0t, 3 * SSD_DI, TN_SSD_IN, extra_w=_pad_rows(w0t[3 * SSD_DI:], LANES), w_t=True,
                      name="ssd_in")
    dtb = _pad_cols(ssd_dt_bias[0].reshape(1, 2 * SSD_HEADS), LANES)
    alog = _pad_cols(ssd_A_log[0].reshape(1, 2 * SSD_HEADS), LANES)
    dl = jnp.repeat(ssd_D[0], SSD_P)[None, :]
    scan_args = (p0, pdt, ssd_conv_w[0], ssd_conv_b[0][None, :], dtb, alog, dl)
    yg_p, new_ssd = _ssd_scan(*scan_args, None, nb=N_PROMPT_SEQ, L=L_PROMPT, emit_state=True)
    (yg_s,) = _ssd_scan(*scan_args, state_ssd, nb=N_SAMPLE_SEQ, L=L_SAMPLE, emit_state=False)
    x = out_proj((xp2, xs2), 0, [("d", yg_p, yg_s, SSD_DI, 0)], [ssd_norm_g[0][None, :]], _pro_rms, ssd_w_out[0], SSD_DI,
                 "ssd_out")
    x = _mlp(x, 0, g2, mod_all, mlp_w1, mlp_w2)

    w_in = mla_w_in[0]
    kr0 = MLA_Q_RANK + MLA_KV_RANK
    half = MLA_ROPE // 2
    zk = jnp.zeros((D, MLA_NOPE), F32)
    zr = jnp.zeros((D, LANES - MLA_NOPE - MLA_ROPE), F32)
    w_kr = jnp.concatenate([zk, w_in[:, kr0:kr0 + MLA_ROPE], zr,
                            zk, w_in[:, kr0 + half:kr0 + MLA_ROPE], w_in[:, kr0:kr0 + half], zr], axis=1)
    p1, p1s = in_proj(x, 1, w_in, kr0, kr0, extra_w=w_kr, name="mla_in")
    wuq = mla_w_uq[0].reshape(MLA_Q_RANK, MLA_HEADS, MLA_NOPE + MLA_ROPE)
    zq = jnp.zeros((MLA_Q_RANK, MLA_HEADS, LANES - MLA_NOPE - MLA_ROPE), F32)
    zqn = jnp.zeros((MLA_Q_RANK, MLA_HEADS, MLA_NOPE), F32)
    wq = jnp.concatenate([wuq, zq], axis=-1).reshape(MLA_Q_RANK, MLA_HEADS * LANES).astype(BF16)
    wqs = jnp.concatenate([zqn, wuq[..., MLA_NOPE + half:], wuq[..., MLA_NOPE:MLA_NOPE + half], zq],
                          axis=-1).reshape(MLA_Q_RANK, MLA_HEADS * LANES).astype(BF16)
    wukv = mla_w_ukv[0].reshape(MLA_KV_RANK, MLA_HEADS, MLA_NOPE + MLA_V)
    zkv = jnp.zeros((MLA_KV_RANK, MLA_HEADS, MLA_NOPE), F32)
    wk = jnp.concatenate([wukv[..., :MLA_NOPE], zkv], axis=-1).reshape(MLA_KV_RANK, MLA_HEADS * LANES).astype(BF16)
    wv_own = wukv[..., MLA_NOPE:]
    odd = (jnp.arange(MLA_HEADS) % 2 == 1)[None, :, None]
    wv = jnp.where(odd, jnp.concatenate([zkv, wv_own], axis=-1), jnp.concatenate([wv_own, zkv], axis=-1))
    wv = wv.reshape(MLA_KV_RANK, MLA_HEADS * LANES).astype(BF16)
    c32, s32 = _rope_tables(MLA_ROPE)
    tz = jnp.zeros((L_SAMPLE, LANES - MLA_NOPE - MLA_ROPE), F32)
    cpad = jnp.concatenate([jnp.ones((L_SAMPLE, MLA_NOPE), F32), c32, tz], axis=1)
    spad = jnp.concatenate([jnp.zeros((L_SAMPLE, MLA_NOPE), F32), s32, tz], axis=1)
    ckr_pad = jnp.pad(cache_mla_krope, ((0, 0), (0, 0), (0, 0), (MLA_NOPE, LANES - MLA_NOPE - MLA_ROPE)))
    mla_w = (mla_q_norm_g[0][None, :], mla_kv_norm_g[0][None, :], wq, wqs, wk, wv)
    x, new_ckv, new_kr = _mla_attn(p1, p1s, *mla_w, None, x, mla_w_o[0], (mod_all, 1), nb=N_PROMPT_SEQ, L=L_PROMPT)
    (x,) = _mla_attn(p1, p1s, *mla_w, (cache_mla_ckv, ckr_pad, cpad, spad), x, mla_w_o[0], (mod_all, 1),
                     nb=N_SAMPLE_SEQ, L=L_SAMPLE)
    x = _mlp(x, 1, g2, mod_all, mlp_w1, mlp_w2)

    w2t = mlstm_w_up[0].T
    p2, gates = in_proj(x, 2, w2t, 2 * ML_DI, TN_IN, extra_w=_pad_rows(w2t[2 * ML_DI:], LANES), w_t=True, name="mlstm_up")
    q, xc = _mlstm_q(p2, mlstm_conv_w[0], mlstm_conv_b[0][None, :], mlstm_w_q[0])
    k = _fused_mm(rows=[("u", xc, ML_DI, 0)], w=mlstm_w_k[0], k_dim=ML_DI, n_cols=ML_HEADS * ML_DQK, tm=TM_WIDE,
                  tn=TN_PROJ, epilogue=lambda acc, e, m, rs: acc * (ML_DQK ** -0.5), name="mlstm_k")
    v = _fused_mm(rows=[("u", p2, ML_DI, 0)], w=mlstm_w_v[0], k_dim=ML_DI, n_cols=ML_DI, tm=TM_WIDE, tn=TN_PROJ,
                  name="mlstm_v")
    gb = _pad_cols(mlstm_gate_b[0].reshape(1, 4 * ML_HEADS), LANES)
    ng = mlstm_norm_g[0][None, :]
    n0 = _pad_cols(state_mlstm_n[:, 0].reshape(N_SAMPLE_SEQ * 2 * ML_HEADS, ML_DQK), LANES).reshape(
        N_SAMPLE_SEQ, 2, ML_HEADS, LANES)
    m0 = jnp.broadcast_to(state_mlstm_m[:, 0][..., None], (N_SAMPLE_SEQ, 2, ML_HEADS, LANES))
    hn_p, new_c, new_n, new_m = _mlstm_scan(q, k, v, gates, gb, ng, None, nb=N_PROMPT_SEQ, L=L_PROMPT, hp=HP_P,
                                            emit_state=True)
    (hn_s,) = _mlstm_scan(q, k, v, gates, gb, ng, (state_mlstm_C, n0, m0), nb=N_SAMPLE_SEQ, L=L_SAMPLE, hp=HP_S,
                          emit_state=False)
    x = out_proj(x, 2, [("d", hn_p, hn_s, ML_DI, 0), ("u", xc, ML_DI, 0), ("u", p2, ML_DI, 1)],
                 [mlstm_skip[0][None, :]], _pro_mlstm_gate, mlstm_w_down[0], ML_DI, "mlstm_down")
    x = _mlp(x, 2, g2, mod_all, mlp_w1, mlp_w2)

    lam_init = 0.8 - 0.6 * math.exp(-0.3 * 3)
    p3 = in_proj(x, 3, diff_w_qkv[0], 3 * D, TN_IN, name="diff_qkv")
    c64, s64 = _rope_tables(DF_D)
    c128 = jnp.concatenate([c64, c64], axis=-1)
    s128 = jnp.concatenate([s64, s64], axis=-1)
    dparams = (diff_lq1, diff_lk1, diff_lq2, diff_lk2, diff_subln_g)
    x, new_dk, new_dv = _diff_attn(p3, *dparams, None, x, diff_w_o[0], (mod_all, 3), nb=N_PROMPT_SEQ, L=L_PROMPT,
                                   lam_init=lam_init)
    ctx = (cache_diff_k.reshape(N_SAMPLE_SEQ, 1, PAST, D), cache_diff_v.reshape(N_SAMPLE_SEQ, 1, PAST, D), c128, s128)
    (x,) = _diff_attn(p3, *dparams, ctx, x, diff_w_o[0], (mod_all, 3), nb=N_SAMPLE_SEQ, L=L_SAMPLE, lam_init=lam_init)
    y_prompt, y_sample = _mlp(x, 3, g2, mod_all, mlp_w1, mlp_w2, final_g=final_g[None, :])
    y_prompt = y_prompt.reshape(N_PROMPT_SEQ, L_PROMPT, D)
    y_sample = y_sample.reshape(N_SAMPLE_SEQ, L_SAMPLE, D)
    return (y_prompt, y_sample, new_ssd, new_ckv, new_kr, new_c,
            new_n[None].reshape(N_PROMPT_SEQ, 1, 2, ML_HEADS, ML_DQK),
            new_m[..., 0].reshape(N_PROMPT_SEQ, 1, 2, ML_HEADS),
            new_dk, new_dv)
```

```python
import functools
import math

import jax
import jax.numpy as jnp
from jax import lax
from jax.experimental import pallas as pl
from jax.experimental.pallas import tpu as pltpu

F32 = jnp.float32
BF16 = jnp.bfloat16

D = 1024
DEPTH = 4
D_FF = 4 * D
EPS = 1e-6
ROPE_BASE = 10000.0
CONV_W = 5
GRID_W = 64

N_PROMPT_SEQ = 32
L_PROMPT = 256
N_SAMPLE_SEQ = 2
L_SAMPLE = 1024
PAST = 256
ROWS_P = N_PROMPT_SEQ * L_PROMPT
ROWS_S = N_SAMPLE_SEQ * L_SAMPLE
ROWS = ROWS_P + ROWS_S

SSD_DI = 2 * D
SSD_HEADS = 32
SSD_P = 64
SSD_GROUPS = 8
SSD_N = 128
SSD_HPG = SSD_HEADS // SSD_GROUPS

MLA_HEADS = 16
MLA_Q_RANK = 512
MLA_KV_RANK = 256
MLA_NOPE = 64
MLA_ROPE = 32
MLA_V = 64

ML_DI = 2 * D
ML_HEADS = 8
ML_DQK = 128
ML_DV = 256

DF_HEADS = 8
DF_D = 64

LANES = 128
VMEM_LIMIT = 56 * 1024 * 1024

TM_IN = 2048
TN_IN = 1024
TN_SSD_IN = 512
TM_OUT = 1024
TM_WIDE = 2048
TN_PROJ = 512
TM_MLP = 1024
TF_MLP = 1024
TM_CONVQ = 1024
TN_ADA = 1536


def _cparams(sem):
    return pltpu.CompilerParams(dimension_semantics=sem, vmem_limit_bytes=VMEM_LIMIT)


def _silu(x):
    return x * jax.nn.sigmoid(x)


def _softplus(x):
    return jnp.maximum(x, 0.0) + jnp.log1p(jnp.exp(-jnp.abs(x)))


def _rms(x, g):
    r = lax.rsqrt(jnp.mean(x * x, axis=-1, keepdims=True) + EPS)
    return (x * r) * g


def _dot(a, b):
    return jnp.dot(a, b, preferred_element_type=F32)


def _dot_nt(a, b):
    return lax.dot_general(a, b, (((1,), (1,)), ((), ())), preferred_element_type=F32)


def _dot_tn(a, b):
    return lax.dot_general(a, b, (((0,), (0,)), ((), ())), preferred_element_type=F32)


MOD_ROWS = 1024


def _group_of_tile(i, tm, sub=0):
    row0 = i * tm + sub * MOD_ROWS
    return jnp.where(row0 < ROWS_P, 0, 1 + (row0 - ROWS_P) // L_SAMPLE)


def _ada_body(c_ref, w_ref, b_ref, o_ref):
    s = _silu(c_ref[...]).astype(BF16)
    o_ref[...] = _dot(s, w_ref[...].astype(BF16)) + b_ref[...]


def _ada_mod(cvec, ada_w, ada_b):
    tn = TN_ADA
    out = pl.pallas_call(
        _ada_body,
        grid=(DEPTH, 6 * D // tn),
        in_specs=[
            pl.BlockSpec((8, D), lambda l, j: (0, 0)),
            pl.BlockSpec((None, D, tn), lambda l, j: (l, 0, j)),
            pl.BlockSpec((None, 1, tn), lambda l, j: (l, 0, j)),
        ],
        out_specs=pl.BlockSpec((None, 8, tn), lambda l, j: (l, 0, j)),
        out_shape=jax.ShapeDtypeStruct((DEPTH, 8, 6 * D), F32),
        compiler_params=_cparams(("parallel", "parallel")),
        name="ada_mod",
    )(cvec, ada_w, ada_b.reshape(DEPTH, 1, 6 * D))
    return out[:, :3].reshape(DEPTH, 3, 6, D)


class _Pro:
    def __init__(self, rows, consts, mod, i, hs, lhs_out):
        self.rows, self.consts, self.mod, self.i = rows, consts, mod, i
        self._hs, self._lhs_out = hs, lhs_out

    def emit(self, val, rows=slice(None), cols=slice(None)):
        vb = val.astype(BF16)
        self._hs[rows, cols] = vb
        if self._lhs_out is not None:
            self._lhs_out[rows, cols] = vb


ROW_CHUNK = 512


def _fused_mm(*, rows, w, k_dim, n_cols, tm, tn, prologue=None, consts=(), mod=None,
              epilogue=None, erows=(), emod=None, w_col0=0, w_t=False, emit_lhs=False, extra_w=None, out_dtype=BF16,
              name):
    npt = ROWS_P // tm
    grid = (ROWS // tm, n_cols // tn)
    has_dual = any(r[0] == "d" for r in rows)
    dual_epi = any(isinstance(e, tuple) for e in erows)
    if prologue is None:
        assert len(rows) == 1 and rows[0][0] == "u" and not emit_lhs

    in_specs, args = [], []
    for r in rows:
        if r[0] == "u":
            _, arr, width, cb = r
            in_specs.append(pl.BlockSpec((tm, width), lambda i, j, cb=cb: (i, cb)))
            args.append(arr)
        else:
            _, arr_p, arr_s, width, cb = r
            in_specs.append(pl.BlockSpec((tm, width), lambda i, j, cb=cb: (jnp.minimum(i, npt - 1), cb)))
            in_specs.append(pl.BlockSpec((tm, width), lambda i, j, cb=cb: (jnp.maximum(i - npt, 0), cb)))
            args += [arr_p, arr_s]
    for c in consts:
        in_specs.append(pl.BlockSpec(c.shape, lambda i, j, nd=c.ndim: (0,) * nd))
        args.append(c)
    n_sub = max(tm // MOD_ROWS, 1)
    if mod is not None:
        mod_arr, mod_layer = mod
        for s in range(n_sub):
            in_specs.append(pl.BlockSpec((None, None, 6, D),
                                         lambda i, j, s=s: (mod_layer, _group_of_tile(i, tm, s), 0, 0)))
            args.append(mod_arr)
    if w_t:
        in_specs.append(pl.BlockSpec((tn, k_dim), lambda i, j: (w_col0 // tn + j, 0)))
    else:
        in_specs.append(pl.BlockSpec((k_dim, tn), lambda i, j: (0, w_col0 // tn + j)))
    args.append(w)
    if extra_w is not None:
        in_specs.append(pl.BlockSpec(extra_w.shape, lambda i, j: (0, 0)))
        args.append(extra_w)
    for e in erows:
        if isinstance(e, tuple):
            in_specs.append(pl.BlockSpec((tm, tn), lambda i, j: (jnp.minimum(i, npt - 1), j)))
            in_specs.append(pl.BlockSpec((tm, tn), lambda i, j: (jnp.maximum(i - npt, 0), j)))
            args += list(e)
        else:
            in_specs.append(pl.BlockSpec((tm, tn), lambda i, j: (i, j)))
            args.append(e)
    if emod is not None:
        emod_arr, emod_layer = emod
        in_specs.append(pl.BlockSpec((None, None, 6, tn), lambda i, j: (emod_layer, _group_of_tile(i, tm), 0, j)))
        args.append(emod_arr)

    out_specs = [pl.BlockSpec((tm, tn), lambda i, j: (i, j))]
    out_shape = [jax.ShapeDtypeStruct((ROWS, n_cols), out_dtype)]
    if emit_lhs:
        out_specs.append(pl.BlockSpec((tm, k_dim), lambda i, j: (i, 0)))
        out_shape.append(jax.ShapeDtypeStruct((ROWS, k_dim), BF16))
    if extra_w is not None:
        n_extra = extra_w.shape[0] if w_t else extra_w.shape[1]
        out_specs.append(pl.BlockSpec((tm, n_extra), lambda i, j: (i, 0)))
        out_shape.append(jax.ShapeDtypeStruct((ROWS, n_extra), F32))
    scratch = [] if prologue is None else [pltpu.VMEM((tm, k_dim), BF16)]
    chunks = [slice(r, r + ROW_CHUNK) for r in range(0, tm, ROW_CHUNK)]

    def body(*refs):
        it = iter(refs)
        row_refs = [(next(it),) if r[0] == "u" else (next(it), next(it)) for r in rows]
        const_refs = [next(it) for _ in consts]
        mod_ref = [next(it) for _ in range(n_sub)] if mod is not None else None
        w_ref = next(it)
        extra_ref = next(it) if extra_w is not None else None
        erow_refs = [(next(it), next(it)) if isinstance(e, tuple) else (next(it),) for e in erows]
        emod_ref = next(it) if emod is not None else None
        out_ref = next(it)
        lhs_out = next(it) if emit_lhs else None
        extra_out = next(it) if extra_w is not None else None
        hs = next(it) if prologue is not None else None
        i = pl.program_id(0)
        j = pl.program_id(1)

        def compute(first, use_prompt=True):
            chosen = [rr[0] if (len(rr) == 1 or use_prompt) else rr[1] for rr in row_refs]
            echosen = [er[0] if (len(er) == 1 or use_prompt) else er[1] for er in erow_refs]
            p = _Pro(chosen, const_refs, mod_ref, i, hs, lhs_out)
            wb = w_ref[...].astype(BF16)
            for rs in chunks:
                if first:
                    prologue(p, rs)
                lhs = chosen[0][rs, :] if prologue is None else hs[rs, :]
                acc = _dot_nt(lhs, wb) if w_t else _dot(lhs, wb)
                if epilogue is not None:
                    acc = epilogue(acc, echosen, emod_ref, rs)
                out_ref[rs, :] = acc.astype(out_dtype)
                if first and extra_w is not None:
                    eb = extra_ref[...].astype(BF16)
                    extra_out[rs, :] = _dot_nt(lhs, eb) if w_t else _dot(lhs, eb)

        if prologue is None:
            if extra_w is None:
                compute(False)
            else:
                pl.when(j == 0)(lambda: compute(True))
                pl.when(j > 0)(lambda: compute(False))
        else:
            if has_dual or dual_epi:
                pl.when(jnp.logical_and(j == 0, i < npt))(lambda: compute(True, True))
                pl.when(jnp.logical_and(j == 0, i >= npt))(lambda: compute(True, False))
            else:
                pl.when(j == 0)(lambda: compute(True))
            if dual_epi:
                pl.when(jnp.logical_and(j > 0, i < npt))(lambda: compute(False, True))
                pl.when(jnp.logical_and(j > 0, i >= npt))(lambda: compute(False, False))
            else:
                pl.when(j > 0)(lambda: compute(False))

    res = pl.pallas_call(
        body,
        grid=grid,
        in_specs=in_specs,
        out_specs=out_specs,
        out_shape=out_shape,
        scratch_shapes=scratch,
        compiler_params=_cparams(("parallel", "arbitrary")),
        name=name,
    )(*args)
    return res if len(res) > 1 else res[0]


def _pro_normmod(p, rs):
    m = p.mod[rs.start // MOD_ROWS]
    h = _rms(p.rows[0][rs, :], p.consts[0][...]) * (1.0 + m[1:2, :]) + m[0:1, :]
    p.emit(h, rs)


def _pro_rms(p, rs):
    p.emit(_rms(p.rows[0][rs, :].astype(F32), p.consts[0][...]), rs)


def _epi_residual(gate_row):
    def epi(acc, erows, emod, rs):
        return erows[0][rs, :] + emod[gate_row:gate_row + 1, :] * acc
    return epi


TFC_MLP = 512


def _mlp_body(*refs, layer, final):
    it = iter(refs)
    x_ref, g_ref, mod_ref, w1_hbm, w2_hbm = (next(it) for _ in range(5))
    fg_ref = next(it) if final else None
    outs = [next(it), next(it)] if final else [next(it)]
    hs, w1b, w2b, st1, st2, sem = (next(it) for _ in range(6))
    i = pl.program_id(0)
    tm = x_ref.shape[0]
    nfc = D_FF // TFC_MLP
    chunks = [slice(r, r + ROW_CHUNK) for r in range(0, tm, ROW_CHUNK)]

    def copies(f):
        slot = f % 2
        return (pltpu.make_async_copy(w1_hbm.at[layer, :, pl.ds(f * TFC_MLP, TFC_MLP)], st1.at[slot], sem.at[0, slot]),
                pltpu.make_async_copy(w2_hbm.at[layer, pl.ds(f * TFC_MLP, TFC_MLP), :], st2.at[slot], sem.at[1, slot]))

    def start(f):
        for cp in copies(f):
            cp.start()

    def run(load):
        if load:
            start(0)
            start(1)
        for rs in chunks:
            h = _rms(x_ref[rs, :], g_ref[...]) * (1.0 + mod_ref[4:5, :]) + mod_ref[3:4, :]
            hs[rs, :] = h.astype(BF16)
        tots = [None] * len(chunks)
        for f in range(nfc):
            fc = slice(f * TFC_MLP, (f + 1) * TFC_MLP)
            if load:
                for cp in copies(f):
                    cp.wait()
                w1b[:, fc] = st1[f % 2].astype(BF16)
                w2b[fc, :] = st2[f % 2].astype(BF16)
                if f + 2 < nfc:
                    start(f + 2)
            for k, rs in enumerate(chunks):
                u = jnp.square(jnp.maximum(_dot(hs[rs, :], w1b[:, fc]), 0.0))
                part = _dot(u.astype(BF16), w2b[fc, :])
                tots[k] = part if tots[k] is None else tots[k] + part
        for k, rs in enumerate(chunks):
            y = x_ref[rs, :] + mod_ref[5:6, :] * tots[k]
            if not final:
                outs[0][rs, :] = y
            else:
                y = _rms(y, fg_ref[...])
                npt = ROWS_P // tm

                @pl.when(i < npt)
                def _():
                    outs[0][rs, :] = y

                @pl.when(i >= npt)
                def _():
                    outs[1][rs, :] = y

    pl.when(i == 0)(lambda: run(True))
    pl.when(i > 0)(lambda: run(False))


def _mlp(x, layer, g, mod, w1, w2, final_g=None):
    final = final_g is not None
    tm = TM_MLP // 2
    npt = ROWS_P // tm
    in_specs = [
        pl.BlockSpec((tm, D), lambda i: (i, 0)),
        pl.BlockSpec((None, 1, D), lambda i: (layer, 0, 0)),
        pl.BlockSpec((None, None, 6, D), lambda i: (layer, _group_of_tile(i, tm), 0, 0)),
        pl.BlockSpec(memory_space=pl.ANY),
        pl.BlockSpec(memory_space=pl.ANY),
    ]
    args = [x, g, mod, w1, w2]
    if final:
        in_specs.append(pl.BlockSpec((1, D), lambda i: (0, 0)))
        args.append(final_g)
        out_specs = [pl.BlockSpec((tm, D), lambda i: (jnp.minimum(i, npt - 1), 0)),
                     pl.BlockSpec((tm, D), lambda i: (jnp.maximum(i - npt, 0), 0))]
        out_shape = [jax.ShapeDtypeStruct((ROWS_P, D), F32), jax.ShapeDtypeStruct((ROWS_S, D), F32)]
    else:
        out_specs = pl.BlockSpec((tm, D), lambda i: (i, 0))
        out_shape = jax.ShapeDtypeStruct((ROWS, D), F32)
    return pl.pallas_call(
        functools.partial(_mlp_body, layer=layer, final=final),
        grid=(ROWS // tm,),
        in_specs=in_specs,
        out_specs=out_specs,
        out_shape=out_shape,
        scratch_shapes=[
            pltpu.VMEM((tm, D), BF16),
            pltpu.VMEM((D, D_FF), BF16),
            pltpu.VMEM((D_FF, D), BF16),
            pltpu.VMEM((2, D, TFC_MLP), F32),
            pltpu.VMEM((2, TFC_MLP, D), F32),
            pltpu.SemaphoreType.DMA((2, 2)),
        ],
        compiler_params=_cparams(("arbitrary",)),
        name="mlp_final" if final else "mlp",
    )(*args)


SCAN_T = 256


def _seg_cumsums(a):
    n = a.shape[0]
    ii = lax.broadcasted_iota(jnp.int32, (SCAN_T, SCAN_T), 0)
    jj = lax.broadcasted_iota(jnp.int32, (SCAN_T, SCAN_T), 1)
    lower = jnp.where(ii >= jj, 1.0, 0.0).astype(BF16)
    upper = jnp.where(ii <= jj, 1.0, 0.0).astype(BF16)
    hi = a.astype(BF16)
    rest = a - hi.astype(F32)
    mid = rest.astype(BF16)
    lo = (rest - mid.astype(F32)).astype(BF16)
    parts = jnp.concatenate([hi, mid, lo], axis=1)
    pre, suf = [], []
    for c in range(n // SCAN_T):
        pc = parts[SCAN_T * c:SCAN_T * (c + 1), :]
        for tri, out in ((lower, pre), (upper, suf)):
            s3 = _dot(tri, pc)
            out.append(s3[:, :LANES] + s3[:, LANES:2 * LANES] + s3[:, 2 * LANES:])
    return jnp.concatenate(pre, axis=0), jnp.concatenate(suf, axis=0)


LOG2E = 1.4426950408889634


def _ssd_body(*refs, L, has_s0, emit_state):
    it = iter(refs)
    z_ref, x_ref, b_ref, c_ref, dt_ref = (next(it) for _ in range(5))
    cwx, cbx, cwb, cbb, cwc, cbc = (next(it) for _ in range(6))
    dtb_ref, alog_ref, dl_ref = next(it), next(it), next(it)
    s0_ref = next(it) if has_s0 else None
    y_ref = next(it)
    st_out = next(it) if emit_state else None
    padx, padb, xa, ba, ca, cbs, cumc, crp, yacc, st = (next(it) for _ in range(10))

    T = SCAN_T
    nc = L // T
    g0 = pl.program_id(0)

    def conv(in_ref, w_ref, bias_ref, pad):
        width = in_ref.shape[1]
        pad[0:8, :] = jnp.zeros((8, width), F32)
        pad[L + 8:L + 16, :] = jnp.zeros((8, width), F32)
        pad[8:L + 8, :] = in_ref[...].astype(F32)
        acc = bias_ref[...] + pad[6:6 + L, :] * w_ref[0:1, :]
        for k in range(1, CONV_W):
            acc = acc + pad[6 + k:6 + k + L, :] * w_ref[k:k + 1, :]
        return _silu(acc)

    xa[...] = conv(x_ref, cwx, cbx, padx)
    ba[...] = conv(b_ref, cwb, cbb, padb)
    ca[...] = conv(c_ref, cwc, cbc, padb)

    dt_all = _softplus(dt_ref[...] + dtb_ref[...])
    a_all = dt_all * (-jnp.exp(alog_ref[...]))
    shift = jnp.where(g0 == 0, 0, LANES - SSD_HPG * g0)
    l2dt = jnp.log2(pltpu.roll(dt_all, shift, 1))
    ar = pltpu.roll(a_all, shift, 1) * LOG2E
    cum_f, cum_b = _seg_cumsums(ar)
    cumc[0] = cum_f
    cumc[1] = cum_b
    crp[0] = (cum_f - l2dt).T
    crp[1] = (cum_b - l2dt).T

    yacc[...] = xa[...] * dl_ref[...]
    if has_s0:
        st[...] = s0_ref[...]

    ii = lax.broadcasted_iota(jnp.int32, (T, T), 0)
    jj = lax.broadcasted_iota(jnp.int32, (T, T), 1)

    def chunk(c, d):
        rows = slice(T * c, T * (c + 1))
        first = c == (0 if d == 0 else nc - 1)
        last = c == (nc - 1 if d == 0 else 0)
        zero_state = first and not has_s0
        need_state = emit_state or not last
        mask = (ii >= jj) if d == 0 else (ii <= jj)
        end = T * c + (T - 1 if d == 0 else 0)
        xav = xa[rows, :]
        xab = xav.astype(BF16)
        cab = ca[rows, :].astype(BF16)
        if need_state:
            xat = xav.T.astype(BF16)
            bat = ba[rows, :].T
        for r in range(SSD_HPG):
            lane = SSD_HEADS * d + r
            hs = slice(SSD_P * r, SSD_P * (r + 1))
            cc = cumc[d, rows, lane:lane + 1]
            cr = crp[d, lane:lane + 1, rows]
            dec = jnp.where(mask, jnp.exp2(cc - cr), 0.0)
            y = _dot((cbs[c] * dec).astype(BF16), xab[:, hs])
            if not zero_state:
                y = y + _dot_nt(cab, st[d, r].astype(BF16)) * jnp.exp2(cc)
            yacc[rows, hs] += y
            if need_state:
                tot = cumc[d, end:end + 1, lane:lane + 1]
                upd = _dot_nt(xat[hs, :], (bat * jnp.exp2(tot - cr)).astype(BF16))
                st[d, r] = upd if zero_state else jnp.exp2(tot) * st[d, r] + upd

    for c in range(nc):
        rows = slice(T * c, T * (c + 1))
        cbs[c] = _dot_nt(ca[rows, :].astype(BF16), ba[rows, :].astype(BF16))
    for c in range(nc):
        chunk(c, 0)
        chunk(nc - 1 - c, 1)

    y_ref[...] = (yacc[...] * _silu(z_ref[...].astype(F32))).astype(BF16)
    if emit_state:
        st_out[...] = st[...]


def _ssd_scan(p0, pdt, conv_w, conv_b, dtb, alog, dl, s0, *, nb, L, emit_state):
    rb0 = 0 if s0 is None else ROWS_P // L
    gw = SSD_HPG * SSD_P
    nc = L // SCAN_T
    x0 = SSD_DI // gw
    b0 = 2 * SSD_DI // SSD_N
    c0 = b0 + SSD_GROUPS
    wb0 = SSD_DI // SSD_N
    wc0 = wb0 + SSD_GROUPS
    in_specs = [
        pl.BlockSpec((L, gw), lambda g, b: (rb0 + b, g)),
        pl.BlockSpec((L, gw), lambda g, b: (rb0 + b, x0 + g)),
        pl.BlockSpec((L, SSD_N), lambda g, b: (rb0 + b, b0 + g)),
        pl.BlockSpec((L, SSD_N), lambda g, b: (rb0 + b, c0 + g)),
        pl.BlockSpec((L, LANES), lambda g, b: (rb0 + b, 0)),
        pl.BlockSpec((CONV_W, gw), lambda g, b: (0, g)),
        pl.BlockSpec((1, gw), lambda g, b: (0, g)),
        pl.BlockSpec((CONV_W, SSD_N), lambda g, b: (0, wb0 + g)),
        pl.BlockSpec((1, SSD_N), lambda g, b: (0, wb0 + g)),
        pl.BlockSpec((CONV_W, SSD_N), lambda g, b: (0, wc0 + g)),
        pl.BlockSpec((1, SSD_N), lambda g, b: (0, wc0 + g)),
        pl.BlockSpec((1, LANES), lambda g, b: (0, 0)),
        pl.BlockSpec((1, LANES), lambda g, b: (0, 0)),
        pl.BlockSpec((1, gw), lambda g, b: (0, g)),
    ]
    args = [p0, p0, p0, p0, pdt, conv_w, conv_b, conv_w, conv_b, conv_w, conv_b, dtb, alog, dl]
    state_spec = pl.BlockSpec((None, None, 2, SSD_HPG, SSD_P, SSD_N), lambda g, b: (b, 0, 0, g, 0, 0))
    if s0 is not None:
        in_specs.append(state_spec)
        args.append(s0)
    out_specs = [pl.BlockSpec((L, gw), lambda g, b: (b, g))]
    out_shape = [jax.ShapeDtypeStruct((nb * L, SSD_DI), BF16)]
    if emit_state:
        out_specs.append(state_spec)
        out_shape.append(jax.ShapeDtypeStruct((nb, 1, 2, SSD_HEADS, SSD_P, SSD_N), F32))
    scratch = [
        pltpu.VMEM((L + 16, gw), F32),
        pltpu.VMEM((L + 16, SSD_N), F32),
        pltpu.VMEM((L, gw), F32),
        pltpu.VMEM((L, SSD_N), F32),
        pltpu.VMEM((L, SSD_N), F32),
        pltpu.VMEM((nc, SCAN_T, SCAN_T), F32),
        pltpu.VMEM((2, L, LANES), F32),
        pltpu.VMEM((2, LANES, L), F32),
        pltpu.VMEM((L, gw), F32),
        pltpu.VMEM((2, SSD_HPG, SSD_P, SSD_N), F32),
    ]
    res = pl.pallas_call(
        functools.partial(_ssd_body, L=L, has_s0=s0 is not None, emit_state=emit_state),
        grid=(SSD_GROUPS, nb),
        in_specs=in_specs,
        out_specs=out_specs,
        out_shape=out_shape,
        scratch_shapes=scratch,
        compiler_params=_cparams(("parallel", "parallel")),
        name="ssd_scan_p" if s0 is None else "ssd_scan_s",
    )(*args)
    return res


TQ = 256
HP_P = 4
HP_S = 2


def _mla_body(*refs, L, sample):
    it = iter(refs)
    cq_ref, ckv_ref, krs_ref = next(it), next(it), next(it)
    gq_ref, gkv_ref = next(it), next(it)
    wq, wqs, wk, wv = (next(it) for _ in range(4))
    if sample:
        cckv_ref, ckr_ref, cq_t, sq_t, ck_t, sk_t = (next(it) for _ in range(6))
    x_ref, wo_ref, mod_ref = next(it), next(it), next(it)
    x_out = next(it)
    if not sample:
        ckv_out, kr_out = next(it), next(it)
    kk, vv, o_scr = next(it), next(it), next(it)
    qt = pl.program_id(1)
    scale = (MLA_NOPE + MLA_ROPE) ** -0.5
    hb = 4 * LANES

    @pl.when(qt == 0)
    def _():
        ckv = _rms(ckv_ref[...].astype(F32), gkv_ref[...])
        kr_own = krs_ref[:, 0:LANES]
        if sample:
            keys = jnp.concatenate([cckv_ref[...], ckv], axis=0)
            kr_own = kr_own * ck_t[...] + krs_ref[:, LANES:2 * LANES] * sk_t[...]
            kr_all = jnp.concatenate([ckr_ref[...], kr_own], axis=0)
        else:
            ckv_out[...] = ckv
            kr_out[...] = kr_own[:, MLA_NOPE:MLA_NOPE + MLA_ROPE]
            keys = ckv
            kr_all = kr_own
        kb = keys.astype(BF16)
        for blk in range(MLA_HEADS * LANES // hb):
            cols = slice(hb * blk, hb * (blk + 1))
            kn = _dot(kb, wk[:, cols])
            kk[:, cols] = (kn + jnp.concatenate([kr_all] * 4, axis=1)).astype(BF16)
            vv[:, cols] = _dot(kb, wv[:, cols]).astype(BF16)

    cq = _rms(cq_ref[...].astype(F32), gq_ref[...]).astype(BF16)
    for blk in range(MLA_HEADS * LANES // hb):
        qa = _dot(cq, wq[:, hb * blk:hb * (blk + 1)])
        if sample:
            qs = _dot(cq, wqs[:, hb * blk:hb * (blk + 1)])
        pair = None
        for hh in range(4):
            h = 4 * blk + hh
            cols = slice(LANES * h, LANES * (h + 1))
            qh = qa[:, LANES * hh:LANES * (hh + 1)]
            if sample:
                qh = qh * cq_t[...] + qs[:, LANES * hh:LANES * (hh + 1)] * sq_t[...]
            s = _dot_nt((qh * scale).astype(BF16), kk[:, cols])
            e = jnp.exp(s - jnp.max(s, axis=1, keepdims=True))
            o = _dot(e.astype(BF16), vv[:, cols]) / jnp.sum(e, axis=1, keepdims=True)
            if h % 2 == 0:
                pair = o
            else:
                o_scr[:, LANES * (h // 2):LANES * (h // 2 + 1)] = (pair + o).astype(BF16)
    x_out[...] = x_ref[...] + mod_ref[2:3, :] * _dot(o_scr[...], wo_ref[...].astype(BF16))


def _mla_attn(p1, p1s, gq, gkv, wq, wqs, wk, wv, ctx, x, wo, mod, *, nb, L):
    sample = ctx is not None
    nq = L // TQ
    tk = L + (PAST if sample else 0)
    rbq0 = ROWS_P // TQ if sample else 0
    rbs0 = ROWS_P // L if sample else 0
    ckv_blk = MLA_Q_RANK // MLA_KV_RANK

    def const(a):
        return pl.BlockSpec(a.shape, lambda b, q, nd=a.ndim: (0,) * nd)

    in_specs = [
        pl.BlockSpec((TQ, MLA_Q_RANK), lambda b, q: (rbq0 + b * nq + q, 0)),
        pl.BlockSpec((L, MLA_KV_RANK), lambda b, q: (rbs0 + b, ckv_blk)),
        pl.BlockSpec((L, 2 * LANES), lambda b, q: (rbs0 + b, 0)),
        const(gq), const(gkv), const(wq), const(wqs), const(wk), const(wv),
    ]
    args = [p1, p1, p1s, gq, gkv, wq, wqs, wk, wv]
    if sample:
        cckv, ckr, cpad, spad = ctx
        in_specs += [
            pl.BlockSpec((None, None, PAST, MLA_KV_RANK), lambda b, q: (b, 0, 0, 0)),
            pl.BlockSpec((None, None, PAST, LANES), lambda b, q: (b, 0, 0, 0)),
            pl.BlockSpec((TQ, LANES), lambda b, q: (q, 0)),
            pl.BlockSpec((TQ, LANES), lambda b, q: (q, 0)),
            const(cpad), const(spad),
        ]
        args += [cckv, ckr, cpad, spad, cpad, spad]
    mod_arr, mod_layer = mod
    x_spec = pl.BlockSpec((TQ, D), lambda b, q: (rbq0 + b * nq + q, 0))
    in_specs += [
        x_spec, const(wo),
        pl.BlockSpec((None, None, 6, D), lambda b, q: (mod_layer, (1 + b) if sample else 0, 0, 0)),
    ]
    x_index = len(args)
    args += [x, wo, mod_arr]
    out_specs = [x_spec]
    out_shape = [jax.ShapeDtypeStruct((ROWS, D), F32)]
    if not sample:
        out_specs += [
            pl.BlockSpec((None, None, L, MLA_KV_RANK), lambda b, q: (b, 0, 0, 0)),
            pl.BlockSpec((None, None, L, MLA_ROPE), lambda b, q: (b, 0, 0, 0)),
        ]
        out_shape += [
            jax.ShapeDtypeStruct((nb, 1, L, MLA_KV_RANK), F32),
            jax.ShapeDtypeStruct((nb, 1, L, MLA_ROPE), F32),
        ]
    scratch = [
        pltpu.VMEM((tk, MLA_HEADS * LANES), BF16),
        pltpu.VMEM((tk, MLA_HEADS * LANES), BF16),
        pltpu.VMEM((TQ, MLA_HEADS * MLA_V), BF16),
    ]
    return pl.pallas_call(
        functools.partial(_mla_body, L=L, sample=sample),
        grid=(nb, nq),
        in_specs=in_specs,
        out_specs=out_specs,
        out_shape=out_shape,
        scratch_shapes=scratch,
        input_output_aliases={x_index: 0},
        compiler_params=_cparams(("parallel", "arbitrary")),
        name="mla_attn_s" if sample else "mla_attn_p",
    )(*args)


def _log_sigmoid(x):
    return -_softplus(-x)


def _mlstm_body(*refs, L, hp, has_s0, emit_state):
    it = iter(refs)
    q_ref, k_ref, v_ref, g_ref, gb_ref, ng_ref = (next(it) for _ in range(6))
    if has_s0:
        c0_ref, n0_ref, m0_ref = next(it), next(it), next(it)
    h_ref = next(it)
    if emit_state:
        c_out, n_out, m_out = next(it), next(it), next(it)
    gsc, gtr, kq, vts, kts, hacc, cst, nst, mst = (next(it) for _ in range(9))

    T = SCAN_T
    nc = L // T
    h0 = hp * pl.program_id(1)

    gts = g_ref[...] + gb_ref[...]
    gr = pltpu.roll(gts, jnp.where(h0 == 0, 0, LANES - h0), 1)
    b_f, b_b = _seg_cumsums(_log_sigmoid(gr))
    gsc[0] = gr
    gsc[1] = b_f
    gsc[2] = b_b
    gtr[0] = gr.T
    gtr[1] = b_f.T
    gtr[2] = b_b.T

    jj = lax.broadcasted_iota(jnp.int32, (T, T), 0)
    ii = lax.broadcasted_iota(jnp.int32, (T, T), 1)

    def chunk(hh, c, d):
        rows = slice(T * c, T * (c + 1))
        qcols = slice(ML_DQK * hh, ML_DQK * (hh + 1))
        first = c == (0 if d == 0 else nc - 1)
        last = c == (nc - 1 if d == 0 else 0)
        zero_state = first and not has_s0
        li = 2 * ML_HEADS * d + hh
        lb = li + ML_HEADS
        b_row = gtr[1 + d, lb:lb + 1, rows]
        logi_row = gtr[0, li:li + 1, rows]
        cj = gsc[1 + d, rows, lb:lb + 1] - gsc[0, rows, li:li + 1]
        mask = (jj <= ii) if d == 0 else (jj >= ii)
        dlog = jnp.where(mask, b_row - cj, -jnp.inf)
        m_prev = jnp.zeros((1, 1), F32) if zero_state else mst[hh, d][:, 0:1]
        inter = b_row + m_prev
        mcomb = jnp.maximum(inter, jnp.max(dlog, axis=0, keepdims=True))
        s = kq[hh, c] * jnp.exp(dlog - mcomb)
        vt = vts[hh, c]
        num = _dot(vt, s.astype(BF16))
        den = jnp.sum(s, axis=0, keepdims=True)
        if not zero_state:
            iw = jnp.exp(inter - mcomb)
            qt = q_ref[rows, qcols].T
            num = num + iw * _dot_tn(cst[hh, d].astype(BF16), qt)
            n8 = jnp.broadcast_to(nst[hh, d], (8, ML_DQK)).astype(BF16)
            den = den + iw * _dot(n8, qt)[0:1]
        hc = num / jnp.maximum(jnp.abs(den), jnp.exp(-mcomb))
        if d == 0:
            hacc[hh, :, rows] = hc
        else:
            hacc[hh, :, rows] += hc
        if emit_state or not last:
            end = T * c + (T - 1 if d == 0 else 0)
            bq = gtr[1 + d, lb:lb + 1, end:end + 1]
            wlog = bq - b_row + logi_row
            m_new = jnp.maximum(bq + m_prev, jnp.max(wlog, axis=1, keepdims=True))
            sw = jnp.exp(wlog - m_new)
            upd = _dot_nt((kts[hh, c].astype(F32) * sw).astype(BF16), vt)
            nsum = _dot(jnp.broadcast_to(sw, (8, T)).astype(BF16), k_ref[rows, qcols])[0:1]
            if zero_state:
                cst[hh, d] = upd
                nst[hh, d] = nsum
            else:
                cw = jnp.exp(bq + m_prev - m_new)
                cst[hh, d] = cw * cst[hh, d] + upd
                nst[hh, d] = cw * nst[hh, d] + nsum
            mst[hh, d] = jnp.broadcast_to(m_new, (1, LANES))

    for hh in range(hp):
        qcols = slice(ML_DQK * hh, ML_DQK * (hh + 1))
        vcols = slice(ML_DV * hh, ML_DV * (hh + 1))
        if has_s0:
            for d in range(2):
                cst[hh, d] = c0_ref[d, hh]
                nst[hh, d] = n0_ref[d, pl.ds(h0 + hh, 1), :]
                mst[hh, d] = m0_ref[d, pl.ds(h0 + hh, 1), :]
        for c in range(nc):
            rows = slice(T * c, T * (c + 1))
            kc = k_ref[rows, qcols]
            kq[hh, c] = _dot_nt(kc, q_ref[rows, qcols])
            kts[hh, c] = kc.T
            vts[hh, c] = v_ref[rows, vcols].T
        for c in range(nc):
            chunk(hh, c, 0)
        for c in range(nc):
            chunk(hh, nc - 1 - c, 1)
        ht = hacc[hh]
        r = lax.rsqrt(jnp.mean(ht * ht, axis=0, keepdims=True) + EPS)
        h_ref[:, vcols] = ((ht * r).T * ng_ref[:, vcols]).astype(BF16)
        if emit_state:
            for d in range(2):
                c_out[d, hh] = cst[hh, d]
                n_out[d, pl.ds(h0 + hh, 1), :] = nst[hh, d]
                m_out[d, pl.ds(h0 + hh, 1), :] = mst[hh, d]


def _mlstm_scan(q, k, v, gates, gb, ng, s0, *, nb, L, hp, emit_state):
    rb0 = 0 if s0 is None else ROWS_P // L
    in_specs = [
        pl.BlockSpec((L, hp * ML_DQK), lambda b, h: (rb0 + b, h)),
        pl.BlockSpec((L, hp * ML_DQK), lambda b, h: (rb0 + b, h)),
        pl.BlockSpec((L, hp * ML_DV), lambda b, h: (rb0 + b, h)),
        pl.BlockSpec((L, LANES), lambda b, h: (rb0 + b, 0)),
        pl.BlockSpec((1, LANES), lambda b, h: (0, 0)),
        pl.BlockSpec((1, hp * ML_DV), lambda b, h: (0, h)),
    ]
    args = [q, k, v, gates, gb, ng]
    c_spec = pl.BlockSpec((None, None, 2, hp, ML_DQK, ML_DV), lambda b, h: (b, 0, 0, h, 0, 0))
    n_spec = pl.BlockSpec((None, 2, ML_HEADS, LANES), lambda b, h: (b, 0, 0, 0))
    if s0 is not None:
        in_specs += [c_spec, n_spec, n_spec]
        args += list(s0)
    out_specs = [pl.BlockSpec((L, hp * ML_DV), lambda b, h: (b, h))]
    out_shape = [jax.ShapeDtypeStruct((nb * L, ML_DI), BF16)]
    if emit_state:
        out_specs += [c_spec, n_spec, n_spec]
        out_shape += [
            jax.ShapeDtypeStruct((nb, 1, 2, ML_HEADS, ML_DQK, ML_DV), F32),
            jax.ShapeDtypeStruct((nb, 2, ML_HEADS, LANES), F32),
            jax.ShapeDtypeStruct((nb, 2, ML_HEADS, LANES), F32),
        ]
    scratch = [
        pltpu.VMEM((3, L, LANES), F32),
        pltpu.VMEM((3, LANES, L), F32),
        pltpu.VMEM((hp, L // SCAN_T, SCAN_T, SCAN_T), F32),
        pltpu.VMEM((hp, L // SCAN_T, ML_DV, SCAN_T), BF16),
        pltpu.VMEM((hp, L // SCAN_T, ML_DQK, SCAN_T), BF16),
        pltpu.VMEM((hp, ML_DV, L), F32),
        pltpu.VMEM((hp, 2, ML_DQK, ML_DV), F32),
        pltpu.VMEM((hp, 2, 1, ML_DQK), F32),
        pltpu.VMEM((hp, 2, 1, LANES), F32),
    ]
    return pl.pallas_call(
        functools.partial(_mlstm_body, L=L, hp=hp, has_s0=s0 is not None, emit_state=emit_state),
        grid=(nb, ML_HEADS // hp),
        in_specs=in_specs,
        out_specs=out_specs,
        out_shape=out_shape,
        scratch_shapes=scratch,
        compiler_params=_cparams(("parallel", "arbitrary")),
        name="mlstm_scan_p" if s0 is None else "mlstm_scan_s",
    )(*args)


def _rope_blocks(x, ct, st):
    lane = lax.broadcasted_iota(jnp.int32, (x.shape[0], LANES), 1)
    first_half = lane % DF_D < DF_D // 2
    outs = []
    for blk in range(x.shape[1] // LANES):
        xb = x[:, LANES * blk:LANES * (blk + 1)]
        swapped = jnp.where(first_half, pltpu.roll(xb, LANES - DF_D // 2, 1), pltpu.roll(xb, DF_D // 2, 1))
        outs.append(xb * ct + swapped * st)
    return jnp.concatenate(outs, axis=1)


def _diff_body(*refs, L, sample, lam_init):
    it = iter(refs)
    q_ref, k_ref, v_ref = next(it), next(it), next(it)
    lq1, lk1, lq2, lk2, sg_ref = (next(it) for _ in range(5))
    if sample:
        ck_ref, cv_ref, cq_t, sq_t, ck_t, sk_t = (next(it) for _ in range(6))
    x_ref, wo_ref, mod_ref = next(it), next(it), next(it)
    x_out = next(it)
    if sample:
        ka, va = next(it), next(it)
    else:
        k_out, v_out = next(it), next(it)
    o_scr = next(it)
    qt = pl.program_id(1)

    lam = (jnp.exp(jnp.sum(lq1[...] * lk1[...], axis=1, keepdims=True))
           - jnp.exp(jnp.sum(lq2[...] * lk2[...], axis=1, keepdims=True)) + lam_init)

    @pl.when(qt == 0)
    def _():
        if sample:
            ka[0:PAST, :] = ck_ref[...].astype(BF16)
            ka[PAST:PAST + L, :] = _rope_blocks(k_ref[...].astype(F32), ck_t[...], sk_t[...]).astype(BF16)
            va[0:PAST, :] = cv_ref[...].astype(BF16)
            va[PAST:PAST + L, :] = v_ref[...]
        else:
            k_out[...] = k_ref[...].astype(F32).reshape(L, DF_HEADS, 2 * DF_D)
            v_out[...] = v_ref[...].astype(F32).reshape(L, DF_HEADS, 2 * DF_D)

    q = q_ref[...].astype(F32)
    if sample:
        q = _rope_blocks(q, cq_t[...], sq_t[...])
    q = q * (DF_D ** -0.5)
    lo = lax.broadcasted_iota(jnp.int32, (TQ, LANES), 1) < DF_D

    def attend(qm, kh, vh):
        s = _dot_nt(qm, kh)
        e = jnp.exp(s - jnp.max(s, axis=1, keepdims=True))
        return _dot(e.astype(BF16), vh) / jnp.sum(e, axis=1, keepdims=True)

    for h in range(DF_HEADS):
        cols = slice(LANES * h, LANES * (h + 1))
        qh = q[:, cols]
        if sample:
            kh, vh = ka[:, cols], va[:, cols]
        else:
            kh, vh = k_ref[:, cols], v_ref[:, cols]
        a0 = attend(jnp.where(lo, qh, 0.0).astype(BF16), kh, vh)
        a1 = attend(jnp.where(lo, 0.0, qh).astype(BF16), kh, vh)
        o = a0 - lam * a1
        o_scr[:, cols] = (_rms(o, sg_ref[...]) * (1.0 - lam_init)).astype(BF16)
    x_out[...] = x_ref[...] + mod_ref[2:3, :] * _dot(o_scr[...], wo_ref[...].astype(BF16))


def _diff_attn(p3, lq1, lk1, lq2, lk2, sg, ctx, x, wo, mod, *, nb, L, lam_init):
    sample = ctx is not None
    nq = L // TQ
    rbq0 = ROWS_P // TQ if sample else 0
    rbs0 = ROWS_P // L if sample else 0

    def const(a):
        return pl.BlockSpec(a.shape, lambda b, q, nd=a.ndim: (0,) * nd)

    in_specs = [
        pl.BlockSpec((TQ, D), lambda b, q: (rbq0 + b * nq + q, 0)),
        pl.BlockSpec((L, D), lambda b, q: (rbs0 + b, 1)),
        pl.BlockSpec((L, D), lambda b, q: (rbs0 + b, 2)),
        const(lq1), const(lk1), const(lq2), const(lk2), const(sg),
    ]
    args = [p3, p3, p3, lq1, lk1, lq2, lk2, sg]
    kv_spec = pl.BlockSpec((None, None, PAST if sample else L, D), lambda b, q: (b, 0, 0, 0))
    if sample:
        ck, cv, c128, s128 = ctx
        in_specs += [
            kv_spec, kv_spec,
            pl.BlockSpec((TQ, LANES), lambda b, q: (q, 0)),
            pl.BlockSpec((TQ, LANES), lambda b, q: (q, 0)),
            const(c128), const(s128),
        ]
        args += [ck, cv, c128, s128, c128, s128]
    mod_arr, mod_layer = mod
    x_spec = pl.BlockSpec((TQ, D), lambda b, q: (rbq0 + b * nq + q, 0))
    in_specs += [
        x_spec, const(wo),
        pl.BlockSpec((None, None, 6, D), lambda b, q: (mod_layer, (1 + b) if sample else 0, 0, 0)),
    ]
    x_index = len(args)
    args += [x, wo, mod_arr]
    out_specs = [x_spec]
    out_shape = [jax.ShapeDtypeStruct((ROWS, D), F32)]
    scratch = []
    if sample:
        scratch = [pltpu.VMEM((PAST + L, D), BF16), pltpu.VMEM((PAST + L, D), BF16)]
    else:
        kv_out = pl.BlockSpec((None, None, L, DF_HEADS, 2 * DF_D), lambda b, q: (b, 0, 0, 0, 0))
        out_specs += [kv_out, kv_out]
        out_shape += [jax.ShapeDtypeStruct((nb, 1, L, DF_HEADS, 2 * DF_D), F32)] * 2
    scratch = scratch + [pltpu.VMEM((TQ, D), BF16)]
    return pl.pallas_call(
        functools.partial(_diff_body, L=L, sample=sample, lam_init=lam_init),
        grid=(nb, nq),
        in_specs=in_specs,
        out_specs=out_specs,
        out_shape=out_shape,
        scratch_shapes=scratch,
        input_output_aliases={x_index: 0},
        compiler_params=_cparams(("parallel", "arbitrary")),
        name="diff_attn_s" if sample else "diff_attn_p",
    )(*args)


CONV_COLS = 256


def _mlstm_q_body(x_ref, cw_ref, cb_ref, w_ref, q_ref, xc_ref, msk):
    tm = x_ref.shape[0]
    seq = jnp.where(pl.program_id(0) < ROWS_P // tm, L_PROMPT, L_SAMPLE)
    pos = lax.broadcasted_iota(jnp.int32, (tm, CONV_COLS), 0) & (seq - 1)
    offs = [k - CONV_W // 2 for k in range(CONV_W)]
    for k, off in enumerate(offs):
        if off != 0:
            dst = pos - off
            msk[k] = jnp.where(jnp.logical_and(dst >= 0, dst < seq), 1.0, 0.0)
    acc = None
    kslab = 2 * CONV_COLS
    for kb in range(ML_DI // kslab):
        for cb in range(kslab // CONV_COLS):
            cols = slice(kslab * kb + CONV_COLS * cb, kslab * kb + CONV_COLS * (cb + 1))
            x = x_ref[:, cols].astype(F32)
            y = None
            for k, off in enumerate(offs):
                tap = x if off == 0 else pltpu.roll(x * msk[k], (-off) % tm, 0)
                tap = tap * cw_ref[k:k + 1, cols]
                y = cb_ref[:, cols] + tap if y is None else y + tap
            xc_ref[:, cols] = _silu(y).astype(BF16)
        kcols = slice(kslab * kb, kslab * (kb + 1))
        part = _dot(xc_ref[:, kcols], w_ref[kcols, :].astype(BF16))
        acc = part if acc is None else part + acc
    q_ref[...] = acc.astype(BF16)


def _mlstm_q(xm_src, conv_w, conv_b, w_q):
    tm = TM_CONVQ
    nq = ML_HEADS * ML_DQK
    return pl.pallas_call(
        _mlstm_q_body,
        grid=(ROWS // tm,),
        in_specs=[
            pl.BlockSpec((tm, ML_DI), lambda i: (i, 0)),
            pl.BlockSpec((CONV_W, ML_DI), lambda i: (0, 0)),
            pl.BlockSpec((1, ML_DI), lambda i: (0, 0)),
            pl.BlockSpec((ML_DI, nq), lambda i: (0, 0)),
        ],
        out_specs=[pl.BlockSpec((tm, nq), lambda i: (i, 0)), pl.BlockSpec((tm, ML_DI), lambda i: (i, 0))],
        out_shape=[jax.ShapeDtypeStruct((ROWS, nq), BF16), jax.ShapeDtypeStruct((ROWS, ML_DI), BF16)],
        scratch_shapes=[pltpu.VMEM((CONV_W, tm, CONV_COLS), F32)],
        compiler_params=_cparams(("parallel",)),
        name="mlstm_q",
    )(xm_src, conv_w, conv_b, w_q)


def _pro_mlstm_gate(p, rs):
    hn, xc, z = (r[rs, :].astype(F32) for r in p.rows)
    p.emit((hn + p.consts[0][...] * xc) * _silu(z), rs)


def _rope_tables(d):
    rows = L_SAMPLE // GRID_W
    pos_r = jnp.repeat(jnp.arange(rows, dtype=F32), GRID_W)
    pos_c = jnp.tile(jnp.arange(GRID_W, dtype=F32), rows)
    nf = d // 4
    inv = ROPE_BASE ** (-jnp.arange(nf, dtype=F32) / nf)
    ang = jnp.concatenate([pos_r[:, None] * inv, pos_c[:, None] * inv], axis=-1)
    cos, sin = jnp.cos(ang), jnp.sin(ang)
    return jnp.concatenate([cos, cos], axis=-1), jnp.concatenate([-sin, sin], axis=-1)


def _pad_cols(a, n):
    return jnp.pad(a, ((0, 0), (0, n - a.shape[1])))


def _pad_rows(a, n):
    return jnp.pad(a, ((0, n - a.shape[0]), (0, 0)))


def kernel(x_prompt, x_sample, state_ssd, cache_mla_ckv, cache_mla_krope, state_mlstm_C, state_mlstm_n, state_mlstm_m, cache_diff_k, cache_diff_v, c, c_ctx, norm1_g, norm2_g, ada_w, ada_b, mlp_w1, mlp_w2, final_g, ssd_w_in, ssd_conv_w, ssd_conv_b, ssd_dt_bias, ssd_A_log, ssd_D, ssd_norm_g, ssd_w_out, mla_w_in, mla_q_norm_g, mla_kv_norm_g, mla_w_uq, mla_w_ukv, mla_w_o, mlstm_w_up, mlstm_conv_w, mlstm_conv_b, mlstm_gate_b, mlstm_w_q, mlstm_w_k, mlstm_w_v, mlstm_skip, mlstm_norm_g, mlstm_w_down, diff_w_qkv, diff_lq1, diff_lk1, diff_lq2, diff_lk2, diff_subln_g, diff_w_o):
    xp2, xs2 = x_prompt.reshape(ROWS_P, D), x_sample.reshape(ROWS_S, D)
    cvec = jnp.concatenate([c_ctx[None, :], c, jnp.zeros((5, D), F32)], axis=0)
    mod_all = _ada_mod(cvec, ada_w, ada_b)
    g2 = norm2_g.reshape(DEPTH, 1, D)

    def in_proj(xin, layer, w, n_cols, tn, extra_w=None, w_t=False, name="in_proj"):
        xrow = ("d", xin[0], xin[1], D, 0) if isinstance(xin, tuple) else ("u", xin, D, 0)
        return _fused_mm(rows=[xrow], consts=[norm1_g[layer][None, :]], mod=(mod_all, layer), w=w, k_dim=D,
                         n_cols=n_cols, tm=TM_IN, tn=tn, prologue=_pro_normmod, extra_w=extra_w, w_t=w_t, name=name)

    def out_proj(xin, layer, rows, consts, prologue, w, k_dim, name):
        return _fused_mm(rows=rows, consts=consts, w=w, k_dim=k_dim, n_cols=D, tm=TM_OUT, tn=TN_PROJ, prologue=prologue,
                         epilogue=_epi_residual(2), erows=[xin], emod=(mod_all, layer), out_dtype=F32, name=name)

    w0t = ssd_w_in[0].T
    p0, pdt = in_proj((xp2, xs2), 0, w0t, 3 * SSD_DI, TN_SSD_IN, extra_w=_pad_rows(w0t[3 * SSD_DI:], LANES), w_t=True,
                      name="ssd_in")
    dtb = _pad_cols(ssd_dt_bias[0].reshape(1, 2 * SSD_HEADS), LANES)
    alog = _pad_cols(ssd_A_log[0].reshape(1, 2 * SSD_HEADS), LANES)
    dl = jnp.repeat(ssd_D[0], SSD_P)[None, :]
    scan_args = (p0, pdt, ssd_conv_w[0], ssd_conv_b[0][None, :], dtb, alog, dl)
    yg_p, new_ssd = _ssd_scan(*scan_args, None, nb=N_PROMPT_SEQ, L=L_PROMPT, emit_state=True)
    (yg_s,) = _ssd_scan(*scan_args, state_ssd, nb=N_SAMPLE_SEQ, L=L_SAMPLE, emit_state=False)
    x = out_proj((xp2, xs2), 0, [("d", yg_p, yg_s, SSD_DI, 0)], [ssd_norm_g[0][None, :]], _pro_rms, ssd_w_out[0], SSD_DI,
                 "ssd_out")
    x = _mlp(x, 0, g2, mod_all, mlp_w1, mlp_w2)

    w_in = mla_w_in[0]
    kr0 = MLA_Q_RANK + MLA_KV_RANK
    half = MLA_ROPE // 2
    zk = jnp.zeros((D, MLA_NOPE), F32)
    zr = jnp.zeros((D, LANES - MLA_NOPE - MLA_ROPE), F32)
    w_kr = jnp.concatenate([zk, w_in[:, kr0:kr0 + MLA_ROPE], zr,
                            zk, w_in[:, kr0 + half:kr0 + MLA_ROPE], w_in[:, kr0:kr0 + half], zr], axis=1)
    p1, p1s = in_proj(x, 1, w_in, kr0, kr0, extra_w=w_kr, name="mla_in")
    wuq = mla_w_uq[0].reshape(MLA_Q_RANK, MLA_HEADS, MLA_NOPE + MLA_ROPE)
    zq = jnp.zeros((MLA_Q_RANK, MLA_HEADS, LANES - MLA_NOPE - MLA_ROPE), F32)
    zqn = jnp.zeros((MLA_Q_RANK, MLA_HEADS, MLA_NOPE), F32)
    wq = jnp.concatenate([wuq, zq], axis=-1).reshape(MLA_Q_RANK, MLA_HEADS * LANES).astype(BF16)
    wqs = jnp.concatenate([zqn, wuq[..., MLA_NOPE + half:], wuq[..., MLA_NOPE:MLA_NOPE + half], zq],
                          axis=-1).reshape(MLA_Q_RANK, MLA_HEADS * LANES).astype(BF16)
    wukv = mla_w_ukv[0].reshape(MLA_KV_RANK, MLA_HEADS, MLA_NOPE + MLA_V)
    zkv = jnp.zeros((MLA_KV_RANK, MLA_HEADS, MLA_NOPE), F32)
    wk = jnp.concatenate([wukv[..., :MLA_NOPE], zkv], axis=-1).reshape(MLA_KV_RANK, MLA_HEADS * LANES).astype(BF16)
    wv_own = wukv[..., MLA_NOPE:]
    odd = (jnp.arange(MLA_HEADS) % 2 == 1)[None, :, None]
    wv = jnp.where(odd, jnp.concatenate([zkv, wv_own], axis=-1), jnp.concatenate([wv_own, zkv], axis=-1))
    wv = wv.reshape(MLA_KV_RANK, MLA_HEADS * LANES).astype(BF16)
    c32, s32 = _rope_tables(MLA_ROPE)
    tz = jnp.zeros((L_SAMPLE, LANES - MLA_NOPE - MLA_ROPE), F32)
    cpad = jnp.concatenate([jnp.ones((L_SAMPLE, MLA_NOPE), F32), c32, tz], axis=1)
    spad = jnp.concatenate([jnp.zeros((L_SAMPLE, MLA_NOPE), F32), s32, tz], axis=1)
    ckr_pad = jnp.pad(cache_mla_krope, ((0, 0), (0, 0), (0, 0), (MLA_NOPE, LANES - MLA_NOPE - MLA_ROPE)))
    mla_w = (mla_q_norm_g[0][None, :], mla_kv_norm_g[0][None, :], wq, wqs, wk, wv)
    x, new_ckv, new_kr = _mla_attn(p1, p1s, *mla_w, None, x, mla_w_o[0], (mod_all, 1), nb=N_PROMPT_SEQ, L=L_PROMPT)
    (x,) = _mla_attn(p1, p1s, *mla_w, (cache_mla_ckv, ckr_pad, cpad, spad), x, mla_w_o[0], (mod_all, 1),
                     nb=N_SAMPLE_SEQ, L=L_SAMPLE)
    x = _mlp(x, 1, g2, mod_all, mlp_w1, mlp_w2)

    w2t = mlstm_w_up[0].T
    p2, gates = in_proj(x, 2, w2t, 2 * ML_DI, TN_IN, extra_w=_pad_rows(w2t[2 * ML_DI:], LANES), w_t=True, name="mlstm_up")
    q, xc = _mlstm_q(p2, mlstm_conv_w[0], mlstm_conv_b[0][None, :], mlstm_w_q[0])
    k = _fused_mm(rows=[("u", xc, ML_DI, 0)], w=mlstm_w_k[0], k_dim=ML_DI, n_cols=ML_HEADS * ML_DQK, tm=TM_WIDE,
                  tn=TN_PROJ, epilogue=lambda acc, e, m, rs: acc * (ML_DQK ** -0.5), name="mlstm_k")
    v = _fused_mm(rows=[("u", p2, ML_DI, 0)], w=mlstm_w_v[0], k_dim=ML_DI, n_cols=ML_DI, tm=TM_WIDE, tn=TN_PROJ,
                  name="mlstm_v")
    gb = _pad_cols(mlstm_gate_b[0].reshape(1, 4 * ML_HEADS), LANES)
    ng = mlstm_norm_g[0][None, :]
    n0 = _pad_cols(state_mlstm_n[:, 0].reshape(N_SAMPLE_SEQ * 2 * ML_HEADS, ML_DQK), LANES).reshape(
        N_SAMPLE_SEQ, 2, ML_HEADS, LANES)
    m0 = jnp.broadcast_to(state_mlstm_m[:, 0][..., None], (N_SAMPLE_SEQ, 2, ML_HEADS, LANES))
    hn_p, new_c, new_n, new_m = _mlstm_scan(q, k, v, gates, gb, ng, None, nb=N_PROMPT_SEQ, L=L_PROMPT, hp=HP_P,
                                            emit_state=True)
    (hn_s,) = _mlstm_scan(q, k, v, gates, gb, ng, (state_mlstm_C, n0, m0), nb=N_SAMPLE_SEQ, L=L_SAMPLE, hp=HP_S,
                          emit_state=False)
    x = out_proj(x, 2, [("d", hn_p, hn_s, ML_DI, 0), ("u", xc, ML_DI, 0), ("u", p2, ML_DI, 1)],
                 [mlstm_skip[0][None, :]], _pro_mlstm_gate, mlstm_w_down[0], ML_DI, "mlstm_down")
    x = _mlp(x, 2, g2, mod_all, mlp_w1, mlp_w2)

    lam_init = 0.8 - 0.6 * math.exp(-0.3 * 3)
    p3 = in_proj(x, 3, diff_w_qkv[0], 3 * D, TN_IN, name="diff_qkv")
    c64, s64 = _rope_tables(DF_D)
    c128 = jnp.concatenate([c64, c64], axis=-1)
    s128 = jnp.concatenate([s64, s64], axis=-1)
    dparams = (diff_lq1, diff_lk1, diff_lq2, diff_lk2, diff_subln_g)
    x, new_dk, new_dv = _diff_attn(p3, *dparams, None, x, diff_w_o[0], (mod_all, 3), nb=N_PROMPT_SEQ, L=L_PROMPT,
                                   lam_init=lam_init)
    ctx = (cache_diff_k.reshape(N_SAMPLE_SEQ, 1, PAST, D), cache_diff_v.reshape(N_SAMPLE_SEQ, 1, PAST, D), c128, s128)
    (x,) = _diff_attn(p3, *dparams, ctx, x, diff_w_o[0], (mod_all, 3), nb=N_SAMPLE_SEQ, L=L_SAMPLE, lam_init=lam_init)
    y_prompt, y_sample = _mlp(x, 3, g2, mod_all, mlp_w1, mlp_w2, final_g=final_g[None, :])
    y_prompt = y_prompt.reshape(N_PROMPT_SEQ, L_PROMPT, D)
    y_sample = y_sample.reshape(N_SAMPLE_SEQ, L_SAMPLE, D)
    return (y_prompt, y_sample, new_ssd, new_ckv, new_kr, new_c,
            new_n[None].reshape(N_PROMPT_SEQ, 1, 2, ML_HEADS, ML_DQK),
            new_m[..., 0].reshape(N_PROMPT_SEQ, 1, 2, ML_HEADS),
            new_dk, new_dv)
```
